```python
import math
import jax
import jax.numpy as jnp
from jax import lax
import numpy as np

D_MODEL = 1024
BATCH = 8
SEQ = 4096
DEPTH = 4

GRID_W = 64
CTX_LEN = 256
N_EVEN = (DEPTH + 1) // 2
N_ODD = DEPTH // 2
EPS = 1e-6
ROPE_BASE = 10000.0
Q_BLOCK = 128
FFN_HIDDEN = ((8 * D_MODEL + 3 * 256 - 1) // (3 * 256)) * 256

HG_WIDTH = D_MODEL // 2
HG_HEAD_DIM = 128
HG_HEADS = HG_WIDTH // HG_HEAD_DIM
HG_CHUNK = 64
HY_WIDTH = D_MODEL - HG_WIDTH
HY_BANDS = 16
HY_EMB = 2 * HY_BANDS + 1
HY_FILTER_DIM = 64
HY_TARGET = 1e-2
HY_STEEP_PCT = 0.3
HY_GENTLE_PCT = 1.5
EVEN_IN = 5 * HG_WIDTH + 3 * HY_WIDTH
EVEN_OUT = HG_WIDTH + HY_WIDTH
DA_WIDTH = D_MODEL // 2
DA_HEAD_DIM = 64
DA_HEADS = DA_WIDTH // (2 * DA_HEAD_DIM)
DA_V_DIM = 2 * DA_HEAD_DIM
MLA_HEADS = 4
MLA_NOPE = 64
MLA_ROPE = 32
MLA_V = 128
MLA_Q_RANK = D_MODEL // 4
MLA_KV_RANK = D_MODEL // 8
ODD_IN = 3 * DA_WIDTH + MLA_Q_RANK + MLA_KV_RANK + MLA_ROPE
ODD_OUT = DA_HEADS * DA_V_DIM + MLA_HEADS * MLA_V

kernel_name = 'hybrid_hgrn2_hyena_diffattn_mla_dit'


def rms_norm(x, g):
    xf = x.astype(jnp.float32)
    y = xf * lax.rsqrt(jnp.mean(xf * xf, axis=-1, keepdims=True) + EPS)
    return (y * g.astype(jnp.float32)).astype(x.dtype)


def modulate(h, shift, scale):
    return h * (1 + scale) + shift


def swiglu(h, w_gu, w_down):
    g, u = jnp.split(h @ w_gu, 2, axis=-1)
    return (jax.nn.silu(g) * u) @ w_down


def rope_1d(x, pos):
    half = x.shape[-1] // 2
    inv = ROPE_BASE ** (-jnp.arange(half, dtype=jnp.float32) / half)
    ang = pos.astype(jnp.float32)[:, None] * inv
    cos, sin = jnp.cos(ang), jnp.sin(ang)
    xf = x.astype(jnp.float32)
    x1, x2 = xf[..., :half], xf[..., half:]
    return jnp.concatenate([x1 * cos - x2 * sin, x2 * cos + x1 * sin], axis=-1).astype(x.dtype)


def rope_2d(x, row, col):
    h = x.shape[-1] // 2
    return jnp.concatenate([rope_1d(x[..., :h], row), rope_1d(x[..., h:], col)], axis=-1)


def heads_to_tokens(o):
    b_, h_, l_, d_ = o.shape
    return o.transpose(0, 2, 1, 3).reshape(b_, l_, h_ * d_)


def block_attention(q, k, v, scale):
    b_, h_, lq, dq = q.shape
    nb = lq // Q_BLOCK
    qb = q.reshape(b_, h_, nb, Q_BLOCK, dq).transpose(2, 0, 1, 3, 4)

    def one(qi):
        s = jnp.einsum('bhqd,bhkd->bhqk', qi, k, preferred_element_type=jnp.float32) * scale
        p = jax.nn.softmax(s, axis=-1)
        return jnp.einsum('bhqk,bhkd->bhqd', p.astype(v.dtype), v)

    o = lax.map(one, qb)
    return o.transpose(1, 2, 0, 3, 4).reshape(b_, h_, lq, v.shape[-1])


def chunk_scan(q, k, v, log_f, s0):
    b_, h_, length, _ = q.shape
    dv = v.shape[-1]
    n = length // HG_CHUNK

    def chunks(t):
        return t.reshape(b_, h_, n, HG_CHUNK, t.shape[-1]).transpose(2, 0, 1, 3, 4)

    mask = jnp.tril(jnp.ones((HG_CHUNK, HG_CHUNK), dtype=bool))[:, :, None]

    def step(s, inp):
        qc, kc, vc, gc = inp
        cum = jnp.cumsum(gc, axis=-2)
        rel = cum[..., :, None, :] - cum[..., None, :, :]
        dec = jnp.exp(jnp.where(mask, rel, -jnp.inf))
        attn = jnp.einsum('bhtd,bhsd,bhtsd->bhts', qc, kc, dec)
        o = jnp.einsum('bhtd,bhde->bhte', qc * jnp.exp(cum), s) + jnp.einsum('bhts,bhse->bhte', attn, vc)
        last = cum[..., -1:, :]
        s = jnp.exp(last[..., 0, :])[..., None] * s + jnp.einsum('bhsd,bhse->bhde', kc * jnp.exp(last - cum), vc)
        return s, o

    s_fin, o = lax.scan(step, s0, (chunks(q), chunks(k), chunks(v), chunks(log_f)))
    return o.transpose(1, 2, 0, 3, 4).reshape(b_, h_, length, dv), s_fin


def hgrn2_mixer(z_lat, z_ctx, lb, out_norm_g):
    def split_heads(t):
        b_, l_, _ = t.shape
        return t.astype(jnp.float32).reshape(b_, l_, HG_HEADS, HG_HEAD_DIM).transpose(0, 2, 1, 3)

    def prep(z):
        q, ff, fb, i, g = jnp.split(z, 5, axis=-1)
        return split_heads(q), split_heads(ff), split_heads(fb), split_heads(i), g

    def gate(zf, lbd):
        lbh = lbd.reshape(HG_HEADS, 1, HG_HEAD_DIM)
        f = lbh + (1.0 - lbh) * jax.nn.sigmoid(zf)
        return jnp.log(f), 1.0 - f

    def flip(t):
        return jnp.flip(t, axis=2)

    def bidir(q, ff, fb, i, s_f, s_b):
        lf, kf = gate(ff, lb[0])
        lbk, kb = gate(fb, lb[1])
        o_f, s_f = chunk_scan(q, kf, i, lf, s_f)
        o_b, s_b = chunk_scan(flip(q), flip(kb), flip(i), flip(lbk), s_b)
        return o_f + flip(o_b), s_f, s_b

    def readout(o, g):
        o = heads_to_tokens(rms_norm(o, out_norm_g))
        return (o * jax.nn.silu(g.astype(jnp.float32))).astype(g.dtype)

    qc, ffc, fbc, ic, gc = prep(z_ctx)
    s0 = jnp.zeros((z_ctx.shape[0], HG_HEADS, HG_HEAD_DIM, HG_HEAD_DIM), jnp.float32)
    o_c, s_f, s_b = bidir(qc, ffc, fbc, ic, s0, s0)
    ql, ffl, fbl, il, gl = prep(z_lat)
    o_l, _, _ = bidir(ql, ffl, fbl, il, s_f, s_b)
    return readout(o_l, gl), readout(o_c, gc)


def hyena_filter(length, w1, b1, fr1, w2, b2, fr2, w3):
    f32 = jnp.float32
    t = jnp.linspace(0.0, 1.0, length, dtype=f32)[:, None]
    w = 2.0 * math.pi * jnp.arange(length, dtype=f32)[:, None] / length
    bands = jnp.linspace(1e-4, HY_BANDS - 1, HY_BANDS, dtype=f32)[None, :]
    z = jnp.concatenate([t, jnp.cos(bands * w), -jnp.sin(bands * w)], axis=-1)
    h = jnp.sin(fr1.astype(f32) * (z @ w1.astype(f32) + b1.astype(f32)))
    h = jnp.sin(fr2.astype(f32) * (h @ w2.astype(f32) + b2.astype(f32)))
    h = (h @ w3.astype(f32)).reshape(length, 2, HY_WIDTH)
    d_lo = -math.log(HY_TARGET) / HY_GENTLE_PCT
    d_hi = -math.log(HY_TARGET) / HY_STEEP_PCT
    deltas = jnp.linspace(d_lo, d_hi, HY_WIDTH, dtype=f32)
    h = h * jnp.exp(-t * deltas)[:, None, :]
    h_fwd, h_bwd = h[:, 0], h[:, 1]
    filt = jnp.concatenate([h_fwd, jnp.zeros((1, HY_WIDTH), f32), h_bwd[: length - 1][::-1]], axis=0)
    return filt / jnp.sum(jnp.abs(filt), axis=0, keepdims=True)


def hyena_mixer(z, short_w, short_b, filt, skip):
    length = z.shape[1]
    zp = jnp.pad(z, ((0, 0), (1, 1), (0, 0)))
    zc = zp[:, :-2] * short_w[0] + zp[:, 1:-1] * short_w[1] + zp[:, 2:] * short_w[2] + short_b
    x0, x1, v = jnp.split(zc, 3, axis=-1)
    u = (x1 * v).astype(jnp.float32)
    uf = jnp.fft.rfft(u, n=2 * length, axis=1)
    ff = jnp.fft.rfft(filt, n=2 * length, axis=0)
    y = jnp.fft.irfft(uf * ff[None], n=2 * length, axis=1)[:, :length]
    y = y + skip.astype(jnp.float32) * u
    return (x0.astype(jnp.float32) * y).astype(z.dtype)


def even_mixer(h_lat, h_ctx, w_in, w_out, lb, out_norm_g, short_w, short_b, filt_params, skip, need_ctx):
    z_lat = h_lat @ w_in
    z_ctx = h_ctx @ w_in
    na = 5 * HG_WIDTH
    a_lat, a_ctx = hgrn2_mixer(z_lat[..., :na], z_ctx[..., :na], lb, out_norm_g)
    b_lat = hyena_mixer(z_lat[..., na:], short_w, short_b, hyena_filter(h_lat.shape[1], *filt_params), skip)
    y_lat = jnp.concatenate([a_lat, b_lat], axis=-1) @ w_out
    if not need_ctx:
        return y_lat, None
    b_ctx = hyena_mixer(z_ctx[..., na:], short_w, short_b, hyena_filter(h_ctx.shape[1], *filt_params), skip)
    y_ctx = jnp.concatenate([a_ctx, b_ctx], axis=-1) @ w_out
    return y_lat, y_ctx


def diff_heads(z, row, col, rotate):
    b_, l_, _ = z.shape
    q, k, v = jnp.split(z, 3, axis=-1)

    def qk(t):
        t = t.reshape(b_, l_, DA_HEADS, 2, DA_HEAD_DIM).transpose(0, 3, 2, 1, 4).reshape(b_, 2 * DA_HEADS, l_, DA_HEAD_DIM)
        return rope_2d(t, row, col) if rotate else t

    v = v.reshape(b_, l_, DA_HEADS, DA_V_DIM).transpose(0, 2, 1, 3)
    return qk(q), qk(k), jnp.concatenate([v, v], axis=1)


def diff_readout(o, lam, lam_init, subln_g):
    b_, _, l_, _ = o.shape
    o = o.reshape(b_, 2, DA_HEADS, l_, DA_V_DIM)
    o = o[:, 0] - lam.astype(o.dtype) * o[:, 1]
    o = rms_norm(o, subln_g) * (1.0 - lam_init)
    return heads_to_tokens(o)


def mla_heads(z, q_norm_g, w_uq, kv_norm_g, w_ukv, row, col, rotate):
    b_, l_, _ = z.shape
    cq = z[..., :MLA_Q_RANK]
    ckv = z[..., MLA_Q_RANK:MLA_Q_RANK + MLA_KV_RANK]
    kr = z[..., MLA_Q_RANK + MLA_KV_RANK:][:, None]
    q = (rms_norm(cq, q_norm_g) @ w_uq).reshape(b_, l_, MLA_HEADS, MLA_NOPE + MLA_ROPE).transpose(0, 2, 1, 3)
    kv = (rms_norm(ckv, kv_norm_g) @ w_ukv).reshape(b_, l_, MLA_HEADS, MLA_NOPE + MLA_V).transpose(0, 2, 1, 3)
    q_nope, q_rope = q[..., :MLA_NOPE], q[..., MLA_NOPE:]
    k_nope, v = kv[..., :MLA_NOPE], kv[..., MLA_NOPE:]
    if rotate:
        q_rope = rope_2d(q_rope, row, col)
        kr = rope_2d(kr, row, col)
    q = jnp.concatenate([q_nope, q_rope], axis=-1)
    k = jnp.concatenate([k_nope, jnp.broadcast_to(kr, (b_, MLA_HEADS, l_, MLA_ROPE))], axis=-1)
    return q, k, v


def odd_mixer(h_lat, h_ctx, row, col, w_in, w_out, lam_p, lam_init, subln_g, q_norm_g, w_uq, kv_norm_g, w_ukv, need_ctx):
    z_lat = h_lat @ w_in
    z_ctx = h_ctx @ w_in
    nc = 3 * DA_WIDTH
    lp = lam_p.astype(jnp.float32)
    lam = jnp.exp(jnp.sum(lp[0] * lp[1])) - jnp.exp(jnp.sum(lp[2] * lp[3])) + lam_init
    qa, ka, va = diff_heads(z_lat[..., :nc], row, col, True)
    qa_c, ka_c, va_c = diff_heads(z_ctx[..., :nc], row, col, False)
    qm, km, vm = mla_heads(z_lat[..., nc:], q_norm_g, w_uq, kv_norm_g, w_ukv, row, col, True)
    qm_c, km_c, vm_c = mla_heads(z_ctx[..., nc:], q_norm_g, w_uq, kv_norm_g, w_ukv, row, col, False)
    sa = DA_HEAD_DIM ** -0.5
    sm = (MLA_NOPE + MLA_ROPE) ** -0.5
    oa = block_attention(qa, jnp.concatenate([ka, ka_c], axis=2), jnp.concatenate([va, va_c], axis=2), sa)
    om = block_attention(qm, jnp.concatenate([km, km_c], axis=2), jnp.concatenate([vm, vm_c], axis=2), sm)
    y_lat = jnp.concatenate([diff_readout(oa, lam, lam_init, subln_g), heads_to_tokens(om)], axis=-1) @ w_out
    if not need_ctx:
        return y_lat, None
    oa_c = block_attention(qa_c, ka_c, va_c, sa)
    om_c = block_attention(qm_c, km_c, vm_c, sm)
    y_ctx = jnp.concatenate([diff_readout(oa_c, lam, lam_init, subln_g), heads_to_tokens(om_c)], axis=-1) @ w_out
    return y_lat, y_ctx


def setup_inputs(seed: int = 0) -> dict:
    key = jax.random.key(seed)
    keys = iter(jax.random.split(key, 64))

    def nrm(shape, scale):
        return scale * jax.random.normal(next(keys), shape, jnp.float32)

    def gain(shape):
        return 1.0 + nrm(shape, 0.02)

    D = D_MODEL
    return {
        'x': nrm((BATCH, SEQ, D), 1.0),
        'c': nrm((BATCH, D), 1.0),
        'ctx': nrm((BATCH, CTX_LEN, D), 1.0),
        'c_ctx': nrm((D,), 1.0),
        'ada_w': nrm((DEPTH, D, 6 * D), 0.5 * D ** -0.5),
        'ada_b': nrm((DEPTH, 6 * D), 0.02),
        'norm_mix_g': gain((DEPTH, D)),
        'norm_ffn_g': gain((DEPTH, D)),
        'ffn_w_gu': nrm((DEPTH, D, 2 * FFN_HIDDEN), D ** -0.5),
        'ffn_w_down': nrm((DEPTH, FFN_HIDDEN, D), FFN_HIDDEN ** -0.5),
        'ev_w_in': nrm((N_EVEN, D, EVEN_IN), D ** -0.5),
        'ev_w_out': nrm((N_EVEN, EVEN_OUT, D), EVEN_OUT ** -0.5),
        'hg_lower_bound': nrm((N_EVEN, 2, HG_WIDTH), 1.0),
        'hg_out_norm_g': gain((N_EVEN, HG_HEAD_DIM)),
        'hy_short_w': nrm((N_EVEN, 3, 3 * HY_WIDTH), 3 ** -0.5),
        'hy_short_b': nrm((N_EVEN, 3 * HY_WIDTH), 0.02),
        'hy_filt_w1': nrm((N_EVEN, HY_EMB, HY_FILTER_DIM), HY_EMB ** -0.5),
        'hy_filt_b1': nrm((N_EVEN, HY_FILTER_DIM), 0.02),
        'hy_filt_freq1': 1.0 + nrm((N_EVEN, HY_FILTER_DIM), 0.1),
        'hy_filt_w2': nrm((N_EVEN, HY_FILTER_DIM, HY_FILTER_DIM), HY_FILTER_DIM ** -0.5),
        'hy_filt_b2': nrm((N_EVEN, HY_FILTER_DIM), 0.02),
        'hy_filt_freq2': 1.0 + nrm((N_EVEN, HY_FILTER_DIM), 0.1),
        'hy_filt_w3': nrm((N_EVEN, HY_FILTER_DIM, 2 * HY_WIDTH), HY_FILTER_DIM ** -0.5),
        'hy_skip': nrm((N_EVEN, HY_WIDTH), 0.5),
        'od_w_in': nrm((N_ODD, D, ODD_IN), D ** -0.5),
        'od_w_out': nrm((N_ODD, ODD_OUT, D), ODD_OUT ** -0.5),
        'da_lambda': nrm((N_ODD, 4, DA_HEAD_DIM), 0.1),
        'da_subln_g': gain((N_ODD, DA_V_DIM)),
        'mla_q_norm_g': gain((N_ODD, MLA_Q_RANK)),
        'mla_w_uq': nrm((N_ODD, MLA_Q_RANK, MLA_HEADS * (MLA_NOPE + MLA_ROPE)), MLA_Q_RANK ** -0.5),
        'mla_kv_norm_g': gain((N_ODD, MLA_KV_RANK)),
        'mla_w_ukv': nrm((N_ODD, MLA_KV_RANK, MLA_HEADS * (MLA_NOPE + MLA_V)), MLA_KV_RANK ** -0.5),
        'final_norm_g': gain((D,)),
    }


def reference(x, c, ctx, c_ctx, ada_w, ada_b, norm_mix_g, norm_ffn_g, ffn_w_gu, ffn_w_down, ev_w_in, ev_w_out, hg_lower_bound, hg_out_norm_g, hy_short_w, hy_short_b, hy_filt_w1, hy_filt_b1, hy_filt_freq1, hy_filt_w2, hy_filt_b2, hy_filt_freq2, hy_filt_w3, hy_skip, od_w_in, od_w_out, da_lambda, da_subln_g, mla_q_norm_g, mla_w_uq, mla_kv_norm_g, mla_w_ukv, final_norm_g):
    n_lat = x.shape[1]
    rows = n_lat // GRID_W
    tok = jnp.arange(rows * GRID_W)
    row = tok // GRID_W
    col = tok % GRID_W
    p_lb = jax.nn.softmax(hg_lower_bound.astype(jnp.float32), axis=0)
    lower_bounds = jnp.cumsum(p_lb, axis=0) - p_lb[0:1]
    sc = jax.nn.silu(c)
    scc = jax.nn.silu(c_ctx)
    h, hc = x, ctx
    for i in range(DEPTH):
        need_ctx = i < DEPTH - 1
        m = jnp.split(sc @ ada_w[i] + ada_b[i], 6, axis=-1)
        mc = jnp.split(scc @ ada_w[i] + ada_b[i], 6, axis=-1)
        hn = modulate(rms_norm(h, norm_mix_g[i]), m[0][:, None], m[1][:, None])
        hcn = modulate(rms_norm(hc, norm_mix_g[i]), mc[0], mc[1])
        if i % 2 == 0:
            e = i // 2
            filt_params = (hy_filt_w1[e], hy_filt_b1[e], hy_filt_freq1[e], hy_filt_w2[e], hy_filt_b2[e], hy_filt_freq2[e], hy_filt_w3[e])
            y, yc = even_mixer(hn, hcn, ev_w_in[e], ev_w_out[e], lower_bounds[e], hg_out_norm_g[e], hy_short_w[e], hy_short_b[e], filt_params, hy_skip[e], need_ctx)
        else:
            o = i // 2
            lam_init = 0.8 - 0.6 * math.exp(-0.3 * i)
            y, yc = odd_mixer(hn, hcn, row, col, od_w_in[o], od_w_out[o], da_lambda[o], lam_init, da_subln_g[o], mla_q_norm_g[o], mla_w_uq[o], mla_kv_norm_g[o], mla_w_ukv[o], need_ctx)
        h = h + m[2][:, None] * y
        h = h + m[5][:, None] * swiglu(modulate(rms_norm(h, norm_ffn_g[i]), m[3][:, None], m[4][:, None]), ffn_w_gu[i], ffn_w_down[i])
        if need_ctx:
            hc = hc + mc[2] * yc
            hc = hc + mc[5] * swiglu(modulate(rms_norm(hc, norm_ffn_g[i]), mc[3], mc[4]), ffn_w_gu[i], ffn_w_down[i])
    return rms_norm(h, final_norm_g)
```

```python
import functools
import math

import numpy as np
import jax
import jax.numpy as jnp
from jax import lax
from jax.experimental import pallas as pl
from jax.experimental.pallas import tpu as pltpu

F32 = jnp.float32
BF16 = jnp.bfloat16
HIGHEST = lax.Precision.HIGHEST

GRID_W = 64
EPS = 1e-6
ROPE_BASE = 10000.0
HG_HEAD_DIM = 128
HG_CHUNK = 64
HY_BANDS = 16
HY_TARGET = 1e-2
HY_STEEP_PCT = 0.3
HY_GENTLE_PCT = 1.5
DA_HEAD_DIM = 64
MLA_HEADS = 4
MLA_NOPE = 64
MLA_ROPE = 32
MLA_V = 128

LANES = 128
SUBLANES = 8
V7X_VMEM_LIMIT_BYTES = 56 * 1024 * 1024
MOD_ROWS = 16
FFT_N2 = 128
FFT_PITCH = FFT_N2 + SUBLANES


def _params(*sem):
    return pltpu.CompilerParams(dimension_semantics=sem, vmem_limit_bytes=V7X_VMEM_LIMIT_BYTES)


def _dot(a, b, precision=None):
    return jnp.dot(a, b, preferred_element_type=F32, precision=precision)


def _dot_nt(a, b):
    return lax.dot_general(a, b, (((1,), (1,)), ((), ())), preferred_element_type=F32)


def _dot_tn(a, b):
    return lax.dot_general(a, b, (((0,), (0,)), ((), ())), preferred_element_type=F32)


def _rms(x):
    return x * lax.rsqrt(jnp.mean(x * x, axis=-1, keepdims=True) + EPS)


def _silu(x):
    return x * jax.nn.sigmoid(x)


def _tile_mod(mod_ref, modc_ref, n_ctx_tiles):
    return jnp.where(pl.program_id(1) < n_ctx_tiles, modc_ref[...], mod_ref[...])


def _ada_kernel(c_ref, w_ref, b_ref, o_ref):
    o_ref[0] = _dot(_silu(c_ref[...]), w_ref[0], HIGHEST) + b_ref[0]


def _ada(cc, ada_w, ada_b):
    depth, d, n = ada_w.shape
    rows = cc.shape[0]
    tn = n // 4
    return pl.pallas_call(
        _ada_kernel,
        out_shape=jax.ShapeDtypeStruct((depth, rows, n), F32),
        grid=(depth, n // tn),
        in_specs=[
            pl.BlockSpec((rows, d), lambda i, j: (0, 0)),
            pl.BlockSpec((1, d, tn), lambda i, j: (i, 0, j)),
            pl.BlockSpec((1, 1, tn), lambda i, j: (i, 0, j)),
        ],
        out_specs=pl.BlockSpec((1, rows, tn), lambda i, j: (i, 0, j)),
        compiler_params=_params("arbitrary", "arbitrary"),
        name="ada_mod",
    )(cc, ada_w, ada_b.reshape(depth, 1, n))


def _tok_spec(tm, width):
    return pl.BlockSpec((1, tm, width), lambda b, t: (b, t, 0))


def _const_spec(shape):
    zeros = (0,) * len(shape)
    return pl.BlockSpec(shape, lambda b, t: zeros)


def _mod_specs(d, n_batch):
    lat = pl.BlockSpec((None, 6, d), lambda b, t: (b, 0, 0))
    ctx = pl.BlockSpec((None, 6, d), lambda b, t: (n_batch, 0, 0))
    return lat, ctx


def _in_even_kernel(x_ref, mod_ref, modc_ref, g_ref, w_ref, o_ref, *, n_ctx_tiles):
    m = _tile_mod(mod_ref, modc_ref, n_ctx_tiles)
    xn = (_rms(x_ref[0]) * g_ref[...]) * (1.0 + m[1:2]) + m[0:1]
    o_ref[0] = _dot(xn.astype(BF16), w_ref[...])


def _in_even(h, mods, g, w, tm, n_ctx_tiles):
    b, t, d = h.shape
    n = w.shape[1]
    lat, ctx = _mod_specs(d, b)
    return pl.pallas_call(
        functools.partial(_in_even_kernel, n_ctx_tiles=n_ctx_tiles),
        out_shape=jax.ShapeDtypeStruct((b, t, n), F32),
        grid=(b, t // tm),
        in_specs=[_tok_spec(tm, d), lat, ctx, _const_spec((1, d)), _const_spec((d, n))],
        out_specs=_tok_spec(tm, n),
        compiler_params=_params("arbitrary", "arbitrary"),
        name="even_in_proj",
    )(h, mods, mods, g.reshape(1, d), w)


def _post_kernel(h_ref, a_ref, b_ref, mod_ref, modc_ref, g_ref, wo_ref, wgu_ref, wd_ref, o_ref,
                 *, n_ctx_tiles, hidden):
    m = _tile_mod(mod_ref, modc_ref, n_ctx_tiles)
    half = a_ref.shape[-1]
    y = _dot(a_ref[0], wo_ref[:half, :]) + _dot(b_ref[0], wo_ref[half:, :])
    h1 = h_ref[0] + m[2:3] * y
    xn = (_rms(h1) * g_ref[...]) * (1.0 + m[4:5]) + m[3:4]
    gu = _dot(xn.astype(BF16), wgu_ref[...])
    act = _silu(gu[:, :hidden]) * gu[:, hidden:]
    o_ref[0] = h1 + m[5:6] * _dot(act.astype(BF16), wd_ref[...])


def _post(h, a, bb, mods, g, wo, wgu, wd, tm, n_ctx_tiles):
    b, t, d = h.shape
    half = a.shape[-1]
    hidden = wd.shape[0]
    lat, ctx = _mod_specs(d, b)
    return pl.pallas_call(
        functools.partial(_post_kernel, n_ctx_tiles=n_ctx_tiles, hidden=hidden),
        out_shape=jax.ShapeDtypeStruct((b, t, d), F32),
        grid=(b, t // tm),
        in_specs=[_tok_spec(tm, d), _tok_spec(tm, half), _tok_spec(tm, half), lat, ctx,
                  _const_spec((1, d)), _const_spec((2 * half, d)), _const_spec((d, 2 * hidden)),
                  _const_spec((hidden, d))],
        out_specs=_tok_spec(tm, d),
        compiler_params=_params("arbitrary", "arbitrary"),
        name="out_proj_ffn",
    )(h, a, bb, mods, mods, g.reshape(1, d), wo, wgu, wd)


def _hgrn_tables():
    c = HG_CHUNK
    r = np.arange(c)
    j = r[None, :]
    blocks = [(j <= r[:, None]), (j > r[:, None])]
    masks, bms = [], []
    m = c
    while m >= 2:
        half = m // 2
        p = r % m
        mid = (r - p + half)[:, None]
        upper = (p >= half)[:, None]
        up = (j >= mid) & (j <= r[:, None]) & upper
        lo = (j > r[:, None]) & (j <= mid - 1) & (~upper)
        blocks.append(up | lo)
        masks.append(np.broadcast_to(upper, (c, LANES)))
        bms.append((r[:, None] // m) == (r[None, :] // m))
        m //= 2
    fwd = np.stack(blocks).astype(np.float32)
    bwd = fwd[:, ::-1, ::-1]
    mall = np.stack([fwd.reshape(-1, c), bwd.reshape(-1, c)])
    mu = np.stack(masks).astype(np.float32)
    masku = np.stack([mu, mu[:, ::-1]])
    return (jnp.asarray(mall, dtype=BF16), jnp.asarray(masku, dtype=F32),
            jnp.asarray(np.stack(bms), dtype=F32))


def _hgrn_kernel(q_ref, ff_ref, fb_ref, i_ref, g_ref, lbp_ref, ng_ref, mall_ref, masku_ref, bm_ref,
                 o_ref, of_ref, ob_ref, *, layer, n_ctx_chunks, n_chunks):
    c = HG_CHUNK
    dk = HG_HEAD_DIM
    n_levels = masku_ref.shape[1]
    lbp = lbp_ref[...]
    ex = jnp.exp(lbp - jnp.max(lbp, axis=0, keepdims=True))
    p = ex / jnp.sum(ex, axis=0, keepdims=True)
    lb = jnp.sum(p[:layer + 1], axis=0) - p[0]

    def chunk(dirn, ci, st):
        rows = pl.ds(pl.multiple_of(ci * c, c), c)
        q = q_ref[0, rows, :]
        zf = (ff_ref if dirn == 0 else fb_ref)[0, rows, :]
        v = i_ref[0, rows, :]
        lbd = lb[dirn:dirn + 1]
        f = lbd + (1.0 - lbd) * jax.nn.sigmoid(zf)
        lf = jnp.log(f)
        k = 1.0 - f
        hi = lf.astype(BF16)
        lo = (lf - hi.astype(F32)).astype(BF16)
        e2 = _dot(mall_ref[dirn], jnp.concatenate([hi, lo], axis=1))
        ee = jnp.exp(e2[:, :dk] + e2[:, dk:])
        g_in = ee[0:c]
        g_out = ee[c:2 * c]
        last = c - 1 if dirn == 0 else 0
        tot = g_in[last:last + 1]
        vb = v.astype(BF16)
        o = _dot_nt((q * g_in).astype(BF16), st.astype(BF16))
        a = jnp.zeros((c, c), F32)
        for lvl in range(n_levels):
            gl = ee[(2 + lvl) * c:(3 + lvl) * c]
            gu = gl * masku_ref[dirn, lvl]
            a = a + bm_ref[lvl] * _dot_nt((q * gu).astype(BF16), (k * (gl - gu)).astype(BF16))
        o = o + _dot(a.astype(BF16), vb) + jnp.sum(q * k, axis=-1, keepdims=True) * v
        st = st * tot + _dot_tn(vb, (k * g_out).astype(BF16))
        return rows, o, st

    def body(i, carry):
        sf, sb = carry
        cb = jnp.where(i < n_ctx_chunks, n_ctx_chunks - 1 - i, n_chunks - 1 - (i - n_ctx_chunks))
        rows, o, sf = chunk(0, i, sf)
        of_ref[rows, :] = o
        rows, o, sb = chunk(1, cb, sb)
        ob_ref[rows, :] = o
        return sf, sb

    s0 = jnp.zeros((dk, dk), F32)
    lax.fori_loop(0, n_chunks, body, (s0, s0))
    o = _rms(of_ref[...] + ob_ref[...]) * ng_ref[...]
    o_ref[0] = (o * _silu(g_ref[0])).astype(o_ref.dtype)


def _hgrn(z, hg_lower_bound, ng, layer, n_heads, ctx_len):
    b, t, _ = z.shape
    dk = HG_HEAD_DIM
    n_even = hg_lower_bound.shape[0]
    mall, masku, bm = _hgrn_tables()

    def col(kind):
        return pl.BlockSpec((1, t, dk), lambda bi, hi: (bi, 0, kind * n_heads + hi))

    def const(arr):
        zeros = (0,) * arr.ndim
        return pl.BlockSpec(arr.shape, lambda bi, hi: zeros)

    return pl.pallas_call(
        functools.partial(_hgrn_kernel, layer=layer, n_ctx_chunks=ctx_len // HG_CHUNK,
                          n_chunks=t // HG_CHUNK),
        out_shape=jax.ShapeDtypeStruct((b, t, n_heads * dk), BF16),
        grid=(b, n_heads),
        in_specs=[col(0), col(1), col(2), col(3), col(4),
                  pl.BlockSpec((n_even, 2, dk), lambda bi, hi: (0, 0, hi)),
                  const(ng.reshape(1, dk)), const(mall), const(masku), const(bm)],
        out_specs=pl.BlockSpec((1, t, dk), lambda bi, hi: (bi, 0, hi)),
        scratch_shapes=[pltpu.VMEM((t, dk), F32), pltpu.VMEM((t, dk), F32)],
        compiler_params=_params("arbitrary", "arbitrary"),
        name="hgrn2_scan",
    )(z, z, z, z, z, hg_lower_bound, ng.reshape(1, dk), mall, masku, bm)


def _hy_pre_kernel(x0_ref, x1_ref, v_ref, w0_ref, w1_ref, wv_ref, b0_ref, b1_ref, bv_ref,
                   x0c_ref, u_ref, *, ctx_len):
    t = x0_ref.shape[1]
    row = lax.broadcasted_iota(jnp.int32, (t, LANES), 0)
    first = (row == 0) | (row == ctx_len)
    final = (row == ctx_len - 1) | (row == t - 1)

    def short_conv(z_ref, w_ref, b_ref):
        z = z_ref[0]
        prev = jnp.where(first, 0.0, pltpu.roll(z, 1, 0))
        nxt = jnp.where(final, 0.0, pltpu.roll(z, t - 1, 0))
        w = w_ref[...]
        return prev * w[0:1] + z * w[1:2] + nxt * w[2:3] + b_ref[...]

    x0c_ref[0] = short_conv(x0_ref, w0_ref, b0_ref).astype(x0c_ref.dtype)
    u = short_conv(x1_ref, w1_ref, b1_ref) * short_conv(v_ref, wv_ref, bv_ref)
    u_ref[0] = u.astype(u_ref.dtype)


def _hy_pre(z, short_w, short_b, col0, width, ctx_len):
    b, t, _ = z.shape
    nb = width // LANES
    c0 = col0 // LANES

    def zcol(kind):
        return pl.BlockSpec((1, t, LANES), lambda bi, j: (bi, 0, c0 + kind * nb + j))

    def wcol(kind, rows):
        return pl.BlockSpec((rows, LANES), lambda bi, j: (0, kind * nb + j))

    out = jax.ShapeDtypeStruct((b, t, width), BF16)
    ospec = pl.BlockSpec((1, t, LANES), lambda bi, j: (bi, 0, j))
    sb = short_b.reshape(1, -1)
    return pl.pallas_call(
        functools.partial(_hy_pre_kernel, ctx_len=ctx_len),
        out_shape=(out, out),
        grid=(b, nb),
        in_specs=[zcol(0), zcol(1), zcol(2), wcol(0, 3), wcol(1, 3), wcol(2, 3),
                  wcol(0, 1), wcol(1, 1), wcol(2, 1)],
        out_specs=(ospec, ospec),
        compiler_params=_params("arbitrary", "arbitrary"),
        name="hyena_short_conv",
    )(z, z, z, short_w, short_w, short_w, sb, sb, sb)


def _filter_features(length, n):
    p = np.arange(n)
    is_f = p < length
    is_b = p > n - length
    lag = np.where(is_f, p, np.where(is_b, n - 1 - p, 0))
    tt = np.linspace(0.0, 1.0, length, dtype=np.float32)[lag][:, None]
    w = (2.0 * math.pi * lag.astype(np.float32) / length)[:, None].astype(np.float32)
    bands = np.linspace(1e-4, HY_BANDS - 1, HY_BANDS, dtype=np.float32)[None, :]
    feat = np.concatenate([tt, np.cos(bands * w), -np.sin(bands * w)], axis=-1).astype(np.float32)
    pad = (-feat.shape[1]) % SUBLANES
    feat = np.pad(feat, ((0, 0), (0, pad)))
    return (jnp.asarray(feat), jnp.asarray(is_f[:, None].astype(np.float32)),
            jnp.asarray(is_b[:, None].astype(np.float32)))


def _filt_kernel(feat_ref, mf_ref, mb_ref, w1_ref, b1_ref, fr1_ref, w2_ref, b2_ref, fr2_ref,
                 w3f_ref, w3b_ref, dl_ref, o_ref):
    z = feat_ref[...]
    hid = jnp.sin(fr1_ref[...] * (_dot(z, w1_ref[...], HIGHEST) + b1_ref[...]))
    hid = jnp.sin(fr2_ref[...] * (_dot(hid, w2_ref[...], HIGHEST) + b2_ref[...]))
    hf = _dot(hid, w3f_ref[...], HIGHEST)
    hb = _dot(hid, w3b_ref[...], HIGHEST)
    win = jnp.exp(-z[:, 0:1] * dl_ref[...])
    f = (mf_ref[...] * hf + mb_ref[...] * hb) * win
    o_ref[...] = f / jnp.sum(jnp.abs(f), axis=0, keepdims=True)


def _hyena_filter(length, n, w1, b1, fr1, w2, b2, fr2, w3, width):
    feat, mf, mb = _filter_features(length, n)
    nf = feat.shape[1]
    hid = w1.shape[1]
    w1p = jnp.pad(w1, ((0, nf - w1.shape[0]), (0, 0)))
    d_lo = -math.log(HY_TARGET) / HY_GENTLE_PCT
    d_hi = -math.log(HY_TARGET) / HY_STEEP_PCT
    deltas = jnp.asarray(np.linspace(d_lo, d_hi, width, dtype=np.float32)[None, :])
    nb = width // LANES

    def full(shape):
        zeros = (0,) * len(shape)
        return pl.BlockSpec(shape, lambda j: zeros)

    return pl.pallas_call(
        _filt_kernel,
        out_shape=jax.ShapeDtypeStruct((n, width), F32),
        grid=(nb,),
        in_specs=[full((n, nf)), full((n, 1)), full((n, 1)), full((nf, hid)), full((1, hid)),
                  full((1, hid)), full((hid, hid)), full((1, hid)), full((1, hid)),
                  pl.BlockSpec((hid, LANES), lambda j: (0, j)),
                  pl.BlockSpec((hid, LANES), lambda j: (0, nb + j)),
                  pl.BlockSpec((1, LANES), lambda j: (0, j))],
        out_specs=pl.BlockSpec((n, LANES), lambda j: (0, j)),
        compiler_params=_params("arbitrary"),
        name="hyena_filter_mlp",
    )(feat, mf, mb, w1p, b1.reshape(1, -1), fr1.reshape(1, -1), w2, b2.reshape(1, -1),
      fr2.reshape(1, -1), w3, w3, deltas)


def _dft_tables(n1, n1_in):
    n2 = FFT_N2
    n = n1 * n2
    a = np.arange(n1)
    j = np.arange(n2)
    ang = -2.0 * np.pi * (a[None, None, :] * a[None, :, None] / n1 + j[:, None, None] * a[None, :, None] / n)
    tr, ti = np.cos(ang), np.sin(ang)
    fwd_a = np.concatenate([np.concatenate([tr, -ti], 2), np.concatenate([ti, tr], 2)], 1)
    trt, tit = np.swapaxes(tr, 1, 2), -np.swapaxes(ti, 1, 2)
    inv_a = np.concatenate([np.concatenate([trt, -tit], 2), np.concatenate([tit, trt], 2)], 1)
    keep = np.concatenate([np.arange(n1_in), n1 + np.arange(n1_in)])
    ang2 = -2.0 * np.pi * (j[:, None] * j[None, :]) / n2
    cr, ci = np.cos(ang2), np.sin(ang2)
    fwd_c = np.block([[cr, -ci], [ci, cr]])
    inv_c = np.block([[cr, ci], [-ci, cr]])
    real_a = np.concatenate([tr, ti], 1)
    return dict(fwd_a=fwd_a[:, :, keep], inv_a=inv_a[:, keep, :], fwd_c=fwd_c, inv_c=inv_c,
                real_a=real_a)


def _slab(idx):
    return pl.ds(pl.multiple_of(idx * FFT_PITCH, SUBLANES), FFT_N2)


def _fft_filter_kernel(f_ref, wa_ref, wc_ref, o_ref, scr_ref, *, n1):
    n2 = FFT_N2
    rows = 2 * n1
    scale = 1.0 / (n1 * n2)

    def stage_a(j, carry):
        scr_ref[pl.ds(j, rows, stride=FFT_PITCH), :] = _dot(wa_ref[j], f_ref[j], HIGHEST)
        return carry

    lax.fori_loop(0, n2, stage_a, 0)

    def stage_c(k, carry):
        x = jnp.concatenate([scr_ref[_slab(k), :], scr_ref[_slab(n1 + k), :]], axis=0)
        o_ref[k] = _dot(wc_ref[...], x, HIGHEST) * scale
        return carry

    lax.fori_loop(0, n1, stage_c, 0)


def _fft_filter(filt, n1, tables):
    n2 = FFT_N2
    width = filt.shape[1]
    ft = filt.reshape(n1, n2, width).transpose(1, 0, 2)
    wa = jnp.asarray(tables["real_a"], dtype=F32)
    wc = jnp.asarray(tables["fwd_c"], dtype=F32)
    return pl.pallas_call(
        functools.partial(_fft_filter_kernel, n1=n1),
        out_shape=jax.ShapeDtypeStruct((n1, 2 * n2, width), F32),
        grid=(width // LANES,),
        in_specs=[pl.BlockSpec((n2, n1, LANES), lambda c: (0, 0, c)),
                  pl.BlockSpec(wa.shape, lambda c: (0, 0, 0)),
                  pl.BlockSpec(wc.shape, lambda c: (0, 0))],
        out_specs=pl.BlockSpec((n1, 2 * n2, LANES), lambda c: (0, 0, c)),
        scratch_shapes=[pltpu.VMEM((2 * n1 * FFT_PITCH, LANES), F32)],
        compiler_params=_params("arbitrary"),
        name="hyena_filter_dft",
    )(ft, wa, wc)


def _fftconv_kernel(u_ref, x0_ref, fh_ref, skip_ref, wfa_ref, wfc_ref, wic_ref, wia_ref, o_ref,
                    scr_ref, *, n1):
    n2 = FFT_N2
    rows = 2 * n1

    def stage_a(j, carry):
        scr_ref[pl.ds(j, rows, stride=FFT_PITCH), :] = _dot(wfa_ref[j], u_ref[0, j])
        return carry

    lax.fori_loop(0, n2, stage_a, 0)

    def stage_c(k, carry):
        x = jnp.concatenate([scr_ref[_slab(k), :], scr_ref[_slab(n1 + k), :]], axis=0)
        xf = _dot(wfc_ref[...], x.astype(BF16))
        fh = fh_ref[k]
        xr, xi, fr, fi = xf[:n2], xf[n2:], fh[:n2], fh[n2:]
        y = jnp.concatenate([xr * fr - xi * fi, xr * fi + xi * fr], axis=0)
        zt = _dot(wic_ref[...], y.astype(BF16))
        scr_ref[_slab(k), :] = zt[:n2]
        scr_ref[_slab(n1 + k), :] = zt[n2:]
        return carry

    lax.fori_loop(0, n1, stage_c, 0)

    def stage_ai(j, carry):
        x = scr_ref[pl.ds(j, rows, stride=FFT_PITCH), :]
        y = _dot(wia_ref[j], x.astype(BF16))
        u = u_ref[0, j].astype(F32)
        o_ref[0, j] = (x0_ref[0, j].astype(F32) * (y + skip_ref[...] * u)).astype(o_ref.dtype)
        return carry

    lax.fori_loop(0, n2, stage_ai, 0)


def _to_fft_layout(x, n1_in):
    b, length, c = x.shape
    n1_used = length // FFT_N2
    x = x.reshape(b // 2, 2, n1_used, FFT_N2, c)
    x = jnp.pad(x, ((0, 0), (0, 0), (0, n1_in - n1_used), (0, 0), (0, 0)))
    return x.transpose(0, 3, 1, 2, 4).reshape(b // 2, FFT_N2, 2 * n1_in, c)


def _from_fft_layout(y, length):
    p, n2, rows, c = y.shape
    n1_in = rows // 2
    y = y.reshape(p, n2, 2, n1_in, c).transpose(0, 2, 3, 1, 4)
    return y.reshape(2 * p, n1_in * n2, c)[:, :length]


def _fftconv(u, x0c, fh, skip, n1, n1_in, tables):
    _, length, width = u.shape
    ut = _to_fft_layout(u, n1_in)
    xt = _to_fft_layout(x0c, n1_in)
    pairs, n2, rin, _ = ut.shape
    wfa = jnp.asarray(tables["fwd_a"], dtype=BF16)
    wia = jnp.asarray(tables["inv_a"], dtype=BF16)
    wfc = jnp.asarray(tables["fwd_c"], dtype=BF16)
    wic = jnp.asarray(tables["inv_c"], dtype=BF16)
    data = pl.BlockSpec((1, n2, rin, LANES), lambda c, p: (p, 0, 0, c))

    def const(arr):
        zeros = (0,) * arr.ndim
        return pl.BlockSpec(arr.shape, lambda c, p: zeros, pipeline_mode=pl.Buffered(1))

    out = pl.pallas_call(
        functools.partial(_fftconv_kernel, n1=n1),
        out_shape=jax.ShapeDtypeStruct(ut.shape, BF16),
        grid=(width // LANES, pairs),
        in_specs=[data, data,
                  pl.BlockSpec((n1, 2 * n2, LANES), lambda c, p: (0, 0, c)),
                  pl.BlockSpec((1, LANES), lambda c, p: (0, c)),
                  const(wfa), const(wfc), const(wic), const(wia)],
        out_specs=data,
        scratch_shapes=[pltpu.VMEM((2 * n1 * FFT_PITCH, LANES), F32)],
        compiler_params=_params("arbitrary", "arbitrary"),
        name="hyena_dft_conv",
    )(ut, xt, fh, skip.reshape(1, width), wfa, wfc, wic, wia)
    return _from_fft_layout(out, length)


def _hyena(z, col0, width, ctx_len, short_w, short_b, filt_params, skip):
    x0c, u = _hy_pre(z, short_w, short_b, col0, width, ctx_len)
    outs = []
    for lo, hi in ((0, ctx_len), (ctx_len, z.shape[1])):
        length = hi - lo
        n1_in = max(length // FFT_N2, SUBLANES)
        n1 = 2 * n1_in
        tables = _dft_tables(n1, n1_in)
        filt = _hyena_filter(length, n1 * FFT_N2, *filt_params, width)
        fh = _fft_filter(filt, n1, tables)
        outs.append(_fftconv(u[:, lo:hi], x0c[:, lo:hi], fh, skip, n1, n1_in, tables))
    return jnp.concatenate(outs, axis=1)


def _rope_tables(n_lat, ctx_len):
    tok = np.arange(n_lat)
    row, colp = tok // GRID_W, tok % GRID_W

    def axial(half):
        inv = ROPE_BASE ** (-np.arange(half, dtype=np.float32) / half)
        parts_c, parts_s = [], []
        for pos in (row, colp):
            ang = pos.astype(np.float32)[:, None] * inv
            parts_c += [np.cos(ang), np.cos(ang)]
            parts_s += [-np.sin(ang), np.sin(ang)]
        return np.concatenate(parts_c, 1), np.concatenate(parts_s, 1)

    dc, ds = axial(DA_HEAD_DIM // 4)
    mc, ms = axial(MLA_ROPE // 4)
    ones, zeros = np.ones((n_lat, MLA_NOPE), np.float32), np.zeros((n_lat, MLA_NOPE), np.float32)
    padc = np.ones((n_lat, LANES - MLA_NOPE - MLA_ROPE), np.float32)
    tabs = [np.concatenate([dc, dc], 1), np.concatenate([ds, ds], 1),
            np.concatenate([ones, mc, padc], 1), np.concatenate([zeros, ms, 0 * padc], 1)]
    out = []
    for i, tb in enumerate(tabs):
        ctx_rows = np.ones((ctx_len, LANES), np.float32) if i % 2 == 0 else np.zeros((ctx_len, LANES), np.float32)
        out.append(jnp.asarray(np.concatenate([ctx_rows, tb.astype(np.float32)], 0)))
    return out


def _swap_perm(width, group, half):
    idx = np.arange(width)
    pos = idx % group
    return np.where((pos % (2 * half)) < half, idx + half, idx - half)


def _in_odd_kernel(x_ref, mod_ref, modc_ref, g_ref, w_ref, dc_ref, ds_ref, mc_ref, ms_ref,
                   qg_ref, kvg_ref, wuq_ref, wukv_ref,
                   qd_ref, kd_ref, vd_ref, qm_ref, km_ref, vm_ref, *, n_ctx_tiles, da_w, q_rank, kv_rank):
    m = _tile_mod(mod_ref, modc_ref, n_ctx_tiles)
    xn = (_rms(x_ref[0]) * g_ref[...]) * (1.0 + m[1:2]) + m[0:1]
    z = _dot(xn.astype(BF16), w_ref[...])
    nrep = da_w // LANES
    dc = jnp.concatenate([dc_ref[...]] * nrep, axis=1)
    ds = jnp.concatenate([ds_ref[...]] * nrep, axis=1)
    mc = jnp.concatenate([mc_ref[...]] * MLA_HEADS, axis=1)
    ms = jnp.concatenate([ms_ref[...]] * MLA_HEADS, axis=1)
    sa = DA_HEAD_DIM ** -0.5
    sm = (MLA_NOPE + MLA_ROPE) ** -0.5
    o = 0
    qd_ref[0] = ((z[:, o:o + da_w] * dc + z[:, o + da_w:o + 2 * da_w] * ds) * sa).astype(BF16)
    o += 2 * da_w
    kd_ref[0] = (z[:, o:o + da_w] * dc + z[:, o + da_w:o + 2 * da_w] * ds).astype(BF16)
    o += 2 * da_w
    vd_ref[0] = z[:, o:o + da_w].astype(BF16)
    o += da_w
    cq = _rms(z[:, o:o + q_rank]) * qg_ref[...]
    o += q_rank
    ckv = _rms(z[:, o:o + kv_rank]) * kvg_ref[...]
    o += kv_rank
    kr = z[:, o:o + LANES] * mc_ref[...] + z[:, o + LANES:o + 2 * LANES] * ms_ref[...]
    mw = MLA_HEADS * LANES
    qu = _dot(cq.astype(BF16), wuq_ref[...])
    qm_ref[0] = ((qu[:, :mw] * mc + qu[:, mw:] * ms) * sm).astype(BF16)
    kvu = _dot(ckv.astype(BF16), wukv_ref[...])
    km_ref[0] = (kvu[:, :mw] + jnp.concatenate([kr] * MLA_HEADS, axis=1)).astype(BF16)
    vm_ref[0] = kvu[:, mw:].astype(BF16)


def _odd_weights(w_in, w_uq, w_ukv, da_w, q_rank, kv_rank):
    qw, kw, vw = w_in[:, :da_w], w_in[:, da_w:2 * da_w], w_in[:, 2 * da_w:3 * da_w]
    o = 3 * da_w
    cqw, ckvw, krw = w_in[:, o:o + q_rank], w_in[:, o + q_rank:o + q_rank + kv_rank], w_in[:, o + q_rank + kv_rank:]
    perm_da = _swap_perm(da_w, DA_HEAD_DIM // 2, DA_HEAD_DIM // 4)
    perm_r = _swap_perm(MLA_ROPE, MLA_ROPE // 2, MLA_ROPE // 4)
    d = w_in.shape[0]

    def rope_group(wr):
        return jnp.pad(wr, ((0, 0), (MLA_NOPE, LANES - MLA_NOPE - MLA_ROPE)))

    w_big = jnp.concatenate([qw, qw[:, perm_da], kw, kw[:, perm_da], vw, cqw, ckvw,
                             rope_group(krw), rope_group(krw[:, perm_r])], axis=1)
    dq = MLA_NOPE + MLA_ROPE
    pad = LANES - dq
    uq = w_uq.reshape(q_rank, MLA_HEADS, dq)
    uq_a = jnp.pad(uq, ((0, 0), (0, 0), (0, pad))).reshape(q_rank, MLA_HEADS * LANES)
    uq_s = jnp.pad(uq[:, :, MLA_NOPE:][:, :, perm_r], ((0, 0), (0, 0), (MLA_NOPE, pad)))
    uq_s = uq_s.reshape(q_rank, MLA_HEADS * LANES)
    ukv = w_ukv.reshape(kv_rank, MLA_HEADS, MLA_NOPE + MLA_V)
    uk = jnp.pad(ukv[:, :, :MLA_NOPE], ((0, 0), (0, 0), (0, LANES - MLA_NOPE))).reshape(kv_rank, MLA_HEADS * LANES)
    uv = ukv[:, :, MLA_NOPE:].reshape(kv_rank, MLA_HEADS * MLA_V)
    return (w_big.astype(BF16), jnp.concatenate([uq_a, uq_s], axis=1).astype(BF16),
            jnp.concatenate([uk, uv], axis=1).astype(BF16))


def _in_odd(h, mods, g, w_big, wuq, wukv, qg, kvg, tabs, tm, n_ctx_tiles, da_w):
    b, t, d = h.shape
    q_rank, kv_rank = qg.shape[0], kvg.shape[0]
    lat, ctx = _mod_specs(d, b)
    tab = pl.BlockSpec((tm, LANES), lambda bi, ti: (ti, 0))
    outs = [jax.ShapeDtypeStruct((b, t, da_w), BF16)] * 3 + [jax.ShapeDtypeStruct((b, t, MLA_HEADS * LANES), BF16)] * 3
    ospecs = [_tok_spec(tm, da_w)] * 3 + [_tok_spec(tm, MLA_HEADS * LANES)] * 3
    return pl.pallas_call(
        functools.partial(_in_odd_kernel, n_ctx_tiles=n_ctx_tiles, da_w=da_w, q_rank=q_rank, kv_rank=kv_rank),
        out_shape=tuple(outs),
        grid=(b, t // tm),
        in_specs=[_tok_spec(tm, d), lat, ctx, _const_spec((1, d)), _const_spec(w_big.shape),
                  tab, tab, tab, tab, _const_spec((1, q_rank)), _const_spec((1, kv_rank)),
                  _const_spec(wuq.shape), _const_spec(wukv.shape)],
        out_specs=tuple(ospecs),
        compiler_params=_params("arbitrary", "arbitrary"),
        name="odd_in_proj",
    )(h, mods, mods, g.reshape(1, d), w_big, *tabs, qg.reshape(1, -1), kvg.reshape(1, -1), wuq, wukv)


def _softmax_pv(s, v):
    p = jnp.exp(s - jnp.max(s, axis=-1, keepdims=True))
    return _dot(p.astype(BF16), v) / jnp.sum(p, axis=-1, keepdims=True)


def _attn_kernel(q_ref, k_ref, v_ref, lam_ref, sg_ref, o_ref, *, n_ctx_tiles, ctx_len, diff, lam_init):
    q = q_ref[0]

    def attend(nk):
        k = k_ref[0, :nk, :]
        v = v_ref[0, :nk, :]
        if not diff:
            return _softmax_pv(_dot_nt(q, k), v)
        lane = lax.broadcasted_iota(jnp.int32, q.shape, 1)
        first = lane < DA_HEAD_DIM
        zero = jnp.zeros_like(q)
        o1 = _softmax_pv(_dot_nt(jnp.where(first, q, zero), k), v)
        o2 = _softmax_pv(_dot_nt(jnp.where(first, zero, q), k), v)
        lp = lam_ref[...]
        lam = (jnp.exp(jnp.sum(lp[0:1] * lp[1:2], axis=-1, keepdims=True))
               - jnp.exp(jnp.sum(lp[2:3] * lp[3:4], axis=-1, keepdims=True)) + lam_init)
        return _rms(o1 - lam * o2) * sg_ref[...] * (1.0 - lam_init)

    is_ctx = pl.program_id(2) < n_ctx_tiles

    @pl.when(is_ctx)
    def _():
        o_ref[0] = attend(ctx_len).astype(o_ref.dtype)

    @pl.when(jnp.logical_not(is_ctx))
    def _():
        o_ref[0] = attend(k_ref.shape[1]).astype(o_ref.dtype)


def _attention(q, k, v, lam_p, subln_g, tq, ctx_len, diff, lam_init):
    b, t, width = q.shape
    heads = width // LANES
    head = pl.BlockSpec((1, t, LANES), lambda bi, hi, ti: (bi, 0, hi))
    tile = pl.BlockSpec((1, tq, LANES), lambda bi, hi, ti: (bi, ti, hi))

    def const(shape):
        zeros = (0,) * len(shape)
        return pl.BlockSpec(shape, lambda bi, hi, ti: zeros)

    return pl.pallas_call(
        functools.partial(_attn_kernel, n_ctx_tiles=ctx_len // tq, ctx_len=ctx_len, diff=diff, lam_init=lam_init),
        out_shape=jax.ShapeDtypeStruct((b, t, width), BF16),
        grid=(b, heads, t // tq),
        in_specs=[tile, head, head, const(lam_p.shape), const((1, LANES))],
        out_specs=tile,
        compiler_params=_params("arbitrary", "arbitrary", "arbitrary"),
        name="diff_attention" if diff else "mla_attention",
    )(q, k, v, lam_p, subln_g.reshape(1, LANES))


def _final_kernel(x_ref, g_ref, o_ref):
    o_ref[0] = _rms(x_ref[0]) * g_ref[...]


def _final_norm(h, g, tm, n_ctx_tiles):
    b, t, d = h.shape
    return pl.pallas_call(
        _final_kernel,
        out_shape=jax.ShapeDtypeStruct((b, t - n_ctx_tiles * tm, d), F32),
        grid=(b, t // tm - n_ctx_tiles),
        in_specs=[pl.BlockSpec((1, tm, d), lambda bi, ti: (bi, ti + n_ctx_tiles, 0)), _const_spec((1, d))],
        out_specs=_tok_spec(tm, d),
        compiler_params=_params("arbitrary", "arbitrary"),
        name="final_norm",
    )(h, g.reshape(1, d))


def kernel(x, c, ctx, c_ctx, ada_w, ada_b, norm_mix_g, norm_ffn_g, ffn_w_gu, ffn_w_down, ev_w_in, ev_w_out, hg_lower_bound, hg_out_norm_g, hy_short_w, hy_short_b, hy_filt_w1, hy_filt_b1, hy_filt_freq1, hy_filt_w2, hy_filt_b2, hy_filt_freq2, hy_filt_w3, hy_skip, od_w_in, od_w_out, da_lambda, da_subln_g, mla_q_norm_g, mla_w_uq, mla_kv_norm_g, mla_w_ukv, final_norm_g):
    n_batch, n_lat, d = x.shape
    ctx_len = ctx.shape[1]
    depth = ada_w.shape[0]
    assert n_batch % 2 == 0 and n_batch < MOD_ROWS
    assert n_lat % GRID_W == 0 and ctx_len % HG_CHUNK == 0 and n_lat % FFT_N2 == 0 and ctx_len % FFT_N2 == 0
    tm = math.gcd(math.gcd(ctx_len, n_lat), 256)
    n_ctx_tiles = ctx_len // tm
    hg_width = d // 2
    hy_width = d - hg_width
    da_w = d // 2
    q_rank, kv_rank = mla_q_norm_g.shape[1], mla_kv_norm_g.shape[1]

    cc = jnp.concatenate([c, c_ctx[None], jnp.zeros((MOD_ROWS - n_batch - 1, d), F32)], axis=0)
    mods = _ada(cc, ada_w, ada_b).reshape(depth, MOD_ROWS, 6, d)
    h = jnp.concatenate([ctx, x], axis=1)
    rope_tabs = _rope_tables(n_lat, ctx_len)

    for i in range(depth):
        if i % 2 == 0:
            e = i // 2
            z = _in_even(h, mods[i], norm_mix_g[i], ev_w_in[e].astype(BF16), tm, n_ctx_tiles)
            a = _hgrn(z, hg_lower_bound, hg_out_norm_g[e], e, hg_width // HG_HEAD_DIM, ctx_len)
            filt_params = (hy_filt_w1[e], hy_filt_b1[e], hy_filt_freq1[e], hy_filt_w2[e], hy_filt_b2[e],
                           hy_filt_freq2[e], hy_filt_w3[e])
            bb = _hyena(z, 5 * hg_width, hy_width, ctx_len, hy_short_w[e], hy_short_b[e], filt_params, hy_skip[e])
            w_out = ev_w_out[e]
        else:
            o = i // 2
            lam_init = 0.8 - 0.6 * math.exp(-0.3 * i)
            w_big, wuq, wukv = _odd_weights(od_w_in[o], mla_w_uq[o], mla_w_ukv[o], da_w, q_rank, kv_rank)
            qd, kd, vd, qm, km, vm = _in_odd(h, mods[i], norm_mix_g[i], w_big, wuq, wukv, mla_q_norm_g[o],
                                             mla_kv_norm_g[o], rope_tabs, tm, n_ctx_tiles, da_w)
            a = _attention(qd, kd, vd, da_lambda[o], da_subln_g[o], tm, ctx_len, True, lam_init)
            bb = _attention(qm, km, vm, da_lambda[o], da_subln_g[o], tm, ctx_len, False, lam_init)
            w_out = od_w_out[o]
        h = _post(h, a, bb, mods[i], norm_ffn_g[i], w_out.astype(BF16), ffn_w_gu[i].astype(BF16),
                  ffn_w_down[i].astype(BF16), tm, n_ctx_tiles)
    return _final_norm(h, final_norm_g, tm, n_ctx_tiles)
```

```python
import functools
import math

import numpy as np
import jax
import jax.numpy as jnp
from jax import lax
from jax.experimental import pallas as pl
from jax.experimental.pallas import tpu as pltpu

F32 = jnp.float32
BF16 = jnp.bfloat16
HIGHEST = lax.Precision.HIGHEST

GRID_W = 64
EPS = 1e-6
ROPE_BASE = 10000.0
HG_HEAD_DIM = 128
HG_CHUNK = 64
HY_BANDS = 16
HY_TARGET = 1e-2
HY_STEEP_PCT = 0.3
HY_GENTLE_PCT = 1.5
DA_HEAD_DIM = 64
MLA_HEADS = 4
MLA_NOPE = 64
MLA_ROPE = 32
MLA_V = 128
LOG2_E = 1.4426950408889634

LANES = 128
SUBLANES = 8
V7X_VMEM_LIMIT_BYTES = 56 * 1024 * 1024
MOD_ROWS = 16
FFT_N2 = 128
FFT_PITCH = FFT_N2 + SUBLANES
FFT_UNROLL = 4


def _params(*sem):
    return pltpu.CompilerParams(dimension_semantics=sem, vmem_limit_bytes=V7X_VMEM_LIMIT_BYTES)


def _dot(a, b, precision=None):
    return jnp.dot(a, b, preferred_element_type=F32, precision=precision)


def _dot_nt(a, b):
    return lax.dot_general(a, b, (((1,), (1,)), ((), ())), preferred_element_type=F32)


def _dot_tn(a, b):
    return lax.dot_general(a, b, (((0,), (0,)), ((), ())), preferred_element_type=F32)


def _rms(x):
    return x * lax.rsqrt(jnp.mean(x * x, axis=-1, keepdims=True) + EPS)


def _silu(x):
    return x * jax.nn.sigmoid(x)


def _tile_mod(mod_ref, modc_ref, n_ctx_tiles):
    return jnp.where(pl.program_id(1) < n_ctx_tiles, modc_ref[...], mod_ref[...])


def _ada_kernel(c_ref, w_ref, b_ref, o_ref):
    o_ref[0] = _dot(_silu(c_ref[...]), w_ref[0], HIGHEST) + b_ref[0]


def _ada(cc, ada_w, ada_b):
    depth, d, n = ada_w.shape
    rows = cc.shape[0]
    tn = n // 4
    return pl.pallas_call(
        _ada_kernel,
        out_shape=jax.ShapeDtypeStruct((depth, rows, n), F32),
        grid=(depth, n // tn),
        in_specs=[
            pl.BlockSpec((rows, d), lambda i, j: (0, 0)),
            pl.BlockSpec((1, d, tn), lambda i, j: (i, 0, j)),
            pl.BlockSpec((1, 1, tn), lambda i, j: (i, 0, j)),
        ],
        out_specs=pl.BlockSpec((1, rows, tn), lambda i, j: (i, 0, j)),
        compiler_params=_params("arbitrary", "arbitrary"),
        name="ada_mod",
    )(cc, ada_w, ada_b.reshape(depth, 1, n))


def _tok_spec(tm, width):
    return pl.BlockSpec((1, tm, width), lambda b, t: (b, t, 0))


def _const_spec(shape):
    zeros = (0,) * len(shape)
    return pl.BlockSpec(shape, lambda b, t: zeros)


def _mod_specs(d, n_batch):
    lat = pl.BlockSpec((None, 6, d), lambda b, t: (b, 0, 0))
    ctx = pl.BlockSpec((None, 6, d), lambda b, t: (n_batch, 0, 0))
    return lat, ctx


def _in_even_kernel(x_ref, mod_ref, modc_ref, g_ref, w_ref, o_ref, *, n_ctx_tiles):
    m = _tile_mod(mod_ref, modc_ref, n_ctx_tiles)
    xn = (_rms(x_ref[0]) * g_ref[...]) * (1.0 + m[1:2]) + m[0:1]
    o_ref[0] = _dot(xn.astype(BF16), w_ref[...])


def _in_even(h, mods, g, w, tm, n_ctx_tiles):
    b, t, d = h.shape
    n = w.shape[1]
    lat, ctx = _mod_specs(d, b)
    return pl.pallas_call(
        functools.partial(_in_even_kernel, n_ctx_tiles=n_ctx_tiles),
        out_shape=jax.ShapeDtypeStruct((b, t, n), F32),
        grid=(b, t // tm),
        in_specs=[_tok_spec(tm, d), lat, ctx, _const_spec((1, d)), _const_spec((d, n))],
        out_specs=_tok_spec(tm, n),
        compiler_params=_params("arbitrary", "arbitrary"),
        name="even_in_proj",
    )(h, mods, mods, g.reshape(1, d), w)


def _post_kernel(h_ref, a_ref, b_ref, mod_ref, modc_ref, g_ref, wo_ref, wgu_ref, wd_ref, o_ref,
                 *, n_ctx_tiles, hidden):
    m = _tile_mod(mod_ref, modc_ref, n_ctx_tiles)
    half = a_ref.shape[-1]
    y = _dot(a_ref[0], wo_ref[:half, :]) + _dot(b_ref[0], wo_ref[half:, :])
    h1 = h_ref[0] + m[2:3] * y
    xn = (_rms(h1) * g_ref[...]) * (1.0 + m[4:5]) + m[3:4]
    gu = _dot(xn.astype(BF16), wgu_ref[...])
    act = _silu(gu[:, :hidden]) * gu[:, hidden:]
    o_ref[0] = h1 + m[5:6] * _dot(act.astype(BF16), wd_ref[...])


def _post(h, a, bb, mods, g, wo, wgu, wd, tm, n_ctx_tiles):
    b, t, d = h.shape
    half = a.shape[-1]
    hidden = wd.shape[0]
    lat, ctx = _mod_specs(d, b)
    return pl.pallas_call(
        functools.partial(_post_kernel, n_ctx_tiles=n_ctx_tiles, hidden=hidden),
        out_shape=jax.ShapeDtypeStruct((b, t, d), F32),
        grid=(b, t // tm),
        in_specs=[_tok_spec(tm, d), _tok_spec(tm, half), _tok_spec(tm, half), lat, ctx,
                  _const_spec((1, d)), _const_spec((2 * half, d)), _const_spec((d, 2 * hidden)),
                  _const_spec((hidden, d))],
        out_specs=_tok_spec(tm, d),
        compiler_params=_params("arbitrary", "arbitrary"),
        name="out_proj_ffn",
    )(h, a, bb, mods, mods, g.reshape(1, d), wo, wgu, wd)


def _hgrn_tables(tile):
    c = HG_CHUNK
    r = np.arange(c)
    j = r[None, :]
    rt = np.arange(tile)
    blocks = [(j <= r[:, None]), (j > r[:, None])]
    masks, bms = [], []
    m = c
    while m >= 2:
        half = m // 2
        p = r % m
        mid = (r - p + half)[:, None]
        upper = (p >= half)[:, None]
        up = (j >= mid) & (j <= r[:, None]) & upper
        lo = (j > r[:, None]) & (j <= mid - 1) & (~upper)
        blocks.append(up | lo)
        masks.append(np.broadcast_to(upper, (c, LANES)))
        bms.append((rt[:, None] // m) == (rt[None, :] // m))
        m //= 2
    fwd = np.stack(blocks).astype(np.float32)
    bwd = fwd[:, ::-1, ::-1]
    mall = np.stack([fwd.reshape(-1, c), bwd.reshape(-1, c)])
    mu = np.stack(masks).astype(np.float32)
    masku = np.tile(np.stack([mu, mu[:, ::-1]]), (1, 1, tile // c, 1))
    return (jnp.asarray(mall, dtype=BF16), jnp.asarray(masku, dtype=F32),
            jnp.asarray(np.stack(bms), dtype=F32))


def _hgrn_kernel(q_ref, ff_ref, fb_ref, i_ref, g_ref, lbp_ref, ng_ref, mall_ref, masku_ref, bm_ref,
                 o_ref, of_ref, ob_ref, *, layer, n_ctx_tiles, n_tiles):
    c = HG_CHUNK
    dk = HG_HEAD_DIM
    n_levels = masku_ref.shape[1]
    tile = masku_ref.shape[2]
    n_sub = tile // c
    lbp = lbp_ref[...]
    ex = jnp.exp(lbp - jnp.max(lbp, axis=0, keepdims=True))
    p = ex / jnp.sum(ex, axis=0, keepdims=True)
    lb = jnp.sum(p[:layer + 1], axis=0) - p[0]

    def sub(x, g):
        return x[g * c:(g + 1) * c]

    def run_tile(dirn, ti, st):
        rows = pl.ds(pl.multiple_of(ti * tile, tile), tile)
        q = q_ref[0, rows, :]
        zf = (ff_ref if dirn == 0 else fb_ref)[0, rows, :]
        v = i_ref[0, rows, :]
        lbd = lb[dirn:dirn + 1]
        f = lbd + (1.0 - lbd) * jax.nn.sigmoid(zf)
        lf = jnp.log(f)
        k = 1.0 - f
        hi = lf.astype(BF16)
        lo = (lf - hi.astype(F32)).astype(BF16)
        x = jnp.concatenate([sub(part, g) for g in range(n_sub) for part in (hi, lo)], axis=1)
        e2 = _dot(mall_ref[dirn], x)
        ee = [jnp.exp(e2[:, 2 * g * dk:(2 * g + 1) * dk] + e2[:, (2 * g + 1) * dk:(2 * g + 2) * dk])
              for g in range(n_sub)]

        def block(idx):
            return jnp.concatenate([ee[g][idx * c:(idx + 1) * c] for g in range(n_sub)], axis=0)

        g_in = block(0)
        qin = (q * g_in).astype(BF16)
        kout = (k * block(1)).astype(BF16)
        vb = v.astype(BF16)
        a = jnp.zeros((tile, tile), F32)
        for lvl in range(n_levels):
            gl = block(2 + lvl)
            gu = gl * masku_ref[dirn, lvl]
            a = a + bm_ref[lvl] * _dot_nt((q * gu).astype(BF16), (k * (gl - gu)).astype(BF16))
        o = _dot(a.astype(BF16), vb) + jnp.sum(q * k, axis=-1, keepdims=True) * v
        outs = [None] * n_sub
        for g in (range(n_sub) if dirn == 0 else reversed(range(n_sub))):
            outs[g] = sub(o, g) + _dot_nt(sub(qin, g), st.astype(BF16))
            last = (g + 1) * c - 1 if dirn == 0 else g * c
            st = st * g_in[last:last + 1] + _dot_tn(sub(vb, g), sub(kout, g))
        return rows, jnp.concatenate(outs, axis=0), st

    def body(i, carry):
        sf, sb = carry
        tb = jnp.where(i < n_ctx_tiles, n_ctx_tiles - 1 - i, n_tiles - 1 - (i - n_ctx_tiles))
        rows, o, sf = run_tile(0, i, sf)
        of_ref[rows, :] = o
        rows, o, sb = run_tile(1, tb, sb)
        ob_ref[rows, :] = o
        return sf, sb

    s0 = jnp.zeros((dk, dk), F32)
    lax.fori_loop(0, n_tiles, body, (s0, s0))
    o = _rms(of_ref[...] + ob_ref[...]) * ng_ref[...]
    o_ref[0] = (o * _silu(g_ref[0])).astype(o_ref.dtype)


def _hgrn(z, hg_lower_bound, ng, layer, n_heads, ctx_len):
    b, t, _ = z.shape
    dk = HG_HEAD_DIM
    n_even = hg_lower_bound.shape[0]
    tile = math.gcd(math.gcd(ctx_len, t - ctx_len), 256)
    mall, masku, bm = _hgrn_tables(tile)

    def col(kind):
        return pl.BlockSpec((1, t, dk), lambda bi, hi: (bi, 0, kind * n_heads + hi))

    def const(arr):
        zeros = (0,) * arr.ndim
        return pl.BlockSpec(arr.shape, lambda bi, hi: zeros)

    return pl.pallas_call(
        functools.partial(_hgrn_kernel, layer=layer, n_ctx_tiles=ctx_len // tile, n_tiles=t // tile),
        out_shape=jax.ShapeDtypeStruct((b, t, n_heads * dk), BF16),
        grid=(b, n_heads),
        in_specs=[col(0), col(1), col(2), col(3), col(4),
                  pl.BlockSpec((n_even, 2, dk), lambda bi, hi: (0, 0, hi)),
                  const(ng.reshape(1, dk)), const(mall), const(masku), const(bm)],
        out_specs=pl.BlockSpec((1, t, dk), lambda bi, hi: (bi, 0, hi)),
        scratch_shapes=[pltpu.VMEM((t, dk), F32), pltpu.VMEM((t, dk), F32)],
        compiler_params=_params("arbitrary", "arbitrary"),
        name="hgrn2_scan",
    )(z, z, z, z, z, hg_lower_bound, ng.reshape(1, dk), mall, masku, bm)


def _hy_pre_kernel(x0_ref, x1_ref, v_ref, w0_ref, w1_ref, wv_ref, b0_ref, b1_ref, bv_ref,
                   x0c_ref, u_ref, *, ctx_len):
    t = x0_ref.shape[1]
    row = lax.broadcasted_iota(jnp.int32, (t, LANES), 0)
    first = (row == 0) | (row == ctx_len)
    final = (row == ctx_len - 1) | (row == t - 1)

    def short_conv(z_ref, w_ref, b_ref):
        z = z_ref[0]
        prev = jnp.where(first, 0.0, pltpu.roll(z, 1, 0))
        nxt = jnp.where(final, 0.0, pltpu.roll(z, t - 1, 0))
        w = w_ref[...]
        return prev * w[0:1] + z * w[1:2] + nxt * w[2:3] + b_ref[...]

    x0c_ref[0] = short_conv(x0_ref, w0_ref, b0_ref).astype(x0c_ref.dtype)
    u = short_conv(x1_ref, w1_ref, b1_ref) * short_conv(v_ref, wv_ref, bv_ref)
    u_ref[0] = u.astype(u_ref.dtype)


def _hy_pre(z, short_w, short_b, col0, width, ctx_len):
    b, t, _ = z.shape
    nb = width // LANES
    c0 = col0 // LANES

    def zcol(kind):
        return pl.BlockSpec((1, t, LANES), lambda bi, j: (bi, 0, c0 + kind * nb + j))

    def wcol(kind, rows):
        return pl.BlockSpec((rows, LANES), lambda bi, j: (0, kind * nb + j))

    out = jax.ShapeDtypeStruct((b, t, width), BF16)
    ospec = pl.BlockSpec((1, t, LANES), lambda bi, j: (bi, 0, j))
    sb = short_b.reshape(1, -1)
    return pl.pallas_call(
        functools.partial(_hy_pre_kernel, ctx_len=ctx_len),
        out_shape=(out, out),
        grid=(b, nb),
        in_specs=[zcol(0), zcol(1), zcol(2), wcol(0, 3), wcol(1, 3), wcol(2, 3),
                  wcol(0, 1), wcol(1, 1), wcol(2, 1)],
        out_specs=(ospec, ospec),
        compiler_params=_params("arbitrary", "arbitrary"),
        name="hyena_short_conv",
    )(z, z, z, short_w, short_w, short_w, sb, sb, sb)


def _filter_features(length, n):
    p = np.arange(n)
    is_f = p < length
    is_b = p > n - length
    lag = np.where(is_f, p, np.where(is_b, n - 1 - p, 0))
    tt = np.linspace(0.0, 1.0, length, dtype=np.float32)[lag][:, None]
    w = (2.0 * math.pi * lag.astype(np.float32) / length)[:, None].astype(np.float32)
    bands = np.linspace(1e-4, HY_BANDS - 1, HY_BANDS, dtype=np.float32)[None, :]
    feat = np.concatenate([tt, np.cos(bands * w), -np.sin(bands * w)], axis=-1).astype(np.float32)
    pad = (-feat.shape[1]) % SUBLANES
    feat = np.pad(feat, ((0, 0), (0, pad)))
    return (jnp.asarray(feat), jnp.asarray(is_f[:, None].astype(np.float32)),
            jnp.asarray(is_b[:, None].astype(np.float32)))


def _filt_kernel(feat_ref, mf_ref, mb_ref, w1_ref, b1_ref, fr1_ref, w2_ref, b2_ref, fr2_ref,
                 w3f_ref, w3b_ref, dl_ref, o_ref):
    z = feat_ref[...]
    hid = jnp.sin(fr1_ref[...] * (_dot(z, w1_ref[...], HIGHEST) + b1_ref[...]))
    hid = jnp.sin(fr2_ref[...] * (_dot(hid, w2_ref[...], HIGHEST) + b2_ref[...]))
    hf = _dot(hid, w3f_ref[...], HIGHEST)
    hb = _dot(hid, w3b_ref[...], HIGHEST)
    win = jnp.exp(-z[:, 0:1] * dl_ref[...])
    f = (mf_ref[...] * hf + mb_ref[...] * hb) * win
    o_ref[...] = f / jnp.sum(jnp.abs(f), axis=0, keepdims=True)


def _hyena_filter(length, n, w1, b1, fr1, w2, b2, fr2, w3, width):
    feat, mf, mb = _filter_features(length, n)
    nf = feat.shape[1]
    hid = w1.shape[1]
    w1p = jnp.pad(w1, ((0, nf - w1.shape[0]), (0, 0)))
    d_lo = -math.log(HY_TARGET) / HY_GENTLE_PCT
    d_hi = -math.log(HY_TARGET) / HY_STEEP_PCT
    deltas = jnp.asarray(np.linspace(d_lo, d_hi, width, dtype=np.float32)[None, :])
    nb = width // LANES

    def full(shape):
        zeros = (0,) * len(shape)
        return pl.BlockSpec(shape, lambda j: zeros)

    return pl.pallas_call(
        _filt_kernel,
        out_shape=jax.ShapeDtypeStruct((n, width), F32),
        grid=(nb,),
        in_specs=[full((n, nf)), full((n, 1)), full((n, 1)), full((nf, hid)), full((1, hid)),
                  full((1, hid)), full((hid, hid)), full((1, hid)), full((1, hid)),
                  pl.BlockSpec((hid, LANES), lambda j: (0, j)),
                  pl.BlockSpec((hid, LANES), lambda j: (0, nb + j)),
                  pl.BlockSpec((1, LANES), lambda j: (0, j))],
        out_specs=pl.BlockSpec((n, LANES), lambda j: (0, j)),
        compiler_params=_params("arbitrary"),
        name="hyena_filter_mlp",
    )(feat, mf, mb, w1p, b1.reshape(1, -1), fr1.reshape(1, -1), w2, b2.reshape(1, -1),
      fr2.reshape(1, -1), w3, w3, deltas)


def _dft_tables(n1, n1_in):
    n2 = FFT_N2
    n = n1 * n2
    a = np.arange(n1)
    j = np.arange(n2)
    ang = -2.0 * np.pi * (a[None, None, :] * a[None, :, None] / n1 + j[:, None, None] * a[None, :, None] / n)
    tr, ti = np.cos(ang), np.sin(ang)
    fwd_a = np.concatenate([np.concatenate([tr, -ti], 2), np.concatenate([ti, tr], 2)], 1)
    trt, tit = np.swapaxes(tr, 1, 2), -np.swapaxes(ti, 1, 2)
    inv_a = np.concatenate([np.concatenate([trt, -tit], 2), np.concatenate([tit, trt], 2)], 1)
    keep = np.concatenate([np.arange(n1_in), n1 + np.arange(n1_in)])
    ang2 = -2.0 * np.pi * (j[:, None] * j[None, :]) / n2
    cr, ci = np.cos(ang2), np.sin(ang2)
    fwd_c = np.block([[cr, -ci], [ci, cr]])
    inv_c = np.block([[cr, ci], [-ci, cr]])
    real_a = np.concatenate([tr, ti], 1)
    return dict(fwd_a=fwd_a[:, :, keep], inv_a=inv_a[:, keep, :], fwd_c=fwd_c, inv_c=inv_c,
                real_a=real_a)


def _slab(idx):
    return pl.ds(pl.multiple_of(idx * FFT_PITCH, SUBLANES), FFT_N2)


def _fft_filter_kernel(f_ref, wa_ref, wc_ref, o_ref, scr_ref, *, n1):
    n2 = FFT_N2
    rows = 2 * n1
    scale = 1.0 / (n1 * n2)

    def stage_a(j, carry):
        scr_ref[pl.ds(j, rows, stride=FFT_PITCH), :] = _dot(wa_ref[j], f_ref[j], HIGHEST)
        return carry

    lax.fori_loop(0, n2, stage_a, 0, unroll=FFT_UNROLL)

    def stage_c(k, carry):
        x = jnp.concatenate([scr_ref[_slab(k), :], scr_ref[_slab(n1 + k), :]], axis=0)
        o_ref[k] = _dot(wc_ref[...], x, HIGHEST) * scale
        return carry

    lax.fori_loop(0, n1, stage_c, 0, unroll=FFT_UNROLL)


def _fft_filter(filt, n1, tables):
    n2 = FFT_N2
    width = filt.shape[1]
    ft = filt.reshape(n1, n2, width).transpose(1, 0, 2)
    wa = jnp.asarray(tables["real_a"], dtype=F32)
    wc = jnp.asarray(tables["fwd_c"], dtype=F32)
    return pl.pallas_call(
        functools.partial(_fft_filter_kernel, n1=n1),
        out_shape=jax.ShapeDtypeStruct((n1, 2 * n2, width), F32),
        grid=(width // LANES,),
        in_specs=[pl.BlockSpec((n2, n1, LANES), lambda c: (0, 0, c)),
                  pl.BlockSpec(wa.shape, lambda c: (0, 0, 0)),
                  pl.BlockSpec(wc.shape, lambda c: (0, 0))],
        out_specs=pl.BlockSpec((n1, 2 * n2, LANES), lambda c: (0, 0, c)),
        scratch_shapes=[pltpu.VMEM((2 * n1 * FFT_PITCH, LANES), F32)],
        compiler_params=_params("arbitrary"),
        name="hyena_filter_dft",
    )(ft, wa, wc)


def _fftconv_kernel(u_ref, x0_ref, fh_ref, skip_ref, wfa_ref, wfc_ref, wic_ref, wia_ref, o_ref,
                    scr_ref, *, n1):
    n2 = FFT_N2
    rows = 2 * n1

    def stage_a(j, carry):
        scr_ref[pl.ds(j, rows, stride=FFT_PITCH), :] = _dot(wfa_ref[j], u_ref[0, j])
        return carry

    lax.fori_loop(0, n2, stage_a, 0, unroll=FFT_UNROLL)

    def stage_c(k, carry):
        x = jnp.concatenate([scr_ref[_slab(k), :], scr_ref[_slab(n1 + k), :]], axis=0)
        xf = _dot(wfc_ref[...], x.astype(BF16))
        fh = fh_ref[k]
        xr, xi, fr, fi = xf[:n2], xf[n2:], fh[:n2], fh[n2:]
        y = jnp.concatenate([xr * fr - xi * fi, xr * fi + xi * fr], axis=0)
        zt = _dot(wic_ref[...], y.astype(BF16))
        scr_ref[_slab(k), :] = zt[:n2]
        scr_ref[_slab(n1 + k), :] = zt[n2:]
        return carry

    lax.fori_loop(0, n1, stage_c, 0, unroll=FFT_UNROLL)

    def stage_ai(j, carry):
        x = scr_ref[pl.ds(j, rows, stride=FFT_PITCH), :]
        y = _dot(wia_ref[j], x.astype(BF16))
        u = u_ref[0, j].astype(F32)
        o_ref[0, j] = (x0_ref[0, j].astype(F32) * (y + skip_ref[...] * u)).astype(o_ref.dtype)
        return carry

    lax.fori_loop(0, n2, stage_ai, 0, unroll=FFT_UNROLL)


def _to_fft_layout(x, n1_in):
    b, length, c = x.shape
    n1_used = length // FFT_N2
    x = x.reshape(b // 2, 2, n1_used, FFT_N2, c)
    x = jnp.pad(x, ((0, 0), (0, 0), (0, n1_in - n1_used), (0, 0), (0, 0)))
    return x.transpose(0, 3, 1, 2, 4).reshape(b // 2, FFT_N2, 2 * n1_in, c)


def _from_fft_layout(y, length):
    p, n2, rows, c = y.shape
    n1_in = rows // 2
    y = y.reshape(p, n2, 2, n1_in, c).transpose(0, 2, 3, 1, 4)
    return y.reshape(2 * p, n1_in * n2, c)[:, :length]


def _fftconv(u, x0c, fh, skip, n1, n1_in, tables):
    _, length, width = u.shape
    ut = _to_fft_layout(u, n1_in)
    xt = _to_fft_layout(x0c, n1_in)
    pairs, n2, rin, _ = ut.shape
    wfa = jnp.asarray(tables["fwd_a"], dtype=BF16)
    wia = jnp.asarray(tables["inv_a"], dtype=BF16)
    wfc = jnp.asarray(tables["fwd_c"], dtype=BF16)
    wic = jnp.asarray(tables["inv_c"], dtype=BF16)
    data = pl.BlockSpec((1, n2, rin, LANES), lambda c, p: (p, 0, 0, c))

    def const(arr):
        zeros = (0,) * arr.ndim
        return pl.BlockSpec(arr.shape, lambda c, p: zeros, pipeline_mode=pl.Buffered(1))

    out = pl.pallas_call(
        functools.partial(_fftconv_kernel, n1=n1),
        out_shape=jax.ShapeDtypeStruct(ut.shape, BF16),
        grid=(width // LANES, pairs),
        in_specs=[data, data,
                  pl.BlockSpec((n1, 2 * n2, LANES), lambda c, p: (0, 0, c)),
                  pl.BlockSpec((1, LANES), lambda c, p: (0, c)),
                  const(wfa), const(wfc), const(wic), const(wia)],
        out_specs=data,
        scratch_shapes=[pltpu.VMEM((2 * n1 * FFT_PITCH, LANES), F32)],
        compiler_params=_params("arbitrary", "arbitrary"),
        name="hyena_dft_conv",
    )(ut, xt, fh, skip.reshape(1, width), wfa, wfc, wic, wia)
    return _from_fft_layout(out, length)


def _hyena(z, col0, width, ctx_len, short_w, short_b, filt_params, skip):
    x0c, u = _hy_pre(z, short_w, short_b, col0, width, ctx_len)
    outs = []
    for lo, hi in ((0, ctx_len), (ctx_len, z.shape[1])):
        length = hi - lo
        n1_in = max(length // FFT_N2, SUBLANES)
        n1 = 2 * n1_in
        tables = _dft_tables(n1, n1_in)
        filt = _hyena_filter(length, n1 * FFT_N2, *filt_params, width)
        fh = _fft_filter(filt, n1, tables)
        outs.append(_fftconv(u[:, lo:hi], x0c[:, lo:hi], fh, skip, n1, n1_in, tables))
    return jnp.concatenate(outs, axis=1)


def _rope_tables(n_lat, ctx_len):
    tok = np.arange(n_lat)
    row, colp = tok // GRID_W, tok % GRID_W

    def axial(half):
        inv = ROPE_BASE ** (-np.arange(half, dtype=np.float32) / half)
        parts_c, parts_s = [], []
        for pos in (row, colp):
            ang = pos.astype(np.float32)[:, None] * inv
            parts_c += [np.cos(ang), np.cos(ang)]
            parts_s += [-np.sin(ang), np.sin(ang)]
        return np.concatenate(parts_c, 1), np.concatenate(parts_s, 1)

    dc, ds = axial(DA_HEAD_DIM // 4)
    mc, ms = axial(MLA_ROPE // 4)
    ones, zeros = np.ones((n_lat, MLA_NOPE), np.float32), np.zeros((n_lat, MLA_NOPE), np.float32)
    padc = np.ones((n_lat, LANES - MLA_NOPE - MLA_ROPE), np.float32)
    tabs = [np.concatenate([dc, dc], 1), np.concatenate([ds, ds], 1),
            np.concatenate([ones, mc, padc], 1), np.concatenate([zeros, ms, 0 * padc], 1)]
    out = []
    for i, tb in enumerate(tabs):
        ctx_rows = np.ones((ctx_len, LANES), np.float32) if i % 2 == 0 else np.zeros((ctx_len, LANES), np.float32)
        out.append(jnp.asarray(np.concatenate([ctx_rows, tb.astype(np.float32)], 0)))
    return out


def _swap_perm(width, group, half):
    idx = np.arange(width)
    pos = idx % group
    return np.where((pos % (2 * half)) < half, idx + half, idx - half)


def _in_odd_kernel(x_ref, mod_ref, modc_ref, g_ref, w_ref, dc_ref, ds_ref, mc_ref, ms_ref,
                   qg_ref, kvg_ref, wuq_ref, wukv_ref,
                   qd_ref, kd_ref, vd_ref, qm_ref, km_ref, vm_ref, *, n_ctx_tiles, da_w, q_rank, kv_rank):
    m = _tile_mod(mod_ref, modc_ref, n_ctx_tiles)
    xn = (_rms(x_ref[0]) * g_ref[...]) * (1.0 + m[1:2]) + m[0:1]
    z = _dot(xn.astype(BF16), w_ref[...])
    nrep = da_w // LANES
    dc = jnp.concatenate([dc_ref[...]] * nrep, axis=1)
    ds = jnp.concatenate([ds_ref[...]] * nrep, axis=1)
    mc = jnp.concatenate([mc_ref[...]] * MLA_HEADS, axis=1)
    ms = jnp.concatenate([ms_ref[...]] * MLA_HEADS, axis=1)
    sa = DA_HEAD_DIM ** -0.5 * LOG2_E
    sm = (MLA_NOPE + MLA_ROPE) ** -0.5 * LOG2_E
    o = 0
    qd_ref[0] = ((z[:, o:o + da_w] * dc + z[:, o + da_w:o + 2 * da_w] * ds) * sa).T.astype(BF16)
    o += 2 * da_w
    kd_ref[0] = (z[:, o:o + da_w] * dc + z[:, o + da_w:o + 2 * da_w] * ds).astype(BF16)
    o += 2 * da_w
    vd_ref[0] = z[:, o:o + da_w].T.astype(BF16)
    o += da_w
    cq = _rms(z[:, o:o + q_rank]) * qg_ref[...]
    o += q_rank
    ckv = _rms(z[:, o:o + kv_rank]) * kvg_ref[...]
    o += kv_rank
    kr = z[:, o:o + LANES] * mc_ref[...] + z[:, o + LANES:o + 2 * LANES] * ms_ref[...]
    mw = MLA_HEADS * LANES
    qu = _dot(cq.astype(BF16), wuq_ref[...])
    qm_ref[0] = ((qu[:, :mw] * mc + qu[:, mw:] * ms) * sm).T.astype(BF16)
    kvu = _dot(ckv.astype(BF16), wukv_ref[...])
    km_ref[0] = (kvu[:, :mw] + jnp.concatenate([kr] * MLA_HEADS, axis=1)).astype(BF16)
    vm_ref[0] = kvu[:, mw:].T.astype(BF16)


def _odd_weights(w_in, w_uq, w_ukv, da_w, q_rank, kv_rank):
    qw, kw, vw = w_in[:, :da_w], w_in[:, da_w:2 * da_w], w_in[:, 2 * da_w:3 * da_w]
    o = 3 * da_w
    cqw, ckvw, krw = w_in[:, o:o + q_rank], w_in[:, o + q_rank:o + q_rank + kv_rank], w_in[:, o + q_rank + kv_rank:]
    perm_da = _swap_perm(da_w, DA_HEAD_DIM // 2, DA_HEAD_DIM // 4)
    perm_r = _swap_perm(MLA_ROPE, MLA_ROPE // 2, MLA_ROPE // 4)
    d = w_in.shape[0]

    def rope_group(wr):
        return jnp.pad(wr, ((0, 0), (MLA_NOPE, LANES - MLA_NOPE - MLA_ROPE)))

    w_big = jnp.concatenate([qw, qw[:, perm_da], kw, kw[:, perm_da], vw, cqw, ckvw,
                             rope_group(krw), rope_group(krw[:, perm_r])], axis=1)
    dq = MLA_NOPE + MLA_ROPE
    pad = LANES - dq
    uq = w_uq.reshape(q_rank, MLA_HEADS, dq)
    uq_a = jnp.pad(uq, ((0, 0), (0, 0), (0, pad))).reshape(q_rank, MLA_HEADS * LANES)
    uq_s = jnp.pad(uq[:, :, MLA_NOPE:][:, :, perm_r], ((0, 0), (0, 0), (MLA_NOPE, pad)))
    uq_s = uq_s.reshape(q_rank, MLA_HEADS * LANES)
    ukv = w_ukv.reshape(kv_rank, MLA_HEADS, MLA_NOPE + MLA_V)
    uk = jnp.pad(ukv[:, :, :MLA_NOPE], ((0, 0), (0, 0), (0, LANES - MLA_NOPE))).reshape(kv_rank, MLA_HEADS * LANES)
    uv = ukv[:, :, MLA_NOPE:].reshape(kv_rank, MLA_HEADS * MLA_V)
    return (w_big.astype(BF16), jnp.concatenate([uq_a, uq_s], axis=1).astype(BF16),
            jnp.concatenate([uk, uv], axis=1).astype(BF16))


def _in_odd(h, mods, g, w_big, wuq, wukv, qg, kvg, tabs, tm, n_ctx_tiles, da_w):
    b, t, d = h.shape
    q_rank, kv_rank = qg.shape[0], kvg.shape[0]
    lat, ctx = _mod_specs(d, b)
    tab = pl.BlockSpec((tm, LANES), lambda bi, ti: (ti, 0))
    mw = MLA_HEADS * LANES

    def tok_major(width):
        return jax.ShapeDtypeStruct((b, t, width), BF16), _tok_spec(tm, width)

    def feat_major(width):
        return (jax.ShapeDtypeStruct((b, width, t), BF16),
                pl.BlockSpec((1, width, tm), lambda bi, ti: (bi, 0, ti)))

    outs, ospecs = zip(feat_major(da_w), tok_major(da_w), feat_major(da_w),
                       feat_major(mw), tok_major(mw), feat_major(mw))
    return pl.pallas_call(
        functools.partial(_in_odd_kernel, n_ctx_tiles=n_ctx_tiles, da_w=da_w, q_rank=q_rank, kv_rank=kv_rank),
        out_shape=tuple(outs),
        grid=(b, t // tm),
        in_specs=[_tok_spec(tm, d), lat, ctx, _const_spec((1, d)), _const_spec(w_big.shape),
                  tab, tab, tab, tab, _const_spec((1, q_rank)), _const_spec((1, kv_rank)),
                  _const_spec(wuq.shape), _const_spec(wukv.shape)],
        out_specs=tuple(ospecs),
        compiler_params=_params("arbitrary", "arbitrary"),
        name="odd_in_proj",
    )(h, mods, mods, g.reshape(1, d), w_big, *tabs, qg.reshape(1, -1), kvg.reshape(1, -1), wuq, wukv)


def _softmax_pv_t(k, qt, vt):
    s = _dot(k, qt)
    p = jnp.exp2(s - jnp.max(s, axis=0, keepdims=True))
    return _dot(vt, p.astype(BF16)) / jnp.sum(p, axis=0, keepdims=True)


def _attn_kernel(qt_ref, k_ref, vt_ref, lam_ref, sg_ref, o_ref, *, n_ctx_tiles, ctx_len, diff, lam_init):
    qt = qt_ref[0]

    def attend(nk):
        k = k_ref[0, :nk, :]
        vt = vt_ref[0, :, :nk]
        if not diff:
            return _softmax_pv_t(k, qt, vt).T
        row = lax.broadcasted_iota(jnp.int32, qt.shape, 0)
        first = row < DA_HEAD_DIM
        zero = jnp.zeros_like(qt)
        o1 = _softmax_pv_t(k, jnp.where(first, qt, zero), vt)
        o2 = _softmax_pv_t(k, jnp.where(first, zero, qt), vt)
        lp = lam_ref[...]
        lam = (jnp.exp(jnp.sum(lp[0:1] * lp[1:2], axis=-1, keepdims=True))
               - jnp.exp(jnp.sum(lp[2:3] * lp[3:4], axis=-1, keepdims=True)) + lam_init)
        d = o1 - lam * o2
        d = d * lax.rsqrt(jnp.mean(d * d, axis=0, keepdims=True) + EPS)
        return (d * sg_ref[...] * (1.0 - lam_init)).T

    is_ctx = pl.program_id(2) < n_ctx_tiles

    @pl.when(is_ctx)
    def _():
        o_ref[0] = attend(ctx_len).astype(o_ref.dtype)

    @pl.when(jnp.logical_not(is_ctx))
    def _():
        o_ref[0] = attend(k_ref.shape[1]).astype(o_ref.dtype)


def _attention(qt, k, vt, lam_p, subln_g, tq, ctx_len, diff, lam_init):
    b, t, width = k.shape
    heads = width // LANES
    khead = pl.BlockSpec((1, t, LANES), lambda bi, hi, ti: (bi, 0, hi))
    vhead = pl.BlockSpec((1, LANES, t), lambda bi, hi, ti: (bi, hi, 0))
    qtile = pl.BlockSpec((1, LANES, tq), lambda bi, hi, ti: (bi, hi, ti))
    otile = pl.BlockSpec((1, tq, LANES), lambda bi, hi, ti: (bi, ti, hi))

    def const(shape):
        zeros = (0,) * len(shape)
        return pl.BlockSpec(shape, lambda bi, hi, ti: zeros)

    return pl.pallas_call(
        functools.partial(_attn_kernel, n_ctx_tiles=ctx_len // tq, ctx_len=ctx_len, diff=diff, lam_init=lam_init),
        out_shape=jax.ShapeDtypeStruct((b, t, width), BF16),
        grid=(b, heads, t // tq),
        in_specs=[qtile, khead, vhead, const(lam_p.shape), const((LANES, 1))],
        out_specs=otile,
        compiler_params=_params("arbitrary", "arbitrary", "arbitrary"),
        name="diff_attention" if diff else "mla_attention",
    )(qt, k, vt, lam_p, subln_g.reshape(LANES, 1))


def _final_kernel(x_ref, g_ref, o_ref):
    o_ref[0] = _rms(x_ref[0]) * g_ref[...]


def _final_norm(h, g, tm, n_ctx_tiles):
    b, t, d = h.shape
    return pl.pallas_call(
        _final_kernel,
        out_shape=jax.ShapeDtypeStruct((b, t - n_ctx_tiles * tm, d), F32),
        grid=(b, t // tm - n_ctx_tiles),
        in_specs=[pl.BlockSpec((1, tm, d), lambda bi, ti: (bi, ti + n_ctx_tiles, 0)), _const_spec((1, d))],
        out_specs=_tok_spec(tm, d),
        compiler_params=_params("arbitrary", "arbitrary"),
        name="final_norm",
    )(h, g.reshape(1, d))


def kernel(x, c, ctx, c_ctx, ada_w, ada_b, norm_mix_g, norm_ffn_g, ffn_w_gu, ffn_w_down, ev_w_in, ev_w_out, hg_lower_bound, hg_out_norm_g, hy_short_w, hy_short_b, hy_filt_w1, hy_filt_b1, hy_filt_freq1, hy_filt_w2, hy_filt_b2, hy_filt_freq2, hy_filt_w3, hy_skip, od_w_in, od_w_out, da_lambda, da_subln_g, mla_q_norm_g, mla_w_uq, mla_kv_norm_g, mla_w_ukv, final_norm_g):
    n_batch, n_lat, d = x.shape
    ctx_len = ctx.shape[1]
    depth = ada_w.shape[0]
    assert n_batch % 2 == 0 and n_batch < MOD_ROWS
    assert n_lat % GRID_W == 0 and ctx_len % HG_CHUNK == 0 and n_lat % FFT_N2 == 0 and ctx_len % FFT_N2 == 0
    tm = math.gcd(math.gcd(ctx_len, n_lat), 256)
    n_ctx_tiles = ctx_len // tm
    hg_width = d // 2
    hy_width = d - hg_width
    da_w = d // 2
    q_rank, kv_rank = mla_q_norm_g.shape[1], mla_kv_norm_g.shape[1]

    cc = jnp.concatenate([c, c_ctx[None], jnp.zeros((MOD_ROWS - n_batch - 1, d), F32)], axis=0)
    mods = _ada(cc, ada_w, ada_b).reshape(depth, MOD_ROWS, 6, d)
    h = jnp.concatenate([ctx, x], axis=1)
    rope_tabs = _rope_tables(n_lat, ctx_len)

    for i in range(depth):
        if i % 2 == 0:
            e = i // 2
            z = _in_even(h, mods[i], norm_mix_g[i], ev_w_in[e].astype(BF16), tm, n_ctx_tiles)
            a = _hgrn(z, hg_lower_bound, hg_out_norm_g[e], e, hg_width // HG_HEAD_DIM, ctx_len)
            filt_params = (hy_filt_w1[e], hy_filt_b1[e], hy_filt_freq1[e], hy_filt_w2[e], hy_filt_b2[e],
                           hy_filt_freq2[e], hy_filt_w3[e])
            bb = _hyena(z, 5 * hg_width, hy_width, ctx_len, hy_short_w[e], hy_short_b[e], filt_params, hy_skip[e])
            w_out = ev_w_out[e]
        else:
            o = i // 2
            lam_init = 0.8 - 0.6 * math.exp(-0.3 * i)
            w_big, wuq, wukv = _odd_weights(od_w_in[o], mla_w_uq[o], mla_w_ukv[o], da_w, q_rank, kv_rank)
            qd, kd, vd, qm, km, vm = _in_odd(h, mods[i], norm_mix_g[i], w_big, wuq, wukv, mla_q_norm_g[o],
                                             mla_kv_norm_g[o], rope_tabs, tm, n_ctx_tiles, da_w)
            a = _attention(qd, kd, vd, da_lambda[o], da_subln_g[o], tm, ctx_len, True, lam_init)
            bb = _attention(qm, km, vm, da_lambda[o], da_subln_g[o], tm, ctx_len, False, lam_init)
            w_out = od_w_out[o]
        h = _post(h, a, bb, mods[i], norm_ffn_g[i], w_out.astype(BF16), ffn_w_gu[i].astype(BF16),
                  ffn_w_down[i].astype(BF16), tm, n_ctx_tiles)
    return _final_norm(h, final_norm_g, tm, n_ctx_tiles)
```

```python
import functools
import math

import numpy as np
import jax
import jax.numpy as jnp
from jax import lax
from jax.experimental import pallas as pl
from jax.experimental.pallas import tpu as pltpu

F32 = jnp.float32
BF16 = jnp.bfloat16
HIGHEST = lax.Precision.HIGHEST

GRID_W = 64
EPS = 1e-6
ROPE_BASE = 10000.0
HG_HEAD_DIM = 128
HG_CHUNK = 64
HY_BANDS = 16
HY_EMB = 2 * HY_BANDS + 1
HY_TARGET = 1e-2
HY_STEEP_PCT = 0.3
HY_GENTLE_PCT = 1.5
DA_HEAD_DIM = 64
MLA_HEADS = 4
MLA_NOPE = 64
MLA_ROPE = 32
MLA_V = 128
LOG2_E = 1.4426950408889634

LANES = 128
SUBLANES = 8
V7X_VMEM_LIMIT_BYTES = 56 * 1024 * 1024
MOD_ROWS = 16
FFT_N2 = 128
FFT_PITCH = FFT_N2 + SUBLANES
FFT_UNROLL = 4


def _params(*sem):
    return pltpu.CompilerParams(dimension_semantics=sem, vmem_limit_bytes=V7X_VMEM_LIMIT_BYTES)


def _dot(a, b, precision=None):
    return jnp.dot(a, b, preferred_element_type=F32, precision=precision)


def _dot_nt(a, b):
    return lax.dot_general(a, b, (((1,), (1,)), ((), ())), preferred_element_type=F32)


def _dot_tn(a, b):
    return lax.dot_general(a, b, (((0,), (0,)), ((), ())), preferred_element_type=F32)


def _rms(x):
    return x * lax.rsqrt(jnp.mean(x * x, axis=-1, keepdims=True) + EPS)


def _silu(x):
    return x * jax.nn.sigmoid(x)


def _tile_mod(mod_ref, modc_ref, n_ctx_tiles):
    return jnp.where(pl.program_id(1) < n_ctx_tiles, modc_ref[...], mod_ref[...])


def _ada_kernel(c_ref, w_ref, b_ref, o_ref):
    o_ref[0] = _dot(_silu(c_ref[...]), w_ref[0], HIGHEST) + b_ref[0]


def _ada(cc, ada_w, ada_b):
    depth, d, n = ada_w.shape
    rows = cc.shape[0]
    tn = n // 4
    return pl.pallas_call(
        _ada_kernel,
        out_shape=jax.ShapeDtypeStruct((depth, rows, n), F32),
        grid=(depth, n // tn),
        in_specs=[
            pl.BlockSpec((rows, d), lambda i, j: (0, 0)),
            pl.BlockSpec((1, d, tn), lambda i, j: (i, 0, j)),
            pl.BlockSpec((1, 1, tn), lambda i, j: (i, 0, j)),
        ],
        out_specs=pl.BlockSpec((1, rows, tn), lambda i, j: (i, 0, j)),
        compiler_params=_params("arbitrary", "arbitrary"),
        name="ada_mod",
    )(cc, ada_w, ada_b.reshape(depth, 1, n))


def _tok_spec(tm, width):
    return pl.BlockSpec((1, tm, width), lambda b, t: (b, t, 0))


def _const_spec(shape):
    zeros = (0,) * len(shape)
    return pl.BlockSpec(shape, lambda b, t: zeros)


def _mod_specs(d, n_batch):
    lat = pl.BlockSpec((None, 6, d), lambda b, t: (b, 0, 0))
    ctx = pl.BlockSpec((None, 6, d), lambda b, t: (n_batch, 0, 0))
    return lat, ctx


def _in_even_kernel(x_ref, mod_ref, modc_ref, g_ref, w_ref, o_ref, *, n_ctx_tiles):
    m = _tile_mod(mod_ref, modc_ref, n_ctx_tiles)
    xn = (_rms(x_ref[0]) * g_ref[...]) * (1.0 + m[1:2]) + m[0:1]
    o_ref[0] = _dot(xn.astype(BF16), w_ref[...])


def _in_even(h, mods, g, w, tm, n_ctx_tiles):
    b, t, d = h.shape
    n = w.shape[1]
    lat, ctx = _mod_specs(d, b)
    return pl.pallas_call(
        functools.partial(_in_even_kernel, n_ctx_tiles=n_ctx_tiles),
        out_shape=jax.ShapeDtypeStruct((b, t, n), F32),
        grid=(b, t // tm),
        in_specs=[_tok_spec(tm, d), lat, ctx, _const_spec((1, d)), _const_spec((d, n))],
        out_specs=_tok_spec(tm, n),
        compiler_params=_params("arbitrary", "arbitrary"),
        name="even_in_proj",
    )(h, mods, mods, g.reshape(1, d), w)


def _post_kernel(h_ref, a_ref, b_ref, mod_ref, modc_ref, g_ref, wo_ref, wgu_ref, wd_ref, o_ref,
                 *, n_ctx_tiles, hidden):
    m = _tile_mod(mod_ref, modc_ref, n_ctx_tiles)
    half = a_ref.shape[-1]
    y = _dot(a_ref[0], wo_ref[:half, :]) + _dot(b_ref[0], wo_ref[half:, :])
    h1 = h_ref[0] + m[2:3] * y
    xn = (_rms(h1) * g_ref[...]) * (1.0 + m[4:5]) + m[3:4]
    gu = _dot(xn.astype(BF16), wgu_ref[...])
    act = _silu(gu[:, :hidden]) * gu[:, hidden:]
    o_ref[0] = h1 + m[5:6] * _dot(act.astype(BF16), wd_ref[...])


def _post(h, a, bb, mods, g, wo, wgu, wd, tm, n_ctx_tiles):
    b, t, d = h.shape
    half = a.shape[-1]
    hidden = wd.shape[0]
    lat, ctx = _mod_specs(d, b)
    return pl.pallas_call(
        functools.partial(_post_kernel, n_ctx_tiles=n_ctx_tiles, hidden=hidden),
        out_shape=jax.ShapeDtypeStruct((b, t, d), F32),
        grid=(b, t // tm),
        in_specs=[_tok_spec(tm, d), _tok_spec(tm, half), _tok_spec(tm, half), lat, ctx,
                  _const_spec((1, d)), _const_spec((2 * half, d)), _const_spec((d, 2 * hidden)),
                  _const_spec((hidden, d))],
        out_specs=_tok_spec(tm, d),
        compiler_params=_params("arbitrary", "arbitrary"),
        name="out_proj_ffn",
    )(h, a, bb, mods, mods, g.reshape(1, d), wo, wgu, wd)


def _hgrn_tables(tile):
    c = HG_CHUNK
    r = np.arange(c)
    j = r[None, :]
    rt = np.arange(tile)
    blocks = [(j <= r[:, None]), (j > r[:, None])]
    masks, bms = [], []
    m = c
    while m >= 2:
        half = m // 2
        p = r % m
        mid = (r - p + half)[:, None]
        upper = (p >= half)[:, None]
        up = (j >= mid) & (j <= r[:, None]) & upper
        lo = (j > r[:, None]) & (j <= mid - 1) & (~upper)
        blocks.append(up | lo)
        masks.append(np.broadcast_to(upper, (c, LANES)))
        bms.append((rt[:, None] // m) == (rt[None, :] // m))
        m //= 2
    fwd = np.stack(blocks).astype(np.float32)
    bwd = fwd[:, ::-1, ::-1]
    mall = np.stack([fwd.reshape(-1, c), bwd.reshape(-1, c)])
    mall = np.concatenate([mall, mall], axis=2)
    mu = np.stack(masks).astype(np.float32)
    masku = np.tile(np.stack([mu, mu[:, ::-1]]), (1, 1, tile // c, 1))
    return (jnp.asarray(mall, dtype=BF16), jnp.asarray(masku, dtype=F32),
            jnp.asarray(np.stack(bms), dtype=F32))


def _hgrn_kernel(q_ref, ff_ref, fb_ref, i_ref, g_ref, lbp_ref, ng_ref, mall_ref, masku_ref, bm_ref,
                 o_ref, of_ref, ob_ref, *, layer, n_ctx_tiles, n_tiles):
    c = HG_CHUNK
    dk = HG_HEAD_DIM
    n_levels = masku_ref.shape[1]
    tile = masku_ref.shape[2]
    n_sub = tile // c
    lbp = lbp_ref[...]
    ex = jnp.exp(lbp - jnp.max(lbp, axis=0, keepdims=True))
    p = ex / jnp.sum(ex, axis=0, keepdims=True)
    lb = jnp.sum(p[:layer + 1], axis=0) - p[0]

    def sub(x, g):
        return x[g * c:(g + 1) * c]

    def run_tile(dirn, ti, st):
        rows = pl.ds(pl.multiple_of(ti * tile, tile), tile)
        q = q_ref[0, rows, :]
        zf = (ff_ref if dirn == 0 else fb_ref)[0, rows, :]
        v = i_ref[0, rows, :]
        lbd = lb[dirn:dirn + 1]
        f = lbd + (1.0 - lbd) * jax.nn.sigmoid(zf)
        lf = jnp.log(f)
        k = 1.0 - f
        hi = lf.astype(BF16)
        lo = (lf - hi.astype(F32)).astype(BF16)
        x = jnp.concatenate([jnp.concatenate([sub(hi, g) for g in range(n_sub)], axis=1),
                             jnp.concatenate([sub(lo, g) for g in range(n_sub)], axis=1)], axis=0)
        e2 = _dot(mall_ref[dirn], x)
        ee = [jnp.exp(e2[:, g * dk:(g + 1) * dk]) for g in range(n_sub)]

        def block(idx):
            return jnp.concatenate([ee[g][idx * c:(idx + 1) * c] for g in range(n_sub)], axis=0)

        g_in = block(0)
        qin = (q * g_in).astype(BF16)
        kout = (k * block(1)).astype(BF16)
        vb = v.astype(BF16)
        a = jnp.zeros((tile, tile), F32)
        for lvl in range(n_levels):
            gl = block(2 + lvl)
            gu = gl * masku_ref[dirn, lvl]
            a = a + bm_ref[lvl] * _dot_nt((q * gu).astype(BF16), (k * (gl - gu)).astype(BF16))
        o = _dot(a.astype(BF16), vb) + jnp.sum(q * k, axis=-1, keepdims=True) * v
        outs = [None] * n_sub
        for g in (range(n_sub) if dirn == 0 else reversed(range(n_sub))):
            outs[g] = sub(o, g) + _dot_nt(sub(qin, g), st.astype(BF16))
            last = (g + 1) * c - 1 if dirn == 0 else g * c
            st = st * g_in[last:last + 1] + _dot_tn(sub(vb, g), sub(kout, g))
        return rows, jnp.concatenate(outs, axis=0), st

    def body(i, carry):
        sf, sb = carry
        tb = jnp.where(i < n_ctx_tiles, n_ctx_tiles - 1 - i, n_tiles - 1 - (i - n_ctx_tiles))
        rows, o, sf = run_tile(0, i, sf)
        of_ref[rows, :] = o
        rows, o, sb = run_tile(1, tb, sb)
        ob_ref[rows, :] = o
        return sf, sb

    s0 = jnp.zeros((dk, dk), F32)
    lax.fori_loop(0, n_tiles, body, (s0, s0))
    o = _rms(of_ref[...] + ob_ref[...]) * ng_ref[...]
    o_ref[0] = (o * _silu(g_ref[0])).astype(o_ref.dtype)


def _hgrn(z, hg_lower_bound, ng, layer, n_heads, ctx_len):
    b, t, _ = z.shape
    dk = HG_HEAD_DIM
    n_even = hg_lower_bound.shape[0]
    tile = math.gcd(math.gcd(ctx_len, t - ctx_len), 256)
    mall, masku, bm = _hgrn_tables(tile)

    def col(kind):
        return pl.BlockSpec((1, t, dk), lambda bi, hi: (bi, 0, kind * n_heads + hi))

    def const(arr):
        zeros = (0,) * arr.ndim
        return pl.BlockSpec(arr.shape, lambda bi, hi: zeros)

    return pl.pallas_call(
        functools.partial(_hgrn_kernel, layer=layer, n_ctx_tiles=ctx_len // tile, n_tiles=t // tile),
        out_shape=jax.ShapeDtypeStruct((b, t, n_heads * dk), BF16),
        grid=(b, n_heads),
        in_specs=[col(0), col(1), col(2), col(3), col(4),
                  pl.BlockSpec((n_even, 2, dk), lambda bi, hi: (0, 0, hi)),
                  const(ng.reshape(1, dk)), const(mall), const(masku), const(bm)],
        out_specs=pl.BlockSpec((1, t, dk), lambda bi, hi: (bi, 0, hi)),
        scratch_shapes=[pltpu.VMEM((t, dk), F32), pltpu.VMEM((t, dk), F32)],
        compiler_params=_params("arbitrary", "arbitrary"),
        name="hgrn2_scan",
    )(z, z, z, z, z, hg_lower_bound, ng.reshape(1, dk), mall, masku, bm)


def _hy_pre_kernel(x0_ref, x1_ref, v_ref, w0_ref, w1_ref, wv_ref, b0_ref, b1_ref, bv_ref,
                   x0c_ref, u_ref, *, ctx_len):
    t = x0_ref.shape[1]
    row = lax.broadcasted_iota(jnp.int32, (t, LANES), 0)
    first = (row == 0) | (row == ctx_len)
    final = (row == ctx_len - 1) | (row == t - 1)

    def short_conv(z_ref, w_ref, b_ref):
        z = z_ref[0]
        prev = jnp.where(first, 0.0, pltpu.roll(z, 1, 0))
        nxt = jnp.where(final, 0.0, pltpu.roll(z, t - 1, 0))
        w = w_ref[...]
        return prev * w[0:1] + z * w[1:2] + nxt * w[2:3] + b_ref[...]

    x0c_ref[0] = short_conv(x0_ref, w0_ref, b0_ref).astype(x0c_ref.dtype)
    u = short_conv(x1_ref, w1_ref, b1_ref) * short_conv(v_ref, wv_ref, bv_ref)
    u_ref[0] = u.astype(u_ref.dtype)


def _hy_pre(z, short_w, short_b, col0, width, ctx_len):
    b, t, _ = z.shape
    nb = width // LANES
    c0 = col0 // LANES

    def zcol(kind):
        return pl.BlockSpec((1, t, LANES), lambda bi, j: (bi, 0, c0 + kind * nb + j))

    def wcol(kind, rows):
        return pl.BlockSpec((rows, LANES), lambda bi, j: (0, kind * nb + j))

    out = jax.ShapeDtypeStruct((b, t, width), BF16)
    ospec = pl.BlockSpec((1, t, LANES), lambda bi, j: (bi, 0, j))
    sb = short_b.reshape(1, -1)
    return pl.pallas_call(
        functools.partial(_hy_pre_kernel, ctx_len=ctx_len),
        out_shape=(out, out),
        grid=(b, nb),
        in_specs=[zcol(0), zcol(1), zcol(2), wcol(0, 3), wcol(1, 3), wcol(2, 3),
                  wcol(0, 1), wcol(1, 1), wcol(2, 1)],
        out_specs=(ospec, ospec),
        compiler_params=_params("arbitrary", "arbitrary"),
        name="hyena_short_conv",
    )(z, z, z, short_w, short_w, short_w, sb, sb, sb)


def _filter_features(length, n):
    p = np.arange(n)
    is_f = p < length
    is_b = p > n - length
    lag = np.where(is_f, p, np.where(is_b, n - 1 - p, 0))
    tt = np.linspace(0.0, 1.0, length, dtype=np.float32)[lag][:, None]
    w = (2.0 * math.pi * lag.astype(np.float32) / length)[:, None].astype(np.float32)
    bands = np.linspace(1e-4, HY_BANDS - 1, HY_BANDS, dtype=np.float32)[None, :]
    feat = np.concatenate([tt, np.cos(bands * w), -np.sin(bands * w), is_f[:, None], is_b[:, None]],
                          axis=-1).astype(np.float32)
    pad = (-feat.shape[1]) % SUBLANES
    return jnp.asarray(np.pad(feat, ((0, 0), (0, pad))))


def _filt_kernel(feat_ref, w1_ref, b1_ref, fr1_ref, w2_ref, b2_ref, fr2_ref,
                 w3f_ref, w3b_ref, dl_ref, o_ref, hid_ref):
    z = feat_ref[...]
    mf = z[:, HY_EMB:HY_EMB + 1]
    mb = z[:, HY_EMB + 1:HY_EMB + 2]

    @pl.when(pl.program_id(0) == 0)
    def _():
        h1 = jnp.sin(fr1_ref[...] * (_dot(z, w1_ref[...], HIGHEST) + b1_ref[...]))
        hid_ref[...] = jnp.sin(fr2_ref[...] * (_dot(h1, w2_ref[...], HIGHEST) + b2_ref[...]))

    hid = hid_ref[...]
    hf = _dot(hid, w3f_ref[...], HIGHEST)
    hb = _dot(hid, w3b_ref[...], HIGHEST)
    win = jnp.exp(-z[:, 0:1] * dl_ref[...])
    f = (mf * hf + mb * hb) * win
    o_ref[...] = f / jnp.sum(jnp.abs(f), axis=0, keepdims=True)


def _hyena_filter(length, n, w1, b1, fr1, w2, b2, fr2, w3, width):
    feat = _filter_features(length, n)
    nf = feat.shape[1]
    hid = w1.shape[1]
    w1p = jnp.pad(w1, ((0, nf - w1.shape[0]), (0, 0)))
    d_lo = -math.log(HY_TARGET) / HY_GENTLE_PCT
    d_hi = -math.log(HY_TARGET) / HY_STEEP_PCT
    deltas = jnp.asarray(np.linspace(d_lo, d_hi, width, dtype=np.float32)[None, :])
    nb = width // LANES

    def full(shape):
        zeros = (0,) * len(shape)
        return pl.BlockSpec(shape, lambda j: zeros)

    return pl.pallas_call(
        _filt_kernel,
        out_shape=jax.ShapeDtypeStruct((n, width), F32),
        grid=(nb,),
        in_specs=[full((n, nf)), full((nf, hid)), full((1, hid)),
                  full((1, hid)), full((hid, hid)), full((1, hid)), full((1, hid)),
                  pl.BlockSpec((hid, LANES), lambda j: (0, j)),
                  pl.BlockSpec((hid, LANES), lambda j: (0, nb + j)),
                  pl.BlockSpec((1, LANES), lambda j: (0, j))],
        out_specs=pl.BlockSpec((n, LANES), lambda j: (0, j)),
        scratch_shapes=[pltpu.VMEM((n, hid), F32)],
        compiler_params=_params("arbitrary"),
        name="hyena_filter_mlp",
    )(feat, w1p, b1.reshape(1, -1), fr1.reshape(1, -1), w2, b2.reshape(1, -1),
      fr2.reshape(1, -1), w3, w3, deltas)


def _dft_tables(n1, n1_in):
    n2 = FFT_N2
    n = n1 * n2
    a = np.arange(n1)
    j = np.arange(n2)
    ang = -2.0 * np.pi * (a[None, None, :] * a[None, :, None] / n1 + j[:, None, None] * a[None, :, None] / n)
    tr, ti = np.cos(ang), np.sin(ang)
    fwd_a = np.concatenate([np.concatenate([tr, -ti], 2), np.concatenate([ti, tr], 2)], 1)
    trt, tit = np.swapaxes(tr, 1, 2), -np.swapaxes(ti, 1, 2)
    inv_a = np.concatenate([np.concatenate([trt, -tit], 2), np.concatenate([tit, trt], 2)], 1)
    keep = np.concatenate([np.arange(n1_in), n1 + np.arange(n1_in)])
    ang2 = -2.0 * np.pi * (j[:, None] * j[None, :]) / n2
    cr, ci = np.cos(ang2), np.sin(ang2)
    fwd_c = np.block([[cr, -ci], [ci, cr]])
    inv_c = np.block([[cr, ci], [-ci, cr]])
    real_a = np.concatenate([tr, ti], 1)
    return dict(fwd_a=fwd_a[:, :, keep], inv_a=inv_a[:, keep, :], fwd_c=fwd_c, inv_c=inv_c,
                real_a=real_a)


def _slab(idx):
    return pl.ds(pl.multiple_of(idx * FFT_PITCH, SUBLANES), FFT_N2)


def _fft_filter_kernel(f_ref, wa_ref, wc_ref, o_ref, scr_ref, *, n1):
    n2 = FFT_N2
    rows = 2 * n1
    scale = 1.0 / (n1 * n2)

    def stage_a(j, carry):
        scr_ref[pl.ds(j, rows, stride=FFT_PITCH), :] = _dot(wa_ref[j], f_ref[j], HIGHEST)
        return carry

    lax.fori_loop(0, n2, stage_a, 0, unroll=FFT_UNROLL)

    def stage_c(k, carry):
        x = jnp.concatenate([scr_ref[_slab(k), :], scr_ref[_slab(n1 + k), :]], axis=0)
        o_ref[k] = _dot(wc_ref[...], x, HIGHEST) * scale
        return carry

    lax.fori_loop(0, n1, stage_c, 0, unroll=FFT_UNROLL)


def _fft_filter(filt, n1, tables):
    n2 = FFT_N2
    width = filt.shape[1]
    ft = filt.reshape(n1, n2, width).transpose(1, 0, 2)
    wa = jnp.asarray(tables["real_a"], dtype=F32)
    wc = jnp.asarray(tables["fwd_c"], dtype=F32)
    return pl.pallas_call(
        functools.partial(_fft_filter_kernel, n1=n1),
        out_shape=jax.ShapeDtypeStruct((n1, 2 * n2, width), F32),
        grid=(width // LANES,),
        in_specs=[pl.BlockSpec((n2, n1, LANES), lambda c: (0, 0, c)),
                  pl.BlockSpec(wa.shape, lambda c: (0, 0, 0)),
                  pl.BlockSpec(wc.shape, lambda c: (0, 0))],
        out_specs=pl.BlockSpec((n1, 2 * n2, LANES), lambda c: (0, 0, c)),
        scratch_shapes=[pltpu.VMEM((2 * n1 * FFT_PITCH, LANES), F32)],
        compiler_params=_params("arbitrary"),
        name="hyena_filter_dft",
    )(ft, wa, wc)


def _fftconv_kernel(u_ref, x0_ref, fh_ref, skip_ref, wfa_ref, wfc_ref, wic_ref, wia_ref, o_ref,
                    scr_ref, *, n1):
    n2 = FFT_N2
    rows = 2 * n1

    def stage_a(j, carry):
        scr_ref[pl.ds(j, rows, stride=FFT_PITCH), :] = _dot(wfa_ref[j], u_ref[0, j])
        return carry

    lax.fori_loop(0, n2, stage_a, 0, unroll=FFT_UNROLL)

    def stage_c(k, carry):
        x = jnp.concatenate([scr_ref[_slab(k), :], scr_ref[_slab(n1 + k), :]], axis=0)
        xf = _dot(wfc_ref[...], x.astype(BF16))
        fh = fh_ref[k]
        xr, xi, fr, fi = xf[:n2], xf[n2:], fh[:n2], fh[n2:]
        y = jnp.concatenate([xr * fr - xi * fi, xr * fi + xi * fr], axis=0)
        zt = _dot(wic_ref[...], y.astype(BF16))
        scr_ref[_slab(k), :] = zt[:n2]
        scr_ref[_slab(n1 + k), :] = zt[n2:]
        return carry

    lax.fori_loop(0, n1, stage_c, 0, unroll=FFT_UNROLL)

    def stage_ai(j, carry):
        x = scr_ref[pl.ds(j, rows, stride=FFT_PITCH), :]
        y = _dot(wia_ref[j], x.astype(BF16))
        u = u_ref[0, j].astype(F32)
        o_ref[0, j] = (x0_ref[0, j].astype(F32) * (y + skip_ref[...] * u)).astype(o_ref.dtype)
        return carry

    lax.fori_loop(0, n2, stage_ai, 0, unroll=FFT_UNROLL)


def _to_fft_layout(x, n1_in):
    b, length, c = x.shape
    n1_used = length // FFT_N2
    x = x.reshape(b // 2, 2, n1_used, FFT_N2, c)
    x = jnp.pad(x, ((0, 0), (0, 0), (0, n1_in - n1_used), (0, 0), (0, 0)))
    return x.transpose(0, 3, 1, 2, 4).reshape(b // 2, FFT_N2, 2 * n1_in, c)


def _from_fft_layout(y, length):
    p, n2, rows, c = y.shape
    n1_in = rows // 2
    y = y.reshape(p, n2, 2, n1_in, c).transpose(0, 2, 3, 1, 4)
    return y.reshape(2 * p, n1_in * n2, c)[:, :length]


def _fftconv(u, x0c, fh, skip, n1, n1_in, tables):
    _, length, width = u.shape
    ut = _to_fft_layout(u, n1_in)
    xt = _to_fft_layout(x0c, n1_in)
    pairs, n2, rin, _ = ut.shape
    wfa = jnp.asarray(tables["fwd_a"], dtype=BF16)
    wia = jnp.asarray(tables["inv_a"], dtype=BF16)
    wfc = jnp.asarray(tables["fwd_c"], dtype=BF16)
    wic = jnp.asarray(tables["inv_c"], dtype=BF16)
    data = pl.BlockSpec((1, n2, rin, LANES), lambda c, p: (p, 0, 0, c))

    def const(arr):
        zeros = (0,) * arr.ndim
        return pl.BlockSpec(arr.shape, lambda c, p: zeros, pipeline_mode=pl.Buffered(1))

    out = pl.pallas_call(
        functools.partial(_fftconv_kernel, n1=n1),
        out_shape=jax.ShapeDtypeStruct(ut.shape, BF16),
        grid=(width // LANES, pairs),
        in_specs=[data, data,
                  pl.BlockSpec((n1, 2 * n2, LANES), lambda c, p: (0, 0, c)),
                  pl.BlockSpec((1, LANES), lambda c, p: (0, c)),
                  const(wfa), const(wfc), const(wic), const(wia)],
        out_specs=data,
        scratch_shapes=[pltpu.VMEM((2 * n1 * FFT_PITCH, LANES), F32)],
        compiler_params=_params("arbitrary", "arbitrary"),
        name="hyena_dft_conv",
    )(ut, xt, fh, skip.reshape(1, width), wfa, wfc, wic, wia)
    return _from_fft_layout(out, length)


def _hyena(z, col0, width, ctx_len, short_w, short_b, filt_params, skip):
    x0c, u = _hy_pre(z, short_w, short_b, col0, width, ctx_len)
    outs = []
    for lo, hi in ((0, ctx_len), (ctx_len, z.shape[1])):
        length = hi - lo
        n1_in = max(length // FFT_N2, SUBLANES)
        n1 = 2 * n1_in
        tables = _dft_tables(n1, n1_in)
        filt = _hyena_filter(length, n1 * FFT_N2, *filt_params, width)
        fh = _fft_filter(filt, n1, tables)
        outs.append(_fftconv(u[:, lo:hi], x0c[:, lo:hi], fh, skip, n1, n1_in, tables))
    return jnp.concatenate(outs, axis=1)


def _rope_tables(n_lat, ctx_len):
    tok = np.arange(n_lat)
    row, colp = tok // GRID_W, tok % GRID_W

    def axial(half):
        inv = ROPE_BASE ** (-np.arange(half, dtype=np.float32) / half)
        parts_c, parts_s = [], []
        for pos in (row, colp):
            ang = pos.astype(np.float32)[:, None] * inv
            parts_c += [np.cos(ang), np.cos(ang)]
            parts_s += [-np.sin(ang), np.sin(ang)]
        return np.concatenate(parts_c, 1), np.concatenate(parts_s, 1)

    dc, ds = axial(DA_HEAD_DIM // 4)
    mc, ms = axial(MLA_ROPE // 4)
    ones, zeros = np.ones((n_lat, MLA_NOPE), np.float32), np.zeros((n_lat, MLA_NOPE), np.float32)
    padc = np.ones((n_lat, LANES - MLA_NOPE - MLA_ROPE), np.float32)
    tabs = [np.concatenate([dc, dc], 1), np.concatenate([ds, ds], 1),
            np.concatenate([ones, mc, padc], 1), np.concatenate([zeros, ms, 0 * padc], 1)]
    out = []
    for i, tb in enumerate(tabs):
        ctx_rows = np.ones((ctx_len, LANES), np.float32) if i % 2 == 0 else np.zeros((ctx_len, LANES), np.float32)
        out.append(jnp.asarray(np.concatenate([ctx_rows, tb.astype(np.float32)], 0)))
    return out


def _swap_perm(width, group, half):
    idx = np.arange(width)
    pos = idx % group
    return np.where((pos % (2 * half)) < half, idx + half, idx - half)


def _with_ones(v):
    ones = jnp.ones((v.shape[0], LANES), v.dtype)
    parts = []
    for hd in range(v.shape[1] // LANES):
        parts += [v[:, hd * LANES:(hd + 1) * LANES], ones]
    return jnp.concatenate(parts, axis=1)


def _in_odd_kernel(x_ref, mod_ref, modc_ref, g_ref, w_ref, dc_ref, ds_ref, mc_ref, ms_ref,
                   qg_ref, kvg_ref, wuq_ref, wukv_ref,
                   qd_ref, kd_ref, vd_ref, qm_ref, km_ref, vm_ref, *, n_ctx_tiles, da_w, q_rank, kv_rank):
    m = _tile_mod(mod_ref, modc_ref, n_ctx_tiles)
    xn = (_rms(x_ref[0]) * g_ref[...]) * (1.0 + m[1:2]) + m[0:1]
    z = _dot(xn.astype(BF16), w_ref[...])
    nrep = da_w // LANES
    dc = jnp.concatenate([dc_ref[...]] * nrep, axis=1)
    ds = jnp.concatenate([ds_ref[...]] * nrep, axis=1)
    mc = jnp.concatenate([mc_ref[...]] * MLA_HEADS, axis=1)
    ms = jnp.concatenate([ms_ref[...]] * MLA_HEADS, axis=1)
    sa = DA_HEAD_DIM ** -0.5 * LOG2_E
    sm = (MLA_NOPE + MLA_ROPE) ** -0.5 * LOG2_E
    o = 0
    qd_ref[0] = ((z[:, o:o + da_w] * dc + z[:, o + da_w:o + 2 * da_w] * ds) * sa).astype(BF16)
    o += 2 * da_w
    kd_ref[0] = (z[:, o:o + da_w] * dc + z[:, o + da_w:o + 2 * da_w] * ds).astype(BF16)
    o += 2 * da_w
    vd_ref[0] = _with_ones(z[:, o:o + da_w]).astype(BF16)
    o += da_w
    cq = _rms(z[:, o:o + q_rank]) * qg_ref[...]
    o += q_rank
    ckv = _rms(z[:, o:o + kv_rank]) * kvg_ref[...]
    o += kv_rank
    kr = z[:, o:o + LANES] * mc_ref[...] + z[:, o + LANES:o + 2 * LANES] * ms_ref[...]
    mw = MLA_HEADS * LANES
    qu = _dot(cq.astype(BF16), wuq_ref[...])
    qm_ref[0] = ((qu[:, :mw] * mc + qu[:, mw:] * ms) * sm).astype(BF16)
    kvu = _dot(ckv.astype(BF16), wukv_ref[...])
    km_ref[0] = (kvu[:, :mw] + jnp.concatenate([kr] * MLA_HEADS, axis=1)).astype(BF16)
    vm_ref[0] = _with_ones(kvu[:, mw:]).astype(BF16)


def _odd_weights(w_in, w_uq, w_ukv, da_w, q_rank, kv_rank):
    qw, kw, vw = w_in[:, :da_w], w_in[:, da_w:2 * da_w], w_in[:, 2 * da_w:3 * da_w]
    o = 3 * da_w
    cqw, ckvw, krw = w_in[:, o:o + q_rank], w_in[:, o + q_rank:o + q_rank + kv_rank], w_in[:, o + q_rank + kv_rank:]
    perm_da = _swap_perm(da_w, DA_HEAD_DIM // 2, DA_HEAD_DIM // 4)
    perm_r = _swap_perm(MLA_ROPE, MLA_ROPE // 2, MLA_ROPE // 4)
    d = w_in.shape[0]

    def rope_group(wr):
        return jnp.pad(wr, ((0, 0), (MLA_NOPE, LANES - MLA_NOPE - MLA_ROPE)))

    w_big = jnp.concatenate([qw, qw[:, perm_da], kw, kw[:, perm_da], vw, cqw, ckvw,
                             rope_group(krw), rope_group(krw[:, perm_r])], axis=1)
    dq = MLA_NOPE + MLA_ROPE
    pad = LANES - dq
    uq = w_uq.reshape(q_rank, MLA_HEADS, dq)
    uq_a = jnp.pad(uq, ((0, 0), (0, 0), (0, pad))).reshape(q_rank, MLA_HEADS * LANES)
    uq_s = jnp.pad(uq[:, :, MLA_NOPE:][:, :, perm_r], ((0, 0), (0, 0), (MLA_NOPE, pad)))
    uq_s = uq_s.reshape(q_rank, MLA_HEADS * LANES)
    ukv = w_ukv.reshape(kv_rank, MLA_HEADS, MLA_NOPE + MLA_V)
    uk = jnp.pad(ukv[:, :, :MLA_NOPE], ((0, 0), (0, 0), (0, LANES - MLA_NOPE))).reshape(kv_rank, MLA_HEADS * LANES)
    uv = ukv[:, :, MLA_NOPE:].reshape(kv_rank, MLA_HEADS * MLA_V)
    return (w_big.astype(BF16), jnp.concatenate([uq_a, uq_s], axis=1).astype(BF16),
            jnp.concatenate([uk, uv], axis=1).astype(BF16))


def _in_odd(h, mods, g, w_big, wuq, wukv, qg, kvg, tabs, tm, n_ctx_tiles, da_w):
    b, t, d = h.shape
    q_rank, kv_rank = qg.shape[0], kvg.shape[0]
    lat, ctx = _mod_specs(d, b)
    tab = pl.BlockSpec((tm, LANES), lambda bi, ti: (ti, 0))
    mw = MLA_HEADS * LANES

    def tok_major(width):
        return jax.ShapeDtypeStruct((b, t, width), BF16), _tok_spec(tm, width)

    outs, ospecs = zip(tok_major(da_w), tok_major(da_w), tok_major(2 * da_w),
                       tok_major(mw), tok_major(mw), tok_major(2 * mw))
    return pl.pallas_call(
        functools.partial(_in_odd_kernel, n_ctx_tiles=n_ctx_tiles, da_w=da_w, q_rank=q_rank, kv_rank=kv_rank),
        out_shape=tuple(outs),
        grid=(b, t // tm),
        in_specs=[_tok_spec(tm, d), lat, ctx, _const_spec((1, d)), _const_spec(w_big.shape),
                  tab, tab, tab, tab, _const_spec((1, q_rank)), _const_spec((1, kv_rank)),
                  _const_spec(wuq.shape), _const_spec(wukv.shape)],
        out_specs=tuple(ospecs),
        compiler_params=_params("arbitrary", "arbitrary"),
        name="odd_in_proj",
    )(h, mods, mods, g.reshape(1, d), w_big, *tabs, qg.reshape(1, -1), kvg.reshape(1, -1), wuq, wukv)


def _softmax_pv(q, k, v1):
    s = _dot_nt(q, k)
    p = jnp.exp2(s - jnp.max(s, axis=-1, keepdims=True))
    ov = _dot(p.astype(BF16), v1)
    return ov[:, :LANES] / ov[:, LANES:]


def _attn_kernel(q_ref, k_ref, v_ref, lam_ref, sg_ref, o_ref, *, diff, lam_init):
    q = q_ref[0]
    k = k_ref[0]
    v1 = v_ref[0]
    if not diff:
        o_ref[0] = _softmax_pv(q, k, v1).astype(o_ref.dtype)
        return
    first = lax.broadcasted_iota(jnp.int32, q.shape, 1) < DA_HEAD_DIM
    zero = jnp.zeros_like(q)
    o1 = _softmax_pv(jnp.where(first, q, zero), k, v1)
    o2 = _softmax_pv(jnp.where(first, zero, q), k, v1)
    lp = lam_ref[...]
    lam = (jnp.exp(jnp.sum(lp[0:1] * lp[1:2], axis=-1, keepdims=True))
           - jnp.exp(jnp.sum(lp[2:3] * lp[3:4], axis=-1, keepdims=True)) + lam_init)
    o_ref[0] = (_rms(o1 - lam * o2) * sg_ref[...] * (1.0 - lam_init)).astype(o_ref.dtype)


def _attention(q, k, v1, lam_p, subln_g, tq, n_keys, diff, lam_init):
    b, t_q, width = q.shape
    heads = width // LANES
    qtile = pl.BlockSpec((1, tq, LANES), lambda bi, hi, ti: (bi, ti, hi))
    khead = pl.BlockSpec((1, n_keys, LANES), lambda bi, hi, ti: (bi, 0, hi))
    vhead = pl.BlockSpec((1, n_keys, 2 * LANES), lambda bi, hi, ti: (bi, 0, hi))

    def const(shape):
        zeros = (0,) * len(shape)
        return pl.BlockSpec(shape, lambda bi, hi, ti: zeros)

    return pl.pallas_call(
        functools.partial(_attn_kernel, diff=diff, lam_init=lam_init),
        out_shape=jax.ShapeDtypeStruct((b, t_q, width), BF16),
        grid=(b, heads, t_q // tq),
        in_specs=[qtile, khead, vhead, const(lam_p.shape), const((1, LANES))],
        out_specs=qtile,
        compiler_params=_params("arbitrary", "arbitrary", "arbitrary"),
        name="diff_attention" if diff else "mla_attention",
    )(q, k, v1, lam_p, subln_g.reshape(1, LANES))


def _attend_segments(q, k, v1, lam_p, subln_g, ctx_len, diff, lam_init):
    t = k.shape[1]
    n_lat = t - ctx_len
    tq_lat = math.gcd(n_lat, 512)
    o_ctx = _attention(q[:, :ctx_len], k, v1, lam_p, subln_g, ctx_len, ctx_len, diff, lam_init)
    o_lat = _attention(q[:, ctx_len:], k, v1, lam_p, subln_g, tq_lat, t, diff, lam_init)
    return jnp.concatenate([o_ctx, o_lat], axis=1)


def _final_kernel(x_ref, g_ref, o_ref):
    o_ref[0] = _rms(x_ref[0]) * g_ref[...]


def _final_norm(h, g, tm, n_ctx_tiles):
    b, t, d = h.shape
    return pl.pallas_call(
        _final_kernel,
        out_shape=jax.ShapeDtypeStruct((b, t - n_ctx_tiles * tm, d), F32),
        grid=(b, t // tm - n_ctx_tiles),
        in_specs=[pl.BlockSpec((1, tm, d), lambda bi, ti: (bi, ti + n_ctx_tiles, 0)), _const_spec((1, d))],
        out_specs=_tok_spec(tm, d),
        compiler_params=_params("arbitrary", "arbitrary"),
        name="final_norm",
    )(h, g.reshape(1, d))


def kernel(x, c, ctx, c_ctx, ada_w, ada_b, norm_mix_g, norm_ffn_g, ffn_w_gu, ffn_w_down, ev_w_in, ev_w_out, hg_lower_bound, hg_out_norm_g, hy_short_w, hy_short_b, hy_filt_w1, hy_filt_b1, hy_filt_freq1, hy_filt_w2, hy_filt_b2, hy_filt_freq2, hy_filt_w3, hy_skip, od_w_in, od_w_out, da_lambda, da_subln_g, mla_q_norm_g, mla_w_uq, mla_kv_norm_g, mla_w_ukv, final_norm_g):
    n_batch, n_lat, d = x.shape
    ctx_len = ctx.shape[1]
    depth = ada_w.shape[0]
    assert n_batch % 2 == 0 and n_batch < MOD_ROWS
    assert n_lat % GRID_W == 0 and ctx_len % HG_CHUNK == 0 and n_lat % FFT_N2 == 0 and ctx_len % FFT_N2 == 0
    tm = math.gcd(math.gcd(ctx_len, n_lat), 256)
    n_ctx_tiles = ctx_len // tm
    hg_width = d // 2
    hy_width = d - hg_width
    da_w = d // 2
    q_rank, kv_rank = mla_q_norm_g.shape[1], mla_kv_norm_g.shape[1]

    cc = jnp.concatenate([c, c_ctx[None], jnp.zeros((MOD_ROWS - n_batch - 1, d), F32)], axis=0)
    mods = _ada(cc, ada_w, ada_b).reshape(depth, MOD_ROWS, 6, d)
    h = jnp.concatenate([ctx, x], axis=1)
    rope_tabs = _rope_tables(n_lat, ctx_len)

    for i in range(depth):
        if i % 2 == 0:
            e = i // 2
            z = _in_even(h, mods[i], norm_mix_g[i], ev_w_in[e].astype(BF16), tm, n_ctx_tiles)
            a = _hgrn(z, hg_lower_bound, hg_out_norm_g[e], e, hg_width // HG_HEAD_DIM, ctx_len)
            filt_params = (hy_filt_w1[e], hy_filt_b1[e], hy_filt_freq1[e], hy_filt_w2[e], hy_filt_b2[e],
                           hy_filt_freq2[e], hy_filt_w3[e])
            bb = _hyena(z, 5 * hg_width, hy_width, ctx_len, hy_short_w[e], hy_short_b[e], filt_params, hy_skip[e])
            w_out = ev_w_out[e]
        else:
            o = i // 2
            lam_init = 0.8 - 0.6 * math.exp(-0.3 * i)
            w_big, wuq, wukv = _odd_weights(od_w_in[o], mla_w_uq[o], mla_w_ukv[o], da_w, q_rank, kv_rank)
            qd, kd, vd, qm, km, vm = _in_odd(h, mods[i], norm_mix_g[i], w_big, wuq, wukv, mla_q_norm_g[o],
                                             mla_kv_norm_g[o], rope_tabs, tm, n_ctx_tiles, da_w)
            a = _attend_segments(qd, kd, vd, da_lambda[o], da_subln_g[o], ctx_len, True, lam_init)
            bb = _attend_segments(qm, km, vm, da_lambda[o], da_subln_g[o], ctx_len, False, lam_init)
            w_out = od_w_out[o]
        h = _post(h, a, bb, mods[i], norm_ffn_g[i], w_out.astype(BF16), ffn_w_gu[i].astype(BF16),
                  ffn_w_down[i].astype(BF16), tm, n_ctx_tiles)
    return _final_norm(h, final_norm_g, tm, n_ctx_tiles)
```

```python
import functools
import math

import numpy as np
import jax
import jax.numpy as jnp
from jax import lax
from jax.experimental import pallas as pl
from jax.experimental.pallas import tpu as pltpu

F32 = jnp.float32
BF16 = jnp.bfloat16
HIGHEST = lax.Precision.HIGHEST

GRID_W = 64
EPS = 1e-6
ROPE_BASE = 10000.0
HG_HEAD_DIM = 128
HG_CHUNK = 64
HY_BANDS = 16
HY_EMB = 2 * HY_BANDS + 1
HY_TARGET = 1e-2
HY_STEEP_PCT = 0.3
HY_GENTLE_PCT = 1.5
DA_HEAD_DIM = 64
MLA_HEADS = 4
MLA_NOPE = 64
MLA_ROPE = 32
MLA_V = 128
LOG2_E = 1.4426950408889634

LANES = 128
SUBLANES = 8
V7X_VMEM_LIMIT_BYTES = 56 * 1024 * 1024
MOD_ROWS = 16
FFT_N2 = 128
FFT_PITCH = FFT_N2 + SUBLANES
FFT_UNROLL = 4
HGRN_UNROLL = 2
ATTN_KEY_BLOCK = 512


def _params(*sem):
    return pltpu.CompilerParams(dimension_semantics=sem, vmem_limit_bytes=V7X_VMEM_LIMIT_BYTES)


def _dot(a, b, precision=None):
    return jnp.dot(a, b, preferred_element_type=F32, precision=precision)


def _dot_nt(a, b):
    return lax.dot_general(a, b, (((1,), (1,)), ((), ())), preferred_element_type=F32)


def _dot_tn(a, b):
    return lax.dot_general(a, b, (((0,), (0,)), ((), ())), preferred_element_type=F32)


def _rms(x):
    return x * lax.rsqrt(jnp.mean(x * x, axis=-1, keepdims=True) + EPS)


def _silu(x):
    return x * jax.nn.sigmoid(x)


def _tile_mod(mod_ref, modc_ref, n_ctx_tiles):
    return jnp.where(pl.program_id(1) < n_ctx_tiles, modc_ref[...], mod_ref[...])


def _ada_kernel(c_ref, w_ref, b_ref, o_ref):
    o_ref[0] = _dot(_silu(c_ref[...]), w_ref[0], HIGHEST) + b_ref[0]


def _ada(cc, ada_w, ada_b):
    depth, d, n = ada_w.shape
    rows = cc.shape[0]
    tn = n // 4
    return pl.pallas_call(
        _ada_kernel,
        out_shape=jax.ShapeDtypeStruct((depth, rows, n), F32),
        grid=(depth, n // tn),
        in_specs=[
            pl.BlockSpec((rows, d), lambda i, j: (0, 0)),
            pl.BlockSpec((1, d, tn), lambda i, j: (i, 0, j)),
            pl.BlockSpec((1, 1, tn), lambda i, j: (i, 0, j)),
        ],
        out_specs=pl.BlockSpec((1, rows, tn), lambda i, j: (i, 0, j)),
        compiler_params=_params("arbitrary", "arbitrary"),
        name="ada_mod",
    )(cc, ada_w, ada_b.reshape(depth, 1, n))


def _tok_spec(tm, width):
    return pl.BlockSpec((1, tm, width), lambda b, t: (b, t, 0))


def _const_spec(shape):
    zeros = (0,) * len(shape)
    return pl.BlockSpec(shape, lambda b, t: zeros)


def _mod_specs(d, n_batch):
    lat = pl.BlockSpec((None, 6, d), lambda b, t: (b, 0, 0))
    ctx = pl.BlockSpec((None, 6, d), lambda b, t: (n_batch, 0, 0))
    return lat, ctx


def _in_even_kernel(x_ref, mod_ref, modc_ref, g_ref, w_ref, o_ref, *, n_ctx_tiles):
    m = _tile_mod(mod_ref, modc_ref, n_ctx_tiles)
    xn = (_rms(x_ref[0]) * g_ref[...]) * (1.0 + m[1:2]) + m[0:1]
    o_ref[0] = _dot(xn.astype(BF16), w_ref[...])


def _in_even(h, mods, g, w, tm, n_ctx_tiles):
    b, t, d = h.shape
    n = w.shape[1]
    lat, ctx = _mod_specs(d, b)
    return pl.pallas_call(
        functools.partial(_in_even_kernel, n_ctx_tiles=n_ctx_tiles),
        out_shape=jax.ShapeDtypeStruct((b, t, n), F32),
        grid=(b, t // tm),
        in_specs=[_tok_spec(tm, d), lat, ctx, _const_spec((1, d)), _const_spec((d, n))],
        out_specs=_tok_spec(tm, n),
        compiler_params=_params("arbitrary", "arbitrary"),
        name="even_in_proj",
    )(h, mods, mods, g.reshape(1, d), w)


def _post_kernel(h_ref, a_ref, b_ref, mod_ref, modc_ref, g_ref, wo_ref, wgu_ref, wd_ref, o_ref,
                 *, n_ctx_tiles, hidden):
    m = _tile_mod(mod_ref, modc_ref, n_ctx_tiles)
    half = a_ref.shape[-1]
    y = _dot(a_ref[0], wo_ref[:half, :]) + _dot(b_ref[0], wo_ref[half:, :])
    h1 = h_ref[0] + m[2:3] * y
    xn = (_rms(h1) * g_ref[...]) * (1.0 + m[4:5]) + m[3:4]
    gu = _dot(xn.astype(BF16), wgu_ref[...])
    act = _silu(gu[:, :hidden]) * gu[:, hidden:]
    o_ref[0] = h1 + m[5:6] * _dot(act.astype(BF16), wd_ref[...])


def _post(h, a, bb, mods, g, wo, wgu, wd, tm, n_ctx_tiles):
    b, t, d = h.shape
    half = a.shape[-1]
    hidden = wd.shape[0]
    lat, ctx = _mod_specs(d, b)
    return pl.pallas_call(
        functools.partial(_post_kernel, n_ctx_tiles=n_ctx_tiles, hidden=hidden),
        out_shape=jax.ShapeDtypeStruct((b, t, d), F32),
        grid=(b, t // tm),
        in_specs=[_tok_spec(tm, d), _tok_spec(tm, half), _tok_spec(tm, half), lat, ctx,
                  _const_spec((1, d)), _const_spec((2 * half, d)), _const_spec((d, 2 * hidden)),
                  _const_spec((hidden, d))],
        out_specs=_tok_spec(tm, d),
        compiler_params=_params("arbitrary", "arbitrary"),
        name="out_proj_ffn",
    )(h, a, bb, mods, mods, g.reshape(1, d), wo, wgu, wd)


def _hgrn_tables(tile):
    c = HG_CHUNK
    r = np.arange(c)
    j = r[None, :]
    rt = np.arange(tile)
    blocks = [(j <= r[:, None]), (j > r[:, None])]
    masks, bms = [], []
    m = c
    while m >= 2:
        half = m // 2
        p = r % m
        mid = (r - p + half)[:, None]
        upper = (p >= half)[:, None]
        up = (j >= mid) & (j <= r[:, None]) & upper
        lo = (j > r[:, None]) & (j <= mid - 1) & (~upper)
        blocks.append(up | lo)
        masks.append(np.broadcast_to(upper, (c, LANES)))
        bms.append((rt[:, None] // m) == (rt[None, :] // m))
        m //= 2
    fwd = np.stack(blocks).astype(np.float32)
    bwd = fwd[:, ::-1, ::-1]
    mall = np.stack([fwd.reshape(-1, c), bwd.reshape(-1, c)])
    mall = np.concatenate([mall, mall], axis=2)
    mu = np.stack(masks).astype(np.float32)
    masku = np.tile(np.stack([mu, mu[:, ::-1]]), (1, 1, tile // c, 1))
    return (jnp.asarray(mall, dtype=BF16), jnp.asarray(masku, dtype=F32),
            jnp.asarray(np.stack(bms), dtype=F32))


def _hgrn_kernel(q_ref, ff_ref, fb_ref, i_ref, g_ref, lbp_ref, ng_ref, mall_ref, masku_ref, bm_ref,
                 o_ref, of_ref, ob_ref, *, layer, n_ctx_tiles, n_tiles):
    c = HG_CHUNK
    dk = HG_HEAD_DIM
    n_levels = masku_ref.shape[1]
    tile = masku_ref.shape[2]
    n_sub = tile // c
    lbp = lbp_ref[...]
    ex = jnp.exp(lbp - jnp.max(lbp, axis=0, keepdims=True))
    p = ex / jnp.sum(ex, axis=0, keepdims=True)
    lb = jnp.sum(p[:layer + 1], axis=0) - p[0]

    def sub(x, g):
        return x[g * c:(g + 1) * c]

    def run_tile(dirn, ti, st):
        rows = pl.ds(pl.multiple_of(ti * tile, tile), tile)
        q = q_ref[0, rows, :]
        zf = (ff_ref if dirn == 0 else fb_ref)[0, rows, :]
        v = i_ref[0, rows, :]
        lbd = lb[dirn:dirn + 1]
        f = lbd + (1.0 - lbd) * jax.nn.sigmoid(zf)
        lf = jnp.log(f)
        k = 1.0 - f
        hi = lf.astype(BF16)
        lo = (lf - hi.astype(F32)).astype(BF16)
        x = jnp.concatenate([jnp.concatenate([sub(hi, g) for g in range(n_sub)], axis=1),
                             jnp.concatenate([sub(lo, g) for g in range(n_sub)], axis=1)], axis=0)
        e2 = _dot(mall_ref[dirn], x)
        ee = [jnp.exp(e2[:, g * dk:(g + 1) * dk]) for g in range(n_sub)]

        def block(idx):
            return jnp.concatenate([ee[g][idx * c:(idx + 1) * c] for g in range(n_sub)], axis=0)

        g_in = block(0)
        qin = (q * g_in).astype(BF16)
        kout = (k * block(1)).astype(BF16)
        vb = v.astype(BF16)
        a = jnp.zeros((tile, tile), F32)
        for lvl in range(n_levels):
            gl = block(2 + lvl)
            gu = gl * masku_ref[dirn, lvl]
            a = a + bm_ref[lvl] * _dot_nt((q * gu).astype(BF16), (k * (gl - gu)).astype(BF16))
        o = _dot(a.astype(BF16), vb) + jnp.sum(q * k, axis=-1, keepdims=True) * v
        outs = [None] * n_sub
        for g in (range(n_sub) if dirn == 0 else reversed(range(n_sub))):
            outs[g] = sub(o, g) + _dot_nt(sub(qin, g), st.astype(BF16))
            last = (g + 1) * c - 1 if dirn == 0 else g * c
            st = st * g_in[last:last + 1] + _dot_tn(sub(vb, g), sub(kout, g))
        return rows, jnp.concatenate(outs, axis=0), st

    def body(i, carry):
        sf, sb = carry
        tb = jnp.where(i < n_ctx_tiles, n_ctx_tiles - 1 - i, n_tiles - 1 - (i - n_ctx_tiles))
        rows, o, sf = run_tile(0, i, sf)
        of_ref[rows, :] = o
        rows, o, sb = run_tile(1, tb, sb)
        ob_ref[rows, :] = o
        return sf, sb

    s0 = jnp.zeros((dk, dk), F32)
    lax.fori_loop(0, n_tiles, body, (s0, s0), unroll=HGRN_UNROLL)
    o = _rms(of_ref[...] + ob_ref[...]) * ng_ref[...]
    o_ref[0] = (o * _silu(g_ref[0])).astype(o_ref.dtype)


def _hgrn(z, hg_lower_bound, ng, layer, n_heads, ctx_len):
    b, t, _ = z.shape
    dk = HG_HEAD_DIM
    n_even = hg_lower_bound.shape[0]
    tile = math.gcd(math.gcd(ctx_len, t - ctx_len), 256)
    mall, masku, bm = _hgrn_tables(tile)

    def col(kind):
        return pl.BlockSpec((1, t, dk), lambda bi, hi: (bi, 0, kind * n_heads + hi))

    def const(arr):
        zeros = (0,) * arr.ndim
        return pl.BlockSpec(arr.shape, lambda bi, hi: zeros)

    return pl.pallas_call(
        functools.partial(_hgrn_kernel, layer=layer, n_ctx_tiles=ctx_len // tile, n_tiles=t // tile),
        out_shape=jax.ShapeDtypeStruct((b, t, n_heads * dk), BF16),
        grid=(b, n_heads),
        in_specs=[col(0), col(1), col(2), col(3), col(4),
                  pl.BlockSpec((n_even, 2, dk), lambda bi, hi: (0, 0, hi)),
                  const(ng.reshape(1, dk)), const(mall), const(masku), const(bm)],
        out_specs=pl.BlockSpec((1, t, dk), lambda bi, hi: (bi, 0, hi)),
        scratch_shapes=[pltpu.VMEM((t, dk), F32), pltpu.VMEM((t, dk), F32)],
        compiler_params=_params("arbitrary", "arbitrary"),
        name="hgrn2_scan",
    )(z, z, z, z, z, hg_lower_bound, ng.reshape(1, dk), mall, masku, bm)


def _hy_pre_kernel(x0_ref, x1_ref, v_ref, w0_ref, w1_ref, wv_ref, b0_ref, b1_ref, bv_ref,
                   x0c_ref, u_ref, *, ctx_len):
    t = x0_ref.shape[1]
    row = lax.broadcasted_iota(jnp.int32, (t, LANES), 0)
    first = (row == 0) | (row == ctx_len)
    final = (row == ctx_len - 1) | (row == t - 1)

    def short_conv(z_ref, w_ref, b_ref):
        z = z_ref[0]
        prev = jnp.where(first, 0.0, pltpu.roll(z, 1, 0))
        nxt = jnp.where(final, 0.0, pltpu.roll(z, t - 1, 0))
        w = w_ref[...]
        return prev * w[0:1] + z * w[1:2] + nxt * w[2:3] + b_ref[...]

    x0c_ref[0] = short_conv(x0_ref, w0_ref, b0_ref).astype(x0c_ref.dtype)
    u = short_conv(x1_ref, w1_ref, b1_ref) * short_conv(v_ref, wv_ref, bv_ref)
    u_ref[0] = u.astype(u_ref.dtype)


def _hy_pre(z, short_w, short_b, col0, width, ctx_len):
    b, t, _ = z.shape
    nb = width // LANES
    c0 = col0 // LANES

    def zcol(kind):
        return pl.BlockSpec((1, t, LANES), lambda bi, j: (bi, 0, c0 + kind * nb + j))

    def wcol(kind, rows):
        return pl.BlockSpec((rows, LANES), lambda bi, j: (0, kind * nb + j))

    out = jax.ShapeDtypeStruct((b, t, width), BF16)
    ospec = pl.BlockSpec((1, t, LANES), lambda bi, j: (bi, 0, j))
    sb = short_b.reshape(1, -1)
    return pl.pallas_call(
        functools.partial(_hy_pre_kernel, ctx_len=ctx_len),
        out_shape=(out, out),
        grid=(b, nb),
        in_specs=[zcol(0), zcol(1), zcol(2), wcol(0, 3), wcol(1, 3), wcol(2, 3),
                  wcol(0, 1), wcol(1, 1), wcol(2, 1)],
        out_specs=(ospec, ospec),
        compiler_params=_params("arbitrary", "arbitrary"),
        name="hyena_short_conv",
    )(z, z, z, short_w, short_w, short_w, sb, sb, sb)


def _filter_features(length, n):
    p = np.arange(n)
    is_f = p < length
    is_b = p > n - length
    lag = np.where(is_f, p, np.where(is_b, n - 1 - p, 0))
    tt = np.linspace(0.0, 1.0, length, dtype=np.float32)[lag][:, None]
    w = (2.0 * math.pi * lag.astype(np.float32) / length)[:, None].astype(np.float32)
    bands = np.linspace(1e-4, HY_BANDS - 1, HY_BANDS, dtype=np.float32)[None, :]
    feat = np.concatenate([tt, np.cos(bands * w), -np.sin(bands * w), is_f[:, None], is_b[:, None]],
                          axis=-1).astype(np.float32)
    pad = (-feat.shape[1]) % SUBLANES
    return jnp.asarray(np.pad(feat, ((0, 0), (0, pad))))


def _filt_kernel(feat_ref, w1_ref, b1_ref, fr1_ref, w2_ref, b2_ref, fr2_ref,
                 w3f_ref, w3b_ref, dl_ref, o_ref, hid_ref):
    z = feat_ref[...]
    mf = z[:, HY_EMB:HY_EMB + 1]
    mb = z[:, HY_EMB + 1:HY_EMB + 2]

    @pl.when(pl.program_id(0) == 0)
    def _():
        h1 = jnp.sin(fr1_ref[...] * (_dot(z, w1_ref[...], HIGHEST) + b1_ref[...]))
        hid_ref[...] = jnp.sin(fr2_ref[...] * (_dot(h1, w2_ref[...], HIGHEST) + b2_ref[...]))

    hid = hid_ref[...]
    hf = _dot(hid, w3f_ref[...], HIGHEST)
    hb = _dot(hid, w3b_ref[...], HIGHEST)
    win = jnp.exp(-z[:, 0:1] * dl_ref[...])
    f = (mf * hf + mb * hb) * win
    o_ref[...] = f / jnp.sum(jnp.abs(f), axis=0, keepdims=True)


def _hyena_filter(length, n, w1, b1, fr1, w2, b2, fr2, w3, width):
    feat = _filter_features(length, n)
    nf = feat.shape[1]
    hid = w1.shape[1]
    w1p = jnp.pad(w1, ((0, nf - w1.shape[0]), (0, 0)))
    d_lo = -math.log(HY_TARGET) / HY_GENTLE_PCT
    d_hi = -math.log(HY_TARGET) / HY_STEEP_PCT
    deltas = jnp.asarray(np.linspace(d_lo, d_hi, width, dtype=np.float32)[None, :])
    nb = width // LANES

    def full(shape):
        zeros = (0,) * len(shape)
        return pl.BlockSpec(shape, lambda j: zeros)

    return pl.pallas_call(
        _filt_kernel,
        out_shape=jax.ShapeDtypeStruct((n, width), F32),
        grid=(nb,),
        in_specs=[full((n, nf)), full((nf, hid)), full((1, hid)),
                  full((1, hid)), full((hid, hid)), full((1, hid)), full((1, hid)),
                  pl.BlockSpec((hid, LANES), lambda j: (0, j)),
                  pl.BlockSpec((hid, LANES), lambda j: (0, nb + j)),
                  pl.BlockSpec((1, LANES), lambda j: (0, j))],
        out_specs=pl.BlockSpec((n, LANES), lambda j: (0, j)),
        scratch_shapes=[pltpu.VMEM((n, hid), F32)],
        compiler_params=_params("arbitrary"),
        name="hyena_filter_mlp",
    )(feat, w1p, b1.reshape(1, -1), fr1.reshape(1, -1), w2, b2.reshape(1, -1),
      fr2.reshape(1, -1), w3, w3, deltas)


def _dft_tables(n1, n1_in):
    n2 = FFT_N2
    n = n1 * n2
    a = np.arange(n1)
    j = np.arange(n2)
    ang = -2.0 * np.pi * (a[None, None, :] * a[None, :, None] / n1 + j[:, None, None] * a[None, :, None] / n)
    tr, ti = np.cos(ang), np.sin(ang)
    fwd_a = np.concatenate([np.concatenate([tr, -ti], 2), np.concatenate([ti, tr], 2)], 1)
    trt, tit = np.swapaxes(tr, 1, 2), -np.swapaxes(ti, 1, 2)
    inv_a = np.concatenate([np.concatenate([trt, -tit], 2), np.concatenate([tit, trt], 2)], 1)
    keep = np.concatenate([np.arange(n1_in), n1 + np.arange(n1_in)])
    ang2 = -2.0 * np.pi * (j[:, None] * j[None, :]) / n2
    cr, ci = np.cos(ang2), np.sin(ang2)
    fwd_c = np.block([[cr, -ci], [ci, cr]])
    inv_c = np.block([[cr, ci], [-ci, cr]])
    real_a = np.concatenate([tr, ti], 1)
    return dict(fwd_a=fwd_a[:, :, keep], inv_a=inv_a[:, keep, :], fwd_c=fwd_c, inv_c=inv_c,
                real_a=real_a)


def _slab(idx):
    return pl.ds(pl.multiple_of(idx * FFT_PITCH, SUBLANES), FFT_N2)


def _fft_filter_kernel(f_ref, wa_ref, wc_ref, o_ref, scr_ref, *, n1):
    n2 = FFT_N2
    rows = 2 * n1
    scale = 1.0 / (n1 * n2)

    def stage_a(j, carry):
        scr_ref[pl.ds(j, rows, stride=FFT_PITCH), :] = _dot(wa_ref[j], f_ref[j], HIGHEST)
        return carry

    lax.fori_loop(0, n2, stage_a, 0, unroll=FFT_UNROLL)

    def stage_c(k, carry):
        x = jnp.concatenate([scr_ref[_slab(k), :], scr_ref[_slab(n1 + k), :]], axis=0)
        o_ref[k] = _dot(wc_ref[...], x, HIGHEST) * scale
        return carry

    lax.fori_loop(0, n1, stage_c, 0, unroll=FFT_UNROLL)


def _fft_filter(filt, n1, tables):
    n2 = FFT_N2
    width = filt.shape[1]
    ft = filt.reshape(n1, n2, width).transpose(1, 0, 2)
    wa = jnp.asarray(tables["real_a"], dtype=F32)
    wc = jnp.asarray(tables["fwd_c"], dtype=F32)
    return pl.pallas_call(
        functools.partial(_fft_filter_kernel, n1=n1),
        out_shape=jax.ShapeDtypeStruct((n1, 2 * n2, width), F32),
        grid=(width // LANES,),
        in_specs=[pl.BlockSpec((n2, n1, LANES), lambda c: (0, 0, c)),
                  pl.BlockSpec(wa.shape, lambda c: (0, 0, 0)),
                  pl.BlockSpec(wc.shape, lambda c: (0, 0))],
        out_specs=pl.BlockSpec((n1, 2 * n2, LANES), lambda c: (0, 0, c)),
        scratch_shapes=[pltpu.VMEM((2 * n1 * FFT_PITCH, LANES), F32)],
        compiler_params=_params("arbitrary"),
        name="hyena_filter_dft",
    )(ft, wa, wc)


def _fftconv_kernel(u_ref, x0_ref, fh_ref, skip_ref, wfa_ref, wfc_ref, wic_ref, wia_ref, o_ref,
                    scr_ref, *, n1):
    n2 = FFT_N2
    rows = 2 * n1

    def stage_a(j, carry):
        scr_ref[pl.ds(j, rows, stride=FFT_PITCH), :] = _dot(wfa_ref[j], u_ref[0, j])
        return carry

    lax.fori_loop(0, n2, stage_a, 0, unroll=FFT_UNROLL)

    def stage_c(k, carry):
        x = jnp.concatenate([scr_ref[_slab(k), :], scr_ref[_slab(n1 + k), :]], axis=0)
        xf = _dot(wfc_ref[...], x.astype(BF16))
        fh = fh_ref[k]
        xr, xi, fr, fi = xf[:n2], xf[n2:], fh[:n2], fh[n2:]
        y = jnp.concatenate([xr * fr - xi * fi, xr * fi + xi * fr], axis=0)
        zt = _dot(wic_ref[...], y.astype(BF16))
        scr_ref[_slab(k), :] = zt[:n2]
        scr_ref[_slab(n1 + k), :] = zt[n2:]
        return carry

    lax.fori_loop(0, n1, stage_c, 0, unroll=FFT_UNROLL)

    def stage_ai(j, carry):
        x = scr_ref[pl.ds(j, rows, stride=FFT_PITCH), :]
        y = _dot(wia_ref[j], x.astype(BF16))
        u = u_ref[0, j].astype(F32)
        o_ref[0, j] = (x0_ref[0, j].astype(F32) * (y + skip_ref[...] * u)).astype(o_ref.dtype)
        return carry

    lax.fori_loop(0, n2, stage_ai, 0, unroll=FFT_UNROLL)


def _to_fft_layout(x, n1_in):
    b, length, c = x.shape
    n1_used = length // FFT_N2
    x = x.reshape(b // 2, 2, n1_used, FFT_N2, c)
    x = jnp.pad(x, ((0, 0), (0, 0), (0, n1_in - n1_used), (0, 0), (0, 0)))
    return x.transpose(0, 3, 1, 2, 4).reshape(b // 2, FFT_N2, 2 * n1_in, c)


def _from_fft_layout(y, length):
    p, n2, rows, c = y.shape
    n1_in = rows // 2
    y = y.reshape(p, n2, 2, n1_in, c).transpose(0, 2, 3, 1, 4)
    return y.reshape(2 * p, n1_in * n2, c)[:, :length]


def _fftconv(u, x0c, fh, skip, n1, n1_in, tables):
    _, length, width = u.shape
    ut = _to_fft_layout(u, n1_in)
    xt = _to_fft_layout(x0c, n1_in)
    pairs, n2, rin, _ = ut.shape
    wfa = jnp.asarray(tables["fwd_a"], dtype=BF16)
    wia = jnp.asarray(tables["inv_a"], dtype=BF16)
    wfc = jnp.asarray(tables["fwd_c"], dtype=BF16)
    wic = jnp.asarray(tables["inv_c"], dtype=BF16)
    data = pl.BlockSpec((1, n2, rin, LANES), lambda c, p: (p, 0, 0, c))

    def const(arr):
        zeros = (0,) * arr.ndim
        return pl.BlockSpec(arr.shape, lambda c, p: zeros, pipeline_mode=pl.Buffered(1))

    out = pl.pallas_call(
        functools.partial(_fftconv_kernel, n1=n1),
        out_shape=jax.ShapeDtypeStruct(ut.shape, BF16),
        grid=(width // LANES, pairs),
        in_specs=[data, data,
                  pl.BlockSpec((n1, 2 * n2, LANES), lambda c, p: (0, 0, c)),
                  pl.BlockSpec((1, LANES), lambda c, p: (0, c)),
                  const(wfa), const(wfc), const(wic), const(wia)],
        out_specs=data,
        scratch_shapes=[pltpu.VMEM((2 * n1 * FFT_PITCH, LANES), F32)],
        compiler_params=_params("arbitrary", "arbitrary"),
        name="hyena_dft_conv",
    )(ut, xt, fh, skip.reshape(1, width), wfa, wfc, wic, wia)
    return _from_fft_layout(out, length)


def _hyena(z, col0, width, ctx_len, short_w, short_b, filt_params, skip):
    x0c, u = _hy_pre(z, short_w, short_b, col0, width, ctx_len)
    outs = []
    for lo, hi in ((0, ctx_len), (ctx_len, z.shape[1])):
        length = hi - lo
        n1_in = max(length // FFT_N2, SUBLANES)
        n1 = 2 * n1_in
        tables = _dft_tables(n1, n1_in)
        filt = _hyena_filter(length, n1 * FFT_N2, *filt_params, width)
        fh = _fft_filter(filt, n1, tables)
        outs.append(_fftconv(u[:, lo:hi], x0c[:, lo:hi], fh, skip, n1, n1_in, tables))
    return jnp.concatenate(outs, axis=1)


def _rope_tables(n_lat, ctx_len):
    tok = np.arange(n_lat)
    row, colp = tok // GRID_W, tok % GRID_W

    def axial(half):
        inv = ROPE_BASE ** (-np.arange(half, dtype=np.float32) / half)
        parts_c, parts_s = [], []
        for pos in (row, colp):
            ang = pos.astype(np.float32)[:, None] * inv
            parts_c += [np.cos(ang), np.cos(ang)]
            parts_s += [-np.sin(ang), np.sin(ang)]
        return np.concatenate(parts_c, 1), np.concatenate(parts_s, 1)

    dc, ds = axial(DA_HEAD_DIM // 4)
    mc, ms = axial(MLA_ROPE // 4)
    ones, zeros = np.ones((n_lat, MLA_NOPE), np.float32), np.zeros((n_lat, MLA_NOPE), np.float32)
    padc = np.ones((n_lat, LANES - MLA_NOPE - MLA_ROPE), np.float32)
    tabs = [np.concatenate([dc, dc], 1), np.concatenate([ds, ds], 1),
            np.concatenate([ones, mc, padc], 1), np.concatenate([zeros, ms, 0 * padc], 1)]
    out = []
    for i, tb in enumerate(tabs):
        ctx_rows = np.ones((ctx_len, LANES), np.float32) if i % 2 == 0 else np.zeros((ctx_len, LANES), np.float32)
        out.append(jnp.asarray(np.concatenate([ctx_rows, tb.astype(np.float32)], 0)))
    return out


def _swap_perm(width, group, half):
    idx = np.arange(width)
    pos = idx % group
    return np.where((pos % (2 * half)) < half, idx + half, idx - half)


def _with_ones(v):
    ones = jnp.ones((v.shape[0], LANES), v.dtype)
    parts = []
    for hd in range(v.shape[1] // LANES):
        parts += [v[:, hd * LANES:(hd + 1) * LANES], ones]
    return jnp.concatenate(parts, axis=1)


def _in_odd_kernel(x_ref, mod_ref, modc_ref, g_ref, w_ref, dc_ref, ds_ref, mc_ref, ms_ref,
                   qg_ref, kvg_ref, wuq_ref, wukv_ref,
                   qd_ref, kd_ref, vd_ref, qm_ref, km_ref, vm_ref, *, n_ctx_tiles, da_w, q_rank, kv_rank):
    m = _tile_mod(mod_ref, modc_ref, n_ctx_tiles)
    xn = (_rms(x_ref[0]) * g_ref[...]) * (1.0 + m[1:2]) + m[0:1]
    z = _dot(xn.astype(BF16), w_ref[...])
    nrep = da_w // LANES
    dc = jnp.concatenate([dc_ref[...]] * nrep, axis=1)
    ds = jnp.concatenate([ds_ref[...]] * nrep, axis=1)
    mc = jnp.concatenate([mc_ref[...]] * MLA_HEADS, axis=1)
    ms = jnp.concatenate([ms_ref[...]] * MLA_HEADS, axis=1)
    sa = DA_HEAD_DIM ** -0.5 * LOG2_E
    sm = (MLA_NOPE + MLA_ROPE) ** -0.5 * LOG2_E
    o = 0
    qd_ref[0] = ((z[:, o:o + da_w] * dc + z[:, o + da_w:o + 2 * da_w] * ds) * sa).astype(BF16)
    o += 2 * da_w
    kd_ref[0] = (z[:, o:o + da_w] * dc + z[:, o + da_w:o + 2 * da_w] * ds).astype(BF16)
    o += 2 * da_w
    vd_ref[0] = _with_ones(z[:, o:o + da_w]).astype(BF16)
    o += da_w
    cq = _rms(z[:, o:o + q_rank]) * qg_ref[...]
    o += q_rank
    ckv = _rms(z[:, o:o + kv_rank]) * kvg_ref[...]
    o += kv_rank
    kr = z[:, o:o + LANES] * mc_ref[...] + z[:, o + LANES:o + 2 * LANES] * ms_ref[...]
    mw = MLA_HEADS * LANES
    qu = _dot(cq.astype(BF16), wuq_ref[...])
    qm_ref[0] = ((qu[:, :mw] * mc + qu[:, mw:] * ms) * sm).astype(BF16)
    kvu = _dot(ckv.astype(BF16), wukv_ref[...])
    km_ref[0] = (kvu[:, :mw] + jnp.concatenate([kr] * MLA_HEADS, axis=1)).astype(BF16)
    vm_ref[0] = _with_ones(kvu[:, mw:]).astype(BF16)


def _odd_weights(w_in, w_uq, w_ukv, da_w, q_rank, kv_rank):
    qw, kw, vw = w_in[:, :da_w], w_in[:, da_w:2 * da_w], w_in[:, 2 * da_w:3 * da_w]
    o = 3 * da_w
    cqw, ckvw, krw = w_in[:, o:o + q_rank], w_in[:, o + q_rank:o + q_rank + kv_rank], w_in[:, o + q_rank + kv_rank:]
    perm_da = _swap_perm(da_w, DA_HEAD_DIM // 2, DA_HEAD_DIM // 4)
    perm_r = _swap_perm(MLA_ROPE, MLA_ROPE // 2, MLA_ROPE // 4)
    d = w_in.shape[0]

    def rope_group(wr):
        return jnp.pad(wr, ((0, 0), (MLA_NOPE, LANES - MLA_NOPE - MLA_ROPE)))

    w_big = jnp.concatenate([qw, qw[:, perm_da], kw, kw[:, perm_da], vw, cqw, ckvw,
                             rope_group(krw), rope_group(krw[:, perm_r])], axis=1)
    dq = MLA_NOPE + MLA_ROPE
    pad = LANES - dq
    uq = w_uq.reshape(q_rank, MLA_HEADS, dq)
    uq_a = jnp.pad(uq, ((0, 0), (0, 0), (0, pad))).reshape(q_rank, MLA_HEADS * LANES)
    uq_s = jnp.pad(uq[:, :, MLA_NOPE:][:, :, perm_r], ((0, 0), (0, 0), (MLA_NOPE, pad)))
    uq_s = uq_s.reshape(q_rank, MLA_HEADS * LANES)
    ukv = w_ukv.reshape(kv_rank, MLA_HEADS, MLA_NOPE + MLA_V)
    uk = jnp.pad(ukv[:, :, :MLA_NOPE], ((0, 0), (0, 0), (0, LANES - MLA_NOPE))).reshape(kv_rank, MLA_HEADS * LANES)
    uv = ukv[:, :, MLA_NOPE:].reshape(kv_rank, MLA_HEADS * MLA_V)
    return (w_big.astype(BF16), jnp.concatenate([uq_a, uq_s], axis=1).astype(BF16),
            jnp.concatenate([uk, uv], axis=1).astype(BF16))


def _in_odd(h, mods, g, w_big, wuq, wukv, qg, kvg, tabs, tm, n_ctx_tiles, da_w):
    b, t, d = h.shape
    q_rank, kv_rank = qg.shape[0], kvg.shape[0]
    lat, ctx = _mod_specs(d, b)
    tab = pl.BlockSpec((tm, LANES), lambda bi, ti: (ti, 0))
    mw = MLA_HEADS * LANES

    def tok_major(width):
        return jax.ShapeDtypeStruct((b, t, width), BF16), _tok_spec(tm, width)

    outs, ospecs = zip(tok_major(da_w), tok_major(da_w), tok_major(2 * da_w),
                       tok_major(mw), tok_major(mw), tok_major(2 * mw))
    return pl.pallas_call(
        functools.partial(_in_odd_kernel, n_ctx_tiles=n_ctx_tiles, da_w=da_w, q_rank=q_rank, kv_rank=kv_rank),
        out_shape=tuple(outs),
        grid=(b, t // tm),
        in_specs=[_tok_spec(tm, d), lat, ctx, _const_spec((1, d)), _const_spec(w_big.shape),
                  tab, tab, tab, tab, _const_spec((1, q_rank)), _const_spec((1, kv_rank)),
                  _const_spec(wuq.shape), _const_spec(wukv.shape)],
        out_specs=tuple(ospecs),
        compiler_params=_params("arbitrary", "arbitrary"),
        name="odd_in_proj",
    )(h, mods, mods, g.reshape(1, d), w_big, *tabs, qg.reshape(1, -1), kvg.reshape(1, -1), wuq, wukv)


def _softmax_pv(qs, k_ref, v_ref):
    n_keys = k_ref.shape[1]
    starts = list(range(0, n_keys, ATTN_KEY_BLOCK))
    m = [None] * len(qs)
    acc = [None] * len(qs)
    for start in starts:
        size = min(ATTN_KEY_BLOCK, n_keys - start)
        k = k_ref[0, start:start + size, :]
        v1 = v_ref[0, start:start + size, :]
        for i, q in enumerate(qs):
            s = _dot_nt(q, k)
            m_blk = jnp.max(s, axis=-1, keepdims=True)
            if start == 0:
                m[i] = m_blk
                acc[i] = _dot(jnp.exp2(s - m_blk).astype(BF16), v1)
            else:
                m_new = jnp.maximum(m[i], m_blk)
                pv = _dot(jnp.exp2(s - m_new).astype(BF16), v1)
                acc[i] = jnp.exp2(m[i] - m_new) * acc[i] + pv
                m[i] = m_new
    return [a[:, :LANES] / a[:, LANES:] for a in acc]


def _attn_kernel(q_ref, k_ref, v_ref, lam_ref, sg_ref, o_ref, *, diff, lam_init):
    q = q_ref[0]
    if not diff:
        o_ref[0] = _softmax_pv([q], k_ref, v_ref)[0].astype(o_ref.dtype)
        return
    first = lax.broadcasted_iota(jnp.int32, q.shape, 1) < DA_HEAD_DIM
    zero = jnp.zeros_like(q)
    o1, o2 = _softmax_pv([jnp.where(first, q, zero), jnp.where(first, zero, q)], k_ref, v_ref)
    lp = lam_ref[...]
    lam = (jnp.exp(jnp.sum(lp[0:1] * lp[1:2], axis=-1, keepdims=True))
           - jnp.exp(jnp.sum(lp[2:3] * lp[3:4], axis=-1, keepdims=True)) + lam_init)
    o_ref[0] = (_rms(o1 - lam * o2) * sg_ref[...] * (1.0 - lam_init)).astype(o_ref.dtype)


def _attention(q, k, v1, lam_p, subln_g, tq, n_keys, diff, lam_init):
    b, t_q, width = q.shape
    heads = width // LANES
    qtile = pl.BlockSpec((1, tq, LANES), lambda bi, hi, ti: (bi, ti, hi))
    khead = pl.BlockSpec((1, n_keys, LANES), lambda bi, hi, ti: (bi, 0, hi))
    vhead = pl.BlockSpec((1, n_keys, 2 * LANES), lambda bi, hi, ti: (bi, 0, hi))

    def const(shape):
        zeros = (0,) * len(shape)
        return pl.BlockSpec(shape, lambda bi, hi, ti: zeros)

    return pl.pallas_call(
        functools.partial(_attn_kernel, diff=diff, lam_init=lam_init),
        out_shape=jax.ShapeDtypeStruct((b, t_q, width), BF16),
        grid=(b, heads, t_q // tq),
        in_specs=[qtile, khead, vhead, const(lam_p.shape), const((1, LANES))],
        out_specs=qtile,
        compiler_params=_params("arbitrary", "arbitrary", "arbitrary"),
        name="diff_attention" if diff else "mla_attention",
    )(q, k, v1, lam_p, subln_g.reshape(1, LANES))


def _attend_segments(q, k, v1, lam_p, subln_g, ctx_len, diff, lam_init):
    t = k.shape[1]
    n_lat = t - ctx_len
    tq_lat = math.gcd(n_lat, 512)
    o_ctx = _attention(q[:, :ctx_len], k, v1, lam_p, subln_g, ctx_len, ctx_len, diff, lam_init)
    o_lat = _attention(q[:, ctx_len:], k, v1, lam_p, subln_g, tq_lat, t, diff, lam_init)
    return jnp.concatenate([o_ctx, o_lat], axis=1)


def _final_kernel(x_ref, g_ref, o_ref):
    o_ref[0] = _rms(x_ref[0]) * g_ref[...]


def _final_norm(h, g, tm, n_ctx_tiles):
    b, t, d = h.shape
    return pl.pallas_call(
        _final_kernel,
        out_shape=jax.ShapeDtypeStruct((b, t - n_ctx_tiles * tm, d), F32),
        grid=(b, t // tm - n_ctx_tiles),
        in_specs=[pl.BlockSpec((1, tm, d), lambda bi, ti: (bi, ti + n_ctx_tiles, 0)), _const_spec((1, d))],
        out_specs=_tok_spec(tm, d),
        compiler_params=_params("arbitrary", "arbitrary"),
        name="final_norm",
    )(h, g.reshape(1, d))


def kernel(x, c, ctx, c_ctx, ada_w, ada_b, norm_mix_g, norm_ffn_g, ffn_w_gu, ffn_w_down, ev_w_in, ev_w_out, hg_lower_bound, hg_out_norm_g, hy_short_w, hy_short_b, hy_filt_w1, hy_filt_b1, hy_filt_freq1, hy_filt_w2, hy_filt_b2, hy_filt_freq2, hy_filt_w3, hy_skip, od_w_in, od_w_out, da_lambda, da_subln_g, mla_q_norm_g, mla_w_uq, mla_kv_norm_g, mla_w_ukv, final_norm_g):
    n_batch, n_lat, d = x.shape
    ctx_len = ctx.shape[1]
    depth = ada_w.shape[0]
    assert n_batch % 2 == 0 and n_batch < MOD_ROWS
    assert n_lat % GRID_W == 0 and ctx_len % HG_CHUNK == 0 and n_lat % FFT_N2 == 0 and ctx_len % FFT_N2 == 0
    tm = math.gcd(math.gcd(ctx_len, n_lat), 256)
    n_ctx_tiles = ctx_len // tm
    hg_width = d // 2
    hy_width = d - hg_width
    da_w = d // 2
    q_rank, kv_rank = mla_q_norm_g.shape[1], mla_kv_norm_g.shape[1]

    cc = jnp.concatenate([c, c_ctx[None], jnp.zeros((MOD_ROWS - n_batch - 1, d), F32)], axis=0)
    mods = _ada(cc, ada_w, ada_b).reshape(depth, MOD_ROWS, 6, d)
    h = jnp.concatenate([ctx, x], axis=1)
    rope_tabs = _rope_tables(n_lat, ctx_len)

    for i in range(depth):
        if i % 2 == 0:
            e = i // 2
            z = _in_even(h, mods[i], norm_mix_g[i], ev_w_in[e].astype(BF16), tm, n_ctx_tiles)
            a = _hgrn(z, hg_lower_bound, hg_out_norm_g[e], e, hg_width // HG_HEAD_DIM, ctx_len)
            filt_params = (hy_filt_w1[e], hy_filt_b1[e], hy_filt_freq1[e], hy_filt_w2[e], hy_filt_b2[e],
                           hy_filt_freq2[e], hy_filt_w3[e])
            bb = _hyena(z, 5 * hg_width, hy_width, ctx_len, hy_short_w[e], hy_short_b[e], filt_params, hy_skip[e])
            w_out = ev_w_out[e]
        else:
            o = i // 2
            lam_init = 0.8 - 0.6 * math.exp(-0.3 * i)
            w_big, wuq, wukv = _odd_weights(od_w_in[o], mla_w_uq[o], mla_w_ukv[o], da_w, q_rank, kv_rank)
            qd, kd, vd, qm, km, vm = _in_odd(h, mods[i], norm_mix_g[i], w_big, wuq, wukv, mla_q_norm_g[o],
                                             mla_kv_norm_g[o], rope_tabs, tm, n_ctx_tiles, da_w)
            a = _attend_segments(qd, kd, vd, da_lambda[o], da_subln_g[o], ctx_len, True, lam_init)
            bb = _attend_segments(qm, km, vm, da_lambda[o], da_subln_g[o], ctx_len, False, lam_init)
            w_out = od_w_out[o]
        h = _post(h, a, bb, mods[i], norm_ffn_g[i], w_out.astype(BF16), ffn_w_gu[i].astype(BF16),
                  ffn_w_down[i].astype(BF16), tm, n_ctx_tiles)
    return _final_norm(h, final_norm_g, tm, n_ctx_tiles)
```

```python
import functools
import math

import numpy as np
import jax
import jax.numpy as jnp
from jax import lax
from jax.experimental import pallas as pl
from jax.experimental.pallas import tpu as pltpu

F32 = jnp.float32
BF16 = jnp.bfloat16
HIGHEST = lax.Precision.HIGHEST

GRID_W = 64
EPS = 1e-6
ROPE_BASE = 10000.0
HG_HEAD_DIM = 128
HG_CHUNK = 64
HY_BANDS = 16
HY_EMB = 2 * HY_BANDS + 1
HY_TARGET = 1e-2
HY_STEEP_PCT = 0.3
HY_GENTLE_PCT = 1.5
DA_HEAD_DIM = 64
MLA_HEADS = 4
MLA_NOPE = 64
MLA_ROPE = 32
MLA_V = 128
LOG2_E = 1.4426950408889634

LANES = 128
SUBLANES = 8
V7X_VMEM_LIMIT_BYTES = 56 * 1024 * 1024
MOD_ROWS = 16
FFT_N2 = 128
FFT_PITCH = FFT_N2 + SUBLANES
FFT_UNROLL = 8
HGRN_UNROLL = 2
ATTN_KEY_BLOCK = 512


def _params(*sem):
    return pltpu.CompilerParams(dimension_semantics=sem, vmem_limit_bytes=V7X_VMEM_LIMIT_BYTES)


def _dot(a, b, precision=None):
    return jnp.dot(a, b, preferred_element_type=F32, precision=precision)


def _dot_nt(a, b):
    return lax.dot_general(a, b, (((1,), (1,)), ((), ())), preferred_element_type=F32)


def _dot_tn(a, b):
    return lax.dot_general(a, b, (((0,), (0,)), ((), ())), preferred_element_type=F32)


def _rms(x):
    return x * lax.rsqrt(jnp.mean(x * x, axis=-1, keepdims=True) + EPS)


def _silu(x):
    return x * jax.nn.sigmoid(x)


def _tile_mod(mod_ref, modc_ref, n_ctx_tiles):
    return jnp.where(pl.program_id(1) < n_ctx_tiles, modc_ref[...], mod_ref[...])


def _ada_kernel(c_ref, w_ref, b_ref, o_ref):
    o_ref[0] = _dot(_silu(c_ref[...]), w_ref[0], HIGHEST) + b_ref[0]


def _ada(cc, ada_w, ada_b):
    depth, d, n = ada_w.shape
    rows = cc.shape[0]
    tn = n // 4
    return pl.pallas_call(
        _ada_kernel,
        out_shape=jax.ShapeDtypeStruct((depth, rows, n), F32),
        grid=(depth, n // tn),
        in_specs=[
            pl.BlockSpec((rows, d), lambda i, j: (0, 0)),
            pl.BlockSpec((1, d, tn), lambda i, j: (i, 0, j)),
            pl.BlockSpec((1, 1, tn), lambda i, j: (i, 0, j)),
        ],
        out_specs=pl.BlockSpec((1, rows, tn), lambda i, j: (i, 0, j)),
        compiler_params=_params("arbitrary", "arbitrary"),
        name="ada_mod",
    )(cc, ada_w, ada_b.reshape(depth, 1, n))


def _tok_spec(tm, width):
    return pl.BlockSpec((1, tm, width), lambda b, t: (b, t, 0))


def _const_spec(shape):
    zeros = (0,) * len(shape)
    return pl.BlockSpec(shape, lambda b, t: zeros)


def _mod_specs(d, n_batch):
    lat = pl.BlockSpec((None, 6, d), lambda b, t: (b, 0, 0))
    ctx = pl.BlockSpec((None, 6, d), lambda b, t: (n_batch, 0, 0))
    return lat, ctx


def _in_even_kernel(x_ref, mod_ref, modc_ref, g_ref, w_ref, o_ref, *, n_ctx_tiles):
    m = _tile_mod(mod_ref, modc_ref, n_ctx_tiles)
    xn = (_rms(x_ref[0]) * g_ref[...]) * (1.0 + m[1:2]) + m[0:1]
    o_ref[0] = _dot(xn.astype(BF16), w_ref[...])


def _in_even(h, mods, g, w, tm, n_ctx_tiles):
    b, t, d = h.shape
    n = w.shape[1]
    lat, ctx = _mod_specs(d, b)
    return pl.pallas_call(
        functools.partial(_in_even_kernel, n_ctx_tiles=n_ctx_tiles),
        out_shape=jax.ShapeDtypeStruct((b, t, n), F32),
        grid=(b, t // tm),
        in_specs=[_tok_spec(tm, d), lat, ctx, _const_spec((1, d)), _const_spec((d, n))],
        out_specs=_tok_spec(tm, n),
        compiler_params=_params("arbitrary", "arbitrary"),
        name="even_in_proj",
    )(h, mods, mods, g.reshape(1, d), w)


def _post_body(m, h, a, bb, g_ref, wo_ref, wgu_ref, wd_ref, hidden):
    half = a.shape[-1]
    y = _dot(a, wo_ref[:half, :]) + _dot(bb, wo_ref[half:, :])
    h1 = h + m[2:3] * y
    xn = (_rms(h1) * g_ref[...]) * (1.0 + m[4:5]) + m[3:4]
    gu = _dot(xn.astype(BF16), wgu_ref[...])
    act = _silu(gu[:, :hidden]) * gu[:, hidden:]
    return h1 + m[5:6] * _dot(act.astype(BF16), wd_ref[...])


def _post_kernel(h_ref, a_ref, b_ref, mod_ref, modc_ref, g_ref, wo_ref, wgu_ref, wd_ref, o_ref,
                 *, n_ctx_tiles, hidden):
    m = _tile_mod(mod_ref, modc_ref, n_ctx_tiles)
    o_ref[0] = _post_body(m, h_ref[0], a_ref[0], b_ref[0], g_ref, wo_ref, wgu_ref, wd_ref, hidden)


def _post_final_kernel(h_ref, a_ref, b_ref, mod_ref, g_ref, wo_ref, wgu_ref, wd_ref, fg_ref, o_ref,
                       *, hidden):
    h2 = _post_body(mod_ref[...], h_ref[0], a_ref[0], b_ref[0], g_ref, wo_ref, wgu_ref, wd_ref, hidden)
    o_ref[0] = _rms(h2) * fg_ref[...]


def _post(h, a, bb, mods, g, wo, wgu, wd, tm, n_ctx_tiles):
    b, t, d = h.shape
    half = a.shape[-1]
    hidden = wd.shape[0]
    lat, ctx = _mod_specs(d, b)
    return pl.pallas_call(
        functools.partial(_post_kernel, n_ctx_tiles=n_ctx_tiles, hidden=hidden),
        out_shape=jax.ShapeDtypeStruct((b, t, d), F32),
        grid=(b, t // tm),
        in_specs=[_tok_spec(tm, d), _tok_spec(tm, half), _tok_spec(tm, half), lat, ctx,
                  _const_spec((1, d)), _const_spec((2 * half, d)), _const_spec((d, 2 * hidden)),
                  _const_spec((hidden, d))],
        out_specs=_tok_spec(tm, d),
        compiler_params=_params("arbitrary", "arbitrary"),
        name="out_proj_ffn",
    )(h, a, bb, mods, mods, g.reshape(1, d), wo, wgu, wd)


def _post_final(h, a_lat, b_lat, mods, g, wo, wgu, wd, final_g, tm, n_ctx_tiles):
    b, t, d = h.shape
    n_lat = a_lat.shape[1]
    half = a_lat.shape[-1]
    hidden = wd.shape[0]
    lat, _ = _mod_specs(d, b)
    return pl.pallas_call(
        functools.partial(_post_final_kernel, hidden=hidden),
        out_shape=jax.ShapeDtypeStruct((b, n_lat, d), F32),
        grid=(b, n_lat // tm),
        in_specs=[pl.BlockSpec((1, tm, d), lambda bi, ti: (bi, ti + n_ctx_tiles, 0)),
                  _tok_spec(tm, half), _tok_spec(tm, half), lat,
                  _const_spec((1, d)), _const_spec((2 * half, d)), _const_spec((d, 2 * hidden)),
                  _const_spec((hidden, d)), _const_spec((1, d))],
        out_specs=_tok_spec(tm, d),
        compiler_params=_params("arbitrary", "arbitrary"),
        name="out_proj_ffn_final",
    )(h, a_lat, b_lat, mods, g.reshape(1, d), wo, wgu, wd, final_g.reshape(1, d))


def _hgrn_tables(tile):
    c = HG_CHUNK
    r = np.arange(c)
    j = r[None, :]
    rt = np.arange(tile)
    blocks = [(j <= r[:, None]), (j > r[:, None])]
    masks, bms = [], []
    m = c
    while m >= 2:
        half = m // 2
        p = r % m
        mid = (r - p + half)[:, None]
        upper = (p >= half)[:, None]
        up = (j >= mid) & (j <= r[:, None]) & upper
        lo = (j > r[:, None]) & (j <= mid - 1) & (~upper)
        blocks.append(up | lo)
        masks.append(np.broadcast_to(upper, (c, LANES)))
        bms.append((rt[:, None] // m) == (rt[None, :] // m))
        m //= 2
    fwd = np.stack(blocks).astype(np.float32)
    bwd = fwd[:, ::-1, ::-1]
    mall = np.stack([fwd.reshape(-1, c), bwd.reshape(-1, c)])
    mall = np.concatenate([mall, mall], axis=2)
    mu = np.stack(masks).astype(np.float32)
    masku = np.tile(np.stack([mu, mu[:, ::-1]]), (1, 1, tile // c, 1))
    return (jnp.asarray(mall, dtype=BF16), jnp.asarray(masku, dtype=F32),
            jnp.asarray(np.stack(bms), dtype=F32))


def _hgrn_kernel(q_ref, ff_ref, fb_ref, i_ref, g_ref, lbp_ref, ng_ref, mall_ref, masku_ref, bm_ref,
                 o_ref, of_ref, ob_ref, *, layer, n_ctx_tiles, n_tiles):
    c = HG_CHUNK
    dk = HG_HEAD_DIM
    n_levels = masku_ref.shape[1]
    tile = masku_ref.shape[2]
    n_sub = tile // c
    lbp = lbp_ref[...]
    ex = jnp.exp(lbp - jnp.max(lbp, axis=0, keepdims=True))
    p = ex / jnp.sum(ex, axis=0, keepdims=True)
    lb = jnp.sum(p[:layer + 1], axis=0) - p[0]

    def sub(x, g):
        return x[g * c:(g + 1) * c]

    def run_tile(dirn, ti, st):
        rows = pl.ds(pl.multiple_of(ti * tile, tile), tile)
        q = q_ref[0, rows, :]
        zf = (ff_ref if dirn == 0 else fb_ref)[0, rows, :]
        v = i_ref[0, rows, :]
        lbd = lb[dirn:dirn + 1]
        f = lbd + (1.0 - lbd) * jax.nn.sigmoid(zf)
        lf = jnp.log(f)
        k = 1.0 - f
        hi = lf.astype(BF16)
        lo = (lf - hi.astype(F32)).astype(BF16)
        x = jnp.concatenate([jnp.concatenate([sub(hi, g) for g in range(n_sub)], axis=1),
                             jnp.concatenate([sub(lo, g) for g in range(n_sub)], axis=1)], axis=0)
        e2 = _dot(mall_ref[dirn], x)
        ee = [jnp.exp(e2[:, g * dk:(g + 1) * dk]) for g in range(n_sub)]

        def block(idx):
            return jnp.concatenate([ee[g][idx * c:(idx + 1) * c] for g in range(n_sub)], axis=0)

        g_in = block(0)
        qin = (q * g_in).astype(BF16)
        kout = (k * block(1)).astype(BF16)
        vb = v.astype(BF16)
        a = jnp.zeros((tile, tile), F32)
        for lvl in range(n_levels):
            gl = block(2 + lvl)
            gu = gl * masku_ref[dirn, lvl]
            a = a + bm_ref[lvl] * _dot_nt((q * gu).astype(BF16), (k * (gl - gu)).astype(BF16))
        o = _dot(a.astype(BF16), vb) + jnp.sum(q * k, axis=-1, keepdims=True) * v
        outs = [None] * n_sub
        for g in (range(n_sub) if dirn == 0 else reversed(range(n_sub))):
            outs[g] = sub(o, g) + _dot_nt(sub(qin, g), st.astype(BF16))
            last = (g + 1) * c - 1 if dirn == 0 else g * c
            st = st * g_in[last:last + 1] + _dot_tn(sub(vb, g), sub(kout, g))
        return rows, jnp.concatenate(outs, axis=0), st

    def body(i, carry):
        sf, sb = carry
        tb = jnp.where(i < n_ctx_tiles, n_ctx_tiles - 1 - i, n_tiles - 1 - (i - n_ctx_tiles))
        rows, o, sf = run_tile(0, i, sf)
        of_ref[rows, :] = o
        rows, o, sb = run_tile(1, tb, sb)
        ob_ref[rows, :] = o
        return sf, sb

    s0 = jnp.zeros((dk, dk), F32)
    lax.fori_loop(0, n_tiles, body, (s0, s0), unroll=HGRN_UNROLL)
    o = _rms(of_ref[...] + ob_ref[...]) * ng_ref[...]
    o_ref[0] = (o * _silu(g_ref[0])).astype(o_ref.dtype)


def _hgrn(z, hg_lower_bound, ng, layer, n_heads, ctx_len):
    b, t, _ = z.shape
    dk = HG_HEAD_DIM
    n_even = hg_lower_bound.shape[0]
    tile = math.gcd(math.gcd(ctx_len, t - ctx_len), 256)
    mall, masku, bm = _hgrn_tables(tile)

    def col(kind):
        return pl.BlockSpec((1, t, dk), lambda bi, hi: (bi, 0, kind * n_heads + hi))

    def const(arr):
        zeros = (0,) * arr.ndim
        return pl.BlockSpec(arr.shape, lambda bi, hi: zeros)

    return pl.pallas_call(
        functools.partial(_hgrn_kernel, layer=layer, n_ctx_tiles=ctx_len // tile, n_tiles=t // tile),
        out_shape=jax.ShapeDtypeStruct((b, t, n_heads * dk), BF16),
        grid=(b, n_heads),
        in_specs=[col(0), col(1), col(2), col(3), col(4),
                  pl.BlockSpec((n_even, 2, dk), lambda bi, hi: (0, 0, hi)),
                  const(ng.reshape(1, dk)), const(mall), const(masku), const(bm)],
        out_specs=pl.BlockSpec((1, t, dk), lambda bi, hi: (bi, 0, hi)),
        scratch_shapes=[pltpu.VMEM((t, dk), F32), pltpu.VMEM((t, dk), F32)],
        compiler_params=_params("arbitrary", "arbitrary"),
        name="hgrn2_scan",
    )(z, z, z, z, z, hg_lower_bound, ng.reshape(1, dk), mall, masku, bm)


def _hy_pre_kernel(x0_ref, x1_ref, v_ref, w0_ref, w1_ref, wv_ref, b0_ref, b1_ref, bv_ref,
                   x0c_ref, u_ref, *, ctx_len):
    t = x0_ref.shape[1]
    row = lax.broadcasted_iota(jnp.int32, (t, LANES), 0)
    first = (row == 0) | (row == ctx_len)
    final = (row == ctx_len - 1) | (row == t - 1)

    def short_conv(z_ref, w_ref, b_ref):
        z = z_ref[0]
        prev = jnp.where(first, 0.0, pltpu.roll(z, 1, 0))
        nxt = jnp.where(final, 0.0, pltpu.roll(z, t - 1, 0))
        w = w_ref[...]
        return prev * w[0:1] + z * w[1:2] + nxt * w[2:3] + b_ref[...]

    x0c_ref[0] = short_conv(x0_ref, w0_ref, b0_ref).astype(x0c_ref.dtype)
    u = short_conv(x1_ref, w1_ref, b1_ref) * short_conv(v_ref, wv_ref, bv_ref)
    u_ref[0] = u.astype(u_ref.dtype)


def _hy_pre(z, short_w, short_b, col0, width, ctx_len):
    b, t, _ = z.shape
    nb = width // LANES
    c0 = col0 // LANES

    def zcol(kind):
        return pl.BlockSpec((1, t, LANES), lambda bi, j: (bi, 0, c0 + kind * nb + j))

    def wcol(kind, rows):
        return pl.BlockSpec((rows, LANES), lambda bi, j: (0, kind * nb + j))

    out = jax.ShapeDtypeStruct((b, t, width), BF16)
    ospec = pl.BlockSpec((1, t, LANES), lambda bi, j: (bi, 0, j))
    sb = short_b.reshape(1, -1)
    return pl.pallas_call(
        functools.partial(_hy_pre_kernel, ctx_len=ctx_len),
        out_shape=(out, out),
        grid=(b, nb),
        in_specs=[zcol(0), zcol(1), zcol(2), wcol(0, 3), wcol(1, 3), wcol(2, 3),
                  wcol(0, 1), wcol(1, 1), wcol(2, 1)],
        out_specs=(ospec, ospec),
        compiler_params=_params("arbitrary", "arbitrary"),
        name="hyena_short_conv",
    )(z, z, z, short_w, short_w, short_w, sb, sb, sb)


def _filter_features(length, n):
    p = np.arange(n)
    is_f = p < length
    is_b = p > n - length
    lag = np.where(is_f, p, np.where(is_b, n - 1 - p, 0))
    tt = np.linspace(0.0, 1.0, length, dtype=np.float32)[lag][:, None]
    w = (2.0 * math.pi * lag.astype(np.float32) / length)[:, None].astype(np.float32)
    bands = np.linspace(1e-4, HY_BANDS - 1, HY_BANDS, dtype=np.float32)[None, :]
    feat = np.concatenate([tt, np.cos(bands * w), -np.sin(bands * w), is_f[:, None], is_b[:, None]],
                          axis=-1).astype(np.float32)
    pad = (-feat.shape[1]) % SUBLANES
    return jnp.asarray(np.pad(feat, ((0, 0), (0, pad))))


def _filt_kernel(feat_ref, w1_ref, b1_ref, fr1_ref, w2_ref, b2_ref, fr2_ref,
                 w3f_ref, w3b_ref, dl_ref, o_ref, hid_ref):
    z = feat_ref[...]
    mf = z[:, HY_EMB:HY_EMB + 1]
    mb = z[:, HY_EMB + 1:HY_EMB + 2]

    @pl.when(pl.program_id(0) == 0)
    def _():
        h1 = jnp.sin(fr1_ref[...] * (_dot(z, w1_ref[...], HIGHEST) + b1_ref[...]))
        hid_ref[...] = jnp.sin(fr2_ref[...] * (_dot(h1, w2_ref[...], HIGHEST) + b2_ref[...]))

    hid = hid_ref[...]
    hf = _dot(hid, w3f_ref[...], HIGHEST)
    hb = _dot(hid, w3b_ref[...], HIGHEST)
    win = jnp.exp(-z[:, 0:1] * dl_ref[...])
    f = (mf * hf + mb * hb) * win
    o_ref[...] = f / jnp.sum(jnp.abs(f), axis=0, keepdims=True)


def _hyena_filter(length, n, w1, b1, fr1, w2, b2, fr2, w3, width):
    feat = _filter_features(length, n)
    nf = feat.shape[1]
    hid = w1.shape[1]
    w1p = jnp.pad(w1, ((0, nf - w1.shape[0]), (0, 0)))
    d_lo = -math.log(HY_TARGET) / HY_GENTLE_PCT
    d_hi = -math.log(HY_TARGET) / HY_STEEP_PCT
    deltas = jnp.asarray(np.linspace(d_lo, d_hi, width, dtype=np.float32)[None, :])
    nb = width // LANES

    def full(shape):
        zeros = (0,) * len(shape)
        return pl.BlockSpec(shape, lambda j: zeros)

    return pl.pallas_call(
        _filt_kernel,
        out_shape=jax.ShapeDtypeStruct((n, width), F32),
        grid=(nb,),
        in_specs=[full((n, nf)), full((nf, hid)), full((1, hid)),
                  full((1, hid)), full((hid, hid)), full((1, hid)), full((1, hid)),
                  pl.BlockSpec((hid, LANES), lambda j: (0, j)),
                  pl.BlockSpec((hid, LANES), lambda j: (0, nb + j)),
                  pl.BlockSpec((1, LANES), lambda j: (0, j))],
        out_specs=pl.BlockSpec((n, LANES), lambda j: (0, j)),
        scratch_shapes=[pltpu.VMEM((n, hid), F32)],
        compiler_params=_params("arbitrary"),
        name="hyena_filter_mlp",
    )(feat, w1p, b1.reshape(1, -1), fr1.reshape(1, -1), w2, b2.reshape(1, -1),
      fr2.reshape(1, -1), w3, w3, deltas)


def _dft_tables(n1, n1_in):
    n2 = FFT_N2
    n = n1 * n2
    a = np.arange(n1)
    j = np.arange(n2)
    ang = -2.0 * np.pi * (a[None, None, :] * a[None, :, None] / n1 + j[:, None, None] * a[None, :, None] / n)
    tr, ti = np.cos(ang), np.sin(ang)
    fwd_a = np.concatenate([np.concatenate([tr, -ti], 2), np.concatenate([ti, tr], 2)], 1)
    trt, tit = np.swapaxes(tr, 1, 2), -np.swapaxes(ti, 1, 2)
    inv_a = np.concatenate([np.concatenate([trt, -tit], 2), np.concatenate([tit, trt], 2)], 1)
    keep = np.concatenate([np.arange(n1_in), n1 + np.arange(n1_in)])
    ang2 = -2.0 * np.pi * (j[:, None] * j[None, :]) / n2
    cr, ci = np.cos(ang2), np.sin(ang2)
    fwd_c = np.block([[cr, -ci], [ci, cr]])
    inv_c = np.block([[cr, ci], [-ci, cr]])
    real_a = np.concatenate([tr, ti], 1)
    return dict(fwd_a=fwd_a[:, :, keep], inv_a=inv_a[:, keep, :], fwd_c=fwd_c, inv_c=inv_c,
                real_a=real_a)


def _slab(idx):
    return pl.ds(pl.multiple_of(idx * FFT_PITCH, SUBLANES), FFT_N2)


def _fft_filter_kernel(f_ref, wa_ref, wc_ref, o_ref, scr_ref, *, n1):
    n2 = FFT_N2
    rows = 2 * n1
    scale = 1.0 / (n1 * n2)

    def split3(x):
        hi = x.astype(BF16)
        lo = (x - hi.astype(F32)).astype(BF16)
        return jnp.concatenate([hi, lo, hi], axis=0)

    def stage_a(j, carry):
        scr_ref[pl.ds(j, rows, stride=FFT_PITCH), :] = _dot(wa_ref[j], split3(f_ref[j]))
        return carry

    lax.fori_loop(0, n2, stage_a, 0, unroll=FFT_UNROLL)

    def stage_c(k, carry):
        x = jnp.concatenate([scr_ref[_slab(k), :], scr_ref[_slab(n1 + k), :]], axis=0)
        o_ref[k] = _dot(wc_ref[...], split3(x)) * scale
        return carry

    lax.fori_loop(0, n1, stage_c, 0, unroll=FFT_UNROLL)


def _split3_cols(w):
    hi = w.astype(np.float32).astype(jnp.bfloat16)
    lo = (w.astype(np.float32) - np.asarray(hi, np.float32)).astype(jnp.bfloat16)
    return jnp.asarray(np.concatenate([hi, hi, lo], axis=-1))


def _fft_filter(filt, n1, tables):
    n2 = FFT_N2
    width = filt.shape[1]
    ft = filt.reshape(n1, n2, width).transpose(1, 0, 2)
    wa = _split3_cols(tables["real_a"])
    wc = _split3_cols(tables["fwd_c"])
    return pl.pallas_call(
        functools.partial(_fft_filter_kernel, n1=n1),
        out_shape=jax.ShapeDtypeStruct((n1, 2 * n2, width), F32),
        grid=(width // LANES,),
        in_specs=[pl.BlockSpec((n2, n1, LANES), lambda c: (0, 0, c)),
                  pl.BlockSpec(wa.shape, lambda c: (0, 0, 0)),
                  pl.BlockSpec(wc.shape, lambda c: (0, 0))],
        out_specs=pl.BlockSpec((n1, 2 * n2, LANES), lambda c: (0, 0, c)),
        scratch_shapes=[pltpu.VMEM((2 * n1 * FFT_PITCH, LANES), F32)],
        compiler_params=_params("arbitrary"),
        name="hyena_filter_dft",
    )(ft, wa, wc)


def _fftconv_kernel(u_ref, x0_ref, fh_ref, skip_ref, wfa_ref, wfc_ref, wic_ref, wia_ref, o_ref,
                    scr_ref, *, n1):
    n2 = FFT_N2
    rows = 2 * n1

    def stage_a(j, carry):
        scr_ref[pl.ds(j, rows, stride=FFT_PITCH), :] = _dot(wfa_ref[j], u_ref[0, j])
        return carry

    lax.fori_loop(0, n2, stage_a, 0, unroll=FFT_UNROLL)

    def stage_c(k, carry):
        x = jnp.concatenate([scr_ref[_slab(k), :], scr_ref[_slab(n1 + k), :]], axis=0)
        xf = _dot(wfc_ref[...], x.astype(BF16))
        fh = fh_ref[k]
        xr, xi, fr, fi = xf[:n2], xf[n2:], fh[:n2], fh[n2:]
        y = jnp.concatenate([xr * fr - xi * fi, xr * fi + xi * fr], axis=0)
        zt = _dot(wic_ref[...], y.astype(BF16))
        scr_ref[_slab(k), :] = zt[:n2]
        scr_ref[_slab(n1 + k), :] = zt[n2:]
        return carry

    lax.fori_loop(0, n1, stage_c, 0, unroll=FFT_UNROLL)

    def stage_ai(j, carry):
        x = scr_ref[pl.ds(j, rows, stride=FFT_PITCH), :]
        y = _dot(wia_ref[j], x.astype(BF16))
        u = u_ref[0, j].astype(F32)
        o_ref[0, j] = (x0_ref[0, j].astype(F32) * (y + skip_ref[...] * u)).astype(o_ref.dtype)
        return carry

    lax.fori_loop(0, n2, stage_ai, 0, unroll=FFT_UNROLL)


def _to_fft_layout(x, n1_in):
    b, length, c = x.shape
    n1_used = length // FFT_N2
    x = x.reshape(b // 2, 2, n1_used, FFT_N2, c)
    x = jnp.pad(x, ((0, 0), (0, 0), (0, n1_in - n1_used), (0, 0), (0, 0)))
    return x.transpose(0, 3, 1, 2, 4).reshape(b // 2, FFT_N2, 2 * n1_in, c)


def _from_fft_layout(y, length):
    p, n2, rows, c = y.shape
    n1_in = rows // 2
    y = y.reshape(p, n2, 2, n1_in, c).transpose(0, 2, 3, 1, 4)
    return y.reshape(2 * p, n1_in * n2, c)[:, :length]


def _fftconv(u, x0c, fh, skip, n1, n1_in, tables):
    _, length, width = u.shape
    ut = _to_fft_layout(u, n1_in)
    xt = _to_fft_layout(x0c, n1_in)
    pairs, n2, rin, _ = ut.shape
    wfa = jnp.asarray(tables["fwd_a"], dtype=BF16)
    wia = jnp.asarray(tables["inv_a"], dtype=BF16)
    wfc = jnp.asarray(tables["fwd_c"], dtype=BF16)
    wic = jnp.asarray(tables["inv_c"], dtype=BF16)
    data = pl.BlockSpec((1, n2, rin, LANES), lambda c, p: (p, 0, 0, c))

    def const(arr):
        zeros = (0,) * arr.ndim
        return pl.BlockSpec(arr.shape, lambda c, p: zeros, pipeline_mode=pl.Buffered(1))

    out = pl.pallas_call(
        functools.partial(_fftconv_kernel, n1=n1),
        out_shape=jax.ShapeDtypeStruct(ut.shape, BF16),
        grid=(width // LANES, pairs),
        in_specs=[data, data,
                  pl.BlockSpec((n1, 2 * n2, LANES), lambda c, p: (0, 0, c)),
                  pl.BlockSpec((1, LANES), lambda c, p: (0, c)),
                  const(wfa), const(wfc), const(wic), const(wia)],
        out_specs=data,
        scratch_shapes=[pltpu.VMEM((2 * n1 * FFT_PITCH, LANES), F32)],
        compiler_params=_params("arbitrary", "arbitrary"),
        name="hyena_dft_conv",
    )(ut, xt, fh, skip.reshape(1, width), wfa, wfc, wic, wia)
    return _from_fft_layout(out, length)


def _hyena(z, col0, width, ctx_len, short_w, short_b, filt_params, skip):
    x0c, u = _hy_pre(z, short_w, short_b, col0, width, ctx_len)
    outs = []
    for lo, hi in ((0, ctx_len), (ctx_len, z.shape[1])):
        length = hi - lo
        n1_in = max(length // FFT_N2, SUBLANES)
        n1 = 2 * n1_in
        tables = _dft_tables(n1, n1_in)
        filt = _hyena_filter(length, n1 * FFT_N2, *filt_params, width)
        fh = _fft_filter(filt, n1, tables)
        outs.append(_fftconv(u[:, lo:hi], x0c[:, lo:hi], fh, skip, n1, n1_in, tables))
    return jnp.concatenate(outs, axis=1)


def _rope_tables(n_lat, ctx_len):
    tok = np.arange(n_lat)
    row, colp = tok // GRID_W, tok % GRID_W

    def axial(half):
        inv = ROPE_BASE ** (-np.arange(half, dtype=np.float32) / half)
        parts_c, parts_s = [], []
        for pos in (row, colp):
            ang = pos.astype(np.float32)[:, None] * inv
            parts_c += [np.cos(ang), np.cos(ang)]
            parts_s += [-np.sin(ang), np.sin(ang)]
        return np.concatenate(parts_c, 1), np.concatenate(parts_s, 1)

    dc, ds = axial(DA_HEAD_DIM // 4)
    mc, ms = axial(MLA_ROPE // 4)
    ones, zeros = np.ones((n_lat, MLA_NOPE), np.float32), np.zeros((n_lat, MLA_NOPE), np.float32)
    padc = np.ones((n_lat, LANES - MLA_NOPE - MLA_ROPE), np.float32)
    tabs = [np.concatenate([dc, dc], 1), np.concatenate([ds, ds], 1),
            np.concatenate([ones, mc, padc], 1), np.concatenate([zeros, ms, 0 * padc], 1)]
    out = []
    for i, tb in enumerate(tabs):
        ctx_rows = np.ones((ctx_len, LANES), np.float32) if i % 2 == 0 else np.zeros((ctx_len, LANES), np.float32)
        out.append(jnp.asarray(np.concatenate([ctx_rows, tb.astype(np.float32)], 0)))
    return out


def _swap_perm(width, group, half):
    idx = np.arange(width)
    pos = idx % group
    return np.where((pos % (2 * half)) < half, idx + half, idx - half)


def _with_ones(v):
    ones = jnp.ones((v.shape[0], LANES), v.dtype)
    parts = []
    for hd in range(v.shape[1] // LANES):
        parts += [v[:, hd * LANES:(hd + 1) * LANES], ones]
    return jnp.concatenate(parts, axis=1)


def _in_odd_kernel(x_ref, mod_ref, modc_ref, g_ref, w_ref, dc_ref, ds_ref, mc_ref, ms_ref,
                   qg_ref, kvg_ref, wuq_ref, wukv_ref,
                   qd_ref, kd_ref, vd_ref, qm_ref, km_ref, vm_ref, *, n_ctx_tiles, da_w, q_rank, kv_rank):
    m = _tile_mod(mod_ref, modc_ref, n_ctx_tiles)
    xn = (_rms(x_ref[0]) * g_ref[...]) * (1.0 + m[1:2]) + m[0:1]
    z = _dot(xn.astype(BF16), w_ref[...])
    nrep = da_w // LANES
    dc = jnp.concatenate([dc_ref[...]] * nrep, axis=1)
    ds = jnp.concatenate([ds_ref[...]] * nrep, axis=1)
    mc = jnp.concatenate([mc_ref[...]] * MLA_HEADS, axis=1)
    ms = jnp.concatenate([ms_ref[...]] * MLA_HEADS, axis=1)
    sa = DA_HEAD_DIM ** -0.5 * LOG2_E
    sm = (MLA_NOPE + MLA_ROPE) ** -0.5 * LOG2_E
    o = 0
    qd_ref[0] = ((z[:, o:o + da_w] * dc + z[:, o + da_w:o + 2 * da_w] * ds) * sa).astype(BF16)
    o += 2 * da_w
    kd_ref[0] = (z[:, o:o + da_w] * dc + z[:, o + da_w:o + 2 * da_w] * ds).astype(BF16)
    o += 2 * da_w
    vd_ref[0] = _with_ones(z[:, o:o + da_w]).astype(BF16)
    o += da_w
    cq = _rms(z[:, o:o + q_rank]) * qg_ref[...]
    o += q_rank
    ckv = _rms(z[:, o:o + kv_rank]) * kvg_ref[...]
    o += kv_rank
    kr = z[:, o:o + LANES] * mc_ref[...] + z[:, o + LANES:o + 2 * LANES] * ms_ref[...]
    mw = MLA_HEADS * LANES
    qu = _dot(cq.astype(BF16), wuq_ref[...])
    qm_ref[0] = ((qu[:, :mw] * mc + qu[:, mw:] * ms) * sm).astype(BF16)
    kvu = _dot(ckv.astype(BF16), wukv_ref[...])
    km_ref[0] = (kvu[:, :mw] + jnp.concatenate([kr] * MLA_HEADS, axis=1)).astype(BF16)
    vm_ref[0] = _with_ones(kvu[:, mw:]).astype(BF16)


def _odd_weights(w_in, w_uq, w_ukv, da_w, q_rank, kv_rank):
    qw, kw, vw = w_in[:, :da_w], w_in[:, da_w:2 * da_w], w_in[:, 2 * da_w:3 * da_w]
    o = 3 * da_w
    cqw, ckvw, krw = w_in[:, o:o + q_rank], w_in[:, o + q_rank:o + q_rank + kv_rank], w_in[:, o + q_rank + kv_rank:]
    perm_da = _swap_perm(da_w, DA_HEAD_DIM // 2, DA_HEAD_DIM // 4)
    perm_r = _swap_perm(MLA_ROPE, MLA_ROPE // 2, MLA_ROPE // 4)
    d = w_in.shape[0]

    def rope_group(wr):
        return jnp.pad(wr, ((0, 0), (MLA_NOPE, LANES - MLA_NOPE - MLA_ROPE)))

    w_big = jnp.concatenate([qw, qw[:, perm_da], kw, kw[:, perm_da], vw, cqw, ckvw,
                             rope_group(krw), rope_group(krw[:, perm_r])], axis=1)
    dq = MLA_NOPE + MLA_ROPE
    pad = LANES - dq
    uq = w_uq.reshape(q_rank, MLA_HEADS, dq)
    uq_a = jnp.pad(uq, ((0, 0), (0, 0), (0, pad))).reshape(q_rank, MLA_HEADS * LANES)
    uq_s = jnp.pad(uq[:, :, MLA_NOPE:][:, :, perm_r], ((0, 0), (0, 0), (MLA_NOPE, pad)))
    uq_s = uq_s.reshape(q_rank, MLA_HEADS * LANES)
    ukv = w_ukv.reshape(kv_rank, MLA_HEADS, MLA_NOPE + MLA_V)
    uk = jnp.pad(ukv[:, :, :MLA_NOPE], ((0, 0), (0, 0), (0, LANES - MLA_NOPE))).reshape(kv_rank, MLA_HEADS * LANES)
    uv = ukv[:, :, MLA_NOPE:].reshape(kv_rank, MLA_HEADS * MLA_V)
    return (w_big.astype(BF16), jnp.concatenate([uq_a, uq_s], axis=1).astype(BF16),
            jnp.concatenate([uk, uv], axis=1).astype(BF16))


def _in_odd(h, mods, g, w_big, wuq, wukv, qg, kvg, tabs, tm, n_ctx_tiles, da_w):
    b, t, d = h.shape
    q_rank, kv_rank = qg.shape[0], kvg.shape[0]
    lat, ctx = _mod_specs(d, b)
    tab = pl.BlockSpec((tm, LANES), lambda bi, ti: (ti, 0))
    mw = MLA_HEADS * LANES

    def tok_major(width):
        return jax.ShapeDtypeStruct((b, t, width), BF16), _tok_spec(tm, width)

    outs, ospecs = zip(tok_major(da_w), tok_major(da_w), tok_major(2 * da_w),
                       tok_major(mw), tok_major(mw), tok_major(2 * mw))
    return pl.pallas_call(
        functools.partial(_in_odd_kernel, n_ctx_tiles=n_ctx_tiles, da_w=da_w, q_rank=q_rank, kv_rank=kv_rank),
        out_shape=tuple(outs),
        grid=(b, t // tm),
        in_specs=[_tok_spec(tm, d), lat, ctx, _const_spec((1, d)), _const_spec(w_big.shape),
                  tab, tab, tab, tab, _const_spec((1, q_rank)), _const_spec((1, kv_rank)),
                  _const_spec(wuq.shape), _const_spec(wukv.shape)],
        out_specs=tuple(ospecs),
        compiler_params=_params("arbitrary", "arbitrary"),
        name="odd_in_proj",
    )(h, mods, mods, g.reshape(1, d), w_big, *tabs, qg.reshape(1, -1), kvg.reshape(1, -1), wuq, wukv)


def _softmax_pv(qs, k_ref, v_ref):
    n_keys = k_ref.shape[1]
    starts = list(range(0, n_keys, ATTN_KEY_BLOCK))
    m = [None] * len(qs)
    acc = [None] * len(qs)
    for start in starts:
        size = min(ATTN_KEY_BLOCK, n_keys - start)
        k = k_ref[0, start:start + size, :]
        v1 = v_ref[0, start:start + size, :]
        for i, q in enumerate(qs):
            s = _dot_nt(q, k)
            m_blk = jnp.max(s, axis=-1, keepdims=True)
            if start == 0:
                m[i] = m_blk
                acc[i] = _dot(jnp.exp2(s - m_blk).astype(BF16), v1)
            else:
                m_new = jnp.maximum(m[i], m_blk)
                pv = _dot(jnp.exp2(s - m_new).astype(BF16), v1)
                acc[i] = jnp.exp2(m[i] - m_new) * acc[i] + pv
                m[i] = m_new
    return [a[:, :LANES] / a[:, LANES:] for a in acc]


def _attn_kernel(q_ref, k_ref, v_ref, lam_ref, sg_ref, o_ref, *, diff, lam_init):
    q = q_ref[0]
    if not diff:
        o_ref[0] = _softmax_pv([q], k_ref, v_ref)[0].astype(o_ref.dtype)
        return
    first = lax.broadcasted_iota(jnp.int32, q.shape, 1) < DA_HEAD_DIM
    zero = jnp.zeros_like(q)
    o1, o2 = _softmax_pv([jnp.where(first, q, zero), jnp.where(first, zero, q)], k_ref, v_ref)
    lp = lam_ref[...]
    lam = (jnp.exp(jnp.sum(lp[0:1] * lp[1:2], axis=-1, keepdims=True))
           - jnp.exp(jnp.sum(lp[2:3] * lp[3:4], axis=-1, keepdims=True)) + lam_init)
    o_ref[0] = (_rms(o1 - lam * o2) * sg_ref[...] * (1.0 - lam_init)).astype(o_ref.dtype)


def _attention(q, k, v1, lam_p, subln_g, tq, n_keys, diff, lam_init):
    b, t_q, width = q.shape
    heads = width // LANES
    qtile = pl.BlockSpec((1, tq, LANES), lambda bi, hi, ti: (bi, ti, hi))
    khead = pl.BlockSpec((1, n_keys, LANES), lambda bi, hi, ti: (bi, 0, hi))
    vhead = pl.BlockSpec((1, n_keys, 2 * LANES), lambda bi, hi, ti: (bi, 0, hi))

    def const(shape):
        zeros = (0,) * len(shape)
        return pl.BlockSpec(shape, lambda bi, hi, ti: zeros)

    return pl.pallas_call(
        functools.partial(_attn_kernel, diff=diff, lam_init=lam_init),
        out_shape=jax.ShapeDtypeStruct((b, t_q, width), BF16),
        grid=(b, heads, t_q // tq),
        in_specs=[qtile, khead, vhead, const(lam_p.shape), const((1, LANES))],
        out_specs=qtile,
        compiler_params=_params("arbitrary", "arbitrary", "arbitrary"),
        name="diff_attention" if diff else "mla_attention",
    )(q, k, v1, lam_p, subln_g.reshape(1, LANES))


def _attend_segments(q, k, v1, lam_p, subln_g, ctx_len, diff, lam_init, need_ctx):
    t = k.shape[1]
    n_lat = t - ctx_len
    tq_lat = math.gcd(n_lat, 512)
    o_lat = _attention(q[:, ctx_len:], k, v1, lam_p, subln_g, tq_lat, t, diff, lam_init)
    if not need_ctx:
        return o_lat
    o_ctx = _attention(q[:, :ctx_len], k, v1, lam_p, subln_g, ctx_len, ctx_len, diff, lam_init)
    return jnp.concatenate([o_ctx, o_lat], axis=1)


def kernel(x, c, ctx, c_ctx, ada_w, ada_b, norm_mix_g, norm_ffn_g, ffn_w_gu, ffn_w_down, ev_w_in, ev_w_out, hg_lower_bound, hg_out_norm_g, hy_short_w, hy_short_b, hy_filt_w1, hy_filt_b1, hy_filt_freq1, hy_filt_w2, hy_filt_b2, hy_filt_freq2, hy_filt_w3, hy_skip, od_w_in, od_w_out, da_lambda, da_subln_g, mla_q_norm_g, mla_w_uq, mla_kv_norm_g, mla_w_ukv, final_norm_g):
    n_batch, n_lat, d = x.shape
    ctx_len = ctx.shape[1]
    depth = ada_w.shape[0]
    assert n_batch % 2 == 0 and n_batch < MOD_ROWS
    assert n_lat % GRID_W == 0 and ctx_len % HG_CHUNK == 0 and n_lat % FFT_N2 == 0 and ctx_len % FFT_N2 == 0
    tm = math.gcd(math.gcd(ctx_len, n_lat), 256)
    n_ctx_tiles = ctx_len // tm
    hg_width = d // 2
    hy_width = d - hg_width
    da_w = d // 2
    q_rank, kv_rank = mla_q_norm_g.shape[1], mla_kv_norm_g.shape[1]

    cc = jnp.concatenate([c, c_ctx[None], jnp.zeros((MOD_ROWS - n_batch - 1, d), F32)], axis=0)
    mods = _ada(cc, ada_w, ada_b).reshape(depth, MOD_ROWS, 6, d)
    h = jnp.concatenate([ctx, x], axis=1)
    rope_tabs = _rope_tables(n_lat, ctx_len)

    for i in range(depth):
        last = i == depth - 1
        if i % 2 == 0:
            e = i // 2
            z = _in_even(h, mods[i], norm_mix_g[i], ev_w_in[e].astype(BF16), tm, n_ctx_tiles)
            a = _hgrn(z, hg_lower_bound, hg_out_norm_g[e], e, hg_width // HG_HEAD_DIM, ctx_len)
            filt_params = (hy_filt_w1[e], hy_filt_b1[e], hy_filt_freq1[e], hy_filt_w2[e], hy_filt_b2[e],
                           hy_filt_freq2[e], hy_filt_w3[e])
            bb = _hyena(z, 5 * hg_width, hy_width, ctx_len, hy_short_w[e], hy_short_b[e], filt_params, hy_skip[e])
            w_out = ev_w_out[e]
        else:
            o = i // 2
            lam_init = 0.8 - 0.6 * math.exp(-0.3 * i)
            w_big, wuq, wukv = _odd_weights(od_w_in[o], mla_w_uq[o], mla_w_ukv[o], da_w, q_rank, kv_rank)
            qd, kd, vd, qm, km, vm = _in_odd(h, mods[i], norm_mix_g[i], w_big, wuq, wukv, mla_q_norm_g[o],
                                             mla_kv_norm_g[o], rope_tabs, tm, n_ctx_tiles, da_w)
            a = _attend_segments(qd, kd, vd, da_lambda[o], da_subln_g[o], ctx_len, True, lam_init, not last)
            bb = _attend_segments(qm, km, vm, da_lambda[o], da_subln_g[o], ctx_len, False, lam_init, not last)
            w_out = od_w_out[o]
        weights = (w_out.astype(BF16), ffn_w_gu[i].astype(BF16), ffn_w_down[i].astype(BF16))
        if not last:
            h = _post(h, a, bb, mods[i], norm_ffn_g[i], *weights, tm, n_ctx_tiles)
    if a.shape[1] != n_lat:
        a, bb = a[:, ctx_len:], bb[:, ctx_len:]
    return _post_final(h, a, bb, mods[depth - 1], norm_ffn_g[depth - 1], *weights, final_norm_g, tm, n_ctx_tiles)
```

```python
import functools
import math

import numpy as np
import jax
import jax.numpy as jnp
from jax import lax
from jax.experimental import pallas as pl
from jax.experimental.pallas import tpu as pltpu

F32 = jnp.float32
BF16 = jnp.bfloat16
HIGHEST = lax.Precision.HIGHEST

GRID_W = 64
EPS = 1e-6
ROPE_BASE = 10000.0
HG_HEAD_DIM = 128
HG_CHUNK = 64
HY_BANDS = 16
HY_EMB = 2 * HY_BANDS + 1
HY_TARGET = 1e-2
HY_STEEP_PCT = 0.3
HY_GENTLE_PCT = 1.5
DA_HEAD_DIM = 64
MLA_HEADS = 4
MLA_NOPE = 64
MLA_ROPE = 32
MLA_V = 128
LOG2_E = 1.4426950408889634

LANES = 128
SUBLANES = 8
V7X_VMEM_LIMIT_BYTES = 56 * 1024 * 1024
MOD_ROWS = 16
FFT_N2 = 128
FFT_PITCH = FFT_N2 + SUBLANES
FFT_UNROLL = 16
HGRN_UNROLL = 2
ATTN_KEY_BLOCK = 512


def _params(*sem):
    return pltpu.CompilerParams(dimension_semantics=sem, vmem_limit_bytes=V7X_VMEM_LIMIT_BYTES)


def _dot(a, b, precision=None):
    return jnp.dot(a, b, preferred_element_type=F32, precision=precision)


def _dot_nt(a, b):
    return lax.dot_general(a, b, (((1,), (1,)), ((), ())), preferred_element_type=F32)


def _dot_tn(a, b):
    return lax.dot_general(a, b, (((0,), (0,)), ((), ())), preferred_element_type=F32)


def _rms(x):
    return x * lax.rsqrt(jnp.mean(x * x, axis=-1, keepdims=True) + EPS)


def _silu(x):
    return x * jax.nn.sigmoid(x)


def _tile_mod(mod_ref, modc_ref, n_ctx_tiles):
    return jnp.where(pl.program_id(1) < n_ctx_tiles, modc_ref[...], mod_ref[...])


def _ada_kernel(c_ref, w_ref, b_ref, o_ref):
    o_ref[0] = _dot(_silu(c_ref[...]), w_ref[0], HIGHEST) + b_ref[0]


def _ada(cc, ada_w, ada_b):
    depth, d, n = ada_w.shape
    rows = cc.shape[0]
    tn = n // 4
    return pl.pallas_call(
        _ada_kernel,
        out_shape=jax.ShapeDtypeStruct((depth, rows, n), F32),
        grid=(depth, n // tn),
        in_specs=[
            pl.BlockSpec((rows, d), lambda i, j: (0, 0)),
            pl.BlockSpec((1, d, tn), lambda i, j: (i, 0, j)),
            pl.BlockSpec((1, 1, tn), lambda i, j: (i, 0, j)),
        ],
        out_specs=pl.BlockSpec((1, rows, tn), lambda i, j: (i, 0, j)),
        compiler_params=_params("arbitrary", "arbitrary"),
        name="ada_mod",
    )(cc, ada_w, ada_b.reshape(depth, 1, n))


def _tok_spec(tm, width):
    return pl.BlockSpec((1, tm, width), lambda b, t: (b, t, 0))


def _const_spec(shape):
    zeros = (0,) * len(shape)
    return pl.BlockSpec(shape, lambda b, t: zeros)


def _mod_specs(d, n_batch):
    lat = pl.BlockSpec((None, 6, d), lambda b, t: (b, 0, 0))
    ctx = pl.BlockSpec((None, 6, d), lambda b, t: (n_batch, 0, 0))
    return lat, ctx


def _in_even_kernel(x_ref, mod_ref, modc_ref, g_ref, w_ref, o_ref, *, n_ctx_tiles):
    m = _tile_mod(mod_ref, modc_ref, n_ctx_tiles)
    xn = (_rms(x_ref[0]) * g_ref[...]) * (1.0 + m[1:2]) + m[0:1]
    o_ref[0] = _dot(xn.astype(BF16), w_ref[...])


def _in_even(h, mods, g, w, tm, n_ctx_tiles):
    b, t, d = h.shape
    n = w.shape[1]
    lat, ctx = _mod_specs(d, b)
    return pl.pallas_call(
        functools.partial(_in_even_kernel, n_ctx_tiles=n_ctx_tiles),
        out_shape=jax.ShapeDtypeStruct((b, t, n), F32),
        grid=(b, t // tm),
        in_specs=[_tok_spec(tm, d), lat, ctx, _const_spec((1, d)), _const_spec((d, n))],
        out_specs=_tok_spec(tm, n),
        compiler_params=_params("arbitrary", "arbitrary"),
        name="even_in_proj",
    )(h, mods, mods, g.reshape(1, d), w)


def _post_body(m, h, a, bb, g_ref, wo_ref, wgu_ref, wd_ref, hidden):
    half = a.shape[-1]
    y = _dot(a, wo_ref[:half, :]) + _dot(bb, wo_ref[half:, :])
    h1 = h + m[2:3] * y
    xn = (_rms(h1) * g_ref[...]) * (1.0 + m[4:5]) + m[3:4]
    gu = _dot(xn.astype(BF16), wgu_ref[...])
    act = _silu(gu[:, :hidden]) * gu[:, hidden:]
    return h1 + m[5:6] * _dot(act.astype(BF16), wd_ref[...])


def _post_kernel(h_ref, a_ref, b_ref, mod_ref, modc_ref, g_ref, wo_ref, wgu_ref, wd_ref, o_ref,
                 *, n_ctx_tiles, hidden):
    m = _tile_mod(mod_ref, modc_ref, n_ctx_tiles)
    o_ref[0] = _post_body(m, h_ref[0], a_ref[0], b_ref[0], g_ref, wo_ref, wgu_ref, wd_ref, hidden)


def _post_final_kernel(h_ref, a_ref, b_ref, mod_ref, g_ref, wo_ref, wgu_ref, wd_ref, fg_ref, o_ref,
                       *, hidden):
    h2 = _post_body(mod_ref[...], h_ref[0], a_ref[0], b_ref[0], g_ref, wo_ref, wgu_ref, wd_ref, hidden)
    o_ref[0] = _rms(h2) * fg_ref[...]


def _post(h, a, bb, mods, g, wo, wgu, wd, tm, n_ctx_tiles):
    b, t, d = h.shape
    half = a.shape[-1]
    hidden = wd.shape[0]
    lat, ctx = _mod_specs(d, b)
    return pl.pallas_call(
        functools.partial(_post_kernel, n_ctx_tiles=n_ctx_tiles, hidden=hidden),
        out_shape=jax.ShapeDtypeStruct((b, t, d), F32),
        grid=(b, t // tm),
        in_specs=[_tok_spec(tm, d), _tok_spec(tm, half), _tok_spec(tm, half), lat, ctx,
                  _const_spec((1, d)), _const_spec((2 * half, d)), _const_spec((d, 2 * hidden)),
                  _const_spec((hidden, d))],
        out_specs=_tok_spec(tm, d),
        compiler_params=_params("arbitrary", "arbitrary"),
        name="out_proj_ffn",
    )(h, a, bb, mods, mods, g.reshape(1, d), wo, wgu, wd)


def _post_final(h, a_lat, b_lat, mods, g, wo, wgu, wd, final_g, tm, n_ctx_tiles):
    b, t, d = h.shape
    n_lat = a_lat.shape[1]
    half = a_lat.shape[-1]
    hidden = wd.shape[0]
    lat, _ = _mod_specs(d, b)
    return pl.pallas_call(
        functools.partial(_post_final_kernel, hidden=hidden),
        out_shape=jax.ShapeDtypeStruct((b, n_lat, d), F32),
        grid=(b, n_lat // tm),
        in_specs=[pl.BlockSpec((1, tm, d), lambda bi, ti: (bi, ti + n_ctx_tiles, 0)),
                  _tok_spec(tm, half), _tok_spec(tm, half), lat,
                  _const_spec((1, d)), _const_spec((2 * half, d)), _const_spec((d, 2 * hidden)),
                  _const_spec((hidden, d)), _const_spec((1, d))],
        out_specs=_tok_spec(tm, d),
        compiler_params=_params("arbitrary", "arbitrary"),
        name="out_proj_ffn_final",
    )(h, a_lat, b_lat, mods, g.reshape(1, d), wo, wgu, wd, final_g.reshape(1, d))


def _hgrn_tables(tile):
    c = HG_CHUNK
    r = np.arange(c)
    j = r[None, :]
    rt = np.arange(tile)
    blocks = [(j <= r[:, None]), (j > r[:, None])]
    masks, bms = [], []
    m = c
    while m >= 2:
        half = m // 2
        p = r % m
        mid = (r - p + half)[:, None]
        upper = (p >= half)[:, None]
        up = (j >= mid) & (j <= r[:, None]) & upper
        lo = (j > r[:, None]) & (j <= mid - 1) & (~upper)
        blocks.append(up | lo)
        masks.append(np.broadcast_to(upper, (c, LANES)))
        bms.append((rt[:, None] // m) == (rt[None, :] // m))
        m //= 2
    fwd = np.stack(blocks).astype(np.float32)
    bwd = fwd[:, ::-1, ::-1]
    mall = np.stack([fwd.reshape(-1, c), bwd.reshape(-1, c)])
    mall = np.concatenate([mall, mall], axis=2)
    mu = np.stack(masks).astype(np.float32)
    masku = np.tile(np.stack([mu, mu[:, ::-1]]), (1, 1, tile // c, 1))
    return (jnp.asarray(mall, dtype=BF16), jnp.asarray(masku, dtype=F32),
            jnp.asarray(np.stack(bms), dtype=F32))


def _hgrn_kernel(q_ref, ff_ref, fb_ref, i_ref, g_ref, lbp_ref, ng_ref, mall_ref, masku_ref, bm_ref,
                 o_ref, of_ref, ob_ref, *, layer, n_ctx_tiles, n_tiles):
    c = HG_CHUNK
    dk = HG_HEAD_DIM
    n_levels = masku_ref.shape[1]
    tile = masku_ref.shape[2]
    n_sub = tile // c
    lbp = lbp_ref[...]
    ex = jnp.exp(lbp - jnp.max(lbp, axis=0, keepdims=True))
    p = ex / jnp.sum(ex, axis=0, keepdims=True)
    lb = jnp.sum(p[:layer + 1], axis=0) - p[0]

    def sub(x, g):
        return x[g * c:(g + 1) * c]

    dirs = (0, 1)

    def stack_chunks(x):
        return jnp.concatenate([sub(x, g) for g in range(n_sub)], axis=1)

    def run_tiles(tis, sts):
        rows = [pl.ds(pl.multiple_of(ti * tile, tile), tile) for ti in tis]
        q = [q_ref[0, r, :] for r in rows]
        v = [i_ref[0, r, :] for r in rows]
        zf = [ff_ref[0, rows[0], :], fb_ref[0, rows[1], :]]
        f = [lb[d:d + 1] + (1.0 - lb[d:d + 1]) * jax.nn.sigmoid(zf[d]) for d in dirs]
        lf = [jnp.log(x) for x in f]
        k = [1.0 - x for x in f]
        hi = [x.astype(BF16) for x in lf]
        lo = [(lf[d] - hi[d].astype(F32)).astype(BF16) for d in dirs]
        e2 = [_dot(mall_ref[d], jnp.concatenate([stack_chunks(hi[d]), stack_chunks(lo[d])], axis=0))
              for d in dirs]
        ee = [[jnp.exp(e2[d][:, g * dk:(g + 1) * dk]) for g in range(n_sub)] for d in dirs]

        def block(d, idx):
            return jnp.concatenate([ee[d][g][idx * c:(idx + 1) * c] for g in range(n_sub)], axis=0)

        g_in = [block(d, 0) for d in dirs]
        qin = [(q[d] * g_in[d]).astype(BF16) for d in dirs]
        kout = [(k[d] * block(d, 1)).astype(BF16) for d in dirs]
        vb = [x.astype(BF16) for x in v]
        ds = [[_dot_tn(sub(vb[d], g), sub(kout[d], g)) for g in range(n_sub)] for d in dirs]
        a = [jnp.zeros((tile, tile), F32) for _ in dirs]
        for lvl in range(n_levels):
            for d in dirs:
                gl = block(d, 2 + lvl)
                gu = gl * masku_ref[d, lvl]
                a[d] = a[d] + bm_ref[lvl] * _dot_nt((q[d] * gu).astype(BF16),
                                                    (k[d] * (gl - gu)).astype(BF16))
        o = [_dot(a[d].astype(BF16), vb[d]) + jnp.sum(q[d] * k[d], axis=-1, keepdims=True) * v[d]
             for d in dirs]
        order = [list(range(n_sub)), list(reversed(range(n_sub)))]
        entering = [[None] * n_sub for _ in dirs]
        sts = list(sts)
        for step in range(n_sub):
            for d in dirs:
                g = order[d][step]
                entering[d][g] = sts[d].astype(BF16)
                last = (g + 1) * c - 1 if d == 0 else g * c
                sts[d] = sts[d] * g_in[d][last:last + 1] + ds[d][g]
        outs = [jnp.concatenate([sub(o[d], g) + _dot_nt(sub(qin[d], g), entering[d][g])
                                 for g in range(n_sub)], axis=0) for d in dirs]
        return rows, outs, sts

    def body(i, carry):
        tb = jnp.where(i < n_ctx_tiles, n_ctx_tiles - 1 - i, n_tiles - 1 - (i - n_ctx_tiles))
        rows, outs, sts = run_tiles((i, tb), carry)
        of_ref[rows[0], :] = outs[0]
        ob_ref[rows[1], :] = outs[1]
        return tuple(sts)

    s0 = jnp.zeros((dk, dk), F32)
    lax.fori_loop(0, n_tiles, body, (s0, s0), unroll=HGRN_UNROLL)
    o = _rms(of_ref[...] + ob_ref[...]) * ng_ref[...]
    o_ref[0] = (o * _silu(g_ref[0])).astype(o_ref.dtype)


def _hgrn(z, hg_lower_bound, ng, layer, n_heads, ctx_len):
    b, t, _ = z.shape
    dk = HG_HEAD_DIM
    n_even = hg_lower_bound.shape[0]
    tile = math.gcd(math.gcd(ctx_len, t - ctx_len), 256)
    mall, masku, bm = _hgrn_tables(tile)

    def col(kind):
        return pl.BlockSpec((1, t, dk), lambda bi, hi: (bi, 0, kind * n_heads + hi))

    def const(arr):
        zeros = (0,) * arr.ndim
        return pl.BlockSpec(arr.shape, lambda bi, hi: zeros)

    return pl.pallas_call(
        functools.partial(_hgrn_kernel, layer=layer, n_ctx_tiles=ctx_len // tile, n_tiles=t // tile),
        out_shape=jax.ShapeDtypeStruct((b, t, n_heads * dk), BF16),
        grid=(b, n_heads),
        in_specs=[col(0), col(1), col(2), col(3), col(4),
                  pl.BlockSpec((n_even, 2, dk), lambda bi, hi: (0, 0, hi)),
                  const(ng.reshape(1, dk)), const(mall), const(masku), const(bm)],
        out_specs=pl.BlockSpec((1, t, dk), lambda bi, hi: (bi, 0, hi)),
        scratch_shapes=[pltpu.VMEM((t, dk), F32), pltpu.VMEM((t, dk), F32)],
        compiler_params=_params("arbitrary", "arbitrary"),
        name="hgrn2_scan",
    )(z, z, z, z, z, hg_lower_bound, ng.reshape(1, dk), mall, masku, bm)


def _hy_pre_kernel(x0_ref, x1_ref, v_ref, w0_ref, w1_ref, wv_ref, b0_ref, b1_ref, bv_ref,
                   x0c_ref, u_ref, *, ctx_len):
    t = x0_ref.shape[1]
    row = lax.broadcasted_iota(jnp.int32, (t, LANES), 0)
    first = (row == 0) | (row == ctx_len)
    final = (row == ctx_len - 1) | (row == t - 1)

    def short_conv(z_ref, w_ref, b_ref):
        z = z_ref[0]
        prev = jnp.where(first, 0.0, pltpu.roll(z, 1, 0))
        nxt = jnp.where(final, 0.0, pltpu.roll(z, t - 1, 0))
        w = w_ref[...]
        return prev * w[0:1] + z * w[1:2] + nxt * w[2:3] + b_ref[...]

    x0c_ref[0] = short_conv(x0_ref, w0_ref, b0_ref).astype(x0c_ref.dtype)
    u = short_conv(x1_ref, w1_ref, b1_ref) * short_conv(v_ref, wv_ref, bv_ref)
    u_ref[0] = u.astype(u_ref.dtype)


def _hy_pre(z, short_w, short_b, col0, width, ctx_len):
    b, t, _ = z.shape
    nb = width // LANES
    c0 = col0 // LANES

    def zcol(kind):
        return pl.BlockSpec((1, t, LANES), lambda bi, j: (bi, 0, c0 + kind * nb + j))

    def wcol(kind, rows):
        return pl.BlockSpec((rows, LANES), lambda bi, j: (0, kind * nb + j))

    out = jax.ShapeDtypeStruct((b, t, width), BF16)
    ospec = pl.BlockSpec((1, t, LANES), lambda bi, j: (bi, 0, j))
    sb = short_b.reshape(1, -1)
    return pl.pallas_call(
        functools.partial(_hy_pre_kernel, ctx_len=ctx_len),
        out_shape=(out, out),
        grid=(b, nb),
        in_specs=[zcol(0), zcol(1), zcol(2), wcol(0, 3), wcol(1, 3), wcol(2, 3),
                  wcol(0, 1), wcol(1, 1), wcol(2, 1)],
        out_specs=(ospec, ospec),
        compiler_params=_params("arbitrary", "arbitrary"),
        name="hyena_short_conv",
    )(z, z, z, short_w, short_w, short_w, sb, sb, sb)


def _filter_features(length, n):
    p = np.arange(n)
    is_f = p < length
    is_b = p > n - length
    lag = np.where(is_f, p, np.where(is_b, n - 1 - p, 0))
    tt = np.linspace(0.0, 1.0, length, dtype=np.float32)[lag][:, None]
    w = (2.0 * math.pi * lag.astype(np.float32) / length)[:, None].astype(np.float32)
    bands = np.linspace(1e-4, HY_BANDS - 1, HY_BANDS, dtype=np.float32)[None, :]
    feat = np.concatenate([tt, np.cos(bands * w), -np.sin(bands * w), is_f[:, None], is_b[:, None]],
                          axis=-1).astype(np.float32)
    pad = (-feat.shape[1]) % SUBLANES
    return jnp.asarray(np.pad(feat, ((0, 0), (0, pad))))


def _filt_kernel(feat_ref, w1_ref, b1_ref, fr1_ref, w2_ref, b2_ref, fr2_ref,
                 w3f_ref, w3b_ref, dl_ref, o_ref, hid_ref):
    z = feat_ref[...]
    mf = z[:, HY_EMB:HY_EMB + 1]
    mb = z[:, HY_EMB + 1:HY_EMB + 2]

    @pl.when(pl.program_id(0) == 0)
    def _():
        h1 = jnp.sin(fr1_ref[...] * (_dot(z, w1_ref[...], HIGHEST) + b1_ref[...]))
        hid_ref[...] = jnp.sin(fr2_ref[...] * (_dot(h1, w2_ref[...], HIGHEST) + b2_ref[...]))

    hid = hid_ref[...]
    hf = _dot(hid, w3f_ref[...], HIGHEST)
    hb = _dot(hid, w3b_ref[...], HIGHEST)
    win = jnp.exp(-z[:, 0:1] * dl_ref[...])
    f = (mf * hf + mb * hb) * win
    o_ref[...] = f / jnp.sum(jnp.abs(f), axis=0, keepdims=True)


def _hyena_filter(length, n, w1, b1, fr1, w2, b2, fr2, w3, width):
    feat = _filter_features(length, n)
    nf = feat.shape[1]
    hid = w1.shape[1]
    w1p = jnp.pad(w1, ((0, nf - w1.shape[0]), (0, 0)))
    d_lo = -math.log(HY_TARGET) / HY_GENTLE_PCT
    d_hi = -math.log(HY_TARGET) / HY_STEEP_PCT
    deltas = jnp.asarray(np.linspace(d_lo, d_hi, width, dtype=np.float32)[None, :])
    nb = width // LANES

    def full(shape):
        zeros = (0,) * len(shape)
        return pl.BlockSpec(shape, lambda j: zeros)

    return pl.pallas_call(
        _filt_kernel,
        out_shape=jax.ShapeDtypeStruct((n, width), F32),
        grid=(nb,),
        in_specs=[full((n, nf)), full((nf, hid)), full((1, hid)),
                  full((1, hid)), full((hid, hid)), full((1, hid)), full((1, hid)),
                  pl.BlockSpec((hid, LANES), lambda j: (0, j)),
                  pl.BlockSpec((hid, LANES), lambda j: (0, nb + j)),
                  pl.BlockSpec((1, LANES), lambda j: (0, j))],
        out_specs=pl.BlockSpec((n, LANES), lambda j: (0, j)),
        scratch_shapes=[pltpu.VMEM((n, hid), F32)],
        compiler_params=_params("arbitrary"),
        name="hyena_filter_mlp",
    )(feat, w1p, b1.reshape(1, -1), fr1.reshape(1, -1), w2, b2.reshape(1, -1),
      fr2.reshape(1, -1), w3, w3, deltas)


def _dft_tables(n1, n1_in):
    n2 = FFT_N2
    n = n1 * n2
    a = np.arange(n1)
    j = np.arange(n2)
    ang = -2.0 * np.pi * (a[None, None, :] * a[None, :, None] / n1 + j[:, None, None] * a[None, :, None] / n)
    tr, ti = np.cos(ang), np.sin(ang)
    fwd_a = np.concatenate([np.concatenate([tr, -ti], 2), np.concatenate([ti, tr], 2)], 1)
    trt, tit = np.swapaxes(tr, 1, 2), -np.swapaxes(ti, 1, 2)
    inv_a = np.concatenate([np.concatenate([trt, -tit], 2), np.concatenate([tit, trt], 2)], 1)
    keep = np.concatenate([np.arange(n1_in), n1 + np.arange(n1_in)])
    ang2 = -2.0 * np.pi * (j[:, None] * j[None, :]) / n2
    cr, ci = np.cos(ang2), np.sin(ang2)
    fwd_c = np.block([[cr, -ci], [ci, cr]])
    inv_c = np.block([[cr, ci], [-ci, cr]])
    real_a = np.concatenate([tr, ti], 1)
    return dict(fwd_a=fwd_a[:, :, keep], inv_a=inv_a[:, keep, :], fwd_c=fwd_c, inv_c=inv_c,
                real_a=real_a)


def _slab(idx):
    return pl.ds(pl.multiple_of(idx * FFT_PITCH, SUBLANES), FFT_N2)


def _fft_filter_kernel(f_ref, wa_ref, wc_ref, o_ref, scr_ref, *, n1):
    n2 = FFT_N2
    rows = 2 * n1
    scale = 1.0 / (n1 * n2)

    def split3(x):
        hi = x.astype(BF16)
        lo = (x - hi.astype(F32)).astype(BF16)
        return jnp.concatenate([hi, lo, hi], axis=0)

    def stage_a(j, carry):
        scr_ref[pl.ds(j, rows, stride=FFT_PITCH), :] = _dot(wa_ref[j], split3(f_ref[j]))
        return carry

    lax.fori_loop(0, n2, stage_a, 0, unroll=FFT_UNROLL)

    def stage_c(k, carry):
        x = jnp.concatenate([scr_ref[_slab(k), :], scr_ref[_slab(n1 + k), :]], axis=0)
        o_ref[k] = _dot(wc_ref[...], split3(x)) * scale
        return carry

    lax.fori_loop(0, n1, stage_c, 0, unroll=FFT_UNROLL)


def _split3_cols(w):
    hi = w.astype(np.float32).astype(jnp.bfloat16)
    lo = (w.astype(np.float32) - np.asarray(hi, np.float32)).astype(jnp.bfloat16)
    return jnp.asarray(np.concatenate([hi, hi, lo], axis=-1))


def _fft_filter(filt, n1, tables):
    n2 = FFT_N2
    width = filt.shape[1]
    ft = filt.reshape(n1, n2, width).transpose(1, 0, 2)
    wa = _split3_cols(tables["real_a"])
    wc = _split3_cols(tables["fwd_c"])
    return pl.pallas_call(
        functools.partial(_fft_filter_kernel, n1=n1),
        out_shape=jax.ShapeDtypeStruct((n1, 2 * n2, width), F32),
        grid=(width // LANES,),
        in_specs=[pl.BlockSpec((n2, n1, LANES), lambda c: (0, 0, c)),
                  pl.BlockSpec(wa.shape, lambda c: (0, 0, 0)),
                  pl.BlockSpec(wc.shape, lambda c: (0, 0))],
        out_specs=pl.BlockSpec((n1, 2 * n2, LANES), lambda c: (0, 0, c)),
        scratch_shapes=[pltpu.VMEM((2 * n1 * FFT_PITCH, LANES), F32)],
        compiler_params=_params("arbitrary"),
        name="hyena_filter_dft",
    )(ft, wa, wc)


def _fftconv_kernel(u_ref, x0_ref, fh_ref, skip_ref, wfa_ref, wfc_ref, wic_ref, wia_ref, o_ref,
                    scr_ref, *, n1):
    n2 = FFT_N2
    rows = 2 * n1

    def stage_a(j, carry):
        scr_ref[pl.ds(j, rows, stride=FFT_PITCH), :] = _dot(wfa_ref[j], u_ref[0, j])
        return carry

    lax.fori_loop(0, n2, stage_a, 0, unroll=FFT_UNROLL)

    def stage_c(k, carry):
        x = jnp.concatenate([scr_ref[_slab(k), :], scr_ref[_slab(n1 + k), :]], axis=0)
        xf = _dot(wfc_ref[...], x.astype(BF16))
        fh = fh_ref[k]
        xr, xi, fr, fi = xf[:n2], xf[n2:], fh[:n2], fh[n2:]
        y = jnp.concatenate([xr * fr - xi * fi, xr * fi + xi * fr], axis=0)
        zt = _dot(wic_ref[...], y.astype(BF16))
        scr_ref[_slab(k), :] = zt[:n2]
        scr_ref[_slab(n1 + k), :] = zt[n2:]
        return carry

    lax.fori_loop(0, n1, stage_c, 0, unroll=FFT_UNROLL)

    def stage_ai(j, carry):
        x = scr_ref[pl.ds(j, rows, stride=FFT_PITCH), :]
        y = _dot(wia_ref[j], x.astype(BF16))
        u = u_ref[0, j].astype(F32)
        o_ref[0, j] = (x0_ref[0, j].astype(F32) * (y + skip_ref[...] * u)).astype(o_ref.dtype)
        return carry

    lax.fori_loop(0, n2, stage_ai, 0, unroll=FFT_UNROLL)


def _to_fft_layout(x, n1_in):
    b, length, c = x.shape
    n1_used = length // FFT_N2
    x = x.reshape(b // 2, 2, n1_used, FFT_N2, c)
    x = jnp.pad(x, ((0, 0), (0, 0), (0, n1_in - n1_used), (0, 0), (0, 0)))
    return x.transpose(0, 3, 1, 2, 4).reshape(b // 2, FFT_N2, 2 * n1_in, c)


def _from_fft_layout(y, length):
    p, n2, rows, c = y.shape
    n1_in = rows // 2
    y = y.reshape(p, n2, 2, n1_in, c).transpose(0, 2, 3, 1, 4)
    return y.reshape(2 * p, n1_in * n2, c)[:, :length]


def _fftconv(u, x0c, fh, skip, n1, n1_in, tables):
    _, length, width = u.shape
    ut = _to_fft_layout(u, n1_in)
    xt = _to_fft_layout(x0c, n1_in)
    pairs, n2, rin, _ = ut.shape
    wfa = jnp.asarray(tables["fwd_a"], dtype=BF16)
    wia = jnp.asarray(tables["inv_a"], dtype=BF16)
    wfc = jnp.asarray(tables["fwd_c"], dtype=BF16)
    wic = jnp.asarray(tables["inv_c"], dtype=BF16)
    data = pl.BlockSpec((1, n2, rin, LANES), lambda c, p: (p, 0, 0, c))

    def const(arr):
        zeros = (0,) * arr.ndim
        return pl.BlockSpec(arr.shape, lambda c, p: zeros, pipeline_mode=pl.Buffered(1))

    out = pl.pallas_call(
        functools.partial(_fftconv_kernel, n1=n1),
        out_shape=jax.ShapeDtypeStruct(ut.shape, BF16),
        grid=(width // LANES, pairs),
        in_specs=[data, data,
                  pl.BlockSpec((n1, 2 * n2, LANES), lambda c, p: (0, 0, c)),
                  pl.BlockSpec((1, LANES), lambda c, p: (0, c)),
                  const(wfa), const(wfc), const(wic), const(wia)],
        out_specs=data,
        scratch_shapes=[pltpu.VMEM((2 * n1 * FFT_PITCH, LANES), F32)],
        compiler_params=_params("arbitrary", "arbitrary"),
        name="hyena_dft_conv",
    )(ut, xt, fh, skip.reshape(1, width), wfa, wfc, wic, wia)
    return _from_fft_layout(out, length)


def _hyena(z, col0, width, ctx_len, short_w, short_b, filt_params, skip):
    x0c, u = _hy_pre(z, short_w, short_b, col0, width, ctx_len)
    outs = []
    for lo, hi in ((0, ctx_len), (ctx_len, z.shape[1])):
        length = hi - lo
        n1_in = max(length // FFT_N2, SUBLANES)
        n1 = 2 * n1_in
        tables = _dft_tables(n1, n1_in)
        filt = _hyena_filter(length, n1 * FFT_N2, *filt_params, width)
        fh = _fft_filter(filt, n1, tables)
        outs.append(_fftconv(u[:, lo:hi], x0c[:, lo:hi], fh, skip, n1, n1_in, tables))
    return jnp.concatenate(outs, axis=1)


def _rope_tables(n_lat, ctx_len):
    tok = np.arange(n_lat)
    row, colp = tok // GRID_W, tok % GRID_W

    def axial(half):
        inv = ROPE_BASE ** (-np.arange(half, dtype=np.float32) / half)
        parts_c, parts_s = [], []
        for pos in (row, colp):
            ang = pos.astype(np.float32)[:, None] * inv
            parts_c += [np.cos(ang), np.cos(ang)]
            parts_s += [-np.sin(ang), np.sin(ang)]
        return np.concatenate(parts_c, 1), np.concatenate(parts_s, 1)

    dc, ds = axial(DA_HEAD_DIM // 4)
    mc, ms = axial(MLA_ROPE // 4)
    ones, zeros = np.ones((n_lat, MLA_NOPE), np.float32), np.zeros((n_lat, MLA_NOPE), np.float32)
    padc = np.ones((n_lat, LANES - MLA_NOPE - MLA_ROPE), np.float32)
    tabs = [np.concatenate([dc, dc], 1), np.concatenate([ds, ds], 1),
            np.concatenate([ones, mc, padc], 1), np.concatenate([zeros, ms, 0 * padc], 1)]
    out = []
    for i, tb in enumerate(tabs):
        ctx_rows = np.ones((ctx_len, LANES), np.float32) if i % 2 == 0 else np.zeros((ctx_len, LANES), np.float32)
        out.append(jnp.asarray(np.concatenate([ctx_rows, tb.astype(np.float32)], 0)))
    return out


def _swap_perm(width, group, half):
    idx = np.arange(width)
    pos = idx % group
    return np.where((pos % (2 * half)) < half, idx + half, idx - half)


def _with_ones(v):
    ones = jnp.ones((v.shape[0], LANES), v.dtype)
    parts = []
    for hd in range(v.shape[1] // LANES):
        parts += [v[:, hd * LANES:(hd + 1) * LANES], ones]
    return jnp.concatenate(parts, axis=1)


def _in_odd_kernel(x_ref, mod_ref, modc_ref, g_ref, w_ref, dc_ref, ds_ref, mc_ref, ms_ref,
                   qg_ref, kvg_ref, wuq_ref, wukv_ref,
                   qd_ref, kd_ref, vd_ref, qm_ref, km_ref, vm_ref, *, n_ctx_tiles, da_w, q_rank, kv_rank):
    m = _tile_mod(mod_ref, modc_ref, n_ctx_tiles)
    xn = (_rms(x_ref[0]) * g_ref[...]) * (1.0 + m[1:2]) + m[0:1]
    z = _dot(xn.astype(BF16), w_ref[...])
    nrep = da_w // LANES
    dc = jnp.concatenate([dc_ref[...]] * nrep, axis=1)
    ds = jnp.concatenate([ds_ref[...]] * nrep, axis=1)
    mc = jnp.concatenate([mc_ref[...]] * MLA_HEADS, axis=1)
    ms = jnp.concatenate([ms_ref[...]] * MLA_HEADS, axis=1)
    sa = DA_HEAD_DIM ** -0.5 * LOG2_E
    sm = (MLA_NOPE + MLA_ROPE) ** -0.5 * LOG2_E
    o = 0
    qd_ref[0] = ((z[:, o:o + da_w] * dc + z[:, o + da_w:o + 2 * da_w] * ds) * sa).astype(BF16)
    o += 2 * da_w
    kd_ref[0] = (z[:, o:o + da_w] * dc + z[:, o + da_w:o + 2 * da_w] * ds).astype(BF16)
    o += 2 * da_w
    vd_ref[0] = _with_ones(z[:, o:o + da_w]).astype(BF16)
    o += da_w
    cq = _rms(z[:, o:o + q_rank]) * qg_ref[...]
    o += q_rank
    ckv = _rms(z[:, o:o + kv_rank]) * kvg_ref[...]
    o += kv_rank
    kr = z[:, o:o + LANES] * mc_ref[...] + z[:, o + LANES:o + 2 * LANES] * ms_ref[...]
    mw = MLA_HEADS * LANES
    qu = _dot(cq.astype(BF16), wuq_ref[...])
    qm_ref[0] = ((qu[:, :mw] * mc + qu[:, mw:] * ms) * sm).astype(BF16)
    kvu = _dot(ckv.astype(BF16), wukv_ref[...])
    km_ref[0] = (kvu[:, :mw] + jnp.concatenate([kr] * MLA_HEADS, axis=1)).astype(BF16)
    vm_ref[0] = _with_ones(kvu[:, mw:]).astype(BF16)


def _odd_weights(w_in, w_uq, w_ukv, da_w, q_rank, kv_rank):
    qw, kw, vw = w_in[:, :da_w], w_in[:, da_w:2 * da_w], w_in[:, 2 * da_w:3 * da_w]
    o = 3 * da_w
    cqw, ckvw, krw = w_in[:, o:o + q_rank], w_in[:, o + q_rank:o + q_rank + kv_rank], w_in[:, o + q_rank + kv_rank:]
    perm_da = _swap_perm(da_w, DA_HEAD_DIM // 2, DA_HEAD_DIM // 4)
    perm_r = _swap_perm(MLA_ROPE, MLA_ROPE // 2, MLA_ROPE // 4)
    d = w_in.shape[0]

    def rope_group(wr):
        return jnp.pad(wr, ((0, 0), (MLA_NOPE, LANES - MLA_NOPE - MLA_ROPE)))

    w_big = jnp.concatenate([qw, qw[:, perm_da], kw, kw[:, perm_da], vw, cqw, ckvw,
                             rope_group(krw), rope_group(krw[:, perm_r])], axis=1)
    dq = MLA_NOPE + MLA_ROPE
    pad = LANES - dq
    uq = w_uq.reshape(q_rank, MLA_HEADS, dq)
    uq_a = jnp.pad(uq, ((0, 0), (0, 0), (0, pad))).reshape(q_rank, MLA_HEADS * LANES)
    uq_s = jnp.pad(uq[:, :, MLA_NOPE:][:, :, perm_r], ((0, 0), (0, 0), (MLA_NOPE, pad)))
    uq_s = uq_s.reshape(q_rank, MLA_HEADS * LANES)
    ukv = w_ukv.reshape(kv_rank, MLA_HEADS, MLA_NOPE + MLA_V)
    uk = jnp.pad(ukv[:, :, :MLA_NOPE], ((0, 0), (0, 0), (0, LANES - MLA_NOPE))).reshape(kv_rank, MLA_HEADS * LANES)
    uv = ukv[:, :, MLA_NOPE:].reshape(kv_rank, MLA_HEADS * MLA_V)
    return (w_big.astype(BF16), jnp.concatenate([uq_a, uq_s], axis=1).astype(BF16),
            jnp.concatenate([uk, uv], axis=1).astype(BF16))


def _in_odd(h, mods, g, w_big, wuq, wukv, qg, kvg, tabs, tm, n_ctx_tiles, da_w):
    b, t, d = h.shape
    q_rank, kv_rank = qg.shape[0], kvg.shape[0]
    lat, ctx = _mod_specs(d, b)
    tab = pl.BlockSpec((tm, LANES), lambda bi, ti: (ti, 0))
    mw = MLA_HEADS * LANES

    def tok_major(width):
        return jax.ShapeDtypeStruct((b, t, width), BF16), _tok_spec(tm, width)

    outs, ospecs = zip(tok_major(da_w), tok_major(da_w), tok_major(2 * da_w),
                       tok_major(mw), tok_major(mw), tok_major(2 * mw))
    return pl.pallas_call(
        functools.partial(_in_odd_kernel, n_ctx_tiles=n_ctx_tiles, da_w=da_w, q_rank=q_rank, kv_rank=kv_rank),
        out_shape=tuple(outs),
        grid=(b, t // tm),
        in_specs=[_tok_spec(tm, d), lat, ctx, _const_spec((1, d)), _const_spec(w_big.shape),
                  tab, tab, tab, tab, _const_spec((1, q_rank)), _const_spec((1, kv_rank)),
                  _const_spec(wuq.shape), _const_spec(wukv.shape)],
        out_specs=tuple(ospecs),
        compiler_params=_params("arbitrary", "arbitrary"),
        name="odd_in_proj",
    )(h, mods, mods, g.reshape(1, d), w_big, *tabs, qg.reshape(1, -1), kvg.reshape(1, -1), wuq, wukv)


def _softmax_pv(qs, k_ref, v_ref):
    n_keys = k_ref.shape[1]
    starts = list(range(0, n_keys, ATTN_KEY_BLOCK))
    m = [None] * len(qs)
    acc = [None] * len(qs)
    for start in starts:
        size = min(ATTN_KEY_BLOCK, n_keys - start)
        k = k_ref[0, start:start + size, :]
        v1 = v_ref[0, start:start + size, :]
        for i, q in enumerate(qs):
            s = _dot_nt(q, k)
            m_blk = jnp.max(s, axis=-1, keepdims=True)
            if start == 0:
                m[i] = m_blk
                acc[i] = _dot(jnp.exp2(s - m_blk).astype(BF16), v1)
            else:
                m_new = jnp.maximum(m[i], m_blk)
                pv = _dot(jnp.exp2(s - m_new).astype(BF16), v1)
                acc[i] = jnp.exp2(m[i] - m_new) * acc[i] + pv
                m[i] = m_new
    return [a[:, :LANES] / a[:, LANES:] for a in acc]


def _attn_kernel(q_ref, k_ref, v_ref, lam_ref, sg_ref, o_ref, *, diff, lam_init):
    q = q_ref[0]
    if not diff:
        o_ref[0] = _softmax_pv([q], k_ref, v_ref)[0].astype(o_ref.dtype)
        return
    first = lax.broadcasted_iota(jnp.int32, q.shape, 1) < DA_HEAD_DIM
    zero = jnp.zeros_like(q)
    o1, o2 = _softmax_pv([jnp.where(first, q, zero), jnp.where(first, zero, q)], k_ref, v_ref)
    lp = lam_ref[...]
    lam = (jnp.exp(jnp.sum(lp[0:1] * lp[1:2], axis=-1, keepdims=True))
           - jnp.exp(jnp.sum(lp[2:3] * lp[3:4], axis=-1, keepdims=True)) + lam_init)
    o_ref[0] = (_rms(o1 - lam * o2) * sg_ref[...] * (1.0 - lam_init)).astype(o_ref.dtype)


def _attention(q, k, v1, lam_p, subln_g, tq, n_keys, diff, lam_init):
    b, t_q, width = q.shape
    heads = width // LANES
    qtile = pl.BlockSpec((1, tq, LANES), lambda bi, hi, ti: (bi, ti, hi))
    khead = pl.BlockSpec((1, n_keys, LANES), lambda bi, hi, ti: (bi, 0, hi))
    vhead = pl.BlockSpec((1, n_keys, 2 * LANES), lambda bi, hi, ti: (bi, 0, hi))

    def const(shape):
        zeros = (0,) * len(shape)
        return pl.BlockSpec(shape, lambda bi, hi, ti: zeros)

    return pl.pallas_call(
        functools.partial(_attn_kernel, diff=diff, lam_init=lam_init),
        out_shape=jax.ShapeDtypeStruct((b, t_q, width), BF16),
        grid=(b, heads, t_q // tq),
        in_specs=[qtile, khead, vhead, const(lam_p.shape), const((1, LANES))],
        out_specs=qtile,
        compiler_params=_params("arbitrary", "arbitrary", "arbitrary"),
        name="diff_attention" if diff else "mla_attention",
    )(q, k, v1, lam_p, subln_g.reshape(1, LANES))


def _attend_segments(q, k, v1, lam_p, subln_g, ctx_len, diff, lam_init, need_ctx):
    t = k.shape[1]
    n_lat = t - ctx_len
    tq_lat = math.gcd(n_lat, 512)
    o_lat = _attention(q[:, ctx_len:], k, v1, lam_p, subln_g, tq_lat, t, diff, lam_init)
    if not need_ctx:
        return o_lat
    o_ctx = _attention(q[:, :ctx_len], k, v1, lam_p, subln_g, ctx_len, ctx_len, diff, lam_init)
    return jnp.concatenate([o_ctx, o_lat], axis=1)


def kernel(x, c, ctx, c_ctx, ada_w, ada_b, norm_mix_g, norm_ffn_g, ffn_w_gu, ffn_w_down, ev_w_in, ev_w_out, hg_lower_bound, hg_out_norm_g, hy_short_w, hy_short_b, hy_filt_w1, hy_filt_b1, hy_filt_freq1, hy_filt_w2, hy_filt_b2, hy_filt_freq2, hy_filt_w3, hy_skip, od_w_in, od_w_out, da_lambda, da_subln_g, mla_q_norm_g, mla_w_uq, mla_kv_norm_g, mla_w_ukv, final_norm_g):
    n_batch, n_lat, d = x.shape
    ctx_len = ctx.shape[1]
    depth = ada_w.shape[0]
    assert n_batch % 2 == 0 and n_batch < MOD_ROWS
    assert n_lat % GRID_W == 0 and ctx_len % HG_CHUNK == 0 and n_lat % FFT_N2 == 0 and ctx_len % FFT_N2 == 0
    tm = math.gcd(math.gcd(ctx_len, n_lat), 256)
    n_ctx_tiles = ctx_len // tm
    hg_width = d // 2
    hy_width = d - hg_width
    da_w = d // 2
    q_rank, kv_rank = mla_q_norm_g.shape[1], mla_kv_norm_g.shape[1]

    cc = jnp.concatenate([c, c_ctx[None], jnp.zeros((MOD_ROWS - n_batch - 1, d), F32)], axis=0)
    mods = _ada(cc, ada_w, ada_b).reshape(depth, MOD_ROWS, 6, d)
    h = jnp.concatenate([ctx, x], axis=1)
    rope_tabs = _rope_tables(n_lat, ctx_len)

    for i in range(depth):
        last = i == depth - 1
        if i % 2 == 0:
            e = i // 2
            z = _in_even(h, mods[i], norm_mix_g[i], ev_w_in[e].astype(BF16), tm, n_ctx_tiles)
            a = _hgrn(z, hg_lower_bound, hg_out_norm_g[e], e, hg_width // HG_HEAD_DIM, ctx_len)
            filt_params = (hy_filt_w1[e], hy_filt_b1[e], hy_filt_freq1[e], hy_filt_w2[e], hy_filt_b2[e],
                           hy_filt_freq2[e], hy_filt_w3[e])
            bb = _hyena(z, 5 * hg_width, hy_width, ctx_len, hy_short_w[e], hy_short_b[e], filt_params, hy_skip[e])
            w_out = ev_w_out[e]
        else:
            o = i // 2
            lam_init = 0.8 - 0.6 * math.exp(-0.3 * i)
            w_big, wuq, wukv = _odd_weights(od_w_in[o], mla_w_uq[o], mla_w_ukv[o], da_w, q_rank, kv_rank)
            qd, kd, vd, qm, km, vm = _in_odd(h, mods[i], norm_mix_g[i], w_big, wuq, wukv, mla_q_norm_g[o],
                                             mla_kv_norm_g[o], rope_tabs, tm, n_ctx_tiles, da_w)
            a = _attend_segments(qd, kd, vd, da_lambda[o], da_subln_g[o], ctx_len, True, lam_init, not last)
            bb = _attend_segments(qm, km, vm, da_lambda[o], da_subln_g[o], ctx_len, False, lam_init, not last)
            w_out = od_w_out[o]
        weights = (w_out.astype(BF16), ffn_w_gu[i].astype(BF16), ffn_w_down[i].astype(BF16))
        if not last:
            h = _post(h, a, bb, mods[i], norm_ffn_g[i], *weights, tm, n_ctx_tiles)
    if a.shape[1] != n_lat:
        a, bb = a[:, ctx_len:], bb[:, ctx_len:]
    return _post_final(h, a, bb, mods[depth - 1], norm_ffn_g[depth - 1], *weights, final_norm_g, tm, n_ctx_tiles)
```

```python
import functools
import math

import numpy as np
import jax
import jax.numpy as jnp
from jax import lax
from jax.experimental import pallas as pl
from jax.experimental.pallas import tpu as pltpu

F32 = jnp.float32
BF16 = jnp.bfloat16
HIGHEST = lax.Precision.HIGHEST

GRID_W = 64
EPS = 1e-6
ROPE_BASE = 10000.0
HG_HEAD_DIM = 128
HG_CHUNK = 64
HY_BANDS = 16
HY_EMB = 2 * HY_BANDS + 1
HY_TARGET = 1e-2
HY_STEEP_PCT = 0.3
HY_GENTLE_PCT = 1.5
DA_HEAD_DIM = 64
MLA_HEADS = 4
MLA_NOPE = 64
MLA_ROPE = 32
MLA_V = 128
LOG2_E = 1.4426950408889634

LANES = 128
SUBLANES = 8
V7X_VMEM_LIMIT_BYTES = 56 * 1024 * 1024
MOD_ROWS = 16
FFT_N2 = 128
FFT_PITCH = FFT_N2 + SUBLANES
FFT_UNROLL = 16
HGRN_UNROLL = 2
ATTN_KEY_BLOCK = 512


def _params(*sem):
    return pltpu.CompilerParams(dimension_semantics=sem, vmem_limit_bytes=V7X_VMEM_LIMIT_BYTES)


def _dot(a, b, precision=None):
    return jnp.dot(a, b, preferred_element_type=F32, precision=precision)


def _dot_nt(a, b):
    return lax.dot_general(a, b, (((1,), (1,)), ((), ())), preferred_element_type=F32)


def _dot_tn(a, b):
    return lax.dot_general(a, b, (((0,), (0,)), ((), ())), preferred_element_type=F32)


def _rms(x):
    return x * lax.rsqrt(jnp.mean(x * x, axis=-1, keepdims=True) + EPS)


def _silu(x):
    return x * jax.nn.sigmoid(x)


def _tile_mod(mod_ref, modc_ref, n_ctx_tiles):
    return jnp.where(pl.program_id(1) < n_ctx_tiles, modc_ref[...], mod_ref[...])


def _ada_kernel(c_ref, w_ref, b_ref, o_ref):
    o_ref[0] = _dot(_silu(c_ref[...]), w_ref[0], HIGHEST) + b_ref[0]


def _ada(cc, ada_w, ada_b):
    depth, d, n = ada_w.shape
    rows = cc.shape[0]
    tn = n // 4
    return pl.pallas_call(
        _ada_kernel,
        out_shape=jax.ShapeDtypeStruct((depth, rows, n), F32),
        grid=(depth, n // tn),
        in_specs=[
            pl.BlockSpec((rows, d), lambda i, j: (0, 0)),
            pl.BlockSpec((1, d, tn), lambda i, j: (i, 0, j)),
            pl.BlockSpec((1, 1, tn), lambda i, j: (i, 0, j)),
        ],
        out_specs=pl.BlockSpec((1, rows, tn), lambda i, j: (i, 0, j)),
        compiler_params=_params("arbitrary", "arbitrary"),
        name="ada_mod",
    )(cc, ada_w, ada_b.reshape(depth, 1, n))


def _tok_spec(tm, width):
    return pl.BlockSpec((1, tm, width), lambda b, t: (b, t, 0))


def _const_spec(shape):
    zeros = (0,) * len(shape)
    return pl.BlockSpec(shape, lambda b, t: zeros)


def _mod_specs(d, n_batch):
    lat = pl.BlockSpec((None, 6, d), lambda b, t: (b, 0, 0))
    ctx = pl.BlockSpec((None, 6, d), lambda b, t: (n_batch, 0, 0))
    return lat, ctx


def _in_even_kernel(x_ref, mod_ref, modc_ref, g_ref, w_ref, o_ref, *, n_ctx_tiles):
    m = _tile_mod(mod_ref, modc_ref, n_ctx_tiles)
    xn = (_rms(x_ref[0]) * g_ref[...]) * (1.0 + m[1:2]) + m[0:1]
    o_ref[0] = _dot(xn.astype(BF16), w_ref[...])


def _in_even(h, mods, g, w, tm, n_ctx_tiles):
    b, t, d = h.shape
    n = w.shape[1]
    lat, ctx = _mod_specs(d, b)
    return pl.pallas_call(
        functools.partial(_in_even_kernel, n_ctx_tiles=n_ctx_tiles),
        out_shape=jax.ShapeDtypeStruct((b, t, n), F32),
        grid=(b, t // tm),
        in_specs=[_tok_spec(tm, d), lat, ctx, _const_spec((1, d)), _const_spec((d, n))],
        out_specs=_tok_spec(tm, n),
        compiler_params=_params("arbitrary", "arbitrary"),
        name="even_in_proj",
    )(h, mods, mods, g.reshape(1, d), w)


def _post_body(m, h, a, bb, g_ref, wo_ref, wgu_ref, wd_ref, hidden):
    half = a.shape[-1]
    y = _dot(a, wo_ref[:half, :]) + _dot(bb, wo_ref[half:, :])
    h1 = h + m[2:3] * y
    xn = (_rms(h1) * g_ref[...]) * (1.0 + m[4:5]) + m[3:4]
    gu = _dot(xn.astype(BF16), wgu_ref[...])
    act = _silu(gu[:, :hidden]) * gu[:, hidden:]
    return h1 + m[5:6] * _dot(act.astype(BF16), wd_ref[...])


def _post_kernel(h_ref, a_ref, b_ref, mod_ref, modc_ref, g_ref, wo_ref, wgu_ref, wd_ref, o_ref,
                 *, n_ctx_tiles, hidden):
    m = _tile_mod(mod_ref, modc_ref, n_ctx_tiles)
    o_ref[0] = _post_body(m, h_ref[0], a_ref[0], b_ref[0], g_ref, wo_ref, wgu_ref, wd_ref, hidden)


def _post_final_kernel(h_ref, a_ref, b_ref, mod_ref, g_ref, wo_ref, wgu_ref, wd_ref, fg_ref, o_ref,
                       *, hidden):
    h2 = _post_body(mod_ref[...], h_ref[0], a_ref[0], b_ref[0], g_ref, wo_ref, wgu_ref, wd_ref, hidden)
    o_ref[0] = _rms(h2) * fg_ref[...]


def _post(h, a, bb, mods, g, wo, wgu, wd, tm, n_ctx_tiles):
    b, t, d = h.shape
    half = a.shape[-1]
    hidden = wd.shape[0]
    lat, ctx = _mod_specs(d, b)
    return pl.pallas_call(
        functools.partial(_post_kernel, n_ctx_tiles=n_ctx_tiles, hidden=hidden),
        out_shape=jax.ShapeDtypeStruct((b, t, d), F32),
        grid=(b, t // tm),
        in_specs=[_tok_spec(tm, d), _tok_spec(tm, half), _tok_spec(tm, half), lat, ctx,
                  _const_spec((1, d)), _const_spec((2 * half, d)), _const_spec((d, 2 * hidden)),
                  _const_spec((hidden, d))],
        out_specs=_tok_spec(tm, d),
        compiler_params=_params("arbitrary", "arbitrary"),
        name="out_proj_ffn",
    )(h, a, bb, mods, mods, g.reshape(1, d), wo, wgu, wd)


def _post_final(h, a_lat, b_lat, mods, g, wo, wgu, wd, final_g, tm, n_ctx_tiles):
    b, t, d = h.shape
    n_lat = a_lat.shape[1]
    half = a_lat.shape[-1]
    hidden = wd.shape[0]
    lat, _ = _mod_specs(d, b)
    return pl.pallas_call(
        functools.partial(_post_final_kernel, hidden=hidden),
        out_shape=jax.ShapeDtypeStruct((b, n_lat, d), F32),
        grid=(b, n_lat // tm),
        in_specs=[pl.BlockSpec((1, tm, d), lambda bi, ti: (bi, ti + n_ctx_tiles, 0)),
                  _tok_spec(tm, half), _tok_spec(tm, half), lat,
                  _const_spec((1, d)), _const_spec((2 * half, d)), _const_spec((d, 2 * hidden)),
                  _const_spec((hidden, d)), _const_spec((1, d))],
        out_specs=_tok_spec(tm, d),
        compiler_params=_params("arbitrary", "arbitrary"),
        name="out_proj_ffn_final",
    )(h, a_lat, b_lat, mods, g.reshape(1, d), wo, wgu, wd, final_g.reshape(1, d))


def _hgrn_tables(tile):
    c = HG_CHUNK
    r = np.arange(c)
    j = r[None, :]
    rt = np.arange(tile)
    blocks = [(j <= r[:, None]), (j > r[:, None])]
    masks, bms = [], []
    m = c
    while m >= 2:
        half = m // 2
        p = r % m
        mid = (r - p + half)[:, None]
        upper = (p >= half)[:, None]
        up = (j >= mid) & (j <= r[:, None]) & upper
        lo = (j > r[:, None]) & (j <= mid - 1) & (~upper)
        blocks.append(up | lo)
        masks.append(np.broadcast_to(upper, (c, LANES)))
        bms.append((rt[:, None] // m) == (rt[None, :] // m))
        m //= 2
    fwd = np.stack(blocks).astype(np.float32)
    bwd = fwd[:, ::-1, ::-1]
    mall = np.stack([fwd.reshape(-1, c), bwd.reshape(-1, c)])
    mall = np.concatenate([mall, mall], axis=2)
    mu = np.stack(masks).astype(np.float32)
    masku = np.tile(np.stack([mu, mu[:, ::-1]]), (1, 1, tile // c, 1))
    return (jnp.asarray(mall, dtype=BF16), jnp.asarray(masku, dtype=F32),
            jnp.asarray(np.stack(bms), dtype=F32))


def _hgrn_kernel(q_ref, ff_ref, fb_ref, i_ref, g_ref, lbp_ref, ng_ref, mall_ref, masku_ref, bm_ref,
                 o_ref, of_ref, ob_ref, *, layer, n_ctx_tiles, n_tiles):
    c = HG_CHUNK
    dk = HG_HEAD_DIM
    n_levels = masku_ref.shape[1]
    tile = masku_ref.shape[2]
    n_sub = tile // c
    lbp = lbp_ref[...]
    ex = jnp.exp(lbp - jnp.max(lbp, axis=0, keepdims=True))
    p = ex / jnp.sum(ex, axis=0, keepdims=True)
    lb = jnp.sum(p[:layer + 1], axis=0) - p[0]

    def sub(x, g):
        return x[g * c:(g + 1) * c]

    dirs = (0, 1)

    def stack_chunks(x):
        return jnp.concatenate([sub(x, g) for g in range(n_sub)], axis=1)

    def bwd_tile(t):
        if isinstance(t, int):
            return n_ctx_tiles - 1 - t if t < n_ctx_tiles else n_tiles - 1 - (t - n_ctx_tiles)
        return jnp.where(t < n_ctx_tiles, n_ctx_tiles - 1 - t, n_tiles - 1 - (t - n_ctx_tiles))

    def tile_rows(ti):
        start = ti * tile
        return pl.ds(start if isinstance(ti, int) else pl.multiple_of(start, tile), tile)

    def run_trips(trips, sts):
        lanes = [(s, d) for s in range(len(trips)) for d in dirs]
        n = range(len(lanes))
        tis = [bwd_tile(t) if d else t for t in trips for d in dirs]
        rows = [tile_rows(ti) for ti in tis]
        q = [q_ref[0, r, :] for r in rows]
        v = [i_ref[0, r, :] for r in rows]
        lbd = [lb[d:d + 1] for _, d in lanes]
        zf = [(ff_ref, fb_ref)[d][0, rows[l], :] for l, (_, d) in enumerate(lanes)]
        f = [lbd[l] + (1.0 - lbd[l]) * jax.nn.sigmoid(zf[l]) for l in n]
        lf = [jnp.log(x) for x in f]
        k = [1.0 - x for x in f]
        hi = [x.astype(BF16) for x in lf]
        lo = [(lf[l] - hi[l].astype(F32)).astype(BF16) for l in n]
        e2 = [_dot(mall_ref[lanes[l][1]], jnp.concatenate([stack_chunks(hi[l]), stack_chunks(lo[l])], axis=0))
              for l in n]
        ee = [[jnp.exp(e2[l][:, g * dk:(g + 1) * dk]) for g in range(n_sub)] for l in n]

        def block(l, idx):
            return jnp.concatenate([ee[l][g][idx * c:(idx + 1) * c] for g in range(n_sub)], axis=0)

        g_in = [block(l, 0) for l in n]
        qin = [(q[l] * g_in[l]).astype(BF16) for l in n]
        kout = [(k[l] * block(l, 1)).astype(BF16) for l in n]
        vb = [x.astype(BF16) for x in v]
        ds = [[_dot_tn(sub(vb[l], g), sub(kout[l], g)) for g in range(n_sub)] for l in n]
        a = [jnp.zeros((tile, tile), F32) for _ in n]
        for lvl in range(n_levels):
            for l, (_, d) in enumerate(lanes):
                gl = block(l, 2 + lvl)
                gu = gl * masku_ref[d, lvl]
                a[l] = a[l] + bm_ref[lvl] * _dot_nt((q[l] * gu).astype(BF16),
                                                    (k[l] * (gl - gu)).astype(BF16))
        o = [_dot(a[l].astype(BF16), vb[l]) + jnp.sum(q[l] * k[l], axis=-1, keepdims=True) * v[l]
             for l in n]
        order = [list(range(n_sub)), list(reversed(range(n_sub)))]
        entering = [[None] * n_sub for _ in n]
        sts = list(sts)
        for l, (_, d) in enumerate(lanes):
            for g in order[d]:
                entering[l][g] = sts[d].astype(BF16)
                last = (g + 1) * c - 1 if d == 0 else g * c
                sts[d] = sts[d] * g_in[l][last:last + 1] + ds[l][g]
        for l, (_, d) in enumerate(lanes):
            out = jnp.concatenate([sub(o[l], g) + _dot_nt(sub(qin[l], g), entering[l][g])
                                   for g in range(n_sub)], axis=0)
            (of_ref, ob_ref)[d][rows[l], :] = out
        return tuple(sts)

    def body(i, carry):
        return run_trips([i * HGRN_UNROLL + s for s in range(HGRN_UNROLL)], carry)

    s0 = jnp.zeros((dk, dk), F32)
    sts = lax.fori_loop(0, n_tiles // HGRN_UNROLL, body, (s0, s0))
    rest = list(range(n_tiles - n_tiles % HGRN_UNROLL, n_tiles))
    if rest:
        run_trips(rest, sts)
    o = _rms(of_ref[...] + ob_ref[...]) * ng_ref[...]
    o_ref[0] = (o * _silu(g_ref[0])).astype(o_ref.dtype)


def _hgrn(z, hg_lower_bound, ng, layer, n_heads, ctx_len):
    b, t, _ = z.shape
    dk = HG_HEAD_DIM
    n_even = hg_lower_bound.shape[0]
    tile = math.gcd(math.gcd(ctx_len, t - ctx_len), 256)
    mall, masku, bm = _hgrn_tables(tile)

    def col(kind):
        return pl.BlockSpec((1, t, dk), lambda bi, hi: (bi, 0, kind * n_heads + hi))

    def const(arr):
        zeros = (0,) * arr.ndim
        return pl.BlockSpec(arr.shape, lambda bi, hi: zeros)

    return pl.pallas_call(
        functools.partial(_hgrn_kernel, layer=layer, n_ctx_tiles=ctx_len // tile, n_tiles=t // tile),
        out_shape=jax.ShapeDtypeStruct((b, t, n_heads * dk), BF16),
        grid=(b, n_heads),
        in_specs=[col(0), col(1), col(2), col(3), col(4),
                  pl.BlockSpec((n_even, 2, dk), lambda bi, hi: (0, 0, hi)),
                  const(ng.reshape(1, dk)), const(mall), const(masku), const(bm)],
        out_specs=pl.BlockSpec((1, t, dk), lambda bi, hi: (bi, 0, hi)),
        scratch_shapes=[pltpu.VMEM((t, dk), F32), pltpu.VMEM((t, dk), F32)],
        compiler_params=_params("arbitrary", "arbitrary"),
        name="hgrn2_scan",
    )(z, z, z, z, z, hg_lower_bound, ng.reshape(1, dk), mall, masku, bm)


def _hy_pre_kernel(x0_ref, x1_ref, v_ref, w0_ref, w1_ref, wv_ref, b0_ref, b1_ref, bv_ref,
                   x0c_ref, u_ref, *, ctx_len):
    t = x0_ref.shape[1]
    row = lax.broadcasted_iota(jnp.int32, (t, LANES), 0)
    first = (row == 0) | (row == ctx_len)
    final = (row == ctx_len - 1) | (row == t - 1)

    def short_conv(z_ref, w_ref, b_ref):
        z = z_ref[0]
        prev = jnp.where(first, 0.0, pltpu.roll(z, 1, 0))
        nxt = jnp.where(final, 0.0, pltpu.roll(z, t - 1, 0))
        w = w_ref[...]
        return prev * w[0:1] + z * w[1:2] + nxt * w[2:3] + b_ref[...]

    x0c_ref[0] = short_conv(x0_ref, w0_ref, b0_ref).astype(x0c_ref.dtype)
    u = short_conv(x1_ref, w1_ref, b1_ref) * short_conv(v_ref, wv_ref, bv_ref)
    u_ref[0] = u.astype(u_ref.dtype)


def _hy_pre(z, short_w, short_b, col0, width, ctx_len):
    b, t, _ = z.shape
    nb = width // LANES
    c0 = col0 // LANES

    def zcol(kind):
        return pl.BlockSpec((1, t, LANES), lambda bi, j: (bi, 0, c0 + kind * nb + j))

    def wcol(kind, rows):
        return pl.BlockSpec((rows, LANES), lambda bi, j: (0, kind * nb + j))

    out = jax.ShapeDtypeStruct((b, t, width), BF16)
    ospec = pl.BlockSpec((1, t, LANES), lambda bi, j: (bi, 0, j))
    sb = short_b.reshape(1, -1)
    return pl.pallas_call(
        functools.partial(_hy_pre_kernel, ctx_len=ctx_len),
        out_shape=(out, out),
        grid=(b, nb),
        in_specs=[zcol(0), zcol(1), zcol(2), wcol(0, 3), wcol(1, 3), wcol(2, 3),
                  wcol(0, 1), wcol(1, 1), wcol(2, 1)],
        out_specs=(ospec, ospec),
        compiler_params=_params("arbitrary", "arbitrary"),
        name="hyena_short_conv",
    )(z, z, z, short_w, short_w, short_w, sb, sb, sb)


def _filter_features(length, n):
    p = np.arange(n)
    is_f = p < length
    is_b = p > n - length
    lag = np.where(is_f, p, np.where(is_b, n - 1 - p, 0))
    tt = np.linspace(0.0, 1.0, length, dtype=np.float32)[lag][:, None]
    w = (2.0 * math.pi * lag.astype(np.float32) / length)[:, None].astype(np.float32)
    bands = np.linspace(1e-4, HY_BANDS - 1, HY_BANDS, dtype=np.float32)[None, :]
    feat = np.concatenate([tt, np.cos(bands * w), -np.sin(bands * w), is_f[:, None], is_b[:, None]],
                          axis=-1).astype(np.float32)
    pad = (-feat.shape[1]) % SUBLANES
    return jnp.asarray(np.pad(feat, ((0, 0), (0, pad))))


def _filt_kernel(feat_ref, w1_ref, b1_ref, fr1_ref, w2_ref, b2_ref, fr2_ref,
                 w3f_ref, w3b_ref, dl_ref, o_ref, hid_ref):
    z = feat_ref[...]
    mf = z[:, HY_EMB:HY_EMB + 1]
    mb = z[:, HY_EMB + 1:HY_EMB + 2]

    @pl.when(pl.program_id(0) == 0)
    def _():
        h1 = jnp.sin(fr1_ref[...] * (_dot(z, w1_ref[...], HIGHEST) + b1_ref[...]))
        hid_ref[...] = jnp.sin(fr2_ref[...] * (_dot(h1, w2_ref[...], HIGHEST) + b2_ref[...]))

    hid = hid_ref[...]
    hf = _dot(hid, w3f_ref[...], HIGHEST)
    hb = _dot(hid, w3b_ref[...], HIGHEST)
    win = jnp.exp(-z[:, 0:1] * dl_ref[...])
    f = (mf * hf + mb * hb) * win
    o_ref[...] = f / jnp.sum(jnp.abs(f), axis=0, keepdims=True)


def _hyena_filter(length, n, w1, b1, fr1, w2, b2, fr2, w3, width):
    feat = _filter_features(length, n)
    nf = feat.shape[1]
    hid = w1.shape[1]
    w1p = jnp.pad(w1, ((0, nf - w1.shape[0]), (0, 0)))
    d_lo = -math.log(HY_TARGET) / HY_GENTLE_PCT
    d_hi = -math.log(HY_TARGET) / HY_STEEP_PCT
    deltas = jnp.asarray(np.linspace(d_lo, d_hi, width, dtype=np.float32)[None, :])
    nb = width // LANES

    def full(shape):
        zeros = (0,) * len(shape)
        return pl.BlockSpec(shape, lambda j: zeros)

    return pl.pallas_call(
        _filt_kernel,
        out_shape=jax.ShapeDtypeStruct((n, width), F32),
        grid=(nb,),
        in_specs=[full((n, nf)), full((nf, hid)), full((1, hid)),
                  full((1, hid)), full((hid, hid)), full((1, hid)), full((1, hid)),
                  pl.BlockSpec((hid, LANES), lambda j: (0, j)),
                  pl.BlockSpec((hid, LANES), lambda j: (0, nb + j)),
                  pl.BlockSpec((1, LANES), lambda j: (0, j))],
        out_specs=pl.BlockSpec((n, LANES), lambda j: (0, j)),
        scratch_shapes=[pltpu.VMEM((n, hid), F32)],
        compiler_params=_params("arbitrary"),
        name="hyena_filter_mlp",
    )(feat, w1p, b1.reshape(1, -1), fr1.reshape(1, -1), w2, b2.reshape(1, -1),
      fr2.reshape(1, -1), w3, w3, deltas)


def _dft_tables(n1, n1_in):
    n2 = FFT_N2
    n = n1 * n2
    a = np.arange(n1)
    j = np.arange(n2)
    ang = -2.0 * np.pi * (a[None, None, :] * a[None, :, None] / n1 + j[:, None, None] * a[None, :, None] / n)
    tr, ti = np.cos(ang), np.sin(ang)
    fwd_a = np.concatenate([np.concatenate([tr, -ti], 2), np.concatenate([ti, tr], 2)], 1)
    trt, tit = np.swapaxes(tr, 1, 2), -np.swapaxes(ti, 1, 2)
    inv_a = np.concatenate([np.concatenate([trt, -tit], 2), np.concatenate([tit, trt], 2)], 1)
    keep = np.concatenate([np.arange(n1_in), n1 + np.arange(n1_in)])
    ang2 = -2.0 * np.pi * (j[:, None] * j[None, :]) / n2
    cr, ci = np.cos(ang2), np.sin(ang2)
    fwd_c = np.block([[cr, -ci], [ci, cr]])
    inv_c = np.block([[cr, ci], [-ci, cr]])
    real_a = np.concatenate([tr, ti], 1)
    return dict(fwd_a=fwd_a[:, :, keep], inv_a=inv_a[:, keep, :], fwd_c=fwd_c, inv_c=inv_c,
                real_a=real_a)


def _slab(idx):
    return pl.ds(pl.multiple_of(idx * FFT_PITCH, SUBLANES), FFT_N2)


def _fft_filter_kernel(f_ref, wa_ref, wc_ref, o_ref, scr_ref, *, n1):
    n2 = FFT_N2
    rows = 2 * n1
    scale = 1.0 / (n1 * n2)

    def split3(x):
        hi = x.astype(BF16)
        lo = (x - hi.astype(F32)).astype(BF16)
        return jnp.concatenate([hi, lo, hi], axis=0)

    def stage_a(j, carry):
        scr_ref[pl.ds(j, rows, stride=FFT_PITCH), :] = _dot(wa_ref[j], split3(f_ref[j]))
        return carry

    lax.fori_loop(0, n2, stage_a, 0, unroll=FFT_UNROLL)

    def stage_c(k, carry):
        x = jnp.concatenate([scr_ref[_slab(k), :], scr_ref[_slab(n1 + k), :]], axis=0)
        o_ref[k] = _dot(wc_ref[...], split3(x)) * scale
        return carry

    lax.fori_loop(0, n1, stage_c, 0, unroll=FFT_UNROLL)


def _split3_cols(w):
    hi = w.astype(np.float32).astype(jnp.bfloat16)
    lo = (w.astype(np.float32) - np.asarray(hi, np.float32)).astype(jnp.bfloat16)
    return jnp.asarray(np.concatenate([hi, hi, lo], axis=-1))


def _fft_filter(filt, n1, tables):
    n2 = FFT_N2
    width = filt.shape[1]
    ft = filt.reshape(n1, n2, width).transpose(1, 0, 2)
    wa = _split3_cols(tables["real_a"])
    wc = _split3_cols(tables["fwd_c"])
    return pl.pallas_call(
        functools.partial(_fft_filter_kernel, n1=n1),
        out_shape=jax.ShapeDtypeStruct((n1, 2 * n2, width), F32),
        grid=(width // LANES,),
        in_specs=[pl.BlockSpec((n2, n1, LANES), lambda c: (0, 0, c)),
                  pl.BlockSpec(wa.shape, lambda c: (0, 0, 0)),
                  pl.BlockSpec(wc.shape, lambda c: (0, 0))],
        out_specs=pl.BlockSpec((n1, 2 * n2, LANES), lambda c: (0, 0, c)),
        scratch_shapes=[pltpu.VMEM((2 * n1 * FFT_PITCH, LANES), F32)],
        compiler_params=_params("arbitrary"),
        name="hyena_filter_dft",
    )(ft, wa, wc)


def _fftconv_kernel(u_ref, x0_ref, fh_ref, skip_ref, wfa_ref, wfc_ref, wic_ref, wia_ref, o_ref,
                    scr_ref, *, n1):
    n2 = FFT_N2
    rows = 2 * n1

    def stage_a(j, carry):
        scr_ref[pl.ds(j, rows, stride=FFT_PITCH), :] = _dot(wfa_ref[j], u_ref[0, j])
        return carry

    lax.fori_loop(0, n2, stage_a, 0, unroll=FFT_UNROLL)

    def stage_c(k, carry):
        x = jnp.concatenate([scr_ref[_slab(k), :], scr_ref[_slab(n1 + k), :]], axis=0)
        xf = _dot(wfc_ref[...], x.astype(BF16))
        fh = fh_ref[k]
        xr, xi, fr, fi = xf[:n2], xf[n2:], fh[:n2], fh[n2:]
        y = jnp.concatenate([xr * fr - xi * fi, xr * fi + xi * fr], axis=0)
        zt = _dot(wic_ref[...], y.astype(BF16))
        scr_ref[_slab(k), :] = zt[:n2]
        scr_ref[_slab(n1 + k), :] = zt[n2:]
        return carry

    lax.fori_loop(0, n1, stage_c, 0, unroll=FFT_UNROLL)

    def stage_ai(j, carry):
        x = scr_ref[pl.ds(j, rows, stride=FFT_PITCH), :]
        y = _dot(wia_ref[j], x.astype(BF16))
        u = u_ref[0, j].astype(F32)
        o_ref[0, j] = (x0_ref[0, j].astype(F32) * (y + skip_ref[...] * u)).astype(o_ref.dtype)
        return carry

    lax.fori_loop(0, n2, stage_ai, 0, unroll=FFT_UNROLL)


def _to_fft_layout(x, n1_in):
    b, length, c = x.shape
    n1_used = length // FFT_N2
    x = x.reshape(b // 2, 2, n1_used, FFT_N2, c)
    x = jnp.pad(x, ((0, 0), (0, 0), (0, n1_in - n1_used), (0, 0), (0, 0)))
    return x.transpose(0, 3, 1, 2, 4).reshape(b // 2, FFT_N2, 2 * n1_in, c)


def _from_fft_layout(y, length):
    p, n2, rows, c = y.shape
    n1_in = rows // 2
    y = y.reshape(p, n2, 2, n1_in, c).transpose(0, 2, 3, 1, 4)
    return y.reshape(2 * p, n1_in * n2, c)[:, :length]


def _fftconv(u, x0c, fh, skip, n1, n1_in, tables):
    _, length, width = u.shape
    ut = _to_fft_layout(u, n1_in)
    xt = _to_fft_layout(x0c, n1_in)
    pairs, n2, rin, _ = ut.shape
    wfa = jnp.asarray(tables["fwd_a"], dtype=BF16)
    wia = jnp.asarray(tables["inv_a"], dtype=BF16)
    wfc = jnp.asarray(tables["fwd_c"], dtype=BF16)
    wic = jnp.asarray(tables["inv_c"], dtype=BF16)
    data = pl.BlockSpec((1, n2, rin, LANES), lambda c, p: (p, 0, 0, c))

    def const(arr):
        zeros = (0,) * arr.ndim
        return pl.BlockSpec(arr.shape, lambda c, p: zeros, pipeline_mode=pl.Buffered(1))

    out = pl.pallas_call(
        functools.partial(_fftconv_kernel, n1=n1),
        out_shape=jax.ShapeDtypeStruct(ut.shape, BF16),
        grid=(width // LANES, pairs),
        in_specs=[data, data,
                  pl.BlockSpec((n1, 2 * n2, LANES), lambda c, p: (0, 0, c)),
                  pl.BlockSpec((1, LANES), lambda c, p: (0, c)),
                  const(wfa), const(wfc), const(wic), const(wia)],
        out_specs=data,
        scratch_shapes=[pltpu.VMEM((2 * n1 * FFT_PITCH, LANES), F32)],
        compiler_params=_params("arbitrary", "arbitrary"),
        name="hyena_dft_conv",
    )(ut, xt, fh, skip.reshape(1, width), wfa, wfc, wic, wia)
    return _from_fft_layout(out, length)


def _hyena(z, col0, width, ctx_len, short_w, short_b, filt_params, skip):
    x0c, u = _hy_pre(z, short_w, short_b, col0, width, ctx_len)
    outs = []
    for lo, hi in ((0, ctx_len), (ctx_len, z.shape[1])):
        length = hi - lo
        n1_in = max(length // FFT_N2, SUBLANES)
        n1 = 2 * n1_in
        tables = _dft_tables(n1, n1_in)
        filt = _hyena_filter(length, n1 * FFT_N2, *filt_params, width)
        fh = _fft_filter(filt, n1, tables)
        outs.append(_fftconv(u[:, lo:hi], x0c[:, lo:hi], fh, skip, n1, n1_in, tables))
    return jnp.concatenate(outs, axis=1)


def _rope_tables(n_lat, ctx_len):
    tok = np.arange(n_lat)
    row, colp = tok // GRID_W, tok % GRID_W

    def axial(half):
        inv = ROPE_BASE ** (-np.arange(half, dtype=np.float32) / half)
        parts_c, parts_s = [], []
        for pos in (row, colp):
            ang = pos.astype(np.float32)[:, None] * inv
            parts_c += [np.cos(ang), np.cos(ang)]
            parts_s += [-np.sin(ang), np.sin(ang)]
        return np.concatenate(parts_c, 1), np.concatenate(parts_s, 1)

    dc, ds = axial(DA_HEAD_DIM // 4)
    mc, ms = axial(MLA_ROPE // 4)
    ones, zeros = np.ones((n_lat, MLA_NOPE), np.float32), np.zeros((n_lat, MLA_NOPE), np.float32)
    padc = np.ones((n_lat, LANES - MLA_NOPE - MLA_ROPE), np.float32)
    tabs = [np.concatenate([dc, dc], 1), np.concatenate([ds, ds], 1),
            np.concatenate([ones, mc, padc], 1), np.concatenate([zeros, ms, 0 * padc], 1)]
    out = []
    for i, tb in enumerate(tabs):
        ctx_rows = np.ones((ctx_len, LANES), np.float32) if i % 2 == 0 else np.zeros((ctx_len, LANES), np.float32)
        out.append(jnp.asarray(np.concatenate([ctx_rows, tb.astype(np.float32)], 0)))
    return out


def _swap_perm(width, group, half):
    idx = np.arange(width)
    pos = idx % group
    return np.where((pos % (2 * half)) < half, idx + half, idx - half)


def _with_ones(v):
    ones = jnp.ones((v.shape[0], LANES), v.dtype)
    parts = []
    for hd in range(v.shape[1] // LANES):
        parts += [v[:, hd * LANES:(hd + 1) * LANES], ones]
    return jnp.concatenate(parts, axis=1)


def _in_odd_kernel(x_ref, mod_ref, modc_ref, g_ref, w_ref, dc_ref, ds_ref, mc_ref, ms_ref,
                   qg_ref, kvg_ref, wuq_ref, wukv_ref,
                   qd_ref, kd_ref, vd_ref, qm_ref, km_ref, vm_ref, *, n_ctx_tiles, da_w, q_rank, kv_rank):
    m = _tile_mod(mod_ref, modc_ref, n_ctx_tiles)
    xn = (_rms(x_ref[0]) * g_ref[...]) * (1.0 + m[1:2]) + m[0:1]
    z = _dot(xn.astype(BF16), w_ref[...])
    nrep = da_w // LANES
    dc = jnp.concatenate([dc_ref[...]] * nrep, axis=1)
    ds = jnp.concatenate([ds_ref[...]] * nrep, axis=1)
    mc = jnp.concatenate([mc_ref[...]] * MLA_HEADS, axis=1)
    ms = jnp.concatenate([ms_ref[...]] * MLA_HEADS, axis=1)
    sa = DA_HEAD_DIM ** -0.5 * LOG2_E
    sm = (MLA_NOPE + MLA_ROPE) ** -0.5 * LOG2_E
    o = 0
    qd_ref[0] = ((z[:, o:o + da_w] * dc + z[:, o + da_w:o + 2 * da_w] * ds) * sa).astype(BF16)
    o += 2 * da_w
    kd_ref[0] = (z[:, o:o + da_w] * dc + z[:, o + da_w:o + 2 * da_w] * ds).astype(BF16)
    o += 2 * da_w
    vd_ref[0] = _with_ones(z[:, o:o + da_w]).astype(BF16)
    o += da_w
    cq = _rms(z[:, o:o + q_rank]) * qg_ref[...]
    o += q_rank
    ckv = _rms(z[:, o:o + kv_rank]) * kvg_ref[...]
    o += kv_rank
    kr = z[:, o:o + LANES] * mc_ref[...] + z[:, o + LANES:o + 2 * LANES] * ms_ref[...]
    mw = MLA_HEADS * LANES
    qu = _dot(cq.astype(BF16), wuq_ref[...])
    qm_ref[0] = ((qu[:, :mw] * mc + qu[:, mw:] * ms) * sm).astype(BF16)
    kvu = _dot(ckv.astype(BF16), wukv_ref[...])
    km_ref[0] = (kvu[:, :mw] + jnp.concatenate([kr] * MLA_HEADS, axis=1)).astype(BF16)
    vm_ref[0] = _with_ones(kvu[:, mw:]).astype(BF16)


def _odd_weights(w_in, w_uq, w_ukv, da_w, q_rank, kv_rank):
    qw, kw, vw = w_in[:, :da_w], w_in[:, da_w:2 * da_w], w_in[:, 2 * da_w:3 * da_w]
    o = 3 * da_w
    cqw, ckvw, krw = w_in[:, o:o + q_rank], w_in[:, o + q_rank:o + q_rank + kv_rank], w_in[:, o + q_rank + kv_rank:]
    perm_da = _swap_perm(da_w, DA_HEAD_DIM // 2, DA_HEAD_DIM // 4)
    perm_r = _swap_perm(MLA_ROPE, MLA_ROPE // 2, MLA_ROPE // 4)
    d = w_in.shape[0]

    def rope_group(wr):
        return jnp.pad(wr, ((0, 0), (MLA_NOPE, LANES - MLA_NOPE - MLA_ROPE)))

    w_big = jnp.concatenate([qw, qw[:, perm_da], kw, kw[:, perm_da], vw, cqw, ckvw,
                             rope_group(krw), rope_group(krw[:, perm_r])], axis=1)
    dq = MLA_NOPE + MLA_ROPE
    pad = LANES - dq
    uq = w_uq.reshape(q_rank, MLA_HEADS, dq)
    uq_a = jnp.pad(uq, ((0, 0), (0, 0), (0, pad))).reshape(q_rank, MLA_HEADS * LANES)
    uq_s = jnp.pad(uq[:, :, MLA_NOPE:][:, :, perm_r], ((0, 0), (0, 0), (MLA_NOPE, pad)))
    uq_s = uq_s.reshape(q_rank, MLA_HEADS * LANES)
    ukv = w_ukv.reshape(kv_rank, MLA_HEADS, MLA_NOPE + MLA_V)
    uk = jnp.pad(ukv[:, :, :MLA_NOPE], ((0, 0), (0, 0), (0, LANES - MLA_NOPE))).reshape(kv_rank, MLA_HEADS * LANES)
    uv = ukv[:, :, MLA_NOPE:].reshape(kv_rank, MLA_HEADS * MLA_V)
    return (w_big.astype(BF16), jnp.concatenate([uq_a, uq_s], axis=1).astype(BF16),
            jnp.concatenate([uk, uv], axis=1).astype(BF16))


def _in_odd(h, mods, g, w_big, wuq, wukv, qg, kvg, tabs, tm, n_ctx_tiles, da_w):
    b, t, d = h.shape
    q_rank, kv_rank = qg.shape[0], kvg.shape[0]
    lat, ctx = _mod_specs(d, b)
    tab = pl.BlockSpec((tm, LANES), lambda bi, ti: (ti, 0))
    mw = MLA_HEADS * LANES

    def tok_major(width):
        return jax.ShapeDtypeStruct((b, t, width), BF16), _tok_spec(tm, width)

    outs, ospecs = zip(tok_major(da_w), tok_major(da_w), tok_major(2 * da_w),
                       tok_major(mw), tok_major(mw), tok_major(2 * mw))
    return pl.pallas_call(
        functools.partial(_in_odd_kernel, n_ctx_tiles=n_ctx_tiles, da_w=da_w, q_rank=q_rank, kv_rank=kv_rank),
        out_shape=tuple(outs),
        grid=(b, t // tm),
        in_specs=[_tok_spec(tm, d), lat, ctx, _const_spec((1, d)), _const_spec(w_big.shape),
                  tab, tab, tab, tab, _const_spec((1, q_rank)), _const_spec((1, kv_rank)),
                  _const_spec(wuq.shape), _const_spec(wukv.shape)],
        out_specs=tuple(ospecs),
        compiler_params=_params("arbitrary", "arbitrary"),
        name="odd_in_proj",
    )(h, mods, mods, g.reshape(1, d), w_big, *tabs, qg.reshape(1, -1), kvg.reshape(1, -1), wuq, wukv)


def _softmax_pv(qs, k_ref, v_ref):
    n_keys = k_ref.shape[1]
    starts = list(range(0, n_keys, ATTN_KEY_BLOCK))
    m = [None] * len(qs)
    acc = [None] * len(qs)
    for start in starts:
        size = min(ATTN_KEY_BLOCK, n_keys - start)
        k = k_ref[0, start:start + size, :]
        v1 = v_ref[0, start:start + size, :]
        for i, q in enumerate(qs):
            s = _dot_nt(q, k)
            m_blk = jnp.max(s, axis=-1, keepdims=True)
            if start == 0:
                m[i] = m_blk
                acc[i] = _dot(jnp.exp2(s - m_blk).astype(BF16), v1)
            else:
                m_new = jnp.maximum(m[i], m_blk)
                pv = _dot(jnp.exp2(s - m_new).astype(BF16), v1)
                acc[i] = jnp.exp2(m[i] - m_new) * acc[i] + pv
                m[i] = m_new
    return [a[:, :LANES] / a[:, LANES:] for a in acc]


def _attn_kernel(q_ref, k_ref, v_ref, lam_ref, sg_ref, o_ref, *, diff, lam_init):
    q = q_ref[0]
    if not diff:
        o_ref[0] = _softmax_pv([q], k_ref, v_ref)[0].astype(o_ref.dtype)
        return
    first = lax.broadcasted_iota(jnp.int32, q.shape, 1) < DA_HEAD_DIM
    zero = jnp.zeros_like(q)
    o1, o2 = _softmax_pv([jnp.where(first, q, zero), jnp.where(first, zero, q)], k_ref, v_ref)
    lp = lam_ref[...]
    lam = (jnp.exp(jnp.sum(lp[0:1] * lp[1:2], axis=-1, keepdims=True))
           - jnp.exp(jnp.sum(lp[2:3] * lp[3:4], axis=-1, keepdims=True)) + lam_init)
    o_ref[0] = (_rms(o1 - lam * o2) * sg_ref[...] * (1.0 - lam_init)).astype(o_ref.dtype)


def _attention(q, k, v1, lam_p, subln_g, tq, n_keys, diff, lam_init):
    b, t_q, width = q.shape
    heads = width // LANES
    qtile = pl.BlockSpec((1, tq, LANES), lambda bi, hi, ti: (bi, ti, hi))
    khead = pl.BlockSpec((1, n_keys, LANES), lambda bi, hi, ti: (bi, 0, hi))
    vhead = pl.BlockSpec((1, n_keys, 2 * LANES), lambda bi, hi, ti: (bi, 0, hi))

    def const(shape):
        zeros = (0,) * len(shape)
        return pl.BlockSpec(shape, lambda bi, hi, ti: zeros)

    return pl.pallas_call(
        functools.partial(_attn_kernel, diff=diff, lam_init=lam_init),
        out_shape=jax.ShapeDtypeStruct((b, t_q, width), BF16),
        grid=(b, heads, t_q // tq),
        in_specs=[qtile, khead, vhead, const(lam_p.shape), const((1, LANES))],
        out_specs=qtile,
        compiler_params=_params("arbitrary", "arbitrary", "arbitrary"),
        name="diff_attention" if diff else "mla_attention",
    )(q, k, v1, lam_p, subln_g.reshape(1, LANES))


def _attend_segments(q, k, v1, lam_p, subln_g, ctx_len, diff, lam_init, need_ctx):
    t = k.shape[1]
    n_lat = t - ctx_len
    tq_lat = math.gcd(n_lat, 512)
    o_lat = _attention(q[:, ctx_len:], k, v1, lam_p, subln_g, tq_lat, t, diff, lam_init)
    if not need_ctx:
        return o_lat
    o_ctx = _attention(q[:, :ctx_len], k, v1, lam_p, subln_g, ctx_len, ctx_len, diff, lam_init)
    return jnp.concatenate([o_ctx, o_lat], axis=1)


def kernel(x, c, ctx, c_ctx, ada_w, ada_b, norm_mix_g, norm_ffn_g, ffn_w_gu, ffn_w_down, ev_w_in, ev_w_out, hg_lower_bound, hg_out_norm_g, hy_short_w, hy_short_b, hy_filt_w1, hy_filt_b1, hy_filt_freq1, hy_filt_w2, hy_filt_b2, hy_filt_freq2, hy_filt_w3, hy_skip, od_w_in, od_w_out, da_lambda, da_subln_g, mla_q_norm_g, mla_w_uq, mla_kv_norm_g, mla_w_ukv, final_norm_g):
    n_batch, n_lat, d = x.shape
    ctx_len = ctx.shape[1]
    depth = ada_w.shape[0]
    assert n_batch % 2 == 0 and n_batch < MOD_ROWS
    assert n_lat % GRID_W == 0 and ctx_len % HG_CHUNK == 0 and n_lat % FFT_N2 == 0 and ctx_len % FFT_N2 == 0
    tm = math.gcd(math.gcd(ctx_len, n_lat), 256)
    n_ctx_tiles = ctx_len // tm
    hg_width = d // 2
    hy_width = d - hg_width
    da_w = d // 2
    q_rank, kv_rank = mla_q_norm_g.shape[1], mla_kv_norm_g.shape[1]

    cc = jnp.concatenate([c, c_ctx[None], jnp.zeros((MOD_ROWS - n_batch - 1, d), F32)], axis=0)
    mods = _ada(cc, ada_w, ada_b).reshape(depth, MOD_ROWS, 6, d)
    h = jnp.concatenate([ctx, x], axis=1)
    rope_tabs = _rope_tables(n_lat, ctx_len)

    for i in range(depth):
        last = i == depth - 1
        if i % 2 == 0:
            e = i // 2
            z = _in_even(h, mods[i], norm_mix_g[i], ev_w_in[e].astype(BF16), tm, n_ctx_tiles)
            a = _hgrn(z, hg_lower_bound, hg_out_norm_g[e], e, hg_width // HG_HEAD_DIM, ctx_len)
            filt_params = (hy_filt_w1[e], hy_filt_b1[e], hy_filt_freq1[e], hy_filt_w2[e], hy_filt_b2[e],
                           hy_filt_freq2[e], hy_filt_w3[e])
            bb = _hyena(z, 5 * hg_width, hy_width, ctx_len, hy_short_w[e], hy_short_b[e], filt_params, hy_skip[e])
            w_out = ev_w_out[e]
        else:
            o = i // 2
            lam_init = 0.8 - 0.6 * math.exp(-0.3 * i)
            w_big, wuq, wukv = _odd_weights(od_w_in[o], mla_w_uq[o], mla_w_ukv[o], da_w, q_rank, kv_rank)
            qd, kd, vd, qm, km, vm = _in_odd(h, mods[i], norm_mix_g[i], w_big, wuq, wukv, mla_q_norm_g[o],
                                             mla_kv_norm_g[o], rope_tabs, tm, n_ctx_tiles, da_w)
            a = _attend_segments(qd, kd, vd, da_lambda[o], da_subln_g[o], ctx_len, True, lam_init, not last)
            bb = _attend_segments(qm, km, vm, da_lambda[o], da_subln_g[o], ctx_len, False, lam_init, not last)
            w_out = od_w_out[o]
        weights = (w_out.astype(BF16), ffn_w_gu[i].astype(BF16), ffn_w_down[i].astype(BF16))
        if not last:
            h = _post(h, a, bb, mods[i], norm_ffn_g[i], *weights, tm, n_ctx_tiles)
    if a.shape[1] != n_lat:
        a, bb = a[:, ctx_len:], bb[:, ctx_len:]
    return _post_final(h, a, bb, mods[depth - 1], norm_ffn_g[depth - 1], *weights, final_norm_g, tm, n_ctx_tiles)
```

```python
import functools
import math

import numpy as np
import jax
import jax.numpy as jnp
from jax import lax
from jax.experimental import pallas as pl
from jax.experimental.pallas import tpu as pltpu

F32 = jnp.float32
BF16 = jnp.bfloat16
HIGHEST = lax.Precision.HIGHEST

GRID_W = 64
EPS = 1e-6
ROPE_BASE = 10000.0
HG_HEAD_DIM = 128
HG_CHUNK = 64
HY_BANDS = 16
HY_EMB = 2 * HY_BANDS + 1
HY_TARGET = 1e-2
HY_STEEP_PCT = 0.3
HY_GENTLE_PCT = 1.5
DA_HEAD_DIM = 64
MLA_HEADS = 4
MLA_NOPE = 64
MLA_ROPE = 32
MLA_V = 128
LOG2_E = 1.4426950408889634

LANES = 128
SUBLANES = 8
V7X_VMEM_LIMIT_BYTES = 56 * 1024 * 1024
MOD_ROWS = 16
FFT_N2 = 128
FFT_PITCH = FFT_N2 + SUBLANES
FFT_UNROLL = 16
HGRN_LOCKSTEP = 1
HGRN_UNROLL = 2
ATTN_KEY_BLOCK = 512


def _params(*sem):
    return pltpu.CompilerParams(dimension_semantics=sem, vmem_limit_bytes=V7X_VMEM_LIMIT_BYTES)


def _dot(a, b, precision=None):
    return jnp.dot(a, b, preferred_element_type=F32, precision=precision)


def _dot_nt(a, b):
    return lax.dot_general(a, b, (((1,), (1,)), ((), ())), preferred_element_type=F32)


def _dot_tn(a, b):
    return lax.dot_general(a, b, (((0,), (0,)), ((), ())), preferred_element_type=F32)


def _rms(x):
    return x * lax.rsqrt(jnp.mean(x * x, axis=-1, keepdims=True) + EPS)


def _silu(x):
    return x * jax.nn.sigmoid(x)


def _tile_mod(mod_ref, modc_ref, n_ctx_tiles):
    return jnp.where(pl.program_id(1) < n_ctx_tiles, modc_ref[...], mod_ref[...])


def _ada_kernel(c_ref, w_ref, b_ref, o_ref):
    o_ref[0] = _dot(_silu(c_ref[...]), w_ref[0], HIGHEST) + b_ref[0]


def _ada(cc, ada_w, ada_b):
    depth, d, n = ada_w.shape
    rows = cc.shape[0]
    tn = n // 4
    return pl.pallas_call(
        _ada_kernel,
        out_shape=jax.ShapeDtypeStruct((depth, rows, n), F32),
        grid=(depth, n // tn),
        in_specs=[
            pl.BlockSpec((rows, d), lambda i, j: (0, 0)),
            pl.BlockSpec((1, d, tn), lambda i, j: (i, 0, j)),
            pl.BlockSpec((1, 1, tn), lambda i, j: (i, 0, j)),
        ],
        out_specs=pl.BlockSpec((1, rows, tn), lambda i, j: (i, 0, j)),
        compiler_params=_params("arbitrary", "arbitrary"),
        name="ada_mod",
    )(cc, ada_w, ada_b.reshape(depth, 1, n))


def _tok_spec(tm, width):
    return pl.BlockSpec((1, tm, width), lambda b, t: (b, t, 0))


def _const_spec(shape):
    zeros = (0,) * len(shape)
    return pl.BlockSpec(shape, lambda b, t: zeros)


def _mod_specs(d, n_batch):
    lat = pl.BlockSpec((None, 6, d), lambda b, t: (b, 0, 0))
    ctx = pl.BlockSpec((None, 6, d), lambda b, t: (n_batch, 0, 0))
    return lat, ctx


def _in_even_kernel(x_ref, mod_ref, modc_ref, g_ref, w_ref, o_ref, *, n_ctx_tiles):
    m = _tile_mod(mod_ref, modc_ref, n_ctx_tiles)
    xn = (_rms(x_ref[0]) * g_ref[...]) * (1.0 + m[1:2]) + m[0:1]
    o_ref[0] = _dot(xn.astype(BF16), w_ref[...])


def _in_even(h, mods, g, w, tm, n_ctx_tiles):
    b, t, d = h.shape
    n = w.shape[1]
    lat, ctx = _mod_specs(d, b)
    return pl.pallas_call(
        functools.partial(_in_even_kernel, n_ctx_tiles=n_ctx_tiles),
        out_shape=jax.ShapeDtypeStruct((b, t, n), F32),
        grid=(b, t // tm),
        in_specs=[_tok_spec(tm, d), lat, ctx, _const_spec((1, d)), _const_spec((d, n))],
        out_specs=_tok_spec(tm, n),
        compiler_params=_params("arbitrary", "arbitrary"),
        name="even_in_proj",
    )(h, mods, mods, g.reshape(1, d), w)


def _post_body(m, h, a, bb, g_ref, wo_ref, wgu_ref, wd_ref, hidden):
    half = a.shape[-1]
    y = _dot(a, wo_ref[:half, :]) + _dot(bb, wo_ref[half:, :])
    h1 = h + m[2:3] * y
    xn = (_rms(h1) * g_ref[...]) * (1.0 + m[4:5]) + m[3:4]
    gu = _dot(xn.astype(BF16), wgu_ref[...])
    act = _silu(gu[:, :hidden]) * gu[:, hidden:]
    return h1 + m[5:6] * _dot(act.astype(BF16), wd_ref[...])


def _post_kernel(h_ref, a_ref, b_ref, mod_ref, modc_ref, g_ref, wo_ref, wgu_ref, wd_ref, o_ref,
                 *, n_ctx_tiles, hidden):
    m = _tile_mod(mod_ref, modc_ref, n_ctx_tiles)
    o_ref[0] = _post_body(m, h_ref[0], a_ref[0], b_ref[0], g_ref, wo_ref, wgu_ref, wd_ref, hidden)


def _post_final_kernel(h_ref, a_ref, b_ref, mod_ref, g_ref, wo_ref, wgu_ref, wd_ref, fg_ref, o_ref,
                       *, hidden):
    h2 = _post_body(mod_ref[...], h_ref[0], a_ref[0], b_ref[0], g_ref, wo_ref, wgu_ref, wd_ref, hidden)
    o_ref[0] = _rms(h2) * fg_ref[...]


def _post(h, a, bb, mods, g, wo, wgu, wd, tm, n_ctx_tiles):
    b, t, d = h.shape
    half = a.shape[-1]
    hidden = wd.shape[0]
    lat, ctx = _mod_specs(d, b)
    return pl.pallas_call(
        functools.partial(_post_kernel, n_ctx_tiles=n_ctx_tiles, hidden=hidden),
        out_shape=jax.ShapeDtypeStruct((b, t, d), F32),
        grid=(b, t // tm),
        in_specs=[_tok_spec(tm, d), _tok_spec(tm, half), _tok_spec(tm, half), lat, ctx,
                  _const_spec((1, d)), _const_spec((2 * half, d)), _const_spec((d, 2 * hidden)),
                  _const_spec((hidden, d))],
        out_specs=_tok_spec(tm, d),
        compiler_params=_params("arbitrary", "arbitrary"),
        name="out_proj_ffn",
    )(h, a, bb, mods, mods, g.reshape(1, d), wo, wgu, wd)


def _post_final(h, a_lat, b_lat, mods, g, wo, wgu, wd, final_g, tm, n_ctx_tiles):
    b, t, d = h.shape
    n_lat = a_lat.shape[1]
    half = a_lat.shape[-1]
    hidden = wd.shape[0]
    lat, _ = _mod_specs(d, b)
    return pl.pallas_call(
        functools.partial(_post_final_kernel, hidden=hidden),
        out_shape=jax.ShapeDtypeStruct((b, n_lat, d), F32),
        grid=(b, n_lat // tm),
        in_specs=[pl.BlockSpec((1, tm, d), lambda bi, ti: (bi, ti + n_ctx_tiles, 0)),
                  _tok_spec(tm, half), _tok_spec(tm, half), lat,
                  _const_spec((1, d)), _const_spec((2 * half, d)), _const_spec((d, 2 * hidden)),
                  _const_spec((hidden, d)), _const_spec((1, d))],
        out_specs=_tok_spec(tm, d),
        compiler_params=_params("arbitrary", "arbitrary"),
        name="out_proj_ffn_final",
    )(h, a_lat, b_lat, mods, g.reshape(1, d), wo, wgu, wd, final_g.reshape(1, d))


def _hgrn_tables(tile):
    c = HG_CHUNK
    r = np.arange(c)
    j = r[None, :]
    blocks = [(j <= r[:, None])]
    masks, bms = [], []
    m = c
    while m >= 2:
        half = m // 2
        p = r % m
        mid = (r - p + half)[:, None]
        upper = (p >= half)[:, None]
        up = (j >= mid) & (j <= r[:, None]) & upper
        lo = (j > r[:, None]) & (j <= mid - 1) & (~upper)
        if m > 2:
            blocks.append(up | lo)
        masks.append(np.broadcast_to(upper, (c, LANES)))
        bms.append((r[:, None] // m) == (r[None, :] // m))
        m //= 2
    fwd = np.stack(blocks).astype(np.float32)
    bwd = fwd[:, ::-1, ::-1]
    mall = np.stack([fwd.reshape(-1, c), bwd.reshape(-1, c)])
    mall = np.concatenate([mall, mall], axis=2)
    mu = np.stack(masks).astype(np.float32)
    masku = np.tile(np.stack([mu, mu[:, ::-1]]), (1, 1, tile // c, 1))
    bmp = np.stack([np.concatenate([bms[i], bms[i + 1]], axis=1) for i in range(0, len(bms), 2)])
    return (jnp.asarray(mall, dtype=BF16), jnp.asarray(masku, dtype=F32), jnp.asarray(bmp, dtype=F32))


def _hgrn_kernel(q_ref, ff_ref, fb_ref, i_ref, g_ref, lbp_ref, ng_ref, mall_ref, masku_ref, bmp_ref,
                 o_ref, of_ref, ob_ref, *, layer, n_ctx_tiles, n_tiles):
    c = HG_CHUNK
    dk = HG_HEAD_DIM
    n_levels = masku_ref.shape[1]
    tile = masku_ref.shape[2]
    n_sub = tile // c
    lbp = lbp_ref[...]
    ex = jnp.exp(lbp - jnp.max(lbp, axis=0, keepdims=True))
    p = ex / jnp.sum(ex, axis=0, keepdims=True)
    lb = jnp.sum(p[:layer + 1], axis=0) - p[0]

    def sub(x, g):
        return x[g * c:(g + 1) * c]

    dirs = (0, 1)

    def stack_chunks(x):
        return jnp.concatenate([sub(x, g) for g in range(n_sub)], axis=1)

    def bwd_tile(t):
        if isinstance(t, int):
            return n_ctx_tiles - 1 - t if t < n_ctx_tiles else n_tiles - 1 - (t - n_ctx_tiles)
        return jnp.where(t < n_ctx_tiles, n_ctx_tiles - 1 - t, n_tiles - 1 - (t - n_ctx_tiles))

    def tile_rows(ti):
        start = ti * tile
        return pl.ds(start if isinstance(ti, int) else pl.multiple_of(start, tile), tile)

    def run_trips(trips, sts):
        lanes = [(s, d) for s in range(len(trips)) for d in dirs]
        n = range(len(lanes))
        tis = [bwd_tile(t) if d else t for t in trips for d in dirs]
        rows = [tile_rows(ti) for ti in tis]
        q = [q_ref[0, r, :] for r in rows]
        v = [i_ref[0, r, :] for r in rows]
        lbd = [lb[d:d + 1] for _, d in lanes]
        zf = [(ff_ref, fb_ref)[d][0, rows[l], :] for l, (_, d) in enumerate(lanes)]
        f = [lbd[l] + (1.0 - lbd[l]) * jax.nn.sigmoid(zf[l]) for l in n]
        lf = [jnp.log(x) for x in f]
        k = [1.0 - x for x in f]
        hi = [x.astype(BF16) for x in lf]
        lo = [(lf[l] - hi[l].astype(F32)).astype(BF16) for l in n]
        e2 = [_dot(mall_ref[lanes[l][1]], jnp.concatenate([stack_chunks(hi[l]), stack_chunks(lo[l])], axis=0))
              for l in n]
        ee = [[jnp.exp(e2[l][:, g * dk:(g + 1) * dk]) for g in range(n_sub)] for l in n]

        def block(l, idx):
            return jnp.concatenate([ee[l][g][idx * c:(idx + 1) * c] for g in range(n_sub)], axis=0)

        def tail_decay(l, g):
            cum = e2[l][0:c, g * dk:(g + 1) * dk]
            last = c - 1 if lanes[l][1] == 0 else 0
            return jnp.exp(cum[last:last + 1] - cum)

        g_in = [block(l, 0) for l in n]
        qin = [(q[l] * g_in[l]).astype(BF16) for l in n]
        kout = [(k[l] * jnp.concatenate([tail_decay(l, g) for g in range(n_sub)], axis=0)).astype(BF16)
                for l in n]
        vb = [x.astype(BF16) for x in v]
        zero = jnp.zeros((c, dk), BF16)
        ds_all = [_dot_tn(vb[l], jnp.concatenate(
            [jnp.concatenate([sub(kout[l], g) if gg == g else zero for gg in range(n_sub)], axis=1)
             for g in range(n_sub)], axis=0)) for l in n]
        ds = [[ds_all[l][:, g * dk:(g + 1) * dk] for g in range(n_sub)] for l in n]
        qu = [[None] * n_levels for _ in n]
        kl = [[None] * n_levels for _ in n]
        for lvl in range(n_levels):
            for l, (_, d) in enumerate(lanes):
                mu = masku_ref[d, lvl]
                if lvl == n_levels - 1:
                    gu, gd = f[l] * mu, 1.0 - mu
                else:
                    gl = block(l, 1 + lvl)
                    gu = gl * mu
                    gd = gl - gu
                qu[l][lvl] = (q[l] * gu).astype(BF16)
                kl[l][lvl] = (k[l] * gd).astype(BF16)
        scores = [[[None] * (n_levels // 2) for _ in range(n_sub)] for _ in n]
        for pair in range(n_levels // 2):
            lv_a, lv_b = 2 * pair, 2 * pair + 1
            for g in range(n_sub):
                for l in n:
                    lhs = jnp.concatenate([sub(qu[l][lv_a], g), sub(qu[l][lv_b], g)], axis=1)
                    rhs = jnp.concatenate([jnp.concatenate([sub(kl[l][lv_a], g), zero], axis=1),
                                           jnp.concatenate([zero, sub(kl[l][lv_b], g)], axis=1)], axis=0)
                    scores[l][g][pair] = (_dot_nt(lhs, rhs) * bmp_ref[pair]).astype(BF16)
        o = [jnp.concatenate([_dot(jnp.concatenate(scores[l][g], axis=1),
                                   jnp.concatenate([sub(vb[l], g)] * n_levels, axis=0))
                              for g in range(n_sub)], axis=0)
             + jnp.sum(q[l] * k[l], axis=-1, keepdims=True) * v[l] for l in n]
        order = [list(range(n_sub)), list(reversed(range(n_sub)))]
        entering = [[None] * n_sub for _ in n]
        sts = list(sts)
        for l, (_, d) in enumerate(lanes):
            for g in order[d]:
                entering[l][g] = sts[d].astype(BF16)
                last = (g + 1) * c - 1 if d == 0 else g * c
                sts[d] = sts[d] * g_in[l][last:last + 1] + ds[l][g]
        for l, (_, d) in enumerate(lanes):
            out = jnp.concatenate([sub(o[l], g) + _dot_nt(sub(qin[l], g), entering[l][g])
                                   for g in range(n_sub)], axis=0)
            (of_ref, ob_ref)[d][rows[l], :] = out
        return tuple(sts)

    def body(i, carry):
        return run_trips([i * HGRN_LOCKSTEP + s for s in range(HGRN_LOCKSTEP)], carry)

    s0 = jnp.zeros((dk, dk), F32)
    sts = lax.fori_loop(0, n_tiles // HGRN_LOCKSTEP, body, (s0, s0), unroll=HGRN_UNROLL)
    rest = list(range(n_tiles - n_tiles % HGRN_LOCKSTEP, n_tiles))
    if rest:
        run_trips(rest, sts)
    o = _rms(of_ref[...] + ob_ref[...]) * ng_ref[...]
    o_ref[0] = (o * _silu(g_ref[0])).astype(o_ref.dtype)


def _hgrn(z, hg_lower_bound, ng, layer, n_heads, ctx_len):
    b, t, _ = z.shape
    dk = HG_HEAD_DIM
    n_even = hg_lower_bound.shape[0]
    tile = math.gcd(math.gcd(ctx_len, t - ctx_len), 256)
    mall, masku, bm = _hgrn_tables(tile)

    def col(kind):
        return pl.BlockSpec((1, t, dk), lambda bi, hi: (bi, 0, kind * n_heads + hi))

    def const(arr):
        zeros = (0,) * arr.ndim
        return pl.BlockSpec(arr.shape, lambda bi, hi: zeros)

    return pl.pallas_call(
        functools.partial(_hgrn_kernel, layer=layer, n_ctx_tiles=ctx_len // tile, n_tiles=t // tile),
        out_shape=jax.ShapeDtypeStruct((b, t, n_heads * dk), BF16),
        grid=(b, n_heads),
        in_specs=[col(0), col(1), col(2), col(3), col(4),
                  pl.BlockSpec((n_even, 2, dk), lambda bi, hi: (0, 0, hi)),
                  const(ng.reshape(1, dk)), const(mall), const(masku), const(bm)],
        out_specs=pl.BlockSpec((1, t, dk), lambda bi, hi: (bi, 0, hi)),
        scratch_shapes=[pltpu.VMEM((t, dk), F32), pltpu.VMEM((t, dk), F32)],
        compiler_params=_params("arbitrary", "arbitrary"),
        name="hgrn2_scan",
    )(z, z, z, z, z, hg_lower_bound, ng.reshape(1, dk), mall, masku, bm)


def _hy_pre_kernel(x0_ref, x1_ref, v_ref, w0_ref, w1_ref, wv_ref, b0_ref, b1_ref, bv_ref,
                   x0c_ref, u_ref, *, ctx_len):
    t = x0_ref.shape[1]
    row = lax.broadcasted_iota(jnp.int32, (t, LANES), 0)
    first = (row == 0) | (row == ctx_len)
    final = (row == ctx_len - 1) | (row == t - 1)

    def short_conv(z_ref, w_ref, b_ref):
        z = z_ref[0]
        prev = jnp.where(first, 0.0, pltpu.roll(z, 1, 0))
        nxt = jnp.where(final, 0.0, pltpu.roll(z, t - 1, 0))
        w = w_ref[...]
        return prev * w[0:1] + z * w[1:2] + nxt * w[2:3] + b_ref[...]

    x0c_ref[0] = short_conv(x0_ref, w0_ref, b0_ref).astype(x0c_ref.dtype)
    u = short_conv(x1_ref, w1_ref, b1_ref) * short_conv(v_ref, wv_ref, bv_ref)
    u_ref[0] = u.astype(u_ref.dtype)


def _hy_pre(z, short_w, short_b, col0, width, ctx_len):
    b, t, _ = z.shape
    nb = width // LANES
    c0 = col0 // LANES

    def zcol(kind):
        return pl.BlockSpec((1, t, LANES), lambda bi, j: (bi, 0, c0 + kind * nb + j))

    def wcol(kind, rows):
        return pl.BlockSpec((rows, LANES), lambda bi, j: (0, kind * nb + j))

    out = jax.ShapeDtypeStruct((b, t, width), BF16)
    ospec = pl.BlockSpec((1, t, LANES), lambda bi, j: (bi, 0, j))
    sb = short_b.reshape(1, -1)
    return pl.pallas_call(
        functools.partial(_hy_pre_kernel, ctx_len=ctx_len),
        out_shape=(out, out),
        grid=(b, nb),
        in_specs=[zcol(0), zcol(1), zcol(2), wcol(0, 3), wcol(1, 3), wcol(2, 3),
                  wcol(0, 1), wcol(1, 1), wcol(2, 1)],
        out_specs=(ospec, ospec),
        compiler_params=_params("arbitrary", "arbitrary"),
        name="hyena_short_conv",
    )(z, z, z, short_w, short_w, short_w, sb, sb, sb)


def _filter_features(length, n):
    p = np.arange(n)
    is_f = p < length
    is_b = p > n - length
    lag = np.where(is_f, p, np.where(is_b, n - 1 - p, 0))
    tt = np.linspace(0.0, 1.0, length, dtype=np.float32)[lag][:, None]
    w = (2.0 * math.pi * lag.astype(np.float32) / length)[:, None].astype(np.float32)
    bands = np.linspace(1e-4, HY_BANDS - 1, HY_BANDS, dtype=np.float32)[None, :]
    feat = np.concatenate([tt, np.cos(bands * w), -np.sin(bands * w), is_f[:, None], is_b[:, None]],
                          axis=-1).astype(np.float32)
    pad = (-feat.shape[1]) % SUBLANES
    return jnp.asarray(np.pad(feat, ((0, 0), (0, pad))))


def _filt_kernel(feat_ref, w1_ref, b1_ref, fr1_ref, w2_ref, b2_ref, fr2_ref,
                 w3f_ref, w3b_ref, dl_ref, o_ref, hid_ref):
    z = feat_ref[...]
    mf = z[:, HY_EMB:HY_EMB + 1]
    mb = z[:, HY_EMB + 1:HY_EMB + 2]

    @pl.when(pl.program_id(0) == 0)
    def _():
        h1 = jnp.sin(fr1_ref[...] * (_dot(z, w1_ref[...], HIGHEST) + b1_ref[...]))
        hid_ref[...] = jnp.sin(fr2_ref[...] * (_dot(h1, w2_ref[...], HIGHEST) + b2_ref[...]))

    hid = hid_ref[...]
    hf = _dot(hid, w3f_ref[...], HIGHEST)
    hb = _dot(hid, w3b_ref[...], HIGHEST)
    win = jnp.exp(-z[:, 0:1] * dl_ref[...])
    f = (mf * hf + mb * hb) * win
    o_ref[...] = f / jnp.sum(jnp.abs(f), axis=0, keepdims=True)


def _hyena_filter(length, n, w1, b1, fr1, w2, b2, fr2, w3, width):
    feat = _filter_features(length, n)
    nf = feat.shape[1]
    hid = w1.shape[1]
    w1p = jnp.pad(w1, ((0, nf - w1.shape[0]), (0, 0)))
    d_lo = -math.log(HY_TARGET) / HY_GENTLE_PCT
    d_hi = -math.log(HY_TARGET) / HY_STEEP_PCT
    deltas = jnp.asarray(np.linspace(d_lo, d_hi, width, dtype=np.float32)[None, :])
    nb = width // LANES

    def full(shape):
        zeros = (0,) * len(shape)
        return pl.BlockSpec(shape, lambda j: zeros)

    return pl.pallas_call(
        _filt_kernel,
        out_shape=jax.ShapeDtypeStruct((n, width), F32),
        grid=(nb,),
        in_specs=[full((n, nf)), full((nf, hid)), full((1, hid)),
                  full((1, hid)), full((hid, hid)), full((1, hid)), full((1, hid)),
                  pl.BlockSpec((hid, LANES), lambda j: (0, j)),
                  pl.BlockSpec((hid, LANES), lambda j: (0, nb + j)),
                  pl.BlockSpec((1, LANES), lambda j: (0, j))],
        out_specs=pl.BlockSpec((n, LANES), lambda j: (0, j)),
        scratch_shapes=[pltpu.VMEM((n, hid), F32)],
        compiler_params=_params("arbitrary"),
        name="hyena_filter_mlp",
    )(feat, w1p, b1.reshape(1, -1), fr1.reshape(1, -1), w2, b2.reshape(1, -1),
      fr2.reshape(1, -1), w3, w3, deltas)


def _dft_tables(n1, n1_in):
    n2 = FFT_N2
    n = n1 * n2
    a = np.arange(n1)
    j = np.arange(n2)
    ang = -2.0 * np.pi * (a[None, None, :] * a[None, :, None] / n1 + j[:, None, None] * a[None, :, None] / n)
    tr, ti = np.cos(ang), np.sin(ang)
    fwd_a = np.concatenate([np.concatenate([tr, -ti], 2), np.concatenate([ti, tr], 2)], 1)
    trt, tit = np.swapaxes(tr, 1, 2), -np.swapaxes(ti, 1, 2)
    inv_a = np.concatenate([np.concatenate([trt, -tit], 2), np.concatenate([tit, trt], 2)], 1)
    keep = np.concatenate([np.arange(n1_in), n1 + np.arange(n1_in)])
    ang2 = -2.0 * np.pi * (j[:, None] * j[None, :]) / n2
    cr, ci = np.cos(ang2), np.sin(ang2)
    fwd_c = np.block([[cr, -ci], [ci, cr]])
    inv_c = np.block([[cr, ci], [-ci, cr]])
    real_a = np.concatenate([tr, ti], 1)
    return dict(fwd_a=fwd_a[:, :, keep], inv_a=inv_a[:, keep, :], fwd_c=fwd_c, inv_c=inv_c,
                real_a=real_a)


def _slab(idx):
    return pl.ds(pl.multiple_of(idx * FFT_PITCH, SUBLANES), FFT_N2)


def _fft_filter_kernel(f_ref, wa_ref, wc_ref, o_ref, scr_ref, *, n1):
    n2 = FFT_N2
    rows = 2 * n1
    scale = 1.0 / (n1 * n2)

    def split3(x):
        hi = x.astype(BF16)
        lo = (x - hi.astype(F32)).astype(BF16)
        return jnp.concatenate([hi, lo, hi], axis=0)

    def stage_a(j, carry):
        scr_ref[pl.ds(j, rows, stride=FFT_PITCH), :] = _dot(wa_ref[j], split3(f_ref[j]))
        return carry

    lax.fori_loop(0, n2, stage_a, 0, unroll=FFT_UNROLL)

    def stage_c(k, carry):
        x = jnp.concatenate([scr_ref[_slab(k), :], scr_ref[_slab(n1 + k), :]], axis=0)
        o_ref[k] = _dot(wc_ref[...], split3(x)) * scale
        return carry

    lax.fori_loop(0, n1, stage_c, 0, unroll=FFT_UNROLL)


def _split3_cols(w):
    hi = w.astype(np.float32).astype(jnp.bfloat16)
    lo = (w.astype(np.float32) - np.asarray(hi, np.float32)).astype(jnp.bfloat16)
    return jnp.asarray(np.concatenate([hi, hi, lo], axis=-1))


def _fft_filter(filt, n1, tables):
    n2 = FFT_N2
    width = filt.shape[1]
    ft = filt.reshape(n1, n2, width).transpose(1, 0, 2)
    wa = _split3_cols(tables["real_a"])
    wc = _split3_cols(tables["fwd_c"])
    return pl.pallas_call(
        functools.partial(_fft_filter_kernel, n1=n1),
        out_shape=jax.ShapeDtypeStruct((n1, 2 * n2, width), F32),
        grid=(width // LANES,),
        in_specs=[pl.BlockSpec((n2, n1, LANES), lambda c: (0, 0, c)),
                  pl.BlockSpec(wa.shape, lambda c: (0, 0, 0)),
                  pl.BlockSpec(wc.shape, lambda c: (0, 0))],
        out_specs=pl.BlockSpec((n1, 2 * n2, LANES), lambda c: (0, 0, c)),
        scratch_shapes=[pltpu.VMEM((2 * n1 * FFT_PITCH, LANES), F32)],
        compiler_params=_params("arbitrary"),
        name="hyena_filter_dft",
    )(ft, wa, wc)


def _fftconv_kernel(u_ref, x0_ref, fh_ref, skip_ref, wfa_ref, wfc_ref, wic_ref, wia_ref, o_ref,
                    scr_ref, *, n1):
    n2 = FFT_N2
    rows = 2 * n1

    def stage_a(j, carry):
        scr_ref[pl.ds(j, rows, stride=FFT_PITCH), :] = _dot(wfa_ref[j], u_ref[0, j])
        return carry

    lax.fori_loop(0, n2, stage_a, 0, unroll=FFT_UNROLL)

    def stage_c(k, carry):
        x = jnp.concatenate([scr_ref[_slab(k), :], scr_ref[_slab(n1 + k), :]], axis=0)
        xf = _dot(wfc_ref[...], x.astype(BF16))
        fh = fh_ref[k]
        xr, xi, fr, fi = xf[:n2], xf[n2:], fh[:n2], fh[n2:]
        y = jnp.concatenate([xr * fr - xi * fi, xr * fi + xi * fr], axis=0)
        zt = _dot(wic_ref[...], y.astype(BF16))
        scr_ref[_slab(k), :] = zt[:n2]
        scr_ref[_slab(n1 + k), :] = zt[n2:]
        return carry

    lax.fori_loop(0, n1, stage_c, 0, unroll=FFT_UNROLL)

    def stage_ai(j, carry):
        x = scr_ref[pl.ds(j, rows, stride=FFT_PITCH), :]
        y = _dot(wia_ref[j], x.astype(BF16))
        u = u_ref[0, j].astype(F32)
        o_ref[0, j] = (x0_ref[0, j].astype(F32) * (y + skip_ref[...] * u)).astype(o_ref.dtype)
        return carry

    lax.fori_loop(0, n2, stage_ai, 0, unroll=FFT_UNROLL)


def _to_fft_layout(x, n1_in):
    b, length, c = x.shape
    n1_used = length // FFT_N2
    x = x.reshape(b // 2, 2, n1_used, FFT_N2, c)
    x = jnp.pad(x, ((0, 0), (0, 0), (0, n1_in - n1_used), (0, 0), (0, 0)))
    return x.transpose(0, 3, 1, 2, 4).reshape(b // 2, FFT_N2, 2 * n1_in, c)


def _from_fft_layout(y, length):
    p, n2, rows, c = y.shape
    n1_in = rows // 2
    y = y.reshape(p, n2, 2, n1_in, c).transpose(0, 2, 3, 1, 4)
    return y.reshape(2 * p, n1_in * n2, c)[:, :length]


def _fftconv(u, x0c, fh, skip, n1, n1_in, tables):
    _, length, width = u.shape
    ut = _to_fft_layout(u, n1_in)
    xt = _to_fft_layout(x0c, n1_in)
    pairs, n2, rin, _ = ut.shape
    wfa = jnp.asarray(tables["fwd_a"], dtype=BF16)
    wia = jnp.asarray(tables["inv_a"], dtype=BF16)
    wfc = jnp.asarray(tables["fwd_c"], dtype=BF16)
    wic = jnp.asarray(tables["inv_c"], dtype=BF16)
    data = pl.BlockSpec((1, n2, rin, LANES), lambda c, p: (p, 0, 0, c))

    def const(arr):
        zeros = (0,) * arr.ndim
        return pl.BlockSpec(arr.shape, lambda c, p: zeros, pipeline_mode=pl.Buffered(1))

    out = pl.pallas_call(
        functools.partial(_fftconv_kernel, n1=n1),
        out_shape=jax.ShapeDtypeStruct(ut.shape, BF16),
        grid=(width // LANES, pairs),
        in_specs=[data, data,
                  pl.BlockSpec((n1, 2 * n2, LANES), lambda c, p: (0, 0, c)),
                  pl.BlockSpec((1, LANES), lambda c, p: (0, c)),
                  const(wfa), const(wfc), const(wic), const(wia)],
        out_specs=data,
        scratch_shapes=[pltpu.VMEM((2 * n1 * FFT_PITCH, LANES), F32)],
        compiler_params=_params("arbitrary", "arbitrary"),
        name="hyena_dft_conv",
    )(ut, xt, fh, skip.reshape(1, width), wfa, wfc, wic, wia)
    return _from_fft_layout(out, length)


def _hyena(z, col0, width, ctx_len, short_w, short_b, filt_params, skip):
    x0c, u = _hy_pre(z, short_w, short_b, col0, width, ctx_len)
    outs = []
    for lo, hi in ((0, ctx_len), (ctx_len, z.shape[1])):
        length = hi - lo
        n1_in = max(length // FFT_N2, SUBLANES)
        n1 = 2 * n1_in
        tables = _dft_tables(n1, n1_in)
        filt = _hyena_filter(length, n1 * FFT_N2, *filt_params, width)
        fh = _fft_filter(filt, n1, tables)
        outs.append(_fftconv(u[:, lo:hi], x0c[:, lo:hi], fh, skip, n1, n1_in, tables))
    return jnp.concatenate(outs, axis=1)


def _rope_tables(n_lat, ctx_len):
    tok = np.arange(n_lat)
    row, colp = tok // GRID_W, tok % GRID_W

    def axial(half):
        inv = ROPE_BASE ** (-np.arange(half, dtype=np.float32) / half)
        parts_c, parts_s = [], []
        for pos in (row, colp):
            ang = pos.astype(np.float32)[:, None] * inv
            parts_c += [np.cos(ang), np.cos(ang)]
            parts_s += [-np.sin(ang), np.sin(ang)]
        return np.concatenate(parts_c, 1), np.concatenate(parts_s, 1)

    dc, ds = axial(DA_HEAD_DIM // 4)
    mc, ms = axial(MLA_ROPE // 4)
    ones, zeros = np.ones((n_lat, MLA_NOPE), np.float32), np.zeros((n_lat, MLA_NOPE), np.float32)
    padc = np.ones((n_lat, LANES - MLA_NOPE - MLA_ROPE), np.float32)
    tabs = [np.concatenate([dc, dc], 1), np.concatenate([ds, ds], 1),
            np.concatenate([ones, mc, padc], 1), np.concatenate([zeros, ms, 0 * padc], 1)]
    out = []
    for i, tb in enumerate(tabs):
        ctx_rows = np.ones((ctx_len, LANES), np.float32) if i % 2 == 0 else np.zeros((ctx_len, LANES), np.float32)
        out.append(jnp.asarray(np.concatenate([ctx_rows, tb.astype(np.float32)], 0)))
    return out


def _swap_perm(width, group, half):
    idx = np.arange(width)
    pos = idx % group
    return np.where((pos % (2 * half)) < half, idx + half, idx - half)


def _with_ones(v):
    ones = jnp.ones((v.shape[0], LANES), v.dtype)
    parts = []
    for hd in range(v.shape[1] // LANES):
        parts += [v[:, hd * LANES:(hd + 1) * LANES], ones]
    return jnp.concatenate(parts, axis=1)


def _in_odd_kernel(x_ref, mod_ref, modc_ref, g_ref, w_ref, dc_ref, ds_ref, mc_ref, ms_ref,
                   qg_ref, kvg_ref, wuq_ref, wukv_ref,
                   qd_ref, kd_ref, vd_ref, qm_ref, km_ref, vm_ref, *, n_ctx_tiles, da_w, q_rank, kv_rank):
    m = _tile_mod(mod_ref, modc_ref, n_ctx_tiles)
    xn = (_rms(x_ref[0]) * g_ref[...]) * (1.0 + m[1:2]) + m[0:1]
    z = _dot(xn.astype(BF16), w_ref[...])
    nrep = da_w // LANES
    dc = jnp.concatenate([dc_ref[...]] * nrep, axis=1)
    ds = jnp.concatenate([ds_ref[...]] * nrep, axis=1)
    mc = jnp.concatenate([mc_ref[...]] * MLA_HEADS, axis=1)
    ms = jnp.concatenate([ms_ref[...]] * MLA_HEADS, axis=1)
    sa = DA_HEAD_DIM ** -0.5 * LOG2_E
    sm = (MLA_NOPE + MLA_ROPE) ** -0.5 * LOG2_E
    o = 0
    qd_ref[0] = ((z[:, o:o + da_w] * dc + z[:, o + da_w:o + 2 * da_w] * ds) * sa).astype(BF16)
    o += 2 * da_w
    kd_ref[0] = (z[:, o:o + da_w] * dc + z[:, o + da_w:o + 2 * da_w] * ds).astype(BF16)
    o += 2 * da_w
    vd_ref[0] = _with_ones(z[:, o:o + da_w]).astype(BF16)
    o += da_w
    cq = _rms(z[:, o:o + q_rank]) * qg_ref[...]
    o += q_rank
    ckv = _rms(z[:, o:o + kv_rank]) * kvg_ref[...]
    o += kv_rank
    kr = z[:, o:o + LANES] * mc_ref[...] + z[:, o + LANES:o + 2 * LANES] * ms_ref[...]
    mw = MLA_HEADS * LANES
    qu = _dot(cq.astype(BF16), wuq_ref[...])
    qm_ref[0] = ((qu[:, :mw] * mc + qu[:, mw:] * ms) * sm).astype(BF16)
    kvu = _dot(ckv.astype(BF16), wukv_ref[...])
    km_ref[0] = (kvu[:, :mw] + jnp.concatenate([kr] * MLA_HEADS, axis=1)).astype(BF16)
    vm_ref[0] = _with_ones(kvu[:, mw:]).astype(BF16)


def _odd_weights(w_in, w_uq, w_ukv, da_w, q_rank, kv_rank):
    qw, kw, vw = w_in[:, :da_w], w_in[:, da_w:2 * da_w], w_in[:, 2 * da_w:3 * da_w]
    o = 3 * da_w
    cqw, ckvw, krw = w_in[:, o:o + q_rank], w_in[:, o + q_rank:o + q_rank + kv_rank], w_in[:, o + q_rank + kv_rank:]
    perm_da = _swap_perm(da_w, DA_HEAD_DIM // 2, DA_HEAD_DIM // 4)
    perm_r = _swap_perm(MLA_ROPE, MLA_ROPE // 2, MLA_ROPE // 4)
    d = w_in.shape[0]

    def rope_group(wr):
        return jnp.pad(wr, ((0, 0), (MLA_NOPE, LANES - MLA_NOPE - MLA_ROPE)))

    w_big = jnp.concatenate([qw, qw[:, perm_da], kw, kw[:, perm_da], vw, cqw, ckvw,
                             rope_group(krw), rope_group(krw[:, perm_r])], axis=1)
    dq = MLA_NOPE + MLA_ROPE
    pad = LANES - dq
    uq = w_uq.reshape(q_rank, MLA_HEADS, dq)
    uq_a = jnp.pad(uq, ((0, 0), (0, 0), (0, pad))).reshape(q_rank, MLA_HEADS * LANES)
    uq_s = jnp.pad(uq[:, :, MLA_NOPE:][:, :, perm_r], ((0, 0), (0, 0), (MLA_NOPE, pad)))
    uq_s = uq_s.reshape(q_rank, MLA_HEADS * LANES)
    ukv = w_ukv.reshape(kv_rank, MLA_HEADS, MLA_NOPE + MLA_V)
    uk = jnp.pad(ukv[:, :, :MLA_NOPE], ((0, 0), (0, 0), (0, LANES - MLA_NOPE))).reshape(kv_rank, MLA_HEADS * LANES)
    uv = ukv[:, :, MLA_NOPE:].reshape(kv_rank, MLA_HEADS * MLA_V)
    return (w_big.astype(BF16), jnp.concatenate([uq_a, uq_s], axis=1).astype(BF16),
            jnp.concatenate([uk, uv], axis=1).astype(BF16))


def _in_odd(h, mods, g, w_big, wuq, wukv, qg, kvg, tabs, tm, n_ctx_tiles, da_w):
    b, t, d = h.shape
    q_rank, kv_rank = qg.shape[0], kvg.shape[0]
    lat, ctx = _mod_specs(d, b)
    tab = pl.BlockSpec((tm, LANES), lambda bi, ti: (ti, 0))
    mw = MLA_HEADS * LANES

    def tok_major(width):
        return jax.ShapeDtypeStruct((b, t, width), BF16), _tok_spec(tm, width)

    outs, ospecs = zip(tok_major(da_w), tok_major(da_w), tok_major(2 * da_w),
                       tok_major(mw), tok_major(mw), tok_major(2 * mw))
    return pl.pallas_call(
        functools.partial(_in_odd_kernel, n_ctx_tiles=n_ctx_tiles, da_w=da_w, q_rank=q_rank, kv_rank=kv_rank),
        out_shape=tuple(outs),
        grid=(b, t // tm),
        in_specs=[_tok_spec(tm, d), lat, ctx, _const_spec((1, d)), _const_spec(w_big.shape),
                  tab, tab, tab, tab, _const_spec((1, q_rank)), _const_spec((1, kv_rank)),
                  _const_spec(wuq.shape), _const_spec(wukv.shape)],
        out_specs=tuple(ospecs),
        compiler_params=_params("arbitrary", "arbitrary"),
        name="odd_in_proj",
    )(h, mods, mods, g.reshape(1, d), w_big, *tabs, qg.reshape(1, -1), kvg.reshape(1, -1), wuq, wukv)


def _softmax_pv(qs, k_ref, v_ref):
    n_keys = k_ref.shape[1]
    starts = list(range(0, n_keys, ATTN_KEY_BLOCK))
    m = [None] * len(qs)
    acc = [None] * len(qs)
    for start in starts:
        size = min(ATTN_KEY_BLOCK, n_keys - start)
        k = k_ref[0, start:start + size, :]
        v1 = v_ref[0, start:start + size, :]
        for i, q in enumerate(qs):
            s = _dot_nt(q, k)
            m_blk = jnp.max(s, axis=-1, keepdims=True)
            if start == 0:
                m[i] = m_blk
                acc[i] = _dot(jnp.exp2(s - m_blk).astype(BF16), v1)
            else:
                m_new = jnp.maximum(m[i], m_blk)
                pv = _dot(jnp.exp2(s - m_new).astype(BF16), v1)
                acc[i] = jnp.exp2(m[i] - m_new) * acc[i] + pv
                m[i] = m_new
    return [a[:, :LANES] / a[:, LANES:] for a in acc]


def _attn_kernel(q_ref, k_ref, v_ref, lam_ref, sg_ref, o_ref, *, diff, lam_init):
    q = q_ref[0]
    if not diff:
        o_ref[0] = _softmax_pv([q], k_ref, v_ref)[0].astype(o_ref.dtype)
        return
    first = lax.broadcasted_iota(jnp.int32, q.shape, 1) < DA_HEAD_DIM
    zero = jnp.zeros_like(q)
    o1, o2 = _softmax_pv([jnp.where(first, q, zero), jnp.where(first, zero, q)], k_ref, v_ref)
    lp = lam_ref[...]
    lam = (jnp.exp(jnp.sum(lp[0:1] * lp[1:2], axis=-1, keepdims=True))
           - jnp.exp(jnp.sum(lp[2:3] * lp[3:4], axis=-1, keepdims=True)) + lam_init)
    o_ref[0] = (_rms(o1 - lam * o2) * sg_ref[...] * (1.0 - lam_init)).astype(o_ref.dtype)


def _attention(q, k, v1, lam_p, subln_g, tq, n_keys, diff, lam_init):
    b, t_q, width = q.shape
    heads = width // LANES
    qtile = pl.BlockSpec((1, tq, LANES), lambda bi, hi, ti: (bi, ti, hi))
    khead = pl.BlockSpec((1, n_keys, LANES), lambda bi, hi, ti: (bi, 0, hi))
    vhead = pl.BlockSpec((1, n_keys, 2 * LANES), lambda bi, hi, ti: (bi, 0, hi))

    def const(shape):
        zeros = (0,) * len(shape)
        return pl.BlockSpec(shape, lambda bi, hi, ti: zeros)

    return pl.pallas_call(
        functools.partial(_attn_kernel, diff=diff, lam_init=lam_init),
        out_shape=jax.ShapeDtypeStruct((b, t_q, width), BF16),
        grid=(b, heads, t_q // tq),
        in_specs=[qtile, khead, vhead, const(lam_p.shape), const((1, LANES))],
        out_specs=qtile,
        compiler_params=_params("arbitrary", "arbitrary", "arbitrary"),
        name="diff_attention" if diff else "mla_attention",
    )(q, k, v1, lam_p, subln_g.reshape(1, LANES))


def _attend_segments(q, k, v1, lam_p, subln_g, ctx_len, diff, lam_init, need_ctx):
    t = k.shape[1]
    n_lat = t - ctx_len
    tq_lat = math.gcd(n_lat, 512)
    o_lat = _attention(q[:, ctx_len:], k, v1, lam_p, subln_g, tq_lat, t, diff, lam_init)
    if not need_ctx:
        return o_lat
    o_ctx = _attention(q[:, :ctx_len], k, v1, lam_p, subln_g, ctx_len, ctx_len, diff, lam_init)
    return jnp.concatenate([o_ctx, o_lat], axis=1)


def kernel(x, c, ctx, c_ctx, ada_w, ada_b, norm_mix_g, norm_ffn_g, ffn_w_gu, ffn_w_down, ev_w_in, ev_w_out, hg_lower_bound, hg_out_norm_g, hy_short_w, hy_short_b, hy_filt_w1, hy_filt_b1, hy_filt_freq1, hy_filt_w2, hy_filt_b2, hy_filt_freq2, hy_filt_w3, hy_skip, od_w_in, od_w_out, da_lambda, da_subln_g, mla_q_norm_g, mla_w_uq, mla_kv_norm_g, mla_w_ukv, final_norm_g):
    n_batch, n_lat, d = x.shape
    ctx_len = ctx.shape[1]
    depth = ada_w.shape[0]
    assert n_batch % 2 == 0 and n_batch < MOD_ROWS
    assert n_lat % GRID_W == 0 and ctx_len % HG_CHUNK == 0 and n_lat % FFT_N2 == 0 and ctx_len % FFT_N2 == 0
    tm = math.gcd(math.gcd(ctx_len, n_lat), 256)
    n_ctx_tiles = ctx_len // tm
    hg_width = d // 2
    hy_width = d - hg_width
    da_w = d // 2
    q_rank, kv_rank = mla_q_norm_g.shape[1], mla_kv_norm_g.shape[1]

    cc = jnp.concatenate([c, c_ctx[None], jnp.zeros((MOD_ROWS - n_batch - 1, d), F32)], axis=0)
    mods = _ada(cc, ada_w, ada_b).reshape(depth, MOD_ROWS, 6, d)
    h = jnp.concatenate([ctx, x], axis=1)
    rope_tabs = _rope_tables(n_lat, ctx_len)

    for i in range(depth):
        last = i == depth - 1
        if i % 2 == 0:
            e = i // 2
            z = _in_even(h, mods[i], norm_mix_g[i], ev_w_in[e].astype(BF16), tm, n_ctx_tiles)
            a = _hgrn(z, hg_lower_bound, hg_out_norm_g[e], e, hg_width // HG_HEAD_DIM, ctx_len)
            filt_params = (hy_filt_w1[e], hy_filt_b1[e], hy_filt_freq1[e], hy_filt_w2[e], hy_filt_b2[e],
                           hy_filt_freq2[e], hy_filt_w3[e])
            bb = _hyena(z, 5 * hg_width, hy_width, ctx_len, hy_short_w[e], hy_short_b[e], filt_params, hy_skip[e])
            w_out = ev_w_out[e]
        else:
            o = i // 2
            lam_init = 0.8 - 0.6 * math.exp(-0.3 * i)
            w_big, wuq, wukv = _odd_weights(od_w_in[o], mla_w_uq[o], mla_w_ukv[o], da_w, q_rank, kv_rank)
            qd, kd, vd, qm, km, vm = _in_odd(h, mods[i], norm_mix_g[i], w_big, wuq, wukv, mla_q_norm_g[o],
                                             mla_kv_norm_g[o], rope_tabs, tm, n_ctx_tiles, da_w)
            a = _attend_segments(qd, kd, vd, da_lambda[o], da_subln_g[o], ctx_len, True, lam_init, not last)
            bb = _attend_segments(qm, km, vm, da_lambda[o], da_subln_g[o], ctx_len, False, lam_init, not last)
            w_out = od_w_out[o]
        weights = (w_out.astype(BF16), ffn_w_gu[i].astype(BF16), ffn_w_down[i].astype(BF16))
        if not last:
            h = _post(h, a, bb, mods[i], norm_ffn_g[i], *weights, tm, n_ctx_tiles)
    if a.shape[1] != n_lat:
        a, bb = a[:, ctx_len:], bb[:, ctx_len:]
    return _post_final(h, a, bb, mods[depth - 1], norm_ffn_g[depth - 1], *weights, final_norm_g, tm, n_ctx_tiles)
```

```python
import functools
import math

import numpy as np
import jax
import jax.numpy as jnp
from jax import lax
from jax.experimental import pallas as pl
from jax.experimental.pallas import tpu as pltpu

F32 = jnp.float32
BF16 = jnp.bfloat16
HIGHEST = lax.Precision.HIGHEST

GRID_W = 64
EPS = 1e-6
ROPE_BASE = 10000.0
HG_HEAD_DIM = 128
HG_CHUNK = 64
HY_BANDS = 16
HY_EMB = 2 * HY_BANDS + 1
HY_TARGET = 1e-2
HY_STEEP_PCT = 0.3
HY_GENTLE_PCT = 1.5
DA_HEAD_DIM = 64
MLA_HEADS = 4
MLA_NOPE = 64
MLA_ROPE = 32
MLA_V = 128
LOG2_E = 1.4426950408889634

LANES = 128
SUBLANES = 8
V7X_VMEM_LIMIT_BYTES = 56 * 1024 * 1024
MOD_ROWS = 16
FFT_N2 = 128
FFT_PITCH = FFT_N2 + SUBLANES
FFT_UNROLL = 16
HGRN_LOCKSTEP = 1
HGRN_UNROLL = 2
ATTN_KEY_BLOCK = 256
ATTN_QUERY_TILE = 1024


def _params(*sem):
    return pltpu.CompilerParams(dimension_semantics=sem, vmem_limit_bytes=V7X_VMEM_LIMIT_BYTES)


def _dot(a, b, precision=None):
    return jnp.dot(a, b, preferred_element_type=F32, precision=precision)


def _dot_nt(a, b):
    return lax.dot_general(a, b, (((1,), (1,)), ((), ())), preferred_element_type=F32)


def _dot_tn(a, b):
    return lax.dot_general(a, b, (((0,), (0,)), ((), ())), preferred_element_type=F32)


def _rms(x):
    return x * lax.rsqrt(jnp.mean(x * x, axis=-1, keepdims=True) + EPS)


def _silu(x):
    return x * jax.nn.sigmoid(x)


def _tile_mod(mod_ref, modc_ref, n_ctx_tiles):
    return jnp.where(pl.program_id(1) < n_ctx_tiles, modc_ref[...], mod_ref[...])


def _ada_kernel(c_ref, w_ref, b_ref, o_ref):
    o_ref[0] = _dot(_silu(c_ref[...]), w_ref[0], HIGHEST) + b_ref[0]


def _ada(cc, ada_w, ada_b):
    depth, d, n = ada_w.shape
    rows = cc.shape[0]
    tn = n // 4
    return pl.pallas_call(
        _ada_kernel,
        out_shape=jax.ShapeDtypeStruct((depth, rows, n), F32),
        grid=(depth, n // tn),
        in_specs=[
            pl.BlockSpec((rows, d), lambda i, j: (0, 0)),
            pl.BlockSpec((1, d, tn), lambda i, j: (i, 0, j)),
            pl.BlockSpec((1, 1, tn), lambda i, j: (i, 0, j)),
        ],
        out_specs=pl.BlockSpec((1, rows, tn), lambda i, j: (i, 0, j)),
        compiler_params=_params("arbitrary", "arbitrary"),
        name="ada_mod",
    )(cc, ada_w, ada_b.reshape(depth, 1, n))


def _tok_spec(tm, width):
    return pl.BlockSpec((1, tm, width), lambda b, t: (b, t, 0))


def _const_spec(shape):
    zeros = (0,) * len(shape)
    return pl.BlockSpec(shape, lambda b, t: zeros)


def _mod_specs(d, n_batch):
    lat = pl.BlockSpec((None, 6, d), lambda b, t: (b, 0, 0))
    ctx = pl.BlockSpec((None, 6, d), lambda b, t: (n_batch, 0, 0))
    return lat, ctx


def _in_even_kernel(x_ref, mod_ref, modc_ref, g_ref, w_ref, o_ref, *, n_ctx_tiles):
    m = _tile_mod(mod_ref, modc_ref, n_ctx_tiles)
    xn = (_rms(x_ref[0]) * g_ref[...]) * (1.0 + m[1:2]) + m[0:1]
    o_ref[0] = _dot(xn.astype(BF16), w_ref[...])


def _in_even(h, mods, g, w, tm, n_ctx_tiles):
    b, t, d = h.shape
    n = w.shape[1]
    lat, ctx = _mod_specs(d, b)
    return pl.pallas_call(
        functools.partial(_in_even_kernel, n_ctx_tiles=n_ctx_tiles),
        out_shape=jax.ShapeDtypeStruct((b, t, n), F32),
        grid=(b, t // tm),
        in_specs=[_tok_spec(tm, d), lat, ctx, _const_spec((1, d)), _const_spec((d, n))],
        out_specs=_tok_spec(tm, n),
        compiler_params=_params("arbitrary", "arbitrary"),
        name="even_in_proj",
    )(h, mods, mods, g.reshape(1, d), w)


def _post_body(m, h, a, bb, g_ref, wo_ref, wgu_ref, wd_ref, hidden):
    half = a.shape[-1]
    y = _dot(a, wo_ref[:half, :]) + _dot(bb, wo_ref[half:, :])
    h1 = h + m[2:3] * y
    xn = (_rms(h1) * g_ref[...]) * (1.0 + m[4:5]) + m[3:4]
    gu = _dot(xn.astype(BF16), wgu_ref[...])
    act = _silu(gu[:, :hidden]) * gu[:, hidden:]
    return h1 + m[5:6] * _dot(act.astype(BF16), wd_ref[...])


def _post_kernel(h_ref, a_ref, b_ref, mod_ref, modc_ref, g_ref, wo_ref, wgu_ref, wd_ref, o_ref,
                 *, n_ctx_tiles, hidden):
    m = _tile_mod(mod_ref, modc_ref, n_ctx_tiles)
    o_ref[0] = _post_body(m, h_ref[0], a_ref[0], b_ref[0], g_ref, wo_ref, wgu_ref, wd_ref, hidden)


def _post_final_kernel(h_ref, a_ref, b_ref, mod_ref, g_ref, wo_ref, wgu_ref, wd_ref, fg_ref, o_ref,
                       *, hidden):
    h2 = _post_body(mod_ref[...], h_ref[0], a_ref[0], b_ref[0], g_ref, wo_ref, wgu_ref, wd_ref, hidden)
    o_ref[0] = _rms(h2) * fg_ref[...]


def _post(h, a, bb, mods, g, wo, wgu, wd, tm, n_ctx_tiles):
    b, t, d = h.shape
    half = a.shape[-1]
    hidden = wd.shape[0]
    lat, ctx = _mod_specs(d, b)
    return pl.pallas_call(
        functools.partial(_post_kernel, n_ctx_tiles=n_ctx_tiles, hidden=hidden),
        out_shape=jax.ShapeDtypeStruct((b, t, d), F32),
        grid=(b, t // tm),
        in_specs=[_tok_spec(tm, d), _tok_spec(tm, half), _tok_spec(tm, half), lat, ctx,
                  _const_spec((1, d)), _const_spec((2 * half, d)), _const_spec((d, 2 * hidden)),
                  _const_spec((hidden, d))],
        out_specs=_tok_spec(tm, d),
        compiler_params=_params("arbitrary", "arbitrary"),
        name="out_proj_ffn",
    )(h, a, bb, mods, mods, g.reshape(1, d), wo, wgu, wd)


def _post_final(h, a_lat, b_lat, mods, g, wo, wgu, wd, final_g, tm, n_ctx_tiles):
    b, t, d = h.shape
    n_lat = a_lat.shape[1]
    half = a_lat.shape[-1]
    hidden = wd.shape[0]
    lat, _ = _mod_specs(d, b)
    return pl.pallas_call(
        functools.partial(_post_final_kernel, hidden=hidden),
        out_shape=jax.ShapeDtypeStruct((b, n_lat, d), F32),
        grid=(b, n_lat // tm),
        in_specs=[pl.BlockSpec((1, tm, d), lambda bi, ti: (bi, ti + n_ctx_tiles, 0)),
                  _tok_spec(tm, half), _tok_spec(tm, half), lat,
                  _const_spec((1, d)), _const_spec((2 * half, d)), _const_spec((d, 2 * hidden)),
                  _const_spec((hidden, d)), _const_spec((1, d))],
        out_specs=_tok_spec(tm, d),
        compiler_params=_params("arbitrary", "arbitrary"),
        name="out_proj_ffn_final",
    )(h, a_lat, b_lat, mods, g.reshape(1, d), wo, wgu, wd, final_g.reshape(1, d))


def _hgrn_tables(tile):
    c = HG_CHUNK
    r = np.arange(c)
    j = r[None, :]
    blocks = [(j <= r[:, None])]
    masks, bms = [], []
    m = c
    while m >= 2:
        half = m // 2
        p = r % m
        mid = (r - p + half)[:, None]
        upper = (p >= half)[:, None]
        up = (j >= mid) & (j <= r[:, None]) & upper
        lo = (j > r[:, None]) & (j <= mid - 1) & (~upper)
        if m > 2:
            blocks.append(up | lo)
        masks.append(np.broadcast_to(upper, (c, LANES)))
        bms.append(((r[:, None] // m) == (r[None, :] // m)) & upper & (~upper.T))
        m //= 2
    fwd = np.stack(blocks).astype(np.float32)
    bwd = fwd[:, ::-1, ::-1]
    mall = np.stack([fwd.reshape(-1, c), bwd.reshape(-1, c)])
    mall = np.concatenate([mall, mall], axis=2)
    mu = np.stack(masks).astype(np.float32)
    masku = np.tile(np.stack([mu, mu[:, ::-1]]), (1, 1, tile // c, 1))
    bmp = np.stack([np.concatenate([bms[i], bms[i + 1]], axis=1) for i in range(0, len(bms), 2)])
    bmp_bwd = np.stack([np.concatenate([bms[i][::-1, ::-1], bms[i + 1][::-1, ::-1]], axis=1)
                        for i in range(0, len(bms), 2)])
    return (jnp.asarray(mall, dtype=BF16), jnp.asarray(masku, dtype=F32),
            jnp.asarray(np.stack([bmp, bmp_bwd]), dtype=F32))


def _hgrn_kernel(q_ref, ff_ref, fb_ref, i_ref, g_ref, lbp_ref, ng_ref, mall_ref, masku_ref, bmp_ref,
                 o_ref, of_ref, ob_ref, *, layer, n_ctx_tiles, n_tiles):
    c = HG_CHUNK
    dk = HG_HEAD_DIM
    n_levels = masku_ref.shape[1]
    tile = masku_ref.shape[2]
    n_sub = tile // c
    lbp = lbp_ref[...]
    ex = jnp.exp(lbp - jnp.max(lbp, axis=0, keepdims=True))
    p = ex / jnp.sum(ex, axis=0, keepdims=True)
    lb = jnp.sum(p[:layer + 1], axis=0) - p[0]

    def sub(x, g):
        return x[g * c:(g + 1) * c]

    dirs = (0, 1)

    def stack_chunks(x):
        return jnp.concatenate([sub(x, g) for g in range(n_sub)], axis=1)

    def bwd_tile(t):
        if isinstance(t, int):
            return n_ctx_tiles - 1 - t if t < n_ctx_tiles else n_tiles - 1 - (t - n_ctx_tiles)
        return jnp.where(t < n_ctx_tiles, n_ctx_tiles - 1 - t, n_tiles - 1 - (t - n_ctx_tiles))

    def tile_rows(ti):
        start = ti * tile
        return pl.ds(start if isinstance(ti, int) else pl.multiple_of(start, tile), tile)

    def run_trips(trips, sts):
        lanes = [(s, d) for s in range(len(trips)) for d in dirs]
        n = range(len(lanes))
        tis = [bwd_tile(t) if d else t for t in trips for d in dirs]
        rows = [tile_rows(ti) for ti in tis]
        q = [q_ref[0, r, :] for r in rows]
        v = [i_ref[0, r, :] for r in rows]
        lbd = [lb[d:d + 1] for _, d in lanes]
        zf = [(ff_ref, fb_ref)[d][0, rows[l], :] for l, (_, d) in enumerate(lanes)]
        f = [lbd[l] + (1.0 - lbd[l]) * jax.nn.sigmoid(zf[l]) for l in n]
        lf = [jnp.log(x) for x in f]
        k = [1.0 - x for x in f]
        hi = [x.astype(BF16) for x in lf]
        lo = [(lf[l] - hi[l].astype(F32)).astype(BF16) for l in n]
        e2 = [_dot(mall_ref[lanes[l][1]], jnp.concatenate([stack_chunks(hi[l]), stack_chunks(lo[l])], axis=0))
              for l in n]
        ee = [[jnp.exp(e2[l][:, g * dk:(g + 1) * dk]) for g in range(n_sub)] for l in n]

        def block(l, idx):
            return jnp.concatenate([ee[l][g][idx * c:(idx + 1) * c] for g in range(n_sub)], axis=0)

        def tail_decay(l, g):
            cum = e2[l][0:c, g * dk:(g + 1) * dk]
            last = c - 1 if lanes[l][1] == 0 else 0
            return jnp.exp(cum[last:last + 1] - cum)

        g_in = [block(l, 0) for l in n]
        qin = [(q[l] * g_in[l]).astype(BF16) for l in n]
        kout = [(k[l] * jnp.concatenate([tail_decay(l, g) for g in range(n_sub)], axis=0)).astype(BF16)
                for l in n]
        vb = [x.astype(BF16) for x in v]
        zero = jnp.zeros((c, dk), BF16)
        ds_all = [_dot_tn(vb[l], jnp.concatenate(
            [jnp.concatenate([sub(kout[l], g) if gg == g else zero for gg in range(n_sub)], axis=1)
             for g in range(n_sub)], axis=0)) for l in n]
        ds = [[ds_all[l][:, g * dk:(g + 1) * dk] for g in range(n_sub)] for l in n]
        qf = [q[l] * f[l] for l in n]
        w = [[None] * n_levels for _ in n]
        for lvl in range(n_levels):
            for l, (_, d) in enumerate(lanes):
                later = masku_ref[d, lvl] != 0.0
                if lvl == n_levels - 1:
                    w[l][lvl] = jnp.where(later, qf[l], k[l]).astype(BF16)
                else:
                    w[l][lvl] = (jnp.where(later, q[l], k[l]) * block(l, 1 + lvl)).astype(BF16)
        scores = [[[None] * (n_levels // 2) for _ in range(n_sub)] for _ in n]
        for pair in range(n_levels // 2):
            lv_a, lv_b = 2 * pair, 2 * pair + 1
            for g in range(n_sub):
                for l, (_, d) in enumerate(lanes):
                    wa, wb = sub(w[l][lv_a], g), sub(w[l][lv_b], g)
                    rhs = jnp.concatenate([jnp.concatenate([wa, zero], axis=1),
                                           jnp.concatenate([zero, wb], axis=1)], axis=0)
                    raw = _dot_nt(jnp.concatenate([wa, wb], axis=1), rhs)
                    scores[l][g][pair] = (raw * bmp_ref[d, pair]).astype(BF16)
        o = [jnp.concatenate([_dot(jnp.concatenate(scores[l][g], axis=1),
                                   jnp.concatenate([sub(vb[l], g)] * n_levels, axis=0))
                              for g in range(n_sub)], axis=0)
             + jnp.sum(q[l] * k[l], axis=-1, keepdims=True) * v[l] for l in n]
        order = [list(range(n_sub)), list(reversed(range(n_sub)))]
        entering = [[None] * n_sub for _ in n]
        sts = list(sts)
        for l, (_, d) in enumerate(lanes):
            for g in order[d]:
                entering[l][g] = sts[d].astype(BF16)
                last = (g + 1) * c - 1 if d == 0 else g * c
                sts[d] = sts[d] * g_in[l][last:last + 1] + ds[l][g]
        for l, (_, d) in enumerate(lanes):
            out = jnp.concatenate([sub(o[l], g) + _dot_nt(sub(qin[l], g), entering[l][g])
                                   for g in range(n_sub)], axis=0)
            (of_ref, ob_ref)[d][rows[l], :] = out
        return tuple(sts)

    def body(i, carry):
        return run_trips([i * HGRN_LOCKSTEP + s for s in range(HGRN_LOCKSTEP)], carry)

    s0 = jnp.zeros((dk, dk), F32)
    sts = lax.fori_loop(0, n_tiles // HGRN_LOCKSTEP, body, (s0, s0), unroll=HGRN_UNROLL)
    rest = list(range(n_tiles - n_tiles % HGRN_LOCKSTEP, n_tiles))
    if rest:
        run_trips(rest, sts)
    o = _rms(of_ref[...] + ob_ref[...]) * ng_ref[...]
    o_ref[0] = (o * _silu(g_ref[0])).astype(o_ref.dtype)


def _hgrn(z, hg_lower_bound, ng, layer, n_heads, ctx_len):
    b, t, _ = z.shape
    dk = HG_HEAD_DIM
    n_even = hg_lower_bound.shape[0]
    tile = math.gcd(math.gcd(ctx_len, t - ctx_len), 256)
    mall, masku, bm = _hgrn_tables(tile)

    def col(kind):
        return pl.BlockSpec((1, t, dk), lambda bi, hi: (bi, 0, kind * n_heads + hi))

    def const(arr):
        zeros = (0,) * arr.ndim
        return pl.BlockSpec(arr.shape, lambda bi, hi: zeros)

    return pl.pallas_call(
        functools.partial(_hgrn_kernel, layer=layer, n_ctx_tiles=ctx_len // tile, n_tiles=t // tile),
        out_shape=jax.ShapeDtypeStruct((b, t, n_heads * dk), BF16),
        grid=(b, n_heads),
        in_specs=[col(0), col(1), col(2), col(3), col(4),
                  pl.BlockSpec((n_even, 2, dk), lambda bi, hi: (0, 0, hi)),
                  const(ng.reshape(1, dk)), const(mall), const(masku), const(bm)],
        out_specs=pl.BlockSpec((1, t, dk), lambda bi, hi: (bi, 0, hi)),
        scratch_shapes=[pltpu.VMEM((t, dk), F32), pltpu.VMEM((t, dk), F32)],
        compiler_params=_params("arbitrary", "arbitrary"),
        name="hgrn2_scan",
    )(z, z, z, z, z, hg_lower_bound, ng.reshape(1, dk), mall, masku, bm)


def _hy_pre_kernel(x0_ref, x1_ref, v_ref, w0_ref, w1_ref, wv_ref, b0_ref, b1_ref, bv_ref,
                   x0c_ref, u_ref, *, ctx_len):
    t = x0_ref.shape[1]
    row = lax.broadcasted_iota(jnp.int32, (t, LANES), 0)
    first = (row == 0) | (row == ctx_len)
    final = (row == ctx_len - 1) | (row == t - 1)

    def short_conv(z_ref, w_ref, b_ref):
        z = z_ref[0]
        prev = jnp.where(first, 0.0, pltpu.roll(z, 1, 0))
        nxt = jnp.where(final, 0.0, pltpu.roll(z, t - 1, 0))
        w = w_ref[...]
        return prev * w[0:1] + z * w[1:2] + nxt * w[2:3] + b_ref[...]

    x0c_ref[0] = short_conv(x0_ref, w0_ref, b0_ref).astype(x0c_ref.dtype)
    u = short_conv(x1_ref, w1_ref, b1_ref) * short_conv(v_ref, wv_ref, bv_ref)
    u_ref[0] = u.astype(u_ref.dtype)


def _hy_pre(z, short_w, short_b, col0, width, ctx_len):
    b, t, _ = z.shape
    nb = width // LANES
    c0 = col0 // LANES

    def zcol(kind):
        return pl.BlockSpec((1, t, LANES), lambda bi, j: (bi, 0, c0 + kind * nb + j))

    def wcol(kind, rows):
        return pl.BlockSpec((rows, LANES), lambda bi, j: (0, kind * nb + j))

    out = jax.ShapeDtypeStruct((b, t, width), BF16)
    ospec = pl.BlockSpec((1, t, LANES), lambda bi, j: (bi, 0, j))
    sb = short_b.reshape(1, -1)
    return pl.pallas_call(
        functools.partial(_hy_pre_kernel, ctx_len=ctx_len),
        out_shape=(out, out),
        grid=(b, nb),
        in_specs=[zcol(0), zcol(1), zcol(2), wcol(0, 3), wcol(1, 3), wcol(2, 3),
                  wcol(0, 1), wcol(1, 1), wcol(2, 1)],
        out_specs=(ospec, ospec),
        compiler_params=_params("arbitrary", "arbitrary"),
        name="hyena_short_conv",
    )(z, z, z, short_w, short_w, short_w, sb, sb, sb)


def _filter_features(length, n):
    p = np.arange(n)
    is_f = p < length
    is_b = p > n - length
    lag = np.where(is_f, p, np.where(is_b, n - 1 - p, 0))
    tt = np.linspace(0.0, 1.0, length, dtype=np.float32)[lag][:, None]
    w = (2.0 * math.pi * lag.astype(np.float32) / length)[:, None].astype(np.float32)
    bands = np.linspace(1e-4, HY_BANDS - 1, HY_BANDS, dtype=np.float32)[None, :]
    feat = np.concatenate([tt, np.cos(bands * w), -np.sin(bands * w), is_f[:, None], is_b[:, None]],
                          axis=-1).astype(np.float32)
    pad = (-feat.shape[1]) % SUBLANES
    return jnp.asarray(np.pad(feat, ((0, 0), (0, pad))))


def _filt_kernel(feat_ref, w1_ref, b1_ref, fr1_ref, w2_ref, b2_ref, fr2_ref,
                 w3f_ref, w3b_ref, dl_ref, o_ref, hid_ref):
    z = feat_ref[...]
    mf = z[:, HY_EMB:HY_EMB + 1]
    mb = z[:, HY_EMB + 1:HY_EMB + 2]

    @pl.when(pl.program_id(0) == 0)
    def _():
        h1 = jnp.sin(fr1_ref[...] * (_dot(z, w1_ref[...], HIGHEST) + b1_ref[...]))
        hid_ref[...] = jnp.sin(fr2_ref[...] * (_dot(h1, w2_ref[...], HIGHEST) + b2_ref[...]))

    hid = hid_ref[...]
    hf = _dot(hid, w3f_ref[...], HIGHEST)
    hb = _dot(hid, w3b_ref[...], HIGHEST)
    win = jnp.exp(-z[:, 0:1] * dl_ref[...])
    f = (mf * hf + mb * hb) * win
    o_ref[...] = f / jnp.sum(jnp.abs(f), axis=0, keepdims=True)


def _hyena_filter(length, n, w1, b1, fr1, w2, b2, fr2, w3, width):
    feat = _filter_features(length, n)
    nf = feat.shape[1]
    hid = w1.shape[1]
    w1p = jnp.pad(w1, ((0, nf - w1.shape[0]), (0, 0)))
    d_lo = -math.log(HY_TARGET) / HY_GENTLE_PCT
    d_hi = -math.log(HY_TARGET) / HY_STEEP_PCT
    deltas = jnp.asarray(np.linspace(d_lo, d_hi, width, dtype=np.float32)[None, :])
    nb = width // LANES

    def full(shape):
        zeros = (0,) * len(shape)
        return pl.BlockSpec(shape, lambda j: zeros)

    return pl.pallas_call(
        _filt_kernel,
        out_shape=jax.ShapeDtypeStruct((n, width), F32),
        grid=(nb,),
        in_specs=[full((n, nf)), full((nf, hid)), full((1, hid)),
                  full((1, hid)), full((hid, hid)), full((1, hid)), full((1, hid)),
                  pl.BlockSpec((hid, LANES), lambda j: (0, j)),
                  pl.BlockSpec((hid, LANES), lambda j: (0, nb + j)),
                  pl.BlockSpec((1, LANES), lambda j: (0, j))],
        out_specs=pl.BlockSpec((n, LANES), lambda j: (0, j)),
        scratch_shapes=[pltpu.VMEM((n, hid), F32)],
        compiler_params=_params("arbitrary"),
        name="hyena_filter_mlp",
    )(feat, w1p, b1.reshape(1, -1), fr1.reshape(1, -1), w2, b2.reshape(1, -1),
      fr2.reshape(1, -1), w3, w3, deltas)


def _dft_tables(n1, n1_in):
    n2 = FFT_N2
    n = n1 * n2
    a = np.arange(n1)
    j = np.arange(n2)
    ang = -2.0 * np.pi * (a[None, None, :] * a[None, :, None] / n1 + j[:, None, None] * a[None, :, None] / n)
    tr, ti = np.cos(ang), np.sin(ang)
    fwd_a = np.concatenate([np.concatenate([tr, -ti], 2), np.concatenate([ti, tr], 2)], 1)
    trt, tit = np.swapaxes(tr, 1, 2), -np.swapaxes(ti, 1, 2)
    inv_a = np.concatenate([np.concatenate([trt, -tit], 2), np.concatenate([tit, trt], 2)], 1)
    keep = np.concatenate([np.arange(n1_in), n1 + np.arange(n1_in)])
    ang2 = -2.0 * np.pi * (j[:, None] * j[None, :]) / n2
    cr, ci = np.cos(ang2), np.sin(ang2)
    fwd_c = np.block([[cr, -ci], [ci, cr]])
    inv_c = np.block([[cr, ci], [-ci, cr]])
    real_a = np.concatenate([tr, ti], 1)
    return dict(fwd_a=fwd_a[:, :, keep], inv_a=inv_a[:, keep, :], fwd_c=fwd_c, inv_c=inv_c,
                real_a=real_a)


def _slab(idx):
    return pl.ds(pl.multiple_of(idx * FFT_PITCH, SUBLANES), FFT_N2)


def _fft_filter_kernel(f_ref, wa_ref, wc_ref, o_ref, scr_ref, *, n1):
    n2 = FFT_N2
    rows = 2 * n1
    scale = 1.0 / (n1 * n2)

    def split3(x):
        hi = x.astype(BF16)
        lo = (x - hi.astype(F32)).astype(BF16)
        return jnp.concatenate([hi, lo, hi], axis=0)

    def stage_a(j, carry):
        scr_ref[pl.ds(j, rows, stride=FFT_PITCH), :] = _dot(wa_ref[j], split3(f_ref[j]))
        return carry

    lax.fori_loop(0, n2, stage_a, 0, unroll=FFT_UNROLL)

    def stage_c(k, carry):
        x = jnp.concatenate([scr_ref[_slab(k), :], scr_ref[_slab(n1 + k), :]], axis=0)
        o_ref[k] = _dot(wc_ref[...], split3(x)) * scale
        return carry

    lax.fori_loop(0, n1, stage_c, 0, unroll=FFT_UNROLL)


def _split3_cols(w):
    hi = w.astype(np.float32).astype(jnp.bfloat16)
    lo = (w.astype(np.float32) - np.asarray(hi, np.float32)).astype(jnp.bfloat16)
    return jnp.asarray(np.concatenate([hi, hi, lo], axis=-1))


def _fft_filter(filt, n1, tables):
    n2 = FFT_N2
    width = filt.shape[1]
    ft = filt.reshape(n1, n2, width).transpose(1, 0, 2)
    wa = _split3_cols(tables["real_a"])
    wc = _split3_cols(tables["fwd_c"])
    return pl.pallas_call(
        functools.partial(_fft_filter_kernel, n1=n1),
        out_shape=jax.ShapeDtypeStruct((n1, 2 * n2, width), F32),
        grid=(width // LANES,),
        in_specs=[pl.BlockSpec((n2, n1, LANES), lambda c: (0, 0, c)),
                  pl.BlockSpec(wa.shape, lambda c: (0, 0, 0)),
                  pl.BlockSpec(wc.shape, lambda c: (0, 0))],
        out_specs=pl.BlockSpec((n1, 2 * n2, LANES), lambda c: (0, 0, c)),
        scratch_shapes=[pltpu.VMEM((2 * n1 * FFT_PITCH, LANES), F32)],
        compiler_params=_params("arbitrary"),
        name="hyena_filter_dft",
    )(ft, wa, wc)


def _fftconv_kernel(u_ref, x0_ref, fh_ref, skip_ref, wfa_ref, wfc_ref, wic_ref, wia_ref, o_ref,
                    scr_ref, *, n1):
    n2 = FFT_N2
    rows = 2 * n1

    def stage_a(j, carry):
        scr_ref[pl.ds(j, rows, stride=FFT_PITCH), :] = _dot(wfa_ref[j], u_ref[0, j])
        return carry

    lax.fori_loop(0, n2, stage_a, 0, unroll=FFT_UNROLL)

    def stage_c(k, carry):
        x = jnp.concatenate([scr_ref[_slab(k), :], scr_ref[_slab(n1 + k), :]], axis=0)
        xf = _dot(wfc_ref[...], x.astype(BF16))
        fh = fh_ref[k]
        xr, xi, fr, fi = xf[:n2], xf[n2:], fh[:n2], fh[n2:]
        y = jnp.concatenate([xr * fr - xi * fi, xr * fi + xi * fr], axis=0)
        zt = _dot(wic_ref[...], y.astype(BF16))
        scr_ref[_slab(k), :] = zt[:n2]
        scr_ref[_slab(n1 + k), :] = zt[n2:]
        return carry

    lax.fori_loop(0, n1, stage_c, 0, unroll=FFT_UNROLL)

    def stage_ai(j, carry):
        x = scr_ref[pl.ds(j, rows, stride=FFT_PITCH), :]
        y = _dot(wia_ref[j], x.astype(BF16))
        u = u_ref[0, j].astype(F32)
        o_ref[0, j] = (x0_ref[0, j].astype(F32) * (y + skip_ref[...] * u)).astype(o_ref.dtype)
        return carry

    lax.fori_loop(0, n2, stage_ai, 0, unroll=FFT_UNROLL)


def _to_fft_layout(x, n1_in):
    b, length, c = x.shape
    n1_used = length // FFT_N2
    x = x.reshape(b // 2, 2, n1_used, FFT_N2, c)
    x = jnp.pad(x, ((0, 0), (0, 0), (0, n1_in - n1_used), (0, 0), (0, 0)))
    return x.transpose(0, 3, 1, 2, 4).reshape(b // 2, FFT_N2, 2 * n1_in, c)


def _from_fft_layout(y, length):
    p, n2, rows, c = y.shape
    n1_in = rows // 2
    y = y.reshape(p, n2, 2, n1_in, c).transpose(0, 2, 3, 1, 4)
    return y.reshape(2 * p, n1_in * n2, c)[:, :length]


def _fftconv(u, x0c, fh, skip, n1, n1_in, tables):
    _, length, width = u.shape
    ut = _to_fft_layout(u, n1_in)
    xt = _to_fft_layout(x0c, n1_in)
    pairs, n2, rin, _ = ut.shape
    wfa = jnp.asarray(tables["fwd_a"], dtype=BF16)
    wia = jnp.asarray(tables["inv_a"], dtype=BF16)
    wfc = jnp.asarray(tables["fwd_c"], dtype=BF16)
    wic = jnp.asarray(tables["inv_c"], dtype=BF16)
    data = pl.BlockSpec((1, n2, rin, LANES), lambda c, p: (p, 0, 0, c))

    def const(arr):
        zeros = (0,) * arr.ndim
        return pl.BlockSpec(arr.shape, lambda c, p: zeros, pipeline_mode=pl.Buffered(1))

    out = pl.pallas_call(
        functools.partial(_fftconv_kernel, n1=n1),
        out_shape=jax.ShapeDtypeStruct(ut.shape, BF16),
        grid=(width // LANES, pairs),
        in_specs=[data, data,
                  pl.BlockSpec((n1, 2 * n2, LANES), lambda c, p: (0, 0, c)),
                  pl.BlockSpec((1, LANES), lambda c, p: (0, c)),
                  const(wfa), const(wfc), const(wic), const(wia)],
        out_specs=data,
        scratch_shapes=[pltpu.VMEM((2 * n1 * FFT_PITCH, LANES), F32)],
        compiler_params=_params("arbitrary", "arbitrary"),
        name="hyena_dft_conv",
    )(ut, xt, fh, skip.reshape(1, width), wfa, wfc, wic, wia)
    return _from_fft_layout(out, length)


def _hyena(z, col0, width, ctx_len, short_w, short_b, filt_params, skip):
    x0c, u = _hy_pre(z, short_w, short_b, col0, width, ctx_len)
    outs = []
    for lo, hi in ((0, ctx_len), (ctx_len, z.shape[1])):
        length = hi - lo
        n1_in = max(length // FFT_N2, SUBLANES)
        n1 = 2 * n1_in
        tables = _dft_tables(n1, n1_in)
        filt = _hyena_filter(length, n1 * FFT_N2, *filt_params, width)
        fh = _fft_filter(filt, n1, tables)
        outs.append(_fftconv(u[:, lo:hi], x0c[:, lo:hi], fh, skip, n1, n1_in, tables))
    return jnp.concatenate(outs, axis=1)


def _rope_tables(n_lat, ctx_len):
    tok = np.arange(n_lat)
    row, colp = tok // GRID_W, tok % GRID_W

    def axial(half):
        inv = ROPE_BASE ** (-np.arange(half, dtype=np.float32) / half)
        parts_c, parts_s = [], []
        for pos in (row, colp):
            ang = pos.astype(np.float32)[:, None] * inv
            parts_c += [np.cos(ang), np.cos(ang)]
            parts_s += [-np.sin(ang), np.sin(ang)]
        return np.concatenate(parts_c, 1), np.concatenate(parts_s, 1)

    dc, ds = axial(DA_HEAD_DIM // 4)
    mc, ms = axial(MLA_ROPE // 4)
    ones, zeros = np.ones((n_lat, MLA_NOPE), np.float32), np.zeros((n_lat, MLA_NOPE), np.float32)
    padc = np.ones((n_lat, LANES - MLA_NOPE - MLA_ROPE), np.float32)
    tabs = [np.concatenate([dc, dc], 1), np.concatenate([ds, ds], 1),
            np.concatenate([ones, mc, padc], 1), np.concatenate([zeros, ms, 0 * padc], 1)]
    out = []
    for i, tb in enumerate(tabs):
        ctx_rows = np.ones((ctx_len, LANES), np.float32) if i % 2 == 0 else np.zeros((ctx_len, LANES), np.float32)
        out.append(jnp.asarray(np.concatenate([ctx_rows, tb.astype(np.float32)], 0)))
    return out


def _swap_perm(width, group, half):
    idx = np.arange(width)
    pos = idx % group
    return np.where((pos % (2 * half)) < half, idx + half, idx - half)


def _with_ones(v):
    ones = jnp.ones((v.shape[0], LANES), v.dtype)
    parts = []
    for hd in range(v.shape[1] // LANES):
        parts += [v[:, hd * LANES:(hd + 1) * LANES], ones]
    return jnp.concatenate(parts, axis=1)


def _in_odd_kernel(x_ref, mod_ref, modc_ref, g_ref, w_ref, dc_ref, ds_ref, mc_ref, ms_ref,
                   qg_ref, kvg_ref, wuq_ref, wukv_ref,
                   qd_ref, kd_ref, vd_ref, qm_ref, km_ref, vm_ref, *, n_ctx_tiles, da_w, q_rank, kv_rank):
    m = _tile_mod(mod_ref, modc_ref, n_ctx_tiles)
    xn = (_rms(x_ref[0]) * g_ref[...]) * (1.0 + m[1:2]) + m[0:1]
    z = _dot(xn.astype(BF16), w_ref[...])
    nrep = da_w // LANES
    dc = jnp.concatenate([dc_ref[...]] * nrep, axis=1)
    ds = jnp.concatenate([ds_ref[...]] * nrep, axis=1)
    mc = jnp.concatenate([mc_ref[...]] * MLA_HEADS, axis=1)
    ms = jnp.concatenate([ms_ref[...]] * MLA_HEADS, axis=1)
    sa = DA_HEAD_DIM ** -0.5 * LOG2_E
    sm = (MLA_NOPE + MLA_ROPE) ** -0.5 * LOG2_E
    o = 0
    qd_ref[0] = ((z[:, o:o + da_w] * dc + z[:, o + da_w:o + 2 * da_w] * ds) * sa).astype(BF16)
    o += 2 * da_w
    kd_ref[0] = (z[:, o:o + da_w] * dc + z[:, o + da_w:o + 2 * da_w] * ds).astype(BF16)
    o += 2 * da_w
    vd_ref[0] = _with_ones(z[:, o:o + da_w]).astype(BF16)
    o += da_w
    cq = _rms(z[:, o:o + q_rank]) * qg_ref[...]
    o += q_rank
    ckv = _rms(z[:, o:o + kv_rank]) * kvg_ref[...]
    o += kv_rank
    kr = z[:, o:o + LANES] * mc_ref[...] + z[:, o + LANES:o + 2 * LANES] * ms_ref[...]
    mw = MLA_HEADS * LANES
    qu = _dot(cq.astype(BF16), wuq_ref[...])
    qm_ref[0] = ((qu[:, :mw] * mc + qu[:, mw:] * ms) * sm).astype(BF16)
    kvu = _dot(ckv.astype(BF16), wukv_ref[...])
    km_ref[0] = (kvu[:, :mw] + jnp.concatenate([kr] * MLA_HEADS, axis=1)).astype(BF16)
    vm_ref[0] = _with_ones(kvu[:, mw:]).astype(BF16)


def _odd_weights(w_in, w_uq, w_ukv, da_w, q_rank, kv_rank):
    qw, kw, vw = w_in[:, :da_w], w_in[:, da_w:2 * da_w], w_in[:, 2 * da_w:3 * da_w]
    o = 3 * da_w
    cqw, ckvw, krw = w_in[:, o:o + q_rank], w_in[:, o + q_rank:o + q_rank + kv_rank], w_in[:, o + q_rank + kv_rank:]
    perm_da = _swap_perm(da_w, DA_HEAD_DIM // 2, DA_HEAD_DIM // 4)
    perm_r = _swap_perm(MLA_ROPE, MLA_ROPE // 2, MLA_ROPE // 4)
    d = w_in.shape[0]

    def rope_group(wr):
        return jnp.pad(wr, ((0, 0), (MLA_NOPE, LANES - MLA_NOPE - MLA_ROPE)))

    w_big = jnp.concatenate([qw, qw[:, perm_da], kw, kw[:, perm_da], vw, cqw, ckvw,
                             rope_group(krw), rope_group(krw[:, perm_r])], axis=1)
    dq = MLA_NOPE + MLA_ROPE
    pad = LANES - dq
    uq = w_uq.reshape(q_rank, MLA_HEADS, dq)
    uq_a = jnp.pad(uq, ((0, 0), (0, 0), (0, pad))).reshape(q_rank, MLA_HEADS * LANES)
    uq_s = jnp.pad(uq[:, :, MLA_NOPE:][:, :, perm_r], ((0, 0), (0, 0), (MLA_NOPE, pad)))
    uq_s = uq_s.reshape(q_rank, MLA_HEADS * LANES)
    ukv = w_ukv.reshape(kv_rank, MLA_HEADS, MLA_NOPE + MLA_V)
    uk = jnp.pad(ukv[:, :, :MLA_NOPE], ((0, 0), (0, 0), (0, LANES - MLA_NOPE))).reshape(kv_rank, MLA_HEADS * LANES)
    uv = ukv[:, :, MLA_NOPE:].reshape(kv_rank, MLA_HEADS * MLA_V)
    return (w_big.astype(BF16), jnp.concatenate([uq_a, uq_s], axis=1).astype(BF16),
            jnp.concatenate([uk, uv], axis=1).astype(BF16))


def _in_odd(h, mods, g, w_big, wuq, wukv, qg, kvg, tabs, tm, n_ctx_tiles, da_w):
    b, t, d = h.shape
    q_rank, kv_rank = qg.shape[0], kvg.shape[0]
    lat, ctx = _mod_specs(d, b)
    tab = pl.BlockSpec((tm, LANES), lambda bi, ti: (ti, 0))
    mw = MLA_HEADS * LANES

    def tok_major(width):
        return jax.ShapeDtypeStruct((b, t, width), BF16), _tok_spec(tm, width)

    outs, ospecs = zip(tok_major(da_w), tok_major(da_w), tok_major(2 * da_w),
                       tok_major(mw), tok_major(mw), tok_major(2 * mw))
    return pl.pallas_call(
        functools.partial(_in_odd_kernel, n_ctx_tiles=n_ctx_tiles, da_w=da_w, q_rank=q_rank, kv_rank=kv_rank),
        out_shape=tuple(outs),
        grid=(b, t // tm),
        in_specs=[_tok_spec(tm, d), lat, ctx, _const_spec((1, d)), _const_spec(w_big.shape),
                  tab, tab, tab, tab, _const_spec((1, q_rank)), _const_spec((1, kv_rank)),
                  _const_spec(wuq.shape), _const_spec(wukv.shape)],
        out_specs=tuple(ospecs),
        compiler_params=_params("arbitrary", "arbitrary"),
        name="odd_in_proj",
    )(h, mods, mods, g.reshape(1, d), w_big, *tabs, qg.reshape(1, -1), kvg.reshape(1, -1), wuq, wukv)


def _softmax_pv(qs, k_ref, v_ref):
    n_keys = k_ref.shape[1]
    starts = list(range(0, n_keys, ATTN_KEY_BLOCK))
    m = [None] * len(qs)
    acc = [None] * len(qs)
    for start in starts:
        size = min(ATTN_KEY_BLOCK, n_keys - start)
        k = k_ref[0, start:start + size, :]
        v1 = v_ref[0, start:start + size, :]
        for i, q in enumerate(qs):
            s = _dot_nt(q, k)
            m_blk = jnp.max(s, axis=-1, keepdims=True)
            if start == 0:
                m[i] = m_blk
                acc[i] = _dot(jnp.exp2(s - m_blk).astype(BF16), v1)
            else:
                m_new = jnp.maximum(m[i], m_blk)
                pv = _dot(jnp.exp2(s - m_new).astype(BF16), v1)
                acc[i] = jnp.exp2(m[i] - m_new) * acc[i] + pv
                m[i] = m_new
    return [a[:, :LANES] / a[:, LANES:] for a in acc]


def _attn_kernel(q_ref, k_ref, v_ref, lam_ref, sg_ref, o_ref, *, diff, lam_init):
    q = q_ref[0]
    if not diff:
        o_ref[0] = _softmax_pv([q], k_ref, v_ref)[0].astype(o_ref.dtype)
        return
    first = lax.broadcasted_iota(jnp.int32, q.shape, 1) < DA_HEAD_DIM
    zero = jnp.zeros_like(q)
    o1, o2 = _softmax_pv([jnp.where(first, q, zero), jnp.where(first, zero, q)], k_ref, v_ref)
    lp = lam_ref[...]
    lam = (jnp.exp(jnp.sum(lp[0:1] * lp[1:2], axis=-1, keepdims=True))
           - jnp.exp(jnp.sum(lp[2:3] * lp[3:4], axis=-1, keepdims=True)) + lam_init)
    o_ref[0] = (_rms(o1 - lam * o2) * sg_ref[...] * (1.0 - lam_init)).astype(o_ref.dtype)


def _attention(q, k, v1, lam_p, subln_g, tq, n_keys, diff, lam_init):
    b, t_q, width = q.shape
    heads = width // LANES
    qtile = pl.BlockSpec((1, tq, LANES), lambda bi, hi, ti: (bi, ti, hi))
    khead = pl.BlockSpec((1, n_keys, LANES), lambda bi, hi, ti: (bi, 0, hi))
    vhead = pl.BlockSpec((1, n_keys, 2 * LANES), lambda bi, hi, ti: (bi, 0, hi))

    def const(shape):
        zeros = (0,) * len(shape)
        return pl.BlockSpec(shape, lambda bi, hi, ti: zeros)

    return pl.pallas_call(
        functools.partial(_attn_kernel, diff=diff, lam_init=lam_init),
        out_shape=jax.ShapeDtypeStruct((b, t_q, width), BF16),
        grid=(b, heads, t_q // tq),
        in_specs=[qtile, khead, vhead, const(lam_p.shape), const((1, LANES))],
        out_specs=qtile,
        compiler_params=_params("arbitrary", "arbitrary", "arbitrary"),
        name="diff_attention" if diff else "mla_attention",
    )(q, k, v1, lam_p, subln_g.reshape(1, LANES))


def _attend_segments(q, k, v1, lam_p, subln_g, ctx_len, diff, lam_init, need_ctx):
    t = k.shape[1]
    n_lat = t - ctx_len
    tq_lat = math.gcd(n_lat, ATTN_QUERY_TILE)
    o_lat = _attention(q[:, ctx_len:], k, v1, lam_p, subln_g, tq_lat, t, diff, lam_init)
    if not need_ctx:
        return o_lat
    o_ctx = _attention(q[:, :ctx_len], k, v1, lam_p, subln_g, ctx_len, ctx_len, diff, lam_init)
    return jnp.concatenate([o_ctx, o_lat], axis=1)


def kernel(x, c, ctx, c_ctx, ada_w, ada_b, norm_mix_g, norm_ffn_g, ffn_w_gu, ffn_w_down, ev_w_in, ev_w_out, hg_lower_bound, hg_out_norm_g, hy_short_w, hy_short_b, hy_filt_w1, hy_filt_b1, hy_filt_freq1, hy_filt_w2, hy_filt_b2, hy_filt_freq2, hy_filt_w3, hy_skip, od_w_in, od_w_out, da_lambda, da_subln_g, mla_q_norm_g, mla_w_uq, mla_kv_norm_g, mla_w_ukv, final_norm_g):
    n_batch, n_lat, d = x.shape
    ctx_len = ctx.shape[1]
    depth = ada_w.shape[0]
    assert n_batch % 2 == 0 and n_batch < MOD_ROWS
    assert n_lat % GRID_W == 0 and ctx_len % HG_CHUNK == 0 and n_lat % FFT_N2 == 0 and ctx_len % FFT_N2 == 0
    tm = math.gcd(math.gcd(ctx_len, n_lat), 256)
    n_ctx_tiles = ctx_len // tm
    hg_width = d // 2
    hy_width = d - hg_width
    da_w = d // 2
    q_rank, kv_rank = mla_q_norm_g.shape[1], mla_kv_norm_g.shape[1]

    cc = jnp.concatenate([c, c_ctx[None], jnp.zeros((MOD_ROWS - n_batch - 1, d), F32)], axis=0)
    mods = _ada(cc, ada_w, ada_b).reshape(depth, MOD_ROWS, 6, d)
    h = jnp.concatenate([ctx, x], axis=1)
    rope_tabs = _rope_tables(n_lat, ctx_len)

    for i in range(depth):
        last = i == depth - 1
        if i % 2 == 0:
            e = i // 2
            z = _in_even(h, mods[i], norm_mix_g[i], ev_w_in[e].astype(BF16), tm, n_ctx_tiles)
            a = _hgrn(z, hg_lower_bound, hg_out_norm_g[e], e, hg_width // HG_HEAD_DIM, ctx_len)
            filt_params = (hy_filt_w1[e], hy_filt_b1[e], hy_filt_freq1[e], hy_filt_w2[e], hy_filt_b2[e],
                           hy_filt_freq2[e], hy_filt_w3[e])
            bb = _hyena(z, 5 * hg_width, hy_width, ctx_len, hy_short_w[e], hy_short_b[e], filt_params, hy_skip[e])
            w_out = ev_w_out[e]
        else:
            o = i // 2
            lam_init = 0.8 - 0.6 * math.exp(-0.3 * i)
            w_big, wuq, wukv = _odd_weights(od_w_in[o], mla_w_uq[o], mla_w_ukv[o], da_w, q_rank, kv_rank)
            qd, kd, vd, qm, km, vm = _in_odd(h, mods[i], norm_mix_g[i], w_big, wuq, wukv, mla_q_norm_g[o],
                                             mla_kv_norm_g[o], rope_tabs, tm, n_ctx_tiles, da_w)
            a = _attend_segments(qd, kd, vd, da_lambda[o], da_subln_g[o], ctx_len, True, lam_init, not last)
            bb = _attend_segments(qm, km, vm, da_lambda[o], da_subln_g[o], ctx_len, False, lam_init, not last)
            w_out = od_w_out[o]
        weights = (w_out.astype(BF16), ffn_w_gu[i].astype(BF16), ffn_w_down[i].astype(BF16))
        if not last:
            h = _post(h, a, bb, mods[i], norm_ffn_g[i], *weights, tm, n_ctx_tiles)
    if a.shape[1] != n_lat:
        a, bb = a[:, ctx_len:], bb[:, ctx_len:]
    return _post_final(h, a, bb, mods[depth - 1], norm_ffn_g[depth - 1], *weights, final_norm_g, tm, n_ctx_tiles)
```

```python
import functools
import math

import numpy as np
import jax
import jax.numpy as jnp
from jax import lax
from jax.experimental import pallas as pl
from jax.experimental.pallas import tpu as pltpu

F32 = jnp.float32
BF16 = jnp.bfloat16
HIGHEST = lax.Precision.HIGHEST

GRID_W = 64
EPS = 1e-6
ROPE_BASE = 10000.0
HG_HEAD_DIM = 128
HG_CHUNK = 64
HY_BANDS = 16
HY_EMB = 2 * HY_BANDS + 1
HY_TARGET = 1e-2
HY_STEEP_PCT = 0.3
HY_GENTLE_PCT = 1.5
DA_HEAD_DIM = 64
MLA_HEADS = 4
MLA_NOPE = 64
MLA_ROPE = 32
MLA_V = 128
LOG2_E = 1.4426950408889634

LANES = 128
SUBLANES = 8
BF16_SUBLANES = 16
V7X_VMEM_LIMIT_BYTES = 56 * 1024 * 1024
MOD_ROWS = 16
FFT_N2 = 128
FFT_PITCH = FFT_N2 + SUBLANES
FFT_UNROLL = 16
HGRN_LOCKSTEP = 1
HGRN_UNROLL = 2
ATTN_KEY_BLOCK = 256
ATTN_QUERY_TILE = 1024


def _params(*sem):
    return pltpu.CompilerParams(dimension_semantics=sem, vmem_limit_bytes=V7X_VMEM_LIMIT_BYTES)


def _dot(a, b, precision=None):
    return jnp.dot(a, b, preferred_element_type=F32, precision=precision)


def _dot3(a, b):
    ah, bh = a.astype(BF16), b.astype(BF16)
    al, bl = (a - ah.astype(F32)).astype(BF16), (b - bh.astype(F32)).astype(BF16)
    return _dot(ah, bh) + _dot(al, bh) + _dot(ah, bl)


def _dot_nt(a, b):
    return lax.dot_general(a, b, (((1,), (1,)), ((), ())), preferred_element_type=F32)


def _dot_tn(a, b):
    return lax.dot_general(a, b, (((0,), (0,)), ((), ())), preferred_element_type=F32)


def _rms(x):
    return x * lax.rsqrt(jnp.mean(x * x, axis=-1, keepdims=True) + EPS)


def _silu(x):
    return x * jax.nn.sigmoid(x)


def _tile_mod(mod_ref, modc_ref, n_ctx_tiles):
    return jnp.where(pl.program_id(1) < n_ctx_tiles, modc_ref[...], mod_ref[...])


def _ada_kernel(c_ref, w_ref, b_ref, o_ref):
    o_ref[0] = _dot(_silu(c_ref[...]), w_ref[0], HIGHEST) + b_ref[0]


def _ada(cc, ada_w, ada_b):
    depth, d, n = ada_w.shape
    rows = cc.shape[0]
    tn = n // 4
    return pl.pallas_call(
        _ada_kernel,
        out_shape=jax.ShapeDtypeStruct((depth, rows, n), F32),
        grid=(depth, n // tn),
        in_specs=[
            pl.BlockSpec((rows, d), lambda i, j: (0, 0)),
            pl.BlockSpec((1, d, tn), lambda i, j: (i, 0, j)),
            pl.BlockSpec((1, 1, tn), lambda i, j: (i, 0, j)),
        ],
        out_specs=pl.BlockSpec((1, rows, tn), lambda i, j: (i, 0, j)),
        compiler_params=_params("arbitrary", "arbitrary"),
        name="ada_mod",
    )(cc, ada_w, ada_b.reshape(depth, 1, n))


def _tok_spec(tm, width):
    return pl.BlockSpec((1, tm, width), lambda b, t: (b, t, 0))


def _const_spec(shape):
    zeros = (0,) * len(shape)
    return pl.BlockSpec(shape, lambda b, t: zeros)


def _mod_specs(d, n_batch):
    lat = pl.BlockSpec((None, 6, d), lambda b, t: (b, 0, 0))
    ctx = pl.BlockSpec((None, 6, d), lambda b, t: (n_batch, 0, 0))
    return lat, ctx


def _in_even_kernel(x_ref, mod_ref, modc_ref, g_ref, w_ref, o_ref, *, n_ctx_tiles):
    m = _tile_mod(mod_ref, modc_ref, n_ctx_tiles)
    xn = (_rms(x_ref[0]) * g_ref[...]) * (1.0 + m[1:2]) + m[0:1]
    o_ref[0] = _dot(xn.astype(BF16), w_ref[...])


def _in_even(h, mods, g, w, tm, n_ctx_tiles):
    b, t, d = h.shape
    n = w.shape[1]
    lat, ctx = _mod_specs(d, b)
    return pl.pallas_call(
        functools.partial(_in_even_kernel, n_ctx_tiles=n_ctx_tiles),
        out_shape=jax.ShapeDtypeStruct((b, t, n), F32),
        grid=(b, t // tm),
        in_specs=[_tok_spec(tm, d), lat, ctx, _const_spec((1, d)), _const_spec((d, n))],
        out_specs=_tok_spec(tm, n),
        compiler_params=_params("arbitrary", "arbitrary"),
        name="even_in_proj",
    )(h, mods, mods, g.reshape(1, d), w)


def _post_body(m, h, a, bb, g_ref, wo_ref, wgu_ref, wd_ref, hidden):
    half = a.shape[-1]
    y = _dot(a, wo_ref[:half, :]) + _dot(bb, wo_ref[half:, :])
    h1 = h + m[2:3] * y
    xn = (_rms(h1) * g_ref[...]) * (1.0 + m[4:5]) + m[3:4]
    gu = _dot(xn.astype(BF16), wgu_ref[...])
    act = _silu(gu[:, :hidden]) * gu[:, hidden:]
    return h1 + m[5:6] * _dot(act.astype(BF16), wd_ref[...])


def _post_kernel(h_ref, a_ref, b_ref, mod_ref, modc_ref, g_ref, wo_ref, wgu_ref, wd_ref, o_ref,
                 *, n_ctx_tiles, hidden):
    m = _tile_mod(mod_ref, modc_ref, n_ctx_tiles)
    o_ref[0] = _post_body(m, h_ref[0], a_ref[0], b_ref[0], g_ref, wo_ref, wgu_ref, wd_ref, hidden)


def _post_final_kernel(h_ref, a_ref, b_ref, mod_ref, g_ref, wo_ref, wgu_ref, wd_ref, fg_ref, o_ref,
                       *, hidden):
    h2 = _post_body(mod_ref[...], h_ref[0], a_ref[0], b_ref[0], g_ref, wo_ref, wgu_ref, wd_ref, hidden)
    o_ref[0] = _rms(h2) * fg_ref[...]


def _post(h, a, bb, mods, g, wo, wgu, wd, tm, n_ctx_tiles):
    b, t, d = h.shape
    half = a.shape[-1]
    hidden = wd.shape[0]
    lat, ctx = _mod_specs(d, b)
    return pl.pallas_call(
        functools.partial(_post_kernel, n_ctx_tiles=n_ctx_tiles, hidden=hidden),
        out_shape=jax.ShapeDtypeStruct((b, t, d), F32),
        grid=(b, t // tm),
        in_specs=[_tok_spec(tm, d), _tok_spec(tm, half), _tok_spec(tm, half), lat, ctx,
                  _const_spec((1, d)), _const_spec((2 * half, d)), _const_spec((d, 2 * hidden)),
                  _const_spec((hidden, d))],
        out_specs=_tok_spec(tm, d),
        compiler_params=_params("arbitrary", "arbitrary"),
        name="out_proj_ffn",
    )(h, a, bb, mods, mods, g.reshape(1, d), wo, wgu, wd)


def _post_final(h, a_lat, b_lat, mods, g, wo, wgu, wd, final_g, tm, n_ctx_tiles):
    b, t, d = h.shape
    n_lat = a_lat.shape[1]
    half = a_lat.shape[-1]
    hidden = wd.shape[0]
    lat, _ = _mod_specs(d, b)
    return pl.pallas_call(
        functools.partial(_post_final_kernel, hidden=hidden),
        out_shape=jax.ShapeDtypeStruct((b, n_lat, d), F32),
        grid=(b, n_lat // tm),
        in_specs=[pl.BlockSpec((1, tm, d), lambda bi, ti: (bi, ti + n_ctx_tiles, 0)),
                  _tok_spec(tm, half), _tok_spec(tm, half), lat,
                  _const_spec((1, d)), _const_spec((2 * half, d)), _const_spec((d, 2 * hidden)),
                  _const_spec((hidden, d)), _const_spec((1, d))],
        out_specs=_tok_spec(tm, d),
        compiler_params=_params("arbitrary", "arbitrary"),
        name="out_proj_ffn_final",
    )(h, a_lat, b_lat, mods, g.reshape(1, d), wo, wgu, wd, final_g.reshape(1, d))


def _hgrn_tables(tile):
    c = HG_CHUNK
    r = np.arange(c)
    j = r[None, :]
    blocks = [(j <= r[:, None])]
    masks, bms = [], []
    m = c
    while m >= 2:
        half = m // 2
        p = r % m
        mid = (r - p + half)[:, None]
        upper = (p >= half)[:, None]
        up = (j >= mid) & (j <= r[:, None]) & upper
        lo = (j > r[:, None]) & (j <= mid - 1) & (~upper)
        if m > 2:
            blocks.append(up | lo)
        masks.append(np.broadcast_to(upper, (c, LANES)))
        bms.append(((r[:, None] // m) == (r[None, :] // m)) & upper & (~upper.T))
        m //= 2
    fwd = np.stack(blocks).astype(np.float32)
    bwd = fwd[:, ::-1, ::-1]
    mall = np.stack([fwd.reshape(-1, c), bwd.reshape(-1, c)])
    mall = np.concatenate([mall, mall], axis=2)
    mu = np.stack(masks).astype(np.float32)
    masku = np.tile(np.stack([mu, mu[:, ::-1]]), (1, 1, tile // c, 1))
    bmp = np.stack([np.concatenate([bms[i], bms[i + 1]], axis=1) for i in range(0, len(bms), 2)])
    bmp_bwd = np.stack([np.concatenate([bms[i][::-1, ::-1], bms[i + 1][::-1, ::-1]], axis=1)
                        for i in range(0, len(bms), 2)])
    return (jnp.asarray(mall, dtype=BF16), jnp.asarray(masku, dtype=F32),
            jnp.asarray(np.stack([bmp, bmp_bwd]), dtype=F32))


def _hgrn_kernel(q_ref, ff_ref, fb_ref, i_ref, g_ref, lbp_ref, ng_ref, mall_ref, masku_ref, bmp_ref,
                 o_ref, of_ref, ob_ref, *, layer, n_ctx_tiles, n_tiles):
    c = HG_CHUNK
    dk = HG_HEAD_DIM
    n_levels = masku_ref.shape[1]
    tile = masku_ref.shape[2]
    n_sub = tile // c
    lbp = lbp_ref[...]
    ex = jnp.exp(lbp - jnp.max(lbp, axis=0, keepdims=True))
    p = ex / jnp.sum(ex, axis=0, keepdims=True)
    lb = jnp.sum(p[:layer + 1], axis=0) - p[0]

    def sub(x, g):
        return x[g * c:(g + 1) * c]

    dirs = (0, 1)

    def stack_chunks(x):
        return jnp.concatenate([sub(x, g) for g in range(n_sub)], axis=1)

    def bwd_tile(t):
        if isinstance(t, int):
            return n_ctx_tiles - 1 - t if t < n_ctx_tiles else n_tiles - 1 - (t - n_ctx_tiles)
        return jnp.where(t < n_ctx_tiles, n_ctx_tiles - 1 - t, n_tiles - 1 - (t - n_ctx_tiles))

    def tile_rows(ti):
        start = ti * tile
        return pl.ds(start if isinstance(ti, int) else pl.multiple_of(start, tile), tile)

    def run_trips(trips, sts):
        lanes = [(s, d) for s in range(len(trips)) for d in dirs]
        n = range(len(lanes))
        tis = [bwd_tile(t) if d else t for t in trips for d in dirs]
        rows = [tile_rows(ti) for ti in tis]
        q = [q_ref[0, r, :] for r in rows]
        v = [i_ref[0, r, :] for r in rows]
        lbd = [lb[d:d + 1] for _, d in lanes]
        zf = [(ff_ref, fb_ref)[d][0, rows[l], :] for l, (_, d) in enumerate(lanes)]
        f = [lbd[l] + (1.0 - lbd[l]) * jax.nn.sigmoid(zf[l]) for l in n]
        lf = [jnp.log(x) for x in f]
        k = [1.0 - x for x in f]
        hi = [x.astype(BF16) for x in lf]
        lo = [(lf[l] - hi[l].astype(F32)).astype(BF16) for l in n]
        e2 = [_dot(mall_ref[lanes[l][1]], jnp.concatenate([stack_chunks(hi[l]), stack_chunks(lo[l])], axis=0))
              for l in n]
        ee = [[jnp.exp(e2[l][:, g * dk:(g + 1) * dk]) for g in range(n_sub)] for l in n]

        def block(l, idx):
            return jnp.concatenate([ee[l][g][idx * c:(idx + 1) * c] for g in range(n_sub)], axis=0)

        def tail_decay(l, g):
            cum = e2[l][0:c, g * dk:(g + 1) * dk]
            last = c - 1 if lanes[l][1] == 0 else 0
            return jnp.exp(cum[last:last + 1] - cum)

        g_in = [block(l, 0) for l in n]
        qin = [(q[l] * g_in[l]).astype(BF16) for l in n]
        kout = [(k[l] * jnp.concatenate([tail_decay(l, g) for g in range(n_sub)], axis=0)).astype(BF16)
                for l in n]
        vb = [x.astype(BF16) for x in v]
        zero = jnp.zeros((c, dk), BF16)
        ds_all = [_dot_tn(vb[l], jnp.concatenate(
            [jnp.concatenate([sub(kout[l], g) if gg == g else zero for gg in range(n_sub)], axis=1)
             for g in range(n_sub)], axis=0)) for l in n]
        ds = [[ds_all[l][:, g * dk:(g + 1) * dk] for g in range(n_sub)] for l in n]
        qf = [q[l] * f[l] for l in n]
        w = [[None] * n_levels for _ in n]
        for lvl in range(n_levels):
            for l, (_, d) in enumerate(lanes):
                later = masku_ref[d, lvl] != 0.0
                if lvl == n_levels - 1:
                    w[l][lvl] = jnp.where(later, qf[l], k[l]).astype(BF16)
                else:
                    w[l][lvl] = (jnp.where(later, q[l], k[l]) * block(l, 1 + lvl)).astype(BF16)
        scores = [[[None] * (n_levels // 2) for _ in range(n_sub)] for _ in n]
        for pair in range(n_levels // 2):
            lv_a, lv_b = 2 * pair, 2 * pair + 1
            for g in range(n_sub):
                for l, (_, d) in enumerate(lanes):
                    wa, wb = sub(w[l][lv_a], g), sub(w[l][lv_b], g)
                    rhs = jnp.concatenate([jnp.concatenate([wa, zero], axis=1),
                                           jnp.concatenate([zero, wb], axis=1)], axis=0)
                    raw = _dot_nt(jnp.concatenate([wa, wb], axis=1), rhs)
                    scores[l][g][pair] = (raw * bmp_ref[d, pair]).astype(BF16)
        o = [jnp.concatenate([_dot(jnp.concatenate(scores[l][g], axis=1),
                                   jnp.concatenate([sub(vb[l], g)] * n_levels, axis=0))
                              for g in range(n_sub)], axis=0)
             + jnp.sum(q[l] * k[l], axis=-1, keepdims=True) * v[l] for l in n]
        order = [list(range(n_sub)), list(reversed(range(n_sub)))]
        entering = [[None] * n_sub for _ in n]
        sts = list(sts)
        for l, (_, d) in enumerate(lanes):
            for g in order[d]:
                entering[l][g] = sts[d].astype(BF16)
                last = (g + 1) * c - 1 if d == 0 else g * c
                sts[d] = sts[d] * g_in[l][last:last + 1] + ds[l][g]
        for l, (_, d) in enumerate(lanes):
            out = jnp.concatenate([sub(o[l], g) + _dot_nt(sub(qin[l], g), entering[l][g])
                                   for g in range(n_sub)], axis=0)
            (of_ref, ob_ref)[d][rows[l], :] = out
        return tuple(sts)

    def body(i, carry):
        return run_trips([i * HGRN_LOCKSTEP + s for s in range(HGRN_LOCKSTEP)], carry)

    s0 = jnp.zeros((dk, dk), F32)
    sts = lax.fori_loop(0, n_tiles // HGRN_LOCKSTEP, body, (s0, s0), unroll=HGRN_UNROLL)
    rest = list(range(n_tiles - n_tiles % HGRN_LOCKSTEP, n_tiles))
    if rest:
        run_trips(rest, sts)
    o = _rms(of_ref[...] + ob_ref[...]) * ng_ref[...]
    o_ref[0] = (o * _silu(g_ref[0])).astype(o_ref.dtype)


def _hgrn(z, hg_lower_bound, ng, layer, n_heads, ctx_len):
    b, t, _ = z.shape
    dk = HG_HEAD_DIM
    n_even = hg_lower_bound.shape[0]
    tile = math.gcd(math.gcd(ctx_len, t - ctx_len), 256)
    mall, masku, bm = _hgrn_tables(tile)

    def col(kind):
        return pl.BlockSpec((1, t, dk), lambda bi, hi: (bi, 0, kind * n_heads + hi))

    def const(arr):
        zeros = (0,) * arr.ndim
        return pl.BlockSpec(arr.shape, lambda bi, hi: zeros)

    return pl.pallas_call(
        functools.partial(_hgrn_kernel, layer=layer, n_ctx_tiles=ctx_len // tile, n_tiles=t // tile),
        out_shape=jax.ShapeDtypeStruct((b, t, n_heads * dk), BF16),
        grid=(b, n_heads),
        in_specs=[col(0), col(1), col(2), col(3), col(4),
                  pl.BlockSpec((n_even, 2, dk), lambda bi, hi: (0, 0, hi)),
                  const(ng.reshape(1, dk)), const(mall), const(masku), const(bm)],
        out_specs=pl.BlockSpec((1, t, dk), lambda bi, hi: (bi, 0, hi)),
        scratch_shapes=[pltpu.VMEM((t, dk), F32), pltpu.VMEM((t, dk), F32)],
        compiler_params=_params("arbitrary", "arbitrary"),
        name="hgrn2_scan",
    )(z, z, z, z, z, hg_lower_bound, ng.reshape(1, dk), mall, masku, bm)


def _hy_pre_kernel(x0_ref, x1_ref, v_ref, w0_ref, w1_ref, wv_ref, b0_ref, b1_ref, bv_ref,
                   x0c_ref, u_ref, *, ctx_len):
    t = x0_ref.shape[1]
    row = lax.broadcasted_iota(jnp.int32, (t, LANES), 0)
    first = (row == 0) | (row == ctx_len)
    final = (row == ctx_len - 1) | (row == t - 1)

    def short_conv(z_ref, w_ref, b_ref):
        z = z_ref[0]
        prev = jnp.where(first, 0.0, pltpu.roll(z, 1, 0))
        nxt = jnp.where(final, 0.0, pltpu.roll(z, t - 1, 0))
        w = w_ref[...]
        return prev * w[0:1] + z * w[1:2] + nxt * w[2:3] + b_ref[...]

    x0c_ref[0] = short_conv(x0_ref, w0_ref, b0_ref).astype(x0c_ref.dtype)
    u = short_conv(x1_ref, w1_ref, b1_ref) * short_conv(v_ref, wv_ref, bv_ref)
    u_ref[0] = u.astype(u_ref.dtype)


def _hy_pre(z, short_w, short_b, col0, width, ctx_len):
    b, t, _ = z.shape
    nb = width // LANES
    c0 = col0 // LANES

    def zcol(kind):
        return pl.BlockSpec((1, t, LANES), lambda bi, j: (bi, 0, c0 + kind * nb + j))

    def wcol(kind, rows):
        return pl.BlockSpec((rows, LANES), lambda bi, j: (0, kind * nb + j))

    out = jax.ShapeDtypeStruct((b, t, width), BF16)
    ospec = pl.BlockSpec((1, t, LANES), lambda bi, j: (bi, 0, j))
    sb = short_b.reshape(1, -1)
    return pl.pallas_call(
        functools.partial(_hy_pre_kernel, ctx_len=ctx_len),
        out_shape=(out, out),
        grid=(b, nb),
        in_specs=[zcol(0), zcol(1), zcol(2), wcol(0, 3), wcol(1, 3), wcol(2, 3),
                  wcol(0, 1), wcol(1, 1), wcol(2, 1)],
        out_specs=(ospec, ospec),
        compiler_params=_params("arbitrary", "arbitrary"),
        name="hyena_short_conv",
    )(z, z, z, short_w, short_w, short_w, sb, sb, sb)


def _filter_features(length, n):
    p = np.arange(n)
    is_f = p < length
    is_b = p > n - length
    lag = np.where(is_f, p, np.where(is_b, n - 1 - p, 0))
    tt = np.linspace(0.0, 1.0, length, dtype=np.float32)[lag][:, None]
    w = (2.0 * math.pi * lag.astype(np.float32) / length)[:, None].astype(np.float32)
    bands = np.linspace(1e-4, HY_BANDS - 1, HY_BANDS, dtype=np.float32)[None, :]
    feat = np.concatenate([tt, np.cos(bands * w), -np.sin(bands * w), is_f[:, None], is_b[:, None]],
                          axis=-1).astype(np.float32)
    pad = (-feat.shape[1]) % BF16_SUBLANES
    return jnp.asarray(np.pad(feat, ((0, 0), (0, pad))))


def _filt_kernel(feat_ref, w1_ref, b1_ref, fr1_ref, w2_ref, b2_ref, fr2_ref,
                 w3f_ref, w3b_ref, dl_ref, o_ref, hid_ref):
    z = feat_ref[...]
    mf = z[:, HY_EMB:HY_EMB + 1]
    mb = z[:, HY_EMB + 1:HY_EMB + 2]

    @pl.when(pl.program_id(0) == 0)
    def _():
        h1 = jnp.sin(fr1_ref[...] * (_dot3(z, w1_ref[...]) + b1_ref[...]))
        hid_ref[...] = jnp.sin(fr2_ref[...] * (_dot3(h1, w2_ref[...]) + b2_ref[...]))

    hid = hid_ref[...]
    hf = _dot3(hid, w3f_ref[...])
    hb = _dot3(hid, w3b_ref[...])
    win = jnp.exp(-z[:, 0:1] * dl_ref[...])
    f = (mf * hf + mb * hb) * win
    o_ref[...] = f / jnp.sum(jnp.abs(f), axis=0, keepdims=True)


def _hyena_filter(length, n, w1, b1, fr1, w2, b2, fr2, w3, width):
    feat = _filter_features(length, n)
    nf = feat.shape[1]
    hid = w1.shape[1]
    w1p = jnp.pad(w1, ((0, nf - w1.shape[0]), (0, 0)))
    d_lo = -math.log(HY_TARGET) / HY_GENTLE_PCT
    d_hi = -math.log(HY_TARGET) / HY_STEEP_PCT
    deltas = jnp.asarray(np.linspace(d_lo, d_hi, width, dtype=np.float32)[None, :])
    nb = width // LANES

    def full(shape):
        zeros = (0,) * len(shape)
        return pl.BlockSpec(shape, lambda j: zeros)

    return pl.pallas_call(
        _filt_kernel,
        out_shape=jax.ShapeDtypeStruct((n, width), F32),
        grid=(nb,),
        in_specs=[full((n, nf)), full((nf, hid)), full((1, hid)),
                  full((1, hid)), full((hid, hid)), full((1, hid)), full((1, hid)),
                  pl.BlockSpec((hid, LANES), lambda j: (0, j)),
                  pl.BlockSpec((hid, LANES), lambda j: (0, nb + j)),
                  pl.BlockSpec((1, LANES), lambda j: (0, j))],
        out_specs=pl.BlockSpec((n, LANES), lambda j: (0, j)),
        scratch_shapes=[pltpu.VMEM((n, hid), F32)],
        compiler_params=_params("arbitrary"),
        name="hyena_filter_mlp",
    )(feat, w1p, b1.reshape(1, -1), fr1.reshape(1, -1), w2, b2.reshape(1, -1),
      fr2.reshape(1, -1), w3, w3, deltas)


def _dft_tables(n1, n1_in):
    n2 = FFT_N2
    n = n1 * n2
    a = np.arange(n1)
    j = np.arange(n2)
    ang = -2.0 * np.pi * (a[None, None, :] * a[None, :, None] / n1 + j[:, None, None] * a[None, :, None] / n)
    tr, ti = np.cos(ang), np.sin(ang)
    fwd_a = np.concatenate([np.concatenate([tr, -ti], 2), np.concatenate([ti, tr], 2)], 1)
    trt, tit = np.swapaxes(tr, 1, 2), -np.swapaxes(ti, 1, 2)
    inv_a = np.concatenate([np.concatenate([trt, -tit], 2), np.concatenate([tit, trt], 2)], 1)
    keep = np.concatenate([np.arange(n1_in), n1 + np.arange(n1_in)])
    ang2 = -2.0 * np.pi * (j[:, None] * j[None, :]) / n2
    cr, ci = np.cos(ang2), np.sin(ang2)
    fwd_c = np.block([[cr, -ci], [ci, cr]])
    inv_c = np.block([[cr, ci], [-ci, cr]])
    real_a = np.concatenate([tr, ti], 1)
    return dict(fwd_a=fwd_a[:, :, keep], inv_a=inv_a[:, keep, :], fwd_c=fwd_c, inv_c=inv_c,
                real_a=real_a)


def _slab(idx):
    return pl.ds(pl.multiple_of(idx * FFT_PITCH, SUBLANES), FFT_N2)


def _fft_filter_kernel(f_ref, wa_ref, wc_ref, o_ref, scr_ref, *, n1):
    n2 = FFT_N2
    rows = 2 * n1
    scale = 1.0 / (n1 * n2)

    def split3(x):
        hi = x.astype(BF16)
        lo = (x - hi.astype(F32)).astype(BF16)
        return jnp.concatenate([hi, lo, hi], axis=0)

    def stage_a(j, carry):
        scr_ref[pl.ds(j, rows, stride=FFT_PITCH), :] = _dot(wa_ref[j], split3(f_ref[j]))
        return carry

    lax.fori_loop(0, n2, stage_a, 0, unroll=FFT_UNROLL)

    def stage_c(k, carry):
        x = jnp.concatenate([scr_ref[_slab(k), :], scr_ref[_slab(n1 + k), :]], axis=0)
        o_ref[k] = _dot(wc_ref[...], split3(x)) * scale
        return carry

    lax.fori_loop(0, n1, stage_c, 0, unroll=FFT_UNROLL)


def _split3_cols(w):
    hi = w.astype(np.float32).astype(jnp.bfloat16)
    lo = (w.astype(np.float32) - np.asarray(hi, np.float32)).astype(jnp.bfloat16)
    return jnp.asarray(np.concatenate([hi, hi, lo], axis=-1))


def _fft_filter(filt, n1, tables):
    n2 = FFT_N2
    width = filt.shape[1]
    ft = filt.reshape(n1, n2, width).transpose(1, 0, 2)
    wa = _split3_cols(tables["real_a"])
    wc = _split3_cols(tables["fwd_c"])
    return pl.pallas_call(
        functools.partial(_fft_filter_kernel, n1=n1),
        out_shape=jax.ShapeDtypeStruct((n1, 2 * n2, width), F32),
        grid=(width // LANES,),
        in_specs=[pl.BlockSpec((n2, n1, LANES), lambda c: (0, 0, c)),
                  pl.BlockSpec(wa.shape, lambda c: (0, 0, 0)),
                  pl.BlockSpec(wc.shape, lambda c: (0, 0))],
        out_specs=pl.BlockSpec((n1, 2 * n2, LANES), lambda c: (0, 0, c)),
        scratch_shapes=[pltpu.VMEM((2 * n1 * FFT_PITCH, LANES), F32)],
        compiler_params=_params("arbitrary"),
        name="hyena_filter_dft",
    )(ft, wa, wc)


def _fftconv_kernel(u_ref, x0_ref, fh_ref, skip_ref, wfa_ref, wfc_ref, wic_ref, wia_ref, o_ref,
                    scr_ref, *, n1):
    n2 = FFT_N2
    rows = 2 * n1

    def stage_a(j, carry):
        scr_ref[pl.ds(j, rows, stride=FFT_PITCH), :] = _dot(wfa_ref[j], u_ref[0, j])
        return carry

    lax.fori_loop(0, n2, stage_a, 0, unroll=FFT_UNROLL)

    def stage_c(k, carry):
        x = jnp.concatenate([scr_ref[_slab(k), :], scr_ref[_slab(n1 + k), :]], axis=0)
        xf = _dot(wfc_ref[...], x.astype(BF16))
        fh = fh_ref[k]
        xr, xi, fr, fi = xf[:n2], xf[n2:], fh[:n2], fh[n2:]
        y = jnp.concatenate([xr * fr - xi * fi, xr * fi + xi * fr], axis=0)
        zt = _dot(wic_ref[...], y.astype(BF16))
        scr_ref[_slab(k), :] = zt[:n2]
        scr_ref[_slab(n1 + k), :] = zt[n2:]
        return carry

    lax.fori_loop(0, n1, stage_c, 0, unroll=FFT_UNROLL)

    def stage_ai(j, carry):
        x = scr_ref[pl.ds(j, rows, stride=FFT_PITCH), :]
        y = _dot(wia_ref[j], x.astype(BF16))
        u = u_ref[0, j].astype(F32)
        o_ref[0, j] = (x0_ref[0, j].astype(F32) * (y + skip_ref[...] * u)).astype(o_ref.dtype)
        return carry

    lax.fori_loop(0, n2, stage_ai, 0, unroll=FFT_UNROLL)


def _to_fft_layout(x, n1_in):
    b, length, c = x.shape
    n1_used = length // FFT_N2
    x = x.reshape(b // 2, 2, n1_used, FFT_N2, c)
    x = jnp.pad(x, ((0, 0), (0, 0), (0, n1_in - n1_used), (0, 0), (0, 0)))
    return x.transpose(0, 3, 1, 2, 4).reshape(b // 2, FFT_N2, 2 * n1_in, c)


def _from_fft_layout(y, length):
    p, n2, rows, c = y.shape
    n1_in = rows // 2
    y = y.reshape(p, n2, 2, n1_in, c).transpose(0, 2, 3, 1, 4)
    return y.reshape(2 * p, n1_in * n2, c)[:, :length]


def _fftconv(u, x0c, fh, skip, n1, n1_in, tables):
    _, length, width = u.shape
    ut = _to_fft_layout(u, n1_in)
    xt = _to_fft_layout(x0c, n1_in)
    pairs, n2, rin, _ = ut.shape
    wfa = jnp.asarray(tables["fwd_a"], dtype=BF16)
    wia = jnp.asarray(tables["inv_a"], dtype=BF16)
    wfc = jnp.asarray(tables["fwd_c"], dtype=BF16)
    wic = jnp.asarray(tables["inv_c"], dtype=BF16)
    data = pl.BlockSpec((1, n2, rin, LANES), lambda c, p: (p, 0, 0, c))

    def const(arr):
        zeros = (0,) * arr.ndim
        return pl.BlockSpec(arr.shape, lambda c, p: zeros, pipeline_mode=pl.Buffered(1))

    out = pl.pallas_call(
        functools.partial(_fftconv_kernel, n1=n1),
        out_shape=jax.ShapeDtypeStruct(ut.shape, BF16),
        grid=(width // LANES, pairs),
        in_specs=[data, data,
                  pl.BlockSpec((n1, 2 * n2, LANES), lambda c, p: (0, 0, c)),
                  pl.BlockSpec((1, LANES), lambda c, p: (0, c)),
                  const(wfa), const(wfc), const(wic), const(wia)],
        out_specs=data,
        scratch_shapes=[pltpu.VMEM((2 * n1 * FFT_PITCH, LANES), F32)],
        compiler_params=_params("arbitrary", "arbitrary"),
        name="hyena_dft_conv",
    )(ut, xt, fh, skip.reshape(1, width), wfa, wfc, wic, wia)
    return _from_fft_layout(out, length)


def _hyena(z, col0, width, ctx_len, short_w, short_b, filt_params, skip):
    x0c, u = _hy_pre(z, short_w, short_b, col0, width, ctx_len)
    outs = []
    for lo, hi in ((0, ctx_len), (ctx_len, z.shape[1])):
        length = hi - lo
        n1_in = max(length // FFT_N2, SUBLANES)
        n1 = 2 * n1_in
        tables = _dft_tables(n1, n1_in)
        filt = _hyena_filter(length, n1 * FFT_N2, *filt_params, width)
        fh = _fft_filter(filt, n1, tables)
        outs.append(_fftconv(u[:, lo:hi], x0c[:, lo:hi], fh, skip, n1, n1_in, tables))
    return jnp.concatenate(outs, axis=1)


def _rope_tables(n_lat, ctx_len):
    tok = np.arange(n_lat)
    row, colp = tok // GRID_W, tok % GRID_W

    def axial(half):
        inv = ROPE_BASE ** (-np.arange(half, dtype=np.float32) / half)
        parts_c, parts_s = [], []
        for pos in (row, colp):
            ang = pos.astype(np.float32)[:, None] * inv
            parts_c += [np.cos(ang), np.cos(ang)]
            parts_s += [-np.sin(ang), np.sin(ang)]
        return np.concatenate(parts_c, 1), np.concatenate(parts_s, 1)

    dc, ds = axial(DA_HEAD_DIM // 4)
    mc, ms = axial(MLA_ROPE // 4)
    ones, zeros = np.ones((n_lat, MLA_NOPE), np.float32), np.zeros((n_lat, MLA_NOPE), np.float32)
    padc = np.ones((n_lat, LANES - MLA_NOPE - MLA_ROPE), np.float32)
    tabs = [np.concatenate([dc, dc], 1), np.concatenate([ds, ds], 1),
            np.concatenate([ones, mc, padc], 1), np.concatenate([zeros, ms, 0 * padc], 1)]
    out = []
    for i, tb in enumerate(tabs):
        ctx_rows = np.ones((ctx_len, LANES), np.float32) if i % 2 == 0 else np.zeros((ctx_len, LANES), np.float32)
        out.append(jnp.asarray(np.concatenate([ctx_rows, tb.astype(np.float32)], 0)))
    return out


def _swap_perm(width, group, half):
    idx = np.arange(width)
    pos = idx % group
    return np.where((pos % (2 * half)) < half, idx + half, idx - half)


def _with_ones(v):
    ones = jnp.ones((v.shape[0], LANES), v.dtype)
    parts = []
    for hd in range(v.shape[1] // LANES):
        parts += [v[:, hd * LANES:(hd + 1) * LANES], ones]
    return jnp.concatenate(parts, axis=1)


def _in_odd_kernel(x_ref, mod_ref, modc_ref, g_ref, w_ref, dc_ref, ds_ref, mc_ref, ms_ref,
                   qg_ref, kvg_ref, wuq_ref, wukv_ref,
                   qd_ref, kd_ref, vd_ref, qm_ref, km_ref, vm_ref, *, n_ctx_tiles, da_w, q_rank, kv_rank):
    m = _tile_mod(mod_ref, modc_ref, n_ctx_tiles)
    xn = (_rms(x_ref[0]) * g_ref[...]) * (1.0 + m[1:2]) + m[0:1]
    z = _dot(xn.astype(BF16), w_ref[...])
    nrep = da_w // LANES
    dc = jnp.concatenate([dc_ref[...]] * nrep, axis=1)
    ds = jnp.concatenate([ds_ref[...]] * nrep, axis=1)
    mc = jnp.concatenate([mc_ref[...]] * MLA_HEADS, axis=1)
    ms = jnp.concatenate([ms_ref[...]] * MLA_HEADS, axis=1)
    sa = DA_HEAD_DIM ** -0.5 * LOG2_E
    sm = (MLA_NOPE + MLA_ROPE) ** -0.5 * LOG2_E
    o = 0
    qd_ref[0] = ((z[:, o:o + da_w] * dc + z[:, o + da_w:o + 2 * da_w] * ds) * sa).astype(BF16)
    o += 2 * da_w
    kd_ref[0] = (z[:, o:o + da_w] * dc + z[:, o + da_w:o + 2 * da_w] * ds).astype(BF16)
    o += 2 * da_w
    vd_ref[0] = _with_ones(z[:, o:o + da_w]).astype(BF16)
    o += da_w
    cq = _rms(z[:, o:o + q_rank]) * qg_ref[...]
    o += q_rank
    ckv = _rms(z[:, o:o + kv_rank]) * kvg_ref[...]
    o += kv_rank
    kr = z[:, o:o + LANES] * mc_ref[...] + z[:, o + LANES:o + 2 * LANES] * ms_ref[...]
    mw = MLA_HEADS * LANES
    qu = _dot(cq.astype(BF16), wuq_ref[...])
    qm_ref[0] = ((qu[:, :mw] * mc + qu[:, mw:] * ms) * sm).astype(BF16)
    kvu = _dot(ckv.astype(BF16), wukv_ref[...])
    km_ref[0] = (kvu[:, :mw] + jnp.concatenate([kr] * MLA_HEADS, axis=1)).astype(BF16)
    vm_ref[0] = _with_ones(kvu[:, mw:]).astype(BF16)


def _odd_weights(w_in, w_uq, w_ukv, da_w, q_rank, kv_rank):
    qw, kw, vw = w_in[:, :da_w], w_in[:, da_w:2 * da_w], w_in[:, 2 * da_w:3 * da_w]
    o = 3 * da_w
    cqw, ckvw, krw = w_in[:, o:o + q_rank], w_in[:, o + q_rank:o + q_rank + kv_rank], w_in[:, o + q_rank + kv_rank:]
    perm_da = _swap_perm(da_w, DA_HEAD_DIM // 2, DA_HEAD_DIM // 4)
    perm_r = _swap_perm(MLA_ROPE, MLA_ROPE // 2, MLA_ROPE // 4)
    d = w_in.shape[0]

    def rope_group(wr):
        return jnp.pad(wr, ((0, 0), (MLA_NOPE, LANES - MLA_NOPE - MLA_ROPE)))

    w_big = jnp.concatenate([qw, qw[:, perm_da], kw, kw[:, perm_da], vw, cqw, ckvw,
                             rope_group(krw), rope_group(krw[:, perm_r])], axis=1)
    dq = MLA_NOPE + MLA_ROPE
    pad = LANES - dq
    uq = w_uq.reshape(q_rank, MLA_HEADS, dq)
    uq_a = jnp.pad(uq, ((0, 0), (0, 0), (0, pad))).reshape(q_rank, MLA_HEADS * LANES)
    uq_s = jnp.pad(uq[:, :, MLA_NOPE:][:, :, perm_r], ((0, 0), (0, 0), (MLA_NOPE, pad)))
    uq_s = uq_s.reshape(q_rank, MLA_HEADS * LANES)
    ukv = w_ukv.reshape(kv_rank, MLA_HEADS, MLA_NOPE + MLA_V)
    uk = jnp.pad(ukv[:, :, :MLA_NOPE], ((0, 0), (0, 0), (0, LANES - MLA_NOPE))).reshape(kv_rank, MLA_HEADS * LANES)
    uv = ukv[:, :, MLA_NOPE:].reshape(kv_rank, MLA_HEADS * MLA_V)
    return (w_big.astype(BF16), jnp.concatenate([uq_a, uq_s], axis=1).astype(BF16),
            jnp.concatenate([uk, uv], axis=1).astype(BF16))


def _in_odd(h, mods, g, w_big, wuq, wukv, qg, kvg, tabs, tm, n_ctx_tiles, da_w):
    b, t, d = h.shape
    q_rank, kv_rank = qg.shape[0], kvg.shape[0]
    lat, ctx = _mod_specs(d, b)
    tab = pl.BlockSpec((tm, LANES), lambda bi, ti: (ti, 0))
    mw = MLA_HEADS * LANES

    def tok_major(width):
        return jax.ShapeDtypeStruct((b, t, width), BF16), _tok_spec(tm, width)

    outs, ospecs = zip(tok_major(da_w), tok_major(da_w), tok_major(2 * da_w),
                       tok_major(mw), tok_major(mw), tok_major(2 * mw))
    return pl.pallas_call(
        functools.partial(_in_odd_kernel, n_ctx_tiles=n_ctx_tiles, da_w=da_w, q_rank=q_rank, kv_rank=kv_rank),
        out_shape=tuple(outs),
        grid=(b, t // tm),
        in_specs=[_tok_spec(tm, d), lat, ctx, _const_spec((1, d)), _const_spec(w_big.shape),
                  tab, tab, tab, tab, _const_spec((1, q_rank)), _const_spec((1, kv_rank)),
                  _const_spec(wuq.shape), _const_spec(wukv.shape)],
        out_specs=tuple(ospecs),
        compiler_params=_params("arbitrary", "arbitrary"),
        name="odd_in_proj",
    )(h, mods, mods, g.reshape(1, d), w_big, *tabs, qg.reshape(1, -1), kvg.reshape(1, -1), wuq, wukv)


def _softmax_pv(qs, k_ref, v_ref):
    n_keys = k_ref.shape[1]
    starts = list(range(0, n_keys, ATTN_KEY_BLOCK))
    m = [None] * len(qs)
    acc = [None] * len(qs)
    for start in starts:
        size = min(ATTN_KEY_BLOCK, n_keys - start)
        k = k_ref[0, start:start + size, :]
        v1 = v_ref[0, start:start + size, :]
        for i, q in enumerate(qs):
            s = _dot_nt(q, k)
            m_blk = jnp.max(s, axis=-1, keepdims=True)
            if start == 0:
                m[i] = m_blk
                acc[i] = _dot(jnp.exp2(s - m_blk).astype(BF16), v1)
            else:
                m_new = jnp.maximum(m[i], m_blk)
                pv = _dot(jnp.exp2(s - m_new).astype(BF16), v1)
                acc[i] = jnp.exp2(m[i] - m_new) * acc[i] + pv
                m[i] = m_new
    return [a[:, :LANES] / a[:, LANES:] for a in acc]


def _attn_kernel(q_ref, k_ref, v_ref, lam_ref, sg_ref, *rest, diff, lam_init):
    o_ref = rest[-1]
    q = q_ref[0]
    if not diff:
        o_ref[0] = _softmax_pv([q], k_ref, v_ref)[0].astype(o_ref.dtype)
        return
    first = lax.broadcasted_iota(jnp.int32, q.shape, 1) < DA_HEAD_DIM
    zero = jnp.zeros_like(q)
    o1, o2 = _softmax_pv([jnp.where(first, q, zero), jnp.where(first, zero, q)], k_ref, v_ref)
    lp = lam_ref[...]
    lam = (jnp.exp(jnp.sum(lp[0:1] * lp[1:2], axis=-1, keepdims=True))
           - jnp.exp(jnp.sum(lp[2:3] * lp[3:4], axis=-1, keepdims=True)) + lam_init)
    o_ref[0] = (_rms(o1 - lam * o2) * sg_ref[...] * (1.0 - lam_init)).astype(o_ref.dtype)


def _attention(q, k, v1, lam_p, subln_g, q_rows, tq, n_keys, out_rows, out_start, diff, lam_init, earlier=None):
    b, _, width = q.shape
    heads = width // LANES
    q_start, q_stop = q_rows

    def rows_from(start):
        return pl.BlockSpec((pl.Element(1), pl.Element(tq), pl.Element(LANES)),
                            lambda bi, hi, ti: (bi, pl.multiple_of(start + ti * tq, math.gcd(start, tq)),
                                                pl.multiple_of(hi * LANES, LANES)))

    khead = pl.BlockSpec((1, n_keys, LANES), lambda bi, hi, ti: (bi, 0, hi))
    vhead = pl.BlockSpec((1, n_keys, 2 * LANES), lambda bi, hi, ti: (bi, 0, hi))

    def const(shape):
        zeros = (0,) * len(shape)
        return pl.BlockSpec(shape, lambda bi, hi, ti: zeros)

    in_specs = [rows_from(q_start), khead, vhead, const(lam_p.shape), const((1, LANES))]
    args = [q, k, v1, lam_p, subln_g.reshape(1, LANES)]
    aliases = {}
    if earlier is not None:
        in_specs.append(pl.BlockSpec(memory_space=pl.ANY))
        args.append(earlier)
        aliases = {len(args) - 1: 0}
    return pl.pallas_call(
        functools.partial(_attn_kernel, diff=diff, lam_init=lam_init),
        out_shape=jax.ShapeDtypeStruct((b, out_rows, width), BF16),
        grid=(b, heads, (q_stop - q_start) // tq),
        in_specs=in_specs,
        out_specs=rows_from(out_start),
        input_output_aliases=aliases,
        compiler_params=_params("arbitrary", "arbitrary", "arbitrary"),
        name="diff_attention" if diff else "mla_attention",
    )(*args)


def _attend_segments(q, k, v1, lam_p, subln_g, ctx_len, diff, lam_init, need_ctx):
    t = k.shape[1]
    n_lat = t - ctx_len
    tq_lat = math.gcd(n_lat, ATTN_QUERY_TILE)
    if not need_ctx:
        return _attention(q, k, v1, lam_p, subln_g, (ctx_len, t), tq_lat, t, n_lat, 0, diff, lam_init)
    o = _attention(q, k, v1, lam_p, subln_g, (ctx_len, t), tq_lat, t, t, ctx_len, diff, lam_init)
    return _attention(q, k, v1, lam_p, subln_g, (0, ctx_len), ctx_len, ctx_len, t, 0, diff, lam_init, earlier=o)


def kernel(x, c, ctx, c_ctx, ada_w, ada_b, norm_mix_g, norm_ffn_g, ffn_w_gu, ffn_w_down, ev_w_in, ev_w_out, hg_lower_bound, hg_out_norm_g, hy_short_w, hy_short_b, hy_filt_w1, hy_filt_b1, hy_filt_freq1, hy_filt_w2, hy_filt_b2, hy_filt_freq2, hy_filt_w3, hy_skip, od_w_in, od_w_out, da_lambda, da_subln_g, mla_q_norm_g, mla_w_uq, mla_kv_norm_g, mla_w_ukv, final_norm_g):
    n_batch, n_lat, d = x.shape
    ctx_len = ctx.shape[1]
    depth = ada_w.shape[0]
    assert n_batch % 2 == 0 and n_batch < MOD_ROWS
    assert n_lat % GRID_W == 0 and ctx_len % HG_CHUNK == 0 and n_lat % FFT_N2 == 0 and ctx_len % FFT_N2 == 0
    tm = math.gcd(math.gcd(ctx_len, n_lat), 256)
    n_ctx_tiles = ctx_len // tm
    hg_width = d // 2
    hy_width = d - hg_width
    da_w = d // 2
    q_rank, kv_rank = mla_q_norm_g.shape[1], mla_kv_norm_g.shape[1]

    cc = jnp.concatenate([c, c_ctx[None], jnp.zeros((MOD_ROWS - n_batch - 1, d), F32)], axis=0)
    mods = _ada(cc, ada_w, ada_b).reshape(depth, MOD_ROWS, 6, d)
    h = jnp.concatenate([ctx, x], axis=1)
    rope_tabs = _rope_tables(n_lat, ctx_len)

    for i in range(depth):
        last = i == depth - 1
        if i % 2 == 0:
            e = i // 2
            z = _in_even(h, mods[i], norm_mix_g[i], ev_w_in[e].astype(BF16), tm, n_ctx_tiles)
            a = _hgrn(z, hg_lower_bound, hg_out_norm_g[e], e, hg_width // HG_HEAD_DIM, ctx_len)
            filt_params = (hy_filt_w1[e], hy_filt_b1[e], hy_filt_freq1[e], hy_filt_w2[e], hy_filt_b2[e],
                           hy_filt_freq2[e], hy_filt_w3[e])
            bb = _hyena(z, 5 * hg_width, hy_width, ctx_len, hy_short_w[e], hy_short_b[e], filt_params, hy_skip[e])
            w_out = ev_w_out[e]
        else:
            o = i // 2
            lam_init = 0.8 - 0.6 * math.exp(-0.3 * i)
            w_big, wuq, wukv = _odd_weights(od_w_in[o], mla_w_uq[o], mla_w_ukv[o], da_w, q_rank, kv_rank)
            qd, kd, vd, qm, km, vm = _in_odd(h, mods[i], norm_mix_g[i], w_big, wuq, wukv, mla_q_norm_g[o],
                                             mla_kv_norm_g[o], rope_tabs, tm, n_ctx_tiles, da_w)
            a = _attend_segments(qd, kd, vd, da_lambda[o], da_subln_g[o], ctx_len, True, lam_init, not last)
            bb = _attend_segments(qm, km, vm, da_lambda[o], da_subln_g[o], ctx_len, False, lam_init, not last)
            w_out = od_w_out[o]
        weights = (w_out.astype(BF16), ffn_w_gu[i].astype(BF16), ffn_w_down[i].astype(BF16))
        if not last:
            h = _post(h, a, bb, mods[i], norm_ffn_g[i], *weights, tm, n_ctx_tiles)
    if a.shape[1] != n_lat:
        a, bb = a[:, ctx_len:], bb[:, ctx_len:]
    return _post_final(h, a, bb, mods[depth - 1], norm_ffn_g[depth - 1], *weights, final_norm_g, tm, n_ctx_tiles)
```

```python
import functools
import math

import numpy as np
import jax
import jax.numpy as jnp
from jax import lax
from jax.experimental import pallas as pl
from jax.experimental.pallas import tpu as pltpu

F32 = jnp.float32
BF16 = jnp.bfloat16
HIGHEST = lax.Precision.HIGHEST

GRID_W = 64
EPS = 1e-6
ROPE_BASE = 10000.0
HG_HEAD_DIM = 128
HG_CHUNK = 64
HY_BANDS = 16
HY_EMB = 2 * HY_BANDS + 1
HY_TARGET = 1e-2
HY_STEEP_PCT = 0.3
HY_GENTLE_PCT = 1.5
DA_HEAD_DIM = 64
MLA_HEADS = 4
MLA_NOPE = 64
MLA_ROPE = 32
MLA_V = 128
LOG2_E = 1.4426950408889634

LANES = 128
SUBLANES = 8
BF16_SUBLANES = 16
V7X_VMEM_LIMIT_BYTES = 56 * 1024 * 1024
MOD_ROWS = 16
FFT_N2 = 128
FFT_PITCH = FFT_N2 + SUBLANES
FFT_UNROLL = 16
HGRN_LOCKSTEP = 2
HGRN_UNROLL = 1
ATTN_KEY_BLOCK = 256
ATTN_QUERY_TILE = 1024


def _params(*sem):
    return pltpu.CompilerParams(dimension_semantics=sem, vmem_limit_bytes=V7X_VMEM_LIMIT_BYTES)


def _dot(a, b, precision=None):
    return jnp.dot(a, b, preferred_element_type=F32, precision=precision)


def _dot3(a, b):
    ah, bh = a.astype(BF16), b.astype(BF16)
    al, bl = (a - ah.astype(F32)).astype(BF16), (b - bh.astype(F32)).astype(BF16)
    return _dot(ah, bh) + _dot(al, bh) + _dot(ah, bl)


def _dot_nt(a, b):
    return lax.dot_general(a, b, (((1,), (1,)), ((), ())), preferred_element_type=F32)


def _dot_tn(a, b):
    return lax.dot_general(a, b, (((0,), (0,)), ((), ())), preferred_element_type=F32)


def _rms(x):
    return x * lax.rsqrt(jnp.mean(x * x, axis=-1, keepdims=True) + EPS)


def _silu(x):
    return x * jax.nn.sigmoid(x)


def _tile_mod(mod_ref, modc_ref, n_ctx_tiles):
    return jnp.where(pl.program_id(1) < n_ctx_tiles, modc_ref[...], mod_ref[...])


def _ada_kernel(c_ref, w_ref, b_ref, o_ref):
    o_ref[0] = _dot(_silu(c_ref[...]), w_ref[0], HIGHEST) + b_ref[0]


def _ada(cc, ada_w, ada_b):
    depth, d, n = ada_w.shape
    rows = cc.shape[0]
    tn = n // 4
    return pl.pallas_call(
        _ada_kernel,
        out_shape=jax.ShapeDtypeStruct((depth, rows, n), F32),
        grid=(depth, n // tn),
        in_specs=[
            pl.BlockSpec((rows, d), lambda i, j: (0, 0)),
            pl.BlockSpec((1, d, tn), lambda i, j: (i, 0, j)),
            pl.BlockSpec((1, 1, tn), lambda i, j: (i, 0, j)),
        ],
        out_specs=pl.BlockSpec((1, rows, tn), lambda i, j: (i, 0, j)),
        compiler_params=_params("arbitrary", "arbitrary"),
        name="ada_mod",
    )(cc, ada_w, ada_b.reshape(depth, 1, n))


def _tok_spec(tm, width):
    return pl.BlockSpec((1, tm, width), lambda b, t: (b, t, 0))


def _const_spec(shape):
    zeros = (0,) * len(shape)
    return pl.BlockSpec(shape, lambda b, t: zeros)


def _mod_specs(d, n_batch):
    lat = pl.BlockSpec((None, 6, d), lambda b, t: (b, 0, 0))
    ctx = pl.BlockSpec((None, 6, d), lambda b, t: (n_batch, 0, 0))
    return lat, ctx


def _in_even_kernel(x_ref, mod_ref, modc_ref, g_ref, w_ref, o_ref, *, n_ctx_tiles):
    m = _tile_mod(mod_ref, modc_ref, n_ctx_tiles)
    xn = (_rms(x_ref[0]) * g_ref[...]) * (1.0 + m[1:2]) + m[0:1]
    o_ref[0] = _dot(xn.astype(BF16), w_ref[...])


def _in_even(h, mods, g, w, tm, n_ctx_tiles):
    b, t, d = h.shape
    n = w.shape[1]
    lat, ctx = _mod_specs(d, b)
    return pl.pallas_call(
        functools.partial(_in_even_kernel, n_ctx_tiles=n_ctx_tiles),
        out_shape=jax.ShapeDtypeStruct((b, t, n), F32),
        grid=(b, t // tm),
        in_specs=[_tok_spec(tm, d), lat, ctx, _const_spec((1, d)), _const_spec((d, n))],
        out_specs=_tok_spec(tm, n),
        compiler_params=_params("arbitrary", "arbitrary"),
        name="even_in_proj",
    )(h, mods, mods, g.reshape(1, d), w)


def _post_body(m, h, a, bb, g_ref, wo_ref, wgu_ref, wd_ref, hidden):
    half = a.shape[-1]
    y = _dot(a, wo_ref[:half, :]) + _dot(bb, wo_ref[half:, :])
    h1 = h + m[2:3] * y
    xn = (_rms(h1) * g_ref[...]) * (1.0 + m[4:5]) + m[3:4]
    gu = _dot(xn.astype(BF16), wgu_ref[...])
    act = _silu(gu[:, :hidden]) * gu[:, hidden:]
    return h1 + m[5:6] * _dot(act.astype(BF16), wd_ref[...])


def _post_kernel(h_ref, a_ref, b_ref, mod_ref, modc_ref, g_ref, wo_ref, wgu_ref, wd_ref, o_ref,
                 *, n_ctx_tiles, hidden):
    m = _tile_mod(mod_ref, modc_ref, n_ctx_tiles)
    o_ref[0] = _post_body(m, h_ref[0], a_ref[0], b_ref[0], g_ref, wo_ref, wgu_ref, wd_ref, hidden)


def _post_final_kernel(h_ref, a_ref, b_ref, mod_ref, g_ref, wo_ref, wgu_ref, wd_ref, fg_ref, o_ref,
                       *, hidden):
    h2 = _post_body(mod_ref[...], h_ref[0], a_ref[0], b_ref[0], g_ref, wo_ref, wgu_ref, wd_ref, hidden)
    o_ref[0] = _rms(h2) * fg_ref[...]


def _post(h, a, bb, mods, g, wo, wgu, wd, tm, n_ctx_tiles):
    b, t, d = h.shape
    half = a.shape[-1]
    hidden = wd.shape[0]
    lat, ctx = _mod_specs(d, b)
    return pl.pallas_call(
        functools.partial(_post_kernel, n_ctx_tiles=n_ctx_tiles, hidden=hidden),
        out_shape=jax.ShapeDtypeStruct((b, t, d), F32),
        grid=(b, t // tm),
        in_specs=[_tok_spec(tm, d), _tok_spec(tm, half), _tok_spec(tm, half), lat, ctx,
                  _const_spec((1, d)), _const_spec((2 * half, d)), _const_spec((d, 2 * hidden)),
                  _const_spec((hidden, d))],
        out_specs=_tok_spec(tm, d),
        compiler_params=_params("arbitrary", "arbitrary"),
        name="out_proj_ffn",
    )(h, a, bb, mods, mods, g.reshape(1, d), wo, wgu, wd)


def _post_final(h, a_lat, b_lat, mods, g, wo, wgu, wd, final_g, tm, n_ctx_tiles):
    b, t, d = h.shape
    n_lat = a_lat.shape[1]
    half = a_lat.shape[-1]
    hidden = wd.shape[0]
    lat, _ = _mod_specs(d, b)
    return pl.pallas_call(
        functools.partial(_post_final_kernel, hidden=hidden),
        out_shape=jax.ShapeDtypeStruct((b, n_lat, d), F32),
        grid=(b, n_lat // tm),
        in_specs=[pl.BlockSpec((1, tm, d), lambda bi, ti: (bi, ti + n_ctx_tiles, 0)),
                  _tok_spec(tm, half), _tok_spec(tm, half), lat,
                  _const_spec((1, d)), _const_spec((2 * half, d)), _const_spec((d, 2 * hidden)),
                  _const_spec((hidden, d)), _const_spec((1, d))],
        out_specs=_tok_spec(tm, d),
        compiler_params=_params("arbitrary", "arbitrary"),
        name="out_proj_ffn_final",
    )(h, a_lat, b_lat, mods, g.reshape(1, d), wo, wgu, wd, final_g.reshape(1, d))


def _hgrn_tables(tile):
    c = HG_CHUNK
    r = np.arange(c)
    j = r[None, :]
    blocks = [(j <= r[:, None])]
    masks, bms = [], []
    m = c
    while m >= 2:
        half = m // 2
        p = r % m
        mid = (r - p + half)[:, None]
        upper = (p >= half)[:, None]
        up = (j >= mid) & (j <= r[:, None]) & upper
        lo = (j > r[:, None]) & (j <= mid - 1) & (~upper)
        if m > 2:
            blocks.append(up | lo)
        masks.append(np.broadcast_to(upper, (c, LANES)))
        bms.append(((r[:, None] // m) == (r[None, :] // m)) & upper & (~upper.T))
        m //= 2
    fwd = np.stack(blocks).astype(np.float32)
    bwd = fwd[:, ::-1, ::-1]
    mall = np.stack([fwd.reshape(-1, c), bwd.reshape(-1, c)])
    mall = np.concatenate([mall, mall], axis=2)
    mu = np.stack(masks).astype(np.float32)
    masku = np.tile(np.stack([mu, mu[:, ::-1]]), (1, 1, tile // c, 1))
    bmp = np.stack([np.concatenate([bms[i], bms[i + 1]], axis=1) for i in range(0, len(bms), 2)])
    bmp_bwd = np.stack([np.concatenate([bms[i][::-1, ::-1], bms[i + 1][::-1, ::-1]], axis=1)
                        for i in range(0, len(bms), 2)])
    return (jnp.asarray(mall, dtype=BF16), jnp.asarray(masku, dtype=F32),
            jnp.asarray(np.stack([bmp, bmp_bwd]), dtype=F32))


def _hgrn_kernel(q_ref, ff_ref, fb_ref, i_ref, g_ref, lbp_ref, ng_ref, mall_ref, masku_ref, bmp_ref,
                 o_ref, of_ref, ob_ref, *, layer, n_ctx_tiles, n_tiles):
    c = HG_CHUNK
    dk = HG_HEAD_DIM
    n_levels = masku_ref.shape[1]
    tile = masku_ref.shape[2]
    n_sub = tile // c
    lbp = lbp_ref[...]
    ex = jnp.exp(lbp - jnp.max(lbp, axis=0, keepdims=True))
    p = ex / jnp.sum(ex, axis=0, keepdims=True)
    lb = jnp.sum(p[:layer + 1], axis=0) - p[0]

    def sub(x, g):
        return x[g * c:(g + 1) * c]

    dirs = (0, 1)

    def stack_chunks(x):
        return jnp.concatenate([sub(x, g) for g in range(n_sub)], axis=1)

    def bwd_tile(t):
        if isinstance(t, int):
            return n_ctx_tiles - 1 - t if t < n_ctx_tiles else n_tiles - 1 - (t - n_ctx_tiles)
        return jnp.where(t < n_ctx_tiles, n_ctx_tiles - 1 - t, n_tiles - 1 - (t - n_ctx_tiles))

    def tile_rows(ti):
        start = ti * tile
        return pl.ds(start if isinstance(ti, int) else pl.multiple_of(start, tile), tile)

    def run_trips(trips, sts):
        lanes = [(s, d) for s in range(len(trips)) for d in dirs]
        n = range(len(lanes))
        tis = [bwd_tile(t) if d else t for t in trips for d in dirs]
        rows = [tile_rows(ti) for ti in tis]
        q = [q_ref[0, r, :] for r in rows]
        v = [i_ref[0, r, :] for r in rows]
        lbd = [lb[d:d + 1] for _, d in lanes]
        zf = [(ff_ref, fb_ref)[d][0, rows[l], :] for l, (_, d) in enumerate(lanes)]
        f = [lbd[l] + (1.0 - lbd[l]) * jax.nn.sigmoid(zf[l]) for l in n]
        lf = [jnp.log(x) for x in f]
        k = [1.0 - x for x in f]
        hi = [x.astype(BF16) for x in lf]
        lo = [(lf[l] - hi[l].astype(F32)).astype(BF16) for l in n]
        e2 = [_dot(mall_ref[lanes[l][1]], jnp.concatenate([stack_chunks(hi[l]), stack_chunks(lo[l])], axis=0))
              for l in n]
        ee = [[jnp.exp(e2[l][:, g * dk:(g + 1) * dk]) for g in range(n_sub)] for l in n]

        def block(l, idx):
            return jnp.concatenate([ee[l][g][idx * c:(idx + 1) * c] for g in range(n_sub)], axis=0)

        def tail_decay(l, g):
            cum = e2[l][0:c, g * dk:(g + 1) * dk]
            last = c - 1 if lanes[l][1] == 0 else 0
            return jnp.exp(cum[last:last + 1] - cum)

        g_in = [block(l, 0) for l in n]
        qin = [(q[l] * g_in[l]).astype(BF16) for l in n]
        kout = [(k[l] * jnp.concatenate([tail_decay(l, g) for g in range(n_sub)], axis=0)).astype(BF16)
                for l in n]
        vb = [x.astype(BF16) for x in v]
        zero = jnp.zeros((c, dk), BF16)
        ds_all = [_dot_tn(vb[l], jnp.concatenate(
            [jnp.concatenate([sub(kout[l], g) if gg == g else zero for gg in range(n_sub)], axis=1)
             for g in range(n_sub)], axis=0)) for l in n]
        ds = [[ds_all[l][:, g * dk:(g + 1) * dk] for g in range(n_sub)] for l in n]
        qf = [q[l] * f[l] for l in n]
        w = [[None] * n_levels for _ in n]
        for lvl in range(n_levels):
            for l, (_, d) in enumerate(lanes):
                later = masku_ref[d, lvl] != 0.0
                if lvl == n_levels - 1:
                    w[l][lvl] = jnp.where(later, qf[l], k[l]).astype(BF16)
                else:
                    w[l][lvl] = (jnp.where(later, q[l], k[l]) * block(l, 1 + lvl)).astype(BF16)
        scores = [[[None] * (n_levels // 2) for _ in range(n_sub)] for _ in n]
        for pair in range(n_levels // 2):
            lv_a, lv_b = 2 * pair, 2 * pair + 1
            for g in range(n_sub):
                for l, (_, d) in enumerate(lanes):
                    wa, wb = sub(w[l][lv_a], g), sub(w[l][lv_b], g)
                    rhs = jnp.concatenate([jnp.concatenate([wa, zero], axis=1),
                                           jnp.concatenate([zero, wb], axis=1)], axis=0)
                    raw = _dot_nt(jnp.concatenate([wa, wb], axis=1), rhs)
                    scores[l][g][pair] = (raw * bmp_ref[d, pair]).astype(BF16)
        o = [jnp.concatenate([_dot(jnp.concatenate(scores[l][g], axis=1),
                                   jnp.concatenate([sub(vb[l], g)] * n_levels, axis=0))
                              for g in range(n_sub)], axis=0)
             + jnp.sum(q[l] * k[l], axis=-1, keepdims=True) * v[l] for l in n]
        order = [list(range(n_sub)), list(reversed(range(n_sub)))]
        entering = [[None] * n_sub for _ in n]
        sts = list(sts)
        for l, (_, d) in enumerate(lanes):
            for g in order[d]:
                entering[l][g] = sts[d].astype(BF16)
                last = (g + 1) * c - 1 if d == 0 else g * c
                sts[d] = sts[d] * g_in[l][last:last + 1] + ds[l][g]
        for l, (_, d) in enumerate(lanes):
            out = jnp.concatenate([sub(o[l], g) + _dot_nt(sub(qin[l], g), entering[l][g])
                                   for g in range(n_sub)], axis=0)
            (of_ref, ob_ref)[d][rows[l], :] = out
        return tuple(sts)

    def body(i, carry):
        return run_trips([i * HGRN_LOCKSTEP + s for s in range(HGRN_LOCKSTEP)], carry)

    s0 = jnp.zeros((dk, dk), F32)
    sts = lax.fori_loop(0, n_tiles // HGRN_LOCKSTEP, body, (s0, s0), unroll=HGRN_UNROLL)
    rest = list(range(n_tiles - n_tiles % HGRN_LOCKSTEP, n_tiles))
    if rest:
        run_trips(rest, sts)
    o = _rms(of_ref[...] + ob_ref[...]) * ng_ref[...]
    o_ref[0] = (o * _silu(g_ref[0])).astype(o_ref.dtype)


def _hgrn(z, hg_lower_bound, ng, layer, n_heads, ctx_len):
    b, t, _ = z.shape
    dk = HG_HEAD_DIM
    n_even = hg_lower_bound.shape[0]
    tile = math.gcd(math.gcd(ctx_len, t - ctx_len), 256)
    mall, masku, bm = _hgrn_tables(tile)

    def col(kind):
        return pl.BlockSpec((1, t, dk), lambda bi, hi: (bi, 0, kind * n_heads + hi))

    def const(arr):
        zeros = (0,) * arr.ndim
        return pl.BlockSpec(arr.shape, lambda bi, hi: zeros)

    return pl.pallas_call(
        functools.partial(_hgrn_kernel, layer=layer, n_ctx_tiles=ctx_len // tile, n_tiles=t // tile),
        out_shape=jax.ShapeDtypeStruct((b, t, n_heads * dk), BF16),
        grid=(b, n_heads),
        in_specs=[col(0), col(1), col(2), col(3), col(4),
                  pl.BlockSpec((n_even, 2, dk), lambda bi, hi: (0, 0, hi)),
                  const(ng.reshape(1, dk)), const(mall), const(masku), const(bm)],
        out_specs=pl.BlockSpec((1, t, dk), lambda bi, hi: (bi, 0, hi)),
        scratch_shapes=[pltpu.VMEM((t, dk), F32), pltpu.VMEM((t, dk), F32)],
        compiler_params=_params("arbitrary", "arbitrary"),
        name="hgrn2_scan",
    )(z, z, z, z, z, hg_lower_bound, ng.reshape(1, dk), mall, masku, bm)


def _hy_pre_kernel(x0_ref, x1_ref, v_ref, w0_ref, w1_ref, wv_ref, b0_ref, b1_ref, bv_ref,
                   x0c_ref, u_ref, *, ctx_len):
    t = x0_ref.shape[1]
    row = lax.broadcasted_iota(jnp.int32, (t, LANES), 0)
    first = (row == 0) | (row == ctx_len)
    final = (row == ctx_len - 1) | (row == t - 1)

    def short_conv(z_ref, w_ref, b_ref):
        z = z_ref[0]
        prev = jnp.where(first, 0.0, pltpu.roll(z, 1, 0))
        nxt = jnp.where(final, 0.0, pltpu.roll(z, t - 1, 0))
        w = w_ref[...]
        return prev * w[0:1] + z * w[1:2] + nxt * w[2:3] + b_ref[...]

    x0c_ref[0] = short_conv(x0_ref, w0_ref, b0_ref).astype(x0c_ref.dtype)
    u = short_conv(x1_ref, w1_ref, b1_ref) * short_conv(v_ref, wv_ref, bv_ref)
    u_ref[0] = u.astype(u_ref.dtype)


def _hy_pre(z, short_w, short_b, col0, width, ctx_len):
    b, t, _ = z.shape
    nb = width // LANES
    c0 = col0 // LANES

    def zcol(kind):
        return pl.BlockSpec((1, t, LANES), lambda bi, j: (bi, 0, c0 + kind * nb + j))

    def wcol(kind, rows):
        return pl.BlockSpec((rows, LANES), lambda bi, j: (0, kind * nb + j))

    out = jax.ShapeDtypeStruct((b, t, width), BF16)
    ospec = pl.BlockSpec((1, t, LANES), lambda bi, j: (bi, 0, j))
    sb = short_b.reshape(1, -1)
    return pl.pallas_call(
        functools.partial(_hy_pre_kernel, ctx_len=ctx_len),
        out_shape=(out, out),
        grid=(b, nb),
        in_specs=[zcol(0), zcol(1), zcol(2), wcol(0, 3), wcol(1, 3), wcol(2, 3),
                  wcol(0, 1), wcol(1, 1), wcol(2, 1)],
        out_specs=(ospec, ospec),
        compiler_params=_params("arbitrary", "arbitrary"),
        name="hyena_short_conv",
    )(z, z, z, short_w, short_w, short_w, sb, sb, sb)


def _filter_features(length, n):
    p = np.arange(n)
    is_f = p < length
    is_b = p > n - length
    lag = np.where(is_f, p, np.where(is_b, n - 1 - p, 0))
    tt = np.linspace(0.0, 1.0, length, dtype=np.float32)[lag][:, None]
    w = (2.0 * math.pi * lag.astype(np.float32) / length)[:, None].astype(np.float32)
    bands = np.linspace(1e-4, HY_BANDS - 1, HY_BANDS, dtype=np.float32)[None, :]
    feat = np.concatenate([tt, np.cos(bands * w), -np.sin(bands * w), is_f[:, None], is_b[:, None]],
                          axis=-1).astype(np.float32)
    pad = (-feat.shape[1]) % BF16_SUBLANES
    return jnp.asarray(np.pad(feat, ((0, 0), (0, pad))))


def _filt_kernel(feat_ref, w1_ref, b1_ref, fr1_ref, w2_ref, b2_ref, fr2_ref,
                 w3f_ref, w3b_ref, dl_ref, o_ref, hid_ref):
    z = feat_ref[...]
    mf = z[:, HY_EMB:HY_EMB + 1]
    mb = z[:, HY_EMB + 1:HY_EMB + 2]

    @pl.when(pl.program_id(0) == 0)
    def _():
        h1 = jnp.sin(fr1_ref[...] * (_dot3(z, w1_ref[...]) + b1_ref[...]))
        hid_ref[...] = jnp.sin(fr2_ref[...] * (_dot3(h1, w2_ref[...]) + b2_ref[...]))

    hid = hid_ref[...]
    hf = _dot3(hid, w3f_ref[...])
    hb = _dot3(hid, w3b_ref[...])
    win = jnp.exp(-z[:, 0:1] * dl_ref[...])
    f = (mf * hf + mb * hb) * win
    o_ref[...] = f / jnp.sum(jnp.abs(f), axis=0, keepdims=True)


def _hyena_filter(length, n, w1, b1, fr1, w2, b2, fr2, w3, width):
    feat = _filter_features(length, n)
    nf = feat.shape[1]
    hid = w1.shape[1]
    w1p = jnp.pad(w1, ((0, nf - w1.shape[0]), (0, 0)))
    d_lo = -math.log(HY_TARGET) / HY_GENTLE_PCT
    d_hi = -math.log(HY_TARGET) / HY_STEEP_PCT
    deltas = jnp.asarray(np.linspace(d_lo, d_hi, width, dtype=np.float32)[None, :])
    nb = width // LANES

    def full(shape):
        zeros = (0,) * len(shape)
        return pl.BlockSpec(shape, lambda j: zeros)

    return pl.pallas_call(
        _filt_kernel,
        out_shape=jax.ShapeDtypeStruct((n, width), F32),
        grid=(nb,),
        in_specs=[full((n, nf)), full((nf, hid)), full((1, hid)),
                  full((1, hid)), full((hid, hid)), full((1, hid)), full((1, hid)),
                  pl.BlockSpec((hid, LANES), lambda j: (0, j)),
                  pl.BlockSpec((hid, LANES), lambda j: (0, nb + j)),
                  pl.BlockSpec((1, LANES), lambda j: (0, j))],
        out_specs=pl.BlockSpec((n, LANES), lambda j: (0, j)),
        scratch_shapes=[pltpu.VMEM((n, hid), F32)],
        compiler_params=_params("arbitrary"),
        name="hyena_filter_mlp",
    )(feat, w1p, b1.reshape(1, -1), fr1.reshape(1, -1), w2, b2.reshape(1, -1),
      fr2.reshape(1, -1), w3, w3, deltas)


def _dft_tables(n1, n1_in):
    n2 = FFT_N2
    n = n1 * n2
    a = np.arange(n1)
    j = np.arange(n2)
    ang = -2.0 * np.pi * (a[None, None, :] * a[None, :, None] / n1 + j[:, None, None] * a[None, :, None] / n)
    tr, ti = np.cos(ang), np.sin(ang)
    fwd_a = np.concatenate([np.concatenate([tr, -ti], 2), np.concatenate([ti, tr], 2)], 1)
    trt, tit = np.swapaxes(tr, 1, 2), -np.swapaxes(ti, 1, 2)
    inv_a = np.concatenate([np.concatenate([trt, -tit], 2), np.concatenate([tit, trt], 2)], 1)
    keep = np.concatenate([np.arange(n1_in), n1 + np.arange(n1_in)])
    ang2 = -2.0 * np.pi * (j[:, None] * j[None, :]) / n2
    cr, ci = np.cos(ang2), np.sin(ang2)
    fwd_c = np.block([[cr, -ci], [ci, cr]])
    inv_c = np.block([[cr, ci], [-ci, cr]])
    real_a = np.concatenate([tr, ti], 1)
    return dict(fwd_a=fwd_a[:, :, keep], inv_a=inv_a[:, keep, :], fwd_c=fwd_c, inv_c=inv_c,
                real_a=real_a)


def _slab(idx):
    return pl.ds(pl.multiple_of(idx * FFT_PITCH, SUBLANES), FFT_N2)


def _fft_filter_kernel(f_ref, wa_ref, wc_ref, o_ref, scr_ref, *, n1):
    n2 = FFT_N2
    rows = 2 * n1
    scale = 1.0 / (n1 * n2)

    def split3(x):
        hi = x.astype(BF16)
        lo = (x - hi.astype(F32)).astype(BF16)
        return jnp.concatenate([hi, lo, hi], axis=0)

    def stage_a(j, carry):
        scr_ref[pl.ds(j, rows, stride=FFT_PITCH), :] = _dot(wa_ref[j], split3(f_ref[j]))
        return carry

    lax.fori_loop(0, n2, stage_a, 0, unroll=FFT_UNROLL)

    def stage_c(k, carry):
        x = jnp.concatenate([scr_ref[_slab(k), :], scr_ref[_slab(n1 + k), :]], axis=0)
        o_ref[k] = _dot(wc_ref[...], split3(x)) * scale
        return carry

    lax.fori_loop(0, n1, stage_c, 0, unroll=FFT_UNROLL)


def _split3_cols(w):
    hi = w.astype(np.float32).astype(jnp.bfloat16)
    lo = (w.astype(np.float32) - np.asarray(hi, np.float32)).astype(jnp.bfloat16)
    return jnp.asarray(np.concatenate([hi, hi, lo], axis=-1))


def _fft_filter(filt, n1, tables):
    n2 = FFT_N2
    width = filt.shape[1]
    ft = filt.reshape(n1, n2, width).transpose(1, 0, 2)
    wa = _split3_cols(tables["real_a"])
    wc = _split3_cols(tables["fwd_c"])
    return pl.pallas_call(
        functools.partial(_fft_filter_kernel, n1=n1),
        out_shape=jax.ShapeDtypeStruct((n1, 2 * n2, width), F32),
        grid=(width // LANES,),
        in_specs=[pl.BlockSpec((n2, n1, LANES), lambda c: (0, 0, c)),
                  pl.BlockSpec(wa.shape, lambda c: (0, 0, 0)),
                  pl.BlockSpec(wc.shape, lambda c: (0, 0))],
        out_specs=pl.BlockSpec((n1, 2 * n2, LANES), lambda c: (0, 0, c)),
        scratch_shapes=[pltpu.VMEM((2 * n1 * FFT_PITCH, LANES), F32)],
        compiler_params=_params("arbitrary"),
        name="hyena_filter_dft",
    )(ft, wa, wc)


def _fftconv_kernel(u_ref, x0_ref, fh_ref, skip_ref, wfa_ref, wfc_ref, wic_ref, wia_ref, o_ref,
                    scr_ref, *, n1):
    n2 = FFT_N2
    rows = 2 * n1

    def stage_a(j, carry):
        scr_ref[pl.ds(j, rows, stride=FFT_PITCH), :] = _dot(wfa_ref[j], u_ref[0, j])
        return carry

    lax.fori_loop(0, n2, stage_a, 0, unroll=FFT_UNROLL)

    def stage_c(k, carry):
        x = jnp.concatenate([scr_ref[_slab(k), :], scr_ref[_slab(n1 + k), :]], axis=0)
        xf = _dot(wfc_ref[...], x.astype(BF16))
        fh = fh_ref[k]
        xr, xi, fr, fi = xf[:n2], xf[n2:], fh[:n2], fh[n2:]
        y = jnp.concatenate([xr * fr - xi * fi, xr * fi + xi * fr], axis=0)
        zt = _dot(wic_ref[...], y.astype(BF16))
        scr_ref[_slab(k), :] = zt[:n2]
        scr_ref[_slab(n1 + k), :] = zt[n2:]
        return carry

    lax.fori_loop(0, n1, stage_c, 0, unroll=FFT_UNROLL)

    def stage_ai(j, carry):
        x = scr_ref[pl.ds(j, rows, stride=FFT_PITCH), :]
        y = _dot(wia_ref[j], x.astype(BF16))
        u = u_ref[0, j].astype(F32)
        o_ref[0, j] = (x0_ref[0, j].astype(F32) * (y + skip_ref[...] * u)).astype(o_ref.dtype)
        return carry

    lax.fori_loop(0, n2, stage_ai, 0, unroll=FFT_UNROLL)


def _to_fft_layout(x, n1_in):
    b, length, c = x.shape
    n1_used = length // FFT_N2
    x = x.reshape(b // 2, 2, n1_used, FFT_N2, c)
    x = jnp.pad(x, ((0, 0), (0, 0), (0, n1_in - n1_used), (0, 0), (0, 0)))
    return x.transpose(0, 3, 1, 2, 4).reshape(b // 2, FFT_N2, 2 * n1_in, c)


def _from_fft_layout(y, length):
    p, n2, rows, c = y.shape
    n1_in = rows // 2
    y = y.reshape(p, n2, 2, n1_in, c).transpose(0, 2, 3, 1, 4)
    return y.reshape(2 * p, n1_in * n2, c)[:, :length]


def _fftconv(u, x0c, fh, skip, n1, n1_in, tables):
    _, length, width = u.shape
    ut = _to_fft_layout(u, n1_in)
    xt = _to_fft_layout(x0c, n1_in)
    pairs, n2, rin, _ = ut.shape
    wfa = jnp.asarray(tables["fwd_a"], dtype=BF16)
    wia = jnp.asarray(tables["inv_a"], dtype=BF16)
    wfc = jnp.asarray(tables["fwd_c"], dtype=BF16)
    wic = jnp.asarray(tables["inv_c"], dtype=BF16)
    data = pl.BlockSpec((1, n2, rin, LANES), lambda c, p: (p, 0, 0, c))

    def const(arr):
        zeros = (0,) * arr.ndim
        return pl.BlockSpec(arr.shape, lambda c, p: zeros, pipeline_mode=pl.Buffered(1))

    out = pl.pallas_call(
        functools.partial(_fftconv_kernel, n1=n1),
        out_shape=jax.ShapeDtypeStruct(ut.shape, BF16),
        grid=(width // LANES, pairs),
        in_specs=[data, data,
                  pl.BlockSpec((n1, 2 * n2, LANES), lambda c, p: (0, 0, c)),
                  pl.BlockSpec((1, LANES), lambda c, p: (0, c)),
                  const(wfa), const(wfc), const(wic), const(wia)],
        out_specs=data,
        scratch_shapes=[pltpu.VMEM((2 * n1 * FFT_PITCH, LANES), F32)],
        compiler_params=_params("arbitrary", "arbitrary"),
        name="hyena_dft_conv",
    )(ut, xt, fh, skip.reshape(1, width), wfa, wfc, wic, wia)
    return _from_fft_layout(out, length)


def _hyena(z, col0, width, ctx_len, short_w, short_b, filt_params, skip):
    x0c, u = _hy_pre(z, short_w, short_b, col0, width, ctx_len)
    outs = []
    for lo, hi in ((0, ctx_len), (ctx_len, z.shape[1])):
        length = hi - lo
        n1_in = max(length // FFT_N2, SUBLANES)
        n1 = 2 * n1_in
        tables = _dft_tables(n1, n1_in)
        filt = _hyena_filter(length, n1 * FFT_N2, *filt_params, width)
        fh = _fft_filter(filt, n1, tables)
        outs.append(_fftconv(u[:, lo:hi], x0c[:, lo:hi], fh, skip, n1, n1_in, tables))
    return jnp.concatenate(outs, axis=1)


def _rope_tables(n_lat, ctx_len):
    tok = np.arange(n_lat)
    row, colp = tok // GRID_W, tok % GRID_W

    def axial(half):
        inv = ROPE_BASE ** (-np.arange(half, dtype=np.float32) / half)
        parts_c, parts_s = [], []
        for pos in (row, colp):
            ang = pos.astype(np.float32)[:, None] * inv
            parts_c += [np.cos(ang), np.cos(ang)]
            parts_s += [-np.sin(ang), np.sin(ang)]
        return np.concatenate(parts_c, 1), np.concatenate(parts_s, 1)

    dc, ds = axial(DA_HEAD_DIM // 4)
    mc, ms = axial(MLA_ROPE // 4)
    ones, zeros = np.ones((n_lat, MLA_NOPE), np.float32), np.zeros((n_lat, MLA_NOPE), np.float32)
    padc = np.ones((n_lat, LANES - MLA_NOPE - MLA_ROPE), np.float32)
    tabs = [np.concatenate([dc, dc], 1), np.concatenate([ds, ds], 1),
            np.concatenate([ones, mc, padc], 1), np.concatenate([zeros, ms, 0 * padc], 1)]
    out = []
    for i, tb in enumerate(tabs):
        ctx_rows = np.ones((ctx_len, LANES), np.float32) if i % 2 == 0 else np.zeros((ctx_len, LANES), np.float32)
        out.append(jnp.asarray(np.concatenate([ctx_rows, tb.astype(np.float32)], 0)))
    return out


def _with_ones(v):
    ones = jnp.ones((v.shape[0], LANES), v.dtype)
    parts = []
    for hd in range(v.shape[1] // LANES):
        parts += [v[:, hd * LANES:(hd + 1) * LANES], ones]
    return jnp.concatenate(parts, axis=1)


def _in_odd_kernel(x_ref, mod_ref, modc_ref, g_ref, w_ref, dc_ref, ds_ref, mc_ref, ms_ref,
                   qg_ref, kvg_ref, wuq_ref, wukv_ref,
                   qd_ref, kd_ref, vd_ref, qm_ref, km_ref, vm_ref, *, n_ctx_tiles, da_w, q_rank, kv_rank):
    m = _tile_mod(mod_ref, modc_ref, n_ctx_tiles)
    xn = (_rms(x_ref[0]) * g_ref[...]) * (1.0 + m[1:2]) + m[0:1]
    z = _dot(xn.astype(BF16), w_ref[...])
    nrep = da_w // LANES
    dc = jnp.concatenate([dc_ref[...]] * nrep, axis=1)
    ds = jnp.concatenate([ds_ref[...]] * nrep, axis=1)
    mc = jnp.concatenate([mc_ref[...]] * MLA_HEADS, axis=1)
    ms = jnp.concatenate([ms_ref[...]] * MLA_HEADS, axis=1)
    sa = DA_HEAD_DIM ** -0.5 * LOG2_E
    sm = (MLA_NOPE + MLA_ROPE) ** -0.5 * LOG2_E
    lane = lax.broadcasted_iota(jnp.int32, (x_ref.shape[1], LANES), 1)

    def partner(x, half):
        first = (lane % (2 * half)) < half
        cols = []
        for j in range(x.shape[1] // LANES):
            xj = x[:, j * LANES:(j + 1) * LANES]
            cols.append(jnp.where(first, pltpu.roll(xj, LANES - half, 1), pltpu.roll(xj, half, 1)))
        return cols[0] if len(cols) == 1 else jnp.concatenate(cols, axis=1)

    o = 0
    zq = z[:, o:o + da_w]
    qd_ref[0] = ((zq * dc + partner(zq, DA_HEAD_DIM // 4) * ds) * sa).astype(BF16)
    o += da_w
    zk = z[:, o:o + da_w]
    kd_ref[0] = (zk * dc + partner(zk, DA_HEAD_DIM // 4) * ds).astype(BF16)
    o += da_w
    vd_ref[0] = _with_ones(z[:, o:o + da_w]).astype(BF16)
    o += da_w
    cq = _rms(z[:, o:o + q_rank]) * qg_ref[...]
    o += q_rank
    ckv = _rms(z[:, o:o + kv_rank]) * kvg_ref[...]
    o += kv_rank
    zr = z[:, o:o + LANES]
    kr = zr * mc_ref[...] + partner(zr, MLA_ROPE // 4) * ms_ref[...]
    mw = MLA_HEADS * LANES
    qu = _dot(cq.astype(BF16), wuq_ref[...])
    qm_ref[0] = ((qu * mc + partner(qu, MLA_ROPE // 4) * ms) * sm).astype(BF16)
    kvu = _dot(ckv.astype(BF16), wukv_ref[...])
    km_ref[0] = (kvu[:, :mw] + jnp.concatenate([kr] * MLA_HEADS, axis=1)).astype(BF16)
    vm_ref[0] = _with_ones(kvu[:, mw:]).astype(BF16)


def _odd_weights(w_in, w_uq, w_ukv, da_w, q_rank, kv_rank):
    o = 3 * da_w + q_rank + kv_rank
    krw = jnp.pad(w_in[:, o:], ((0, 0), (MLA_NOPE, LANES - MLA_NOPE - MLA_ROPE)))
    w_big = jnp.concatenate([w_in[:, :o], krw], axis=1)
    dq = MLA_NOPE + MLA_ROPE
    uq = jnp.pad(w_uq.reshape(q_rank, MLA_HEADS, dq), ((0, 0), (0, 0), (0, LANES - dq)))
    ukv = w_ukv.reshape(kv_rank, MLA_HEADS, MLA_NOPE + MLA_V)
    uk = jnp.pad(ukv[:, :, :MLA_NOPE], ((0, 0), (0, 0), (0, LANES - MLA_NOPE))).reshape(kv_rank, MLA_HEADS * LANES)
    uv = ukv[:, :, MLA_NOPE:].reshape(kv_rank, MLA_HEADS * MLA_V)
    return (w_big.astype(BF16), uq.reshape(q_rank, MLA_HEADS * LANES).astype(BF16),
            jnp.concatenate([uk, uv], axis=1).astype(BF16))


def _in_odd(h, mods, g, w_big, wuq, wukv, qg, kvg, tabs, tm, n_ctx_tiles, da_w):
    b, t, d = h.shape
    q_rank, kv_rank = qg.shape[0], kvg.shape[0]
    lat, ctx = _mod_specs(d, b)
    tab = pl.BlockSpec((tm, LANES), lambda bi, ti: (ti, 0))
    mw = MLA_HEADS * LANES

    def tok_major(width):
        return jax.ShapeDtypeStruct((b, t, width), BF16), _tok_spec(tm, width)

    outs, ospecs = zip(tok_major(da_w), tok_major(da_w), tok_major(2 * da_w),
                       tok_major(mw), tok_major(mw), tok_major(2 * mw))
    return pl.pallas_call(
        functools.partial(_in_odd_kernel, n_ctx_tiles=n_ctx_tiles, da_w=da_w, q_rank=q_rank, kv_rank=kv_rank),
        out_shape=tuple(outs),
        grid=(b, t // tm),
        in_specs=[_tok_spec(tm, d), lat, ctx, _const_spec((1, d)), _const_spec(w_big.shape),
                  tab, tab, tab, tab, _const_spec((1, q_rank)), _const_spec((1, kv_rank)),
                  _const_spec(wuq.shape), _const_spec(wukv.shape)],
        out_specs=tuple(ospecs),
        compiler_params=_params("arbitrary", "arbitrary"),
        name="odd_in_proj",
    )(h, mods, mods, g.reshape(1, d), w_big, *tabs, qg.reshape(1, -1), kvg.reshape(1, -1), wuq, wukv)


def _softmax_pv(qs, k_ref, v_ref):
    n_keys = k_ref.shape[1]
    starts = list(range(0, n_keys, ATTN_KEY_BLOCK))
    m = [None] * len(qs)
    acc = [None] * len(qs)
    for start in starts:
        size = min(ATTN_KEY_BLOCK, n_keys - start)
        k = k_ref[0, start:start + size, :]
        v1 = v_ref[0, start:start + size, :]
        for i, q in enumerate(qs):
            s = _dot_nt(q, k)
            m_blk = jnp.max(s, axis=-1, keepdims=True)
            if start == 0:
                m[i] = m_blk
                acc[i] = _dot(jnp.exp2(s - m_blk).astype(BF16), v1)
            else:
                m_new = jnp.maximum(m[i], m_blk)
                pv = _dot(jnp.exp2(s - m_new).astype(BF16), v1)
                acc[i] = jnp.exp2(m[i] - m_new) * acc[i] + pv
                m[i] = m_new
    return [a[:, :LANES] / a[:, LANES:] for a in acc]


def _attn_kernel(q_ref, k_ref, v_ref, lam_ref, sg_ref, *rest, diff, lam_init):
    o_ref = rest[-1]
    q = q_ref[0]
    if not diff:
        o_ref[0] = _softmax_pv([q], k_ref, v_ref)[0].astype(o_ref.dtype)
        return
    first = lax.broadcasted_iota(jnp.int32, q.shape, 1) < DA_HEAD_DIM
    zero = jnp.zeros_like(q)
    o1, o2 = _softmax_pv([jnp.where(first, q, zero), jnp.where(first, zero, q)], k_ref, v_ref)
    lp = lam_ref[...]
    lam = (jnp.exp(jnp.sum(lp[0:1] * lp[1:2], axis=-1, keepdims=True))
           - jnp.exp(jnp.sum(lp[2:3] * lp[3:4], axis=-1, keepdims=True)) + lam_init)
    o_ref[0] = (_rms(o1 - lam * o2) * sg_ref[...] * (1.0 - lam_init)).astype(o_ref.dtype)


def _attention(q, k, v1, lam_p, subln_g, q_rows, tq, n_keys, out_rows, out_start, diff, lam_init, earlier=None):
    b, _, width = q.shape
    heads = width // LANES
    q_start, q_stop = q_rows

    def rows_from(start):
        return pl.BlockSpec((pl.Element(1), pl.Element(tq), pl.Element(LANES)),
                            lambda bi, hi, ti: (bi, pl.multiple_of(start + ti * tq, math.gcd(start, tq)),
                                                pl.multiple_of(hi * LANES, LANES)))

    khead = pl.BlockSpec((1, n_keys, LANES), lambda bi, hi, ti: (bi, 0, hi))
    vhead = pl.BlockSpec((1, n_keys, 2 * LANES), lambda bi, hi, ti: (bi, 0, hi))

    def const(shape):
        zeros = (0,) * len(shape)
        return pl.BlockSpec(shape, lambda bi, hi, ti: zeros)

    in_specs = [rows_from(q_start), khead, vhead, const(lam_p.shape), const((1, LANES))]
    args = [q, k, v1, lam_p, subln_g.reshape(1, LANES)]
    aliases = {}
    if earlier is not None:
        in_specs.append(pl.BlockSpec(memory_space=pl.ANY))
        args.append(earlier)
        aliases = {len(args) - 1: 0}
    return pl.pallas_call(
        functools.partial(_attn_kernel, diff=diff, lam_init=lam_init),
        out_shape=jax.ShapeDtypeStruct((b, out_rows, width), BF16),
        grid=(b, heads, (q_stop - q_start) // tq),
        in_specs=in_specs,
        out_specs=rows_from(out_start),
        input_output_aliases=aliases,
        compiler_params=_params("arbitrary", "arbitrary", "arbitrary"),
        name="diff_attention" if diff else "mla_attention",
    )(*args)


def _attend_segments(q, k, v1, lam_p, subln_g, ctx_len, diff, lam_init, need_ctx):
    t = k.shape[1]
    n_lat = t - ctx_len
    tq_lat = math.gcd(n_lat, ATTN_QUERY_TILE)
    if not need_ctx:
        return _attention(q, k, v1, lam_p, subln_g, (ctx_len, t), tq_lat, t, n_lat, 0, diff, lam_init)
    o = _attention(q, k, v1, lam_p, subln_g, (ctx_len, t), tq_lat, t, t, ctx_len, diff, lam_init)
    return _attention(q, k, v1, lam_p, subln_g, (0, ctx_len), ctx_len, ctx_len, t, 0, diff, lam_init, earlier=o)


def kernel(x, c, ctx, c_ctx, ada_w, ada_b, norm_mix_g, norm_ffn_g, ffn_w_gu, ffn_w_down, ev_w_in, ev_w_out, hg_lower_bound, hg_out_norm_g, hy_short_w, hy_short_b, hy_filt_w1, hy_filt_b1, hy_filt_freq1, hy_filt_w2, hy_filt_b2, hy_filt_freq2, hy_filt_w3, hy_skip, od_w_in, od_w_out, da_lambda, da_subln_g, mla_q_norm_g, mla_w_uq, mla_kv_norm_g, mla_w_ukv, final_norm_g):
    n_batch, n_lat, d = x.shape
    ctx_len = ctx.shape[1]
    depth = ada_w.shape[0]
    assert n_batch % 2 == 0 and n_batch < MOD_ROWS
    assert n_lat % GRID_W == 0 and ctx_len % HG_CHUNK == 0 and n_lat % FFT_N2 == 0 and ctx_len % FFT_N2 == 0
    tm = math.gcd(math.gcd(ctx_len, n_lat), 256)
    n_ctx_tiles = ctx_len // tm
    hg_width = d // 2
    hy_width = d - hg_width
    da_w = d // 2
    q_rank, kv_rank = mla_q_norm_g.shape[1], mla_kv_norm_g.shape[1]

    cc = jnp.concatenate([c, c_ctx[None], jnp.zeros((MOD_ROWS - n_batch - 1, d), F32)], axis=0)
    mods = _ada(cc, ada_w, ada_b).reshape(depth, MOD_ROWS, 6, d)
    h = jnp.concatenate([ctx, x], axis=1)
    rope_tabs = _rope_tables(n_lat, ctx_len)

    for i in range(depth):
        last = i == depth - 1
        if i % 2 == 0:
            e = i // 2
            z = _in_even(h, mods[i], norm_mix_g[i], ev_w_in[e].astype(BF16), tm, n_ctx_tiles)
            a = _hgrn(z, hg_lower_bound, hg_out_norm_g[e], e, hg_width // HG_HEAD_DIM, ctx_len)
            filt_params = (hy_filt_w1[e], hy_filt_b1[e], hy_filt_freq1[e], hy_filt_w2[e], hy_filt_b2[e],
                           hy_filt_freq2[e], hy_filt_w3[e])
            bb = _hyena(z, 5 * hg_width, hy_width, ctx_len, hy_short_w[e], hy_short_b[e], filt_params, hy_skip[e])
            w_out = ev_w_out[e]
        else:
            o = i // 2
            lam_init = 0.8 - 0.6 * math.exp(-0.3 * i)
            w_big, wuq, wukv = _odd_weights(od_w_in[o], mla_w_uq[o], mla_w_ukv[o], da_w, q_rank, kv_rank)
            qd, kd, vd, qm, km, vm = _in_odd(h, mods[i], norm_mix_g[i], w_big, wuq, wukv, mla_q_norm_g[o],
                                             mla_kv_norm_g[o], rope_tabs, tm, n_ctx_tiles, da_w)
            a = _attend_segments(qd, kd, vd, da_lambda[o], da_subln_g[o], ctx_len, True, lam_init, not last)
            bb = _attend_segments(qm, km, vm, da_lambda[o], da_subln_g[o], ctx_len, False, lam_init, not last)
            w_out = od_w_out[o]
        weights = (w_out.astype(BF16), ffn_w_gu[i].astype(BF16), ffn_w_down[i].astype(BF16))
        if not last:
            h = _post(h, a, bb, mods[i], norm_ffn_g[i], *weights, tm, n_ctx_tiles)
    if a.shape[1] != n_lat:
        a, bb = a[:, ctx_len:], bb[:, ctx_len:]
    return _post_final(h, a, bb, mods[depth - 1], norm_ffn_g[depth - 1], *weights, final_norm_g, tm, n_ctx_tiles)
```

```python
import functools
import math

import numpy as np
import jax
import jax.numpy as jnp
from jax import lax
from jax.experimental import pallas as pl
from jax.experimental.pallas import tpu as pltpu

F32 = jnp.float32
BF16 = jnp.bfloat16
HIGHEST = lax.Precision.HIGHEST

GRID_W = 64
EPS = 1e-6
ROPE_BASE = 10000.0
HG_HEAD_DIM = 128
HG_CHUNK = 64
HY_BANDS = 16
HY_EMB = 2 * HY_BANDS + 1
HY_TARGET = 1e-2
HY_STEEP_PCT = 0.3
HY_GENTLE_PCT = 1.5
DA_HEAD_DIM = 64
MLA_HEADS = 4
MLA_NOPE = 64
MLA_ROPE = 32
MLA_V = 128
LOG2_E = 1.4426950408889634

LANES = 128
SUBLANES = 8
BF16_SUBLANES = 16
V7X_VMEM_LIMIT_BYTES = 56 * 1024 * 1024
MOD_ROWS = 16
FFT_N2 = 128
FFT_PITCH = FFT_N2 + SUBLANES
FFT_UNROLL = 16
HGRN_LOCKSTEP = 1
HGRN_UNROLL = 2
ATTN_KEY_BLOCK = 256
ATTN_QUERY_TILE = 1024


def _params(*sem):
    return pltpu.CompilerParams(dimension_semantics=sem, vmem_limit_bytes=V7X_VMEM_LIMIT_BYTES)


def _dot(a, b, precision=None):
    return jnp.dot(a, b, preferred_element_type=F32, precision=precision)


def _dot3(a, b):
    ah, bh = a.astype(BF16), b.astype(BF16)
    al, bl = (a - ah.astype(F32)).astype(BF16), (b - bh.astype(F32)).astype(BF16)
    return _dot(ah, bh) + _dot(al, bh) + _dot(ah, bl)


def _dot_nt(a, b):
    return lax.dot_general(a, b, (((1,), (1,)), ((), ())), preferred_element_type=F32)


def _dot_tn(a, b):
    return lax.dot_general(a, b, (((0,), (0,)), ((), ())), preferred_element_type=F32)


def _rms(x):
    return x * lax.rsqrt(jnp.mean(x * x, axis=-1, keepdims=True) + EPS)


def _silu(x):
    return x * jax.nn.sigmoid(x)


def _tile_mod(mod_ref, modc_ref, n_ctx_tiles):
    return jnp.where(pl.program_id(1) < n_ctx_tiles, modc_ref[...], mod_ref[...])


def _ada_kernel(c_ref, w_ref, b_ref, o_ref):
    o_ref[0] = _dot(_silu(c_ref[...]), w_ref[0], HIGHEST) + b_ref[0]


def _ada(cc, ada_w, ada_b):
    depth, d, n = ada_w.shape
    rows = cc.shape[0]
    tn = n // 4
    return pl.pallas_call(
        _ada_kernel,
        out_shape=jax.ShapeDtypeStruct((depth, rows, n), F32),
        grid=(depth, n // tn),
        in_specs=[
            pl.BlockSpec((rows, d), lambda i, j: (0, 0)),
            pl.BlockSpec((1, d, tn), lambda i, j: (i, 0, j)),
            pl.BlockSpec((1, 1, tn), lambda i, j: (i, 0, j)),
        ],
        out_specs=pl.BlockSpec((1, rows, tn), lambda i, j: (i, 0, j)),
        compiler_params=_params("arbitrary", "arbitrary"),
        name="ada_mod",
    )(cc, ada_w, ada_b.reshape(depth, 1, n))


def _tok_spec(tm, width):
    return pl.BlockSpec((1, tm, width), lambda b, t: (b, t, 0))


def _const_spec(shape):
    zeros = (0,) * len(shape)
    return pl.BlockSpec(shape, lambda b, t: zeros)


def _mod_specs(d, n_batch):
    lat = pl.BlockSpec((None, 6, d), lambda b, t: (b, 0, 0))
    ctx = pl.BlockSpec((None, 6, d), lambda b, t: (n_batch, 0, 0))
    return lat, ctx


def _in_even_kernel(x_ref, mod_ref, modc_ref, g_ref, w_ref, o_ref, *, n_ctx_tiles):
    m = _tile_mod(mod_ref, modc_ref, n_ctx_tiles)
    xn = (_rms(x_ref[0]) * g_ref[...]) * (1.0 + m[1:2]) + m[0:1]
    o_ref[0] = _dot(xn.astype(BF16), w_ref[...])


def _in_even(h, mods, g, w, tm, n_ctx_tiles):
    b, t, d = h.shape
    n = w.shape[1]
    lat, ctx = _mod_specs(d, b)
    return pl.pallas_call(
        functools.partial(_in_even_kernel, n_ctx_tiles=n_ctx_tiles),
        out_shape=jax.ShapeDtypeStruct((b, t, n), F32),
        grid=(b, t // tm),
        in_specs=[_tok_spec(tm, d), lat, ctx, _const_spec((1, d)), _const_spec((d, n))],
        out_specs=_tok_spec(tm, n),
        compiler_params=_params("arbitrary", "arbitrary"),
        name="even_in_proj",
    )(h, mods, mods, g.reshape(1, d), w)


def _post_body(m, h, a, bb, g_ref, wo_ref, wgu_ref, wd_ref, hidden):
    half = a.shape[-1]
    y = _dot(a, wo_ref[:half, :]) + _dot(bb, wo_ref[half:, :])
    h1 = h + m[2:3] * y
    xn = (_rms(h1) * g_ref[...]) * (1.0 + m[4:5]) + m[3:4]
    gu = _dot(xn.astype(BF16), wgu_ref[...])
    act = _silu(gu[:, :hidden]) * gu[:, hidden:]
    return h1 + m[5:6] * _dot(act.astype(BF16), wd_ref[...])


def _post_kernel(h_ref, a_ref, b_ref, mod_ref, modc_ref, g_ref, wo_ref, wgu_ref, wd_ref, o_ref,
                 *, n_ctx_tiles, hidden):
    m = _tile_mod(mod_ref, modc_ref, n_ctx_tiles)
    o_ref[0] = _post_body(m, h_ref[0], a_ref[0], b_ref[0], g_ref, wo_ref, wgu_ref, wd_ref, hidden)


def _post_final_kernel(h_ref, a_ref, b_ref, mod_ref, g_ref, wo_ref, wgu_ref, wd_ref, fg_ref, o_ref,
                       *, hidden):
    h2 = _post_body(mod_ref[...], h_ref[0], a_ref[0], b_ref[0], g_ref, wo_ref, wgu_ref, wd_ref, hidden)
    o_ref[0] = _rms(h2) * fg_ref[...]


def _post(h, a, bb, mods, g, wo, wgu, wd, tm, n_ctx_tiles):
    b, t, d = h.shape
    half = a.shape[-1]
    hidden = wd.shape[0]
    lat, ctx = _mod_specs(d, b)
    return pl.pallas_call(
        functools.partial(_post_kernel, n_ctx_tiles=n_ctx_tiles, hidden=hidden),
        out_shape=jax.ShapeDtypeStruct((b, t, d), F32),
        grid=(b, t // tm),
        in_specs=[_tok_spec(tm, d), _tok_spec(tm, half), _tok_spec(tm, half), lat, ctx,
                  _const_spec((1, d)), _const_spec((2 * half, d)), _const_spec((d, 2 * hidden)),
                  _const_spec((hidden, d))],
        out_specs=_tok_spec(tm, d),
        compiler_params=_params("arbitrary", "arbitrary"),
        name="out_proj_ffn",
    )(h, a, bb, mods, mods, g.reshape(1, d), wo, wgu, wd)


def _post_final(h, a_lat, b_lat, mods, g, wo, wgu, wd, final_g, tm, n_ctx_tiles):
    b, t, d = h.shape
    n_lat = a_lat.shape[1]
    half = a_lat.shape[-1]
    hidden = wd.shape[0]
    lat, _ = _mod_specs(d, b)
    return pl.pallas_call(
        functools.partial(_post_final_kernel, hidden=hidden),
        out_shape=jax.ShapeDtypeStruct((b, n_lat, d), F32),
        grid=(b, n_lat // tm),
        in_specs=[pl.BlockSpec((1, tm, d), lambda bi, ti: (bi, ti + n_ctx_tiles, 0)),
                  _tok_spec(tm, half), _tok_spec(tm, half), lat,
                  _const_spec((1, d)), _const_spec((2 * half, d)), _const_spec((d, 2 * hidden)),
                  _const_spec((hidden, d)), _const_spec((1, d))],
        out_specs=_tok_spec(tm, d),
        compiler_params=_params("arbitrary", "arbitrary"),
        name="out_proj_ffn_final",
    )(h, a_lat, b_lat, mods, g.reshape(1, d), wo, wgu, wd, final_g.reshape(1, d))


def _hgrn_tables(tile):
    c = HG_CHUNK
    r = np.arange(c)
    j = r[None, :]
    blocks = [(j <= r[:, None])]
    masks, bms = [], []
    m = c
    while m >= 2:
        half = m // 2
        p = r % m
        mid = (r - p + half)[:, None]
        upper = (p >= half)[:, None]
        up = (j >= mid) & (j <= r[:, None]) & upper
        lo = (j > r[:, None]) & (j <= mid - 1) & (~upper)
        if m > 2:
            blocks.append(up | lo)
        masks.append(np.broadcast_to(upper, (c, LANES)))
        bms.append(((r[:, None] // m) == (r[None, :] // m)) & upper & (~upper.T))
        m //= 2
    fwd = np.stack(blocks).astype(np.float32)
    bwd = fwd[:, ::-1, ::-1]
    mall = np.stack([fwd.reshape(-1, c), bwd.reshape(-1, c)])
    mall = np.concatenate([mall, mall], axis=2)
    mu = np.stack(masks).astype(np.float32)
    masku = np.tile(np.stack([mu, mu[:, ::-1]]), (1, 1, tile // c, 1))
    bmp = np.stack([np.concatenate([bms[i], bms[i + 1]], axis=1) for i in range(0, len(bms), 2)])
    bmp_bwd = np.stack([np.concatenate([bms[i][::-1, ::-1], bms[i + 1][::-1, ::-1]], axis=1)
                        for i in range(0, len(bms), 2)])
    return (jnp.asarray(mall, dtype=BF16), jnp.asarray(masku, dtype=F32),
            jnp.asarray(np.stack([bmp, bmp_bwd]), dtype=F32))


def _hgrn_kernel(q_ref, ff_ref, fb_ref, i_ref, g_ref, lbp_ref, ng_ref, mall_ref, masku_ref, bmp_ref,
                 o_ref, of_ref, ob_ref, *, layer, n_ctx_tiles, n_tiles):
    c = HG_CHUNK
    dk = HG_HEAD_DIM
    n_levels = masku_ref.shape[1]
    tile = masku_ref.shape[2]
    n_sub = tile // c
    lbp = lbp_ref[...]
    ex = jnp.exp(lbp - jnp.max(lbp, axis=0, keepdims=True))
    p = ex / jnp.sum(ex, axis=0, keepdims=True)
    lb = jnp.sum(p[:layer + 1], axis=0) - p[0]

    def sub(x, g):
        return x[g * c:(g + 1) * c]

    dirs = (0, 1)

    def stack_chunks(x):
        return jnp.concatenate([sub(x, g) for g in range(n_sub)], axis=1)

    def bwd_tile(t):
        if isinstance(t, int):
            return n_ctx_tiles - 1 - t if t < n_ctx_tiles else n_tiles - 1 - (t - n_ctx_tiles)
        return jnp.where(t < n_ctx_tiles, n_ctx_tiles - 1 - t, n_tiles - 1 - (t - n_ctx_tiles))

    def tile_rows(ti):
        start = ti * tile
        return pl.ds(start if isinstance(ti, int) else pl.multiple_of(start, tile), tile)

    def run_trips(trips, sts):
        lanes = [(s, d) for s in range(len(trips)) for d in dirs]
        n = range(len(lanes))
        tis = [bwd_tile(t) if d else t for t in trips for d in dirs]
        rows = [tile_rows(ti) for ti in tis]
        q = [q_ref[0, r, :] for r in rows]
        v = [i_ref[0, r, :] for r in rows]
        lbd = [lb[d:d + 1] for _, d in lanes]
        zf = [(ff_ref, fb_ref)[d][0, rows[l], :] for l, (_, d) in enumerate(lanes)]
        f = [lbd[l] + (1.0 - lbd[l]) * jax.nn.sigmoid(zf[l]) for l in n]
        lf = [jnp.log(x) for x in f]
        k = [1.0 - x for x in f]
        hi = [x.astype(BF16) for x in lf]
        lo = [(lf[l] - hi[l].astype(F32)).astype(BF16) for l in n]
        e2 = [_dot(mall_ref[lanes[l][1]], jnp.concatenate([stack_chunks(hi[l]), stack_chunks(lo[l])], axis=0))
              for l in n]
        ee = [[jnp.exp(e2[l][:, g * dk:(g + 1) * dk]) for g in range(n_sub)] for l in n]

        def block(l, idx):
            return jnp.concatenate([ee[l][g][idx * c:(idx + 1) * c] for g in range(n_sub)], axis=0)

        def tail_decay(l, g):
            cum = e2[l][0:c, g * dk:(g + 1) * dk]
            last = c - 1 if lanes[l][1] == 0 else 0
            return jnp.exp(cum[last:last + 1] - cum)

        g_in = [block(l, 0) for l in n]
        qin = [(q[l] * g_in[l]).astype(BF16) for l in n]
        kout = [(k[l] * jnp.concatenate([tail_decay(l, g) for g in range(n_sub)], axis=0)).astype(BF16)
                for l in n]
        vb = [x.astype(BF16) for x in v]
        zero = jnp.zeros((c, dk), BF16)
        ds_all = [_dot_tn(vb[l], jnp.concatenate(
            [jnp.concatenate([sub(kout[l], g) if gg == g else zero for gg in range(n_sub)], axis=1)
             for g in range(n_sub)], axis=0)) for l in n]
        ds = [[ds_all[l][:, g * dk:(g + 1) * dk] for g in range(n_sub)] for l in n]
        qf = [q[l] * f[l] for l in n]
        w = [[None] * n_levels for _ in n]
        for lvl in range(n_levels):
            for l, (_, d) in enumerate(lanes):
                later = masku_ref[d, lvl] != 0.0
                if lvl == n_levels - 1:
                    w[l][lvl] = jnp.where(later, qf[l], k[l]).astype(BF16)
                else:
                    w[l][lvl] = (jnp.where(later, q[l], k[l]) * block(l, 1 + lvl)).astype(BF16)
        scores = [[[None] * (n_levels // 2) for _ in range(n_sub)] for _ in n]
        for pair in range(n_levels // 2):
            lv_a, lv_b = 2 * pair, 2 * pair + 1
            for g in range(n_sub):
                for l, (_, d) in enumerate(lanes):
                    wa, wb = sub(w[l][lv_a], g), sub(w[l][lv_b], g)
                    rhs = jnp.concatenate([jnp.concatenate([wa, zero], axis=1),
                                           jnp.concatenate([zero, wb], axis=1)], axis=0)
                    raw = _dot_nt(jnp.concatenate([wa, wb], axis=1), rhs)
                    scores[l][g][pair] = (raw * bmp_ref[d, pair]).astype(BF16)
        o = [jnp.concatenate([_dot(jnp.concatenate(scores[l][g], axis=1),
                                   jnp.concatenate([sub(vb[l], g)] * n_levels, axis=0))
                              for g in range(n_sub)], axis=0)
             + jnp.sum(q[l] * k[l], axis=-1, keepdims=True) * v[l] for l in n]
        order = [list(range(n_sub)), list(reversed(range(n_sub)))]
        entering = [[None] * n_sub for _ in n]
        sts = list(sts)
        for l, (_, d) in enumerate(lanes):
            for g in order[d]:
                entering[l][g] = sts[d].astype(BF16)
                last = (g + 1) * c - 1 if d == 0 else g * c
                sts[d] = sts[d] * g_in[l][last:last + 1] + ds[l][g]
        for l, (_, d) in enumerate(lanes):
            out = jnp.concatenate([sub(o[l], g) + _dot_nt(sub(qin[l], g), entering[l][g])
                                   for g in range(n_sub)], axis=0)
            (of_ref, ob_ref)[d][rows[l], :] = out
        return tuple(sts)

    def body(i, carry):
        return run_trips([i * HGRN_LOCKSTEP + s for s in range(HGRN_LOCKSTEP)], carry)

    s0 = jnp.zeros((dk, dk), F32)
    sts = lax.fori_loop(0, n_tiles // HGRN_LOCKSTEP, body, (s0, s0), unroll=HGRN_UNROLL)
    rest = list(range(n_tiles - n_tiles % HGRN_LOCKSTEP, n_tiles))
    if rest:
        run_trips(rest, sts)
    o = _rms(of_ref[...] + ob_ref[...]) * ng_ref[...]
    o_ref[0] = (o * _silu(g_ref[0])).astype(o_ref.dtype)


def _hgrn(z, hg_lower_bound, ng, layer, n_heads, ctx_len):
    b, t, _ = z.shape
    dk = HG_HEAD_DIM
    n_even = hg_lower_bound.shape[0]
    tile = math.gcd(math.gcd(ctx_len, t - ctx_len), 256)
    mall, masku, bm = _hgrn_tables(tile)

    def col(kind):
        return pl.BlockSpec((1, t, dk), lambda bi, hi: (bi, 0, kind * n_heads + hi))

    def const(arr):
        zeros = (0,) * arr.ndim
        return pl.BlockSpec(arr.shape, lambda bi, hi: zeros)

    return pl.pallas_call(
        functools.partial(_hgrn_kernel, layer=layer, n_ctx_tiles=ctx_len // tile, n_tiles=t // tile),
        out_shape=jax.ShapeDtypeStruct((b, t, n_heads * dk), BF16),
        grid=(b, n_heads),
        in_specs=[col(0), col(1), col(2), col(3), col(4),
                  pl.BlockSpec((n_even, 2, dk), lambda bi, hi: (0, 0, hi)),
                  const(ng.reshape(1, dk)), const(mall), const(masku), const(bm)],
        out_specs=pl.BlockSpec((1, t, dk), lambda bi, hi: (bi, 0, hi)),
        scratch_shapes=[pltpu.VMEM((t, dk), F32), pltpu.VMEM((t, dk), F32)],
        compiler_params=_params("arbitrary", "arbitrary"),
        name="hgrn2_scan",
    )(z, z, z, z, z, hg_lower_bound, ng.reshape(1, dk), mall, masku, bm)


def _hy_pre_kernel(x0_ref, x1_ref, v_ref, w0_ref, w1_ref, wv_ref, b0_ref, b1_ref, bv_ref,
                   x0c_ctx_ref, x0c_lat_ref, u_ctx_ref, u_lat_ref, *, ctx_len):
    t = x0_ref.shape[1]
    row = lax.broadcasted_iota(jnp.int32, (t, LANES), 0)
    first = (row == 0) | (row == ctx_len)
    final = (row == ctx_len - 1) | (row == t - 1)

    def short_conv(z_ref, w_ref, b_ref):
        z = z_ref[0]
        prev = jnp.where(first, 0.0, pltpu.roll(z, 1, 0))
        nxt = jnp.where(final, 0.0, pltpu.roll(z, t - 1, 0))
        w = w_ref[...]
        return prev * w[0:1] + z * w[1:2] + nxt * w[2:3] + b_ref[...]

    x0c = short_conv(x0_ref, w0_ref, b0_ref).astype(x0c_ctx_ref.dtype)
    u = (short_conv(x1_ref, w1_ref, b1_ref) * short_conv(v_ref, wv_ref, bv_ref)).astype(u_ctx_ref.dtype)
    x0c_ctx_ref[0] = x0c[:ctx_len]
    x0c_lat_ref[0] = x0c[ctx_len:]
    u_ctx_ref[0] = u[:ctx_len]
    u_lat_ref[0] = u[ctx_len:]


def _hy_pre(z, short_w, short_b, col0, width, ctx_len):
    b, t, _ = z.shape
    nb = width // LANES
    c0 = col0 // LANES

    def zcol(kind):
        return pl.BlockSpec((1, t, LANES), lambda bi, j: (bi, 0, c0 + kind * nb + j))

    def wcol(kind, rows):
        return pl.BlockSpec((rows, LANES), lambda bi, j: (0, kind * nb + j))

    def seg(rows):
        return (jax.ShapeDtypeStruct((b, rows, width), BF16),
                pl.BlockSpec((1, rows, LANES), lambda bi, j: (bi, 0, j)))

    (ctx_shape, ctx_spec), (lat_shape, lat_spec) = seg(ctx_len), seg(t - ctx_len)
    sb = short_b.reshape(1, -1)
    return pl.pallas_call(
        functools.partial(_hy_pre_kernel, ctx_len=ctx_len),
        out_shape=(ctx_shape, lat_shape, ctx_shape, lat_shape),
        grid=(b, nb),
        in_specs=[zcol(0), zcol(1), zcol(2), wcol(0, 3), wcol(1, 3), wcol(2, 3),
                  wcol(0, 1), wcol(1, 1), wcol(2, 1)],
        out_specs=(ctx_spec, lat_spec, ctx_spec, lat_spec),
        compiler_params=_params("arbitrary", "arbitrary"),
        name="hyena_short_conv",
    )(z, z, z, short_w, short_w, short_w, sb, sb, sb)


def _filter_features(length, n):
    p = np.arange(n)
    is_f = p < length
    is_b = p > n - length
    lag = np.where(is_f, p, np.where(is_b, n - 1 - p, 0))
    tt = np.linspace(0.0, 1.0, length, dtype=np.float32)[lag][:, None]
    w = (2.0 * math.pi * lag.astype(np.float32) / length)[:, None].astype(np.float32)
    bands = np.linspace(1e-4, HY_BANDS - 1, HY_BANDS, dtype=np.float32)[None, :]
    feat = np.concatenate([tt, np.cos(bands * w), -np.sin(bands * w), is_f[:, None], is_b[:, None]],
                          axis=-1).astype(np.float32)
    pad = (-feat.shape[1]) % BF16_SUBLANES
    return jnp.asarray(np.pad(feat, ((0, 0), (0, pad))))


def _filt_kernel(feat_ref, w1_ref, b1_ref, fr1_ref, w2_ref, b2_ref, fr2_ref,
                 w3f_ref, w3b_ref, dl_ref, o_ref, hid_ref):
    z = feat_ref[...]
    mf = z[:, HY_EMB:HY_EMB + 1]
    mb = z[:, HY_EMB + 1:HY_EMB + 2]

    @pl.when(pl.program_id(0) == 0)
    def _():
        h1 = jnp.sin(fr1_ref[...] * (_dot3(z, w1_ref[...]) + b1_ref[...]))
        hid_ref[...] = jnp.sin(fr2_ref[...] * (_dot3(h1, w2_ref[...]) + b2_ref[...]))

    hid = hid_ref[...]
    hf = _dot3(hid, w3f_ref[...])
    hb = _dot3(hid, w3b_ref[...])
    win = jnp.exp(-z[:, 0:1] * dl_ref[...])
    f = (mf * hf + mb * hb) * win
    o_ref[...] = f / jnp.sum(jnp.abs(f), axis=0, keepdims=True)


def _hyena_filter(length, n, w1, b1, fr1, w2, b2, fr2, w3, width):
    feat = _filter_features(length, n)
    nf = feat.shape[1]
    hid = w1.shape[1]
    w1p = jnp.pad(w1, ((0, nf - w1.shape[0]), (0, 0)))
    d_lo = -math.log(HY_TARGET) / HY_GENTLE_PCT
    d_hi = -math.log(HY_TARGET) / HY_STEEP_PCT
    deltas = jnp.asarray(np.linspace(d_lo, d_hi, width, dtype=np.float32)[None, :])
    nb = width // LANES

    def full(shape):
        zeros = (0,) * len(shape)
        return pl.BlockSpec(shape, lambda j: zeros)

    return pl.pallas_call(
        _filt_kernel,
        out_shape=jax.ShapeDtypeStruct((n, width), F32),
        grid=(nb,),
        in_specs=[full((n, nf)), full((nf, hid)), full((1, hid)),
                  full((1, hid)), full((hid, hid)), full((1, hid)), full((1, hid)),
                  pl.BlockSpec((hid, LANES), lambda j: (0, j)),
                  pl.BlockSpec((hid, LANES), lambda j: (0, nb + j)),
                  pl.BlockSpec((1, LANES), lambda j: (0, j))],
        out_specs=pl.BlockSpec((n, LANES), lambda j: (0, j)),
        scratch_shapes=[pltpu.VMEM((n, hid), F32)],
        compiler_params=_params("arbitrary"),
        name="hyena_filter_mlp",
    )(feat, w1p, b1.reshape(1, -1), fr1.reshape(1, -1), w2, b2.reshape(1, -1),
      fr2.reshape(1, -1), w3, w3, deltas)


def _dft_tables(n1, n1_in):
    n2 = FFT_N2
    n = n1 * n2
    a = np.arange(n1)
    j = np.arange(n2)
    ang = -2.0 * np.pi * (a[None, None, :] * a[None, :, None] / n1 + j[:, None, None] * a[None, :, None] / n)
    tr, ti = np.cos(ang), np.sin(ang)
    fwd_a = np.concatenate([np.concatenate([tr, -ti], 2), np.concatenate([ti, tr], 2)], 1)
    trt, tit = np.swapaxes(tr, 1, 2), -np.swapaxes(ti, 1, 2)
    inv_a = np.concatenate([np.concatenate([trt, -tit], 2), np.concatenate([tit, trt], 2)], 1)
    keep = np.concatenate([np.arange(n1_in), n1 + np.arange(n1_in)])
    ang2 = -2.0 * np.pi * (j[:, None] * j[None, :]) / n2
    cr, ci = np.cos(ang2), np.sin(ang2)
    fwd_c = np.block([[cr, -ci], [ci, cr]])
    inv_c = np.block([[cr, ci], [-ci, cr]])
    real_a = np.concatenate([tr, ti], 1)
    return dict(fwd_a=fwd_a[:, :, keep], inv_a=inv_a[:, keep, :], fwd_c=fwd_c, inv_c=inv_c,
                real_a=real_a)


def _slab(idx):
    return pl.ds(pl.multiple_of(idx * FFT_PITCH, SUBLANES), FFT_N2)


def _fft_filter_kernel(f_ref, wa_ref, wc_ref, o_ref, scr_ref, *, n1):
    n2 = FFT_N2
    rows = 2 * n1
    scale = 1.0 / (n1 * n2)

    def split3(x):
        hi = x.astype(BF16)
        lo = (x - hi.astype(F32)).astype(BF16)
        return jnp.concatenate([hi, lo, hi], axis=0)

    def stage_a(j, carry):
        scr_ref[pl.ds(j, rows, stride=FFT_PITCH), :] = _dot(wa_ref[j], split3(f_ref[j]))
        return carry

    lax.fori_loop(0, n2, stage_a, 0, unroll=FFT_UNROLL)

    def stage_c(k, carry):
        x = jnp.concatenate([scr_ref[_slab(k), :], scr_ref[_slab(n1 + k), :]], axis=0)
        o_ref[k] = _dot(wc_ref[...], split3(x)) * scale
        return carry

    lax.fori_loop(0, n1, stage_c, 0, unroll=FFT_UNROLL)


def _split3_cols(w):
    hi = w.astype(np.float32).astype(jnp.bfloat16)
    lo = (w.astype(np.float32) - np.asarray(hi, np.float32)).astype(jnp.bfloat16)
    return jnp.asarray(np.concatenate([hi, hi, lo], axis=-1))


def _fft_filter(filt, n1, tables):
    n2 = FFT_N2
    width = filt.shape[1]
    ft = filt.reshape(n1, n2, width).transpose(1, 0, 2)
    wa = _split3_cols(tables["real_a"])
    wc = _split3_cols(tables["fwd_c"])
    return pl.pallas_call(
        functools.partial(_fft_filter_kernel, n1=n1),
        out_shape=jax.ShapeDtypeStruct((n1, 2 * n2, width), F32),
        grid=(width // LANES,),
        in_specs=[pl.BlockSpec((n2, n1, LANES), lambda c: (0, 0, c)),
                  pl.BlockSpec(wa.shape, lambda c: (0, 0, 0)),
                  pl.BlockSpec(wc.shape, lambda c: (0, 0))],
        out_specs=pl.BlockSpec((n1, 2 * n2, LANES), lambda c: (0, 0, c)),
        scratch_shapes=[pltpu.VMEM((2 * n1 * FFT_PITCH, LANES), F32)],
        compiler_params=_params("arbitrary"),
        name="hyena_filter_dft",
    )(ft, wa, wc)


def _fftconv_kernel(u_ref, x0_ref, fh_ref, skip_ref, wfa_ref, wfc_ref, wic_ref, wia_ref, o_ref,
                    scr_ref, *, n1):
    n2 = FFT_N2
    rows = 2 * n1

    def stage_a(j, carry):
        scr_ref[pl.ds(j, rows, stride=FFT_PITCH), :] = _dot(wfa_ref[j], u_ref[0, j])
        return carry

    lax.fori_loop(0, n2, stage_a, 0, unroll=FFT_UNROLL)

    def stage_c(k, carry):
        x = jnp.concatenate([scr_ref[_slab(k), :], scr_ref[_slab(n1 + k), :]], axis=0)
        xf = _dot(wfc_ref[...], x.astype(BF16))
        fh = fh_ref[k]
        xr, xi, fr, fi = xf[:n2], xf[n2:], fh[:n2], fh[n2:]
        y = jnp.concatenate([xr * fr - xi * fi, xr * fi + xi * fr], axis=0)
        zt = _dot(wic_ref[...], y.astype(BF16))
        scr_ref[_slab(k), :] = zt[:n2]
        scr_ref[_slab(n1 + k), :] = zt[n2:]
        return carry

    lax.fori_loop(0, n1, stage_c, 0, unroll=FFT_UNROLL)

    def stage_ai(j, carry):
        x = scr_ref[pl.ds(j, rows, stride=FFT_PITCH), :]
        y = _dot(wia_ref[j], x.astype(BF16))
        u = u_ref[0, j].astype(F32)
        o_ref[0, j] = (x0_ref[0, j].astype(F32) * (y + skip_ref[...] * u)).astype(o_ref.dtype)
        return carry

    lax.fori_loop(0, n2, stage_ai, 0, unroll=FFT_UNROLL)


def _to_fft_layout(x, n1_in):
    b, length, c = x.shape
    n1_used = length // FFT_N2
    x = x.reshape(b // 2, 2, n1_used, FFT_N2, c)
    x = jnp.pad(x, ((0, 0), (0, 0), (0, n1_in - n1_used), (0, 0), (0, 0)))
    return x.transpose(0, 3, 1, 2, 4).reshape(b // 2, FFT_N2, 2 * n1_in, c)


def _from_fft_layout(y, length):
    p, n2, rows, c = y.shape
    n1_in = rows // 2
    y = y.reshape(p, n2, 2, n1_in, c).transpose(0, 2, 3, 1, 4)
    return y.reshape(2 * p, n1_in * n2, c)[:, :length]


def _fftconv(u, x0c, fh, skip, n1, n1_in, tables):
    _, length, width = u.shape
    ut = _to_fft_layout(u, n1_in)
    xt = _to_fft_layout(x0c, n1_in)
    pairs, n2, rin, _ = ut.shape
    wfa = jnp.asarray(tables["fwd_a"], dtype=BF16)
    wia = jnp.asarray(tables["inv_a"], dtype=BF16)
    wfc = jnp.asarray(tables["fwd_c"], dtype=BF16)
    wic = jnp.asarray(tables["inv_c"], dtype=BF16)
    data = pl.BlockSpec((1, n2, rin, LANES), lambda c, p: (p, 0, 0, c))

    def const(arr):
        zeros = (0,) * arr.ndim
        return pl.BlockSpec(arr.shape, lambda c, p: zeros, pipeline_mode=pl.Buffered(1))

    out = pl.pallas_call(
        functools.partial(_fftconv_kernel, n1=n1),
        out_shape=jax.ShapeDtypeStruct(ut.shape, BF16),
        grid=(width // LANES, pairs),
        in_specs=[data, data,
                  pl.BlockSpec((n1, 2 * n2, LANES), lambda c, p: (0, 0, c)),
                  pl.BlockSpec((1, LANES), lambda c, p: (0, c)),
                  const(wfa), const(wfc), const(wic), const(wia)],
        out_specs=data,
        scratch_shapes=[pltpu.VMEM((2 * n1 * FFT_PITCH, LANES), F32)],
        compiler_params=_params("arbitrary", "arbitrary"),
        name="hyena_dft_conv",
    )(ut, xt, fh, skip.reshape(1, width), wfa, wfc, wic, wia)
    return _from_fft_layout(out, length)


def _hyena(z, col0, width, ctx_len, short_w, short_b, filt_params, skip):
    x0c_ctx, x0c_lat, u_ctx, u_lat = _hy_pre(z, short_w, short_b, col0, width, ctx_len)
    outs = []
    for x0c, u in ((x0c_ctx, u_ctx), (x0c_lat, u_lat)):
        length = u.shape[1]
        n1_in = max(length // FFT_N2, SUBLANES)
        n1 = 2 * n1_in
        tables = _dft_tables(n1, n1_in)
        filt = _hyena_filter(length, n1 * FFT_N2, *filt_params, width)
        fh = _fft_filter(filt, n1, tables)
        outs.append(_fftconv(u, x0c, fh, skip, n1, n1_in, tables))
    return jnp.concatenate(outs, axis=1)


def _rope_tables(n_lat, ctx_len):
    tok = np.arange(n_lat)
    row, colp = tok // GRID_W, tok % GRID_W

    def axial(half):
        inv = ROPE_BASE ** (-np.arange(half, dtype=np.float32) / half)
        parts_c, parts_s = [], []
        for pos in (row, colp):
            ang = pos.astype(np.float32)[:, None] * inv
            parts_c += [np.cos(ang), np.cos(ang)]
            parts_s += [-np.sin(ang), np.sin(ang)]
        return np.concatenate(parts_c, 1), np.concatenate(parts_s, 1)

    dc, ds = axial(DA_HEAD_DIM // 4)
    mc, ms = axial(MLA_ROPE // 4)
    ones, zeros = np.ones((n_lat, MLA_NOPE), np.float32), np.zeros((n_lat, MLA_NOPE), np.float32)
    padc = np.ones((n_lat, LANES - MLA_NOPE - MLA_ROPE), np.float32)
    tabs = [np.concatenate([dc, dc], 1), np.concatenate([ds, ds], 1),
            np.concatenate([ones, mc, padc], 1), np.concatenate([zeros, ms, 0 * padc], 1)]
    out = []
    for i, tb in enumerate(tabs):
        ctx_rows = np.ones((ctx_len, LANES), np.float32) if i % 2 == 0 else np.zeros((ctx_len, LANES), np.float32)
        out.append(jnp.asarray(np.concatenate([ctx_rows, tb.astype(np.float32)], 0)))
    return out


def _with_ones(v):
    ones = jnp.ones((v.shape[0], LANES), v.dtype)
    parts = []
    for hd in range(v.shape[1] // LANES):
        parts += [v[:, hd * LANES:(hd + 1) * LANES], ones]
    return jnp.concatenate(parts, axis=1)


def _in_odd_kernel(x_ref, mod_ref, modc_ref, g_ref, w_ref, dc_ref, ds_ref, mc_ref, ms_ref,
                   qg_ref, kvg_ref, wuq_ref, wukv_ref,
                   qd_ref, kd_ref, vd_ref, qm_ref, km_ref, vm_ref, *, n_ctx_tiles, da_w, q_rank, kv_rank):
    m = _tile_mod(mod_ref, modc_ref, n_ctx_tiles)
    xn = (_rms(x_ref[0]) * g_ref[...]) * (1.0 + m[1:2]) + m[0:1]
    z = _dot(xn.astype(BF16), w_ref[...])
    nrep = da_w // LANES
    dc = jnp.concatenate([dc_ref[...]] * nrep, axis=1)
    ds = jnp.concatenate([ds_ref[...]] * nrep, axis=1)
    mc = jnp.concatenate([mc_ref[...]] * MLA_HEADS, axis=1)
    ms = jnp.concatenate([ms_ref[...]] * MLA_HEADS, axis=1)
    sa = DA_HEAD_DIM ** -0.5 * LOG2_E
    sm = (MLA_NOPE + MLA_ROPE) ** -0.5 * LOG2_E
    lane = lax.broadcasted_iota(jnp.int32, (x_ref.shape[1], LANES), 1)

    def partner(x, half):
        first = (lane % (2 * half)) < half
        cols = []
        for j in range(x.shape[1] // LANES):
            xj = x[:, j * LANES:(j + 1) * LANES]
            cols.append(jnp.where(first, pltpu.roll(xj, LANES - half, 1), pltpu.roll(xj, half, 1)))
        return cols[0] if len(cols) == 1 else jnp.concatenate(cols, axis=1)

    o = 0
    zq = z[:, o:o + da_w]
    qd_ref[0] = ((zq * dc + partner(zq, DA_HEAD_DIM // 4) * ds) * sa).astype(BF16)
    o += da_w
    zk = z[:, o:o + da_w]
    kd_ref[0] = (zk * dc + partner(zk, DA_HEAD_DIM // 4) * ds).astype(BF16)
    o += da_w
    vd_ref[0] = _with_ones(z[:, o:o + da_w]).astype(BF16)
    o += da_w
    cq = _rms(z[:, o:o + q_rank]) * qg_ref[...]
    o += q_rank
    ckv = _rms(z[:, o:o + kv_rank]) * kvg_ref[...]
    o += kv_rank
    zr = z[:, o:o + LANES]
    kr = zr * mc_ref[...] + partner(zr, MLA_ROPE // 4) * ms_ref[...]
    mw = MLA_HEADS * LANES
    qu = _dot(cq.astype(BF16), wuq_ref[...])
    qm_ref[0] = ((qu * mc + partner(qu, MLA_ROPE // 4) * ms) * sm).astype(BF16)
    kvu = _dot(ckv.astype(BF16), wukv_ref[...])
    km_ref[0] = (kvu[:, :mw] + jnp.concatenate([kr] * MLA_HEADS, axis=1)).astype(BF16)
    vm_ref[0] = _with_ones(kvu[:, mw:]).astype(BF16)


def _odd_weights(w_in, w_uq, w_ukv, da_w, q_rank, kv_rank):
    o = 3 * da_w + q_rank + kv_rank
    krw = jnp.pad(w_in[:, o:], ((0, 0), (MLA_NOPE, LANES - MLA_NOPE - MLA_ROPE)))
    w_big = jnp.concatenate([w_in[:, :o], krw], axis=1)
    dq = MLA_NOPE + MLA_ROPE
    uq = jnp.pad(w_uq.reshape(q_rank, MLA_HEADS, dq), ((0, 0), (0, 0), (0, LANES - dq)))
    ukv = w_ukv.reshape(kv_rank, MLA_HEADS, MLA_NOPE + MLA_V)
    uk = jnp.pad(ukv[:, :, :MLA_NOPE], ((0, 0), (0, 0), (0, LANES - MLA_NOPE))).reshape(kv_rank, MLA_HEADS * LANES)
    uv = ukv[:, :, MLA_NOPE:].reshape(kv_rank, MLA_HEADS * MLA_V)
    return (w_big.astype(BF16), uq.reshape(q_rank, MLA_HEADS * LANES).astype(BF16),
            jnp.concatenate([uk, uv], axis=1).astype(BF16))


def _in_odd(h, mods, g, w_big, wuq, wukv, qg, kvg, tabs, tm, n_ctx_tiles, da_w):
    b, t, d = h.shape
    q_rank, kv_rank = qg.shape[0], kvg.shape[0]
    lat, ctx = _mod_specs(d, b)
    tab = pl.BlockSpec((tm, LANES), lambda bi, ti: (ti, 0))
    mw = MLA_HEADS * LANES

    def tok_major(width):
        return jax.ShapeDtypeStruct((b, t, width), BF16), _tok_spec(tm, width)

    outs, ospecs = zip(tok_major(da_w), tok_major(da_w), tok_major(2 * da_w),
                       tok_major(mw), tok_major(mw), tok_major(2 * mw))
    return pl.pallas_call(
        functools.partial(_in_odd_kernel, n_ctx_tiles=n_ctx_tiles, da_w=da_w, q_rank=q_rank, kv_rank=kv_rank),
        out_shape=tuple(outs),
        grid=(b, t // tm),
        in_specs=[_tok_spec(tm, d), lat, ctx, _const_spec((1, d)), _const_spec(w_big.shape),
                  tab, tab, tab, tab, _const_spec((1, q_rank)), _const_spec((1, kv_rank)),
                  _const_spec(wuq.shape), _const_spec(wukv.shape)],
        out_specs=tuple(ospecs),
        compiler_params=_params("arbitrary", "arbitrary"),
        name="odd_in_proj",
    )(h, mods, mods, g.reshape(1, d), w_big, *tabs, qg.reshape(1, -1), kvg.reshape(1, -1), wuq, wukv)


def _softmax_pv(qs, k_ref, v_ref):
    n_keys = k_ref.shape[1]
    starts = list(range(0, n_keys, ATTN_KEY_BLOCK))
    m = [None] * len(qs)
    acc = [None] * len(qs)
    for start in starts:
        size = min(ATTN_KEY_BLOCK, n_keys - start)
        k = k_ref[0, start:start + size, :]
        v1 = v_ref[0, start:start + size, :]
        for i, q in enumerate(qs):
            s = _dot_nt(q, k)
            m_blk = jnp.max(s, axis=-1, keepdims=True)
            if start == 0:
                m[i] = m_blk
                acc[i] = _dot(jnp.exp2(s - m_blk).astype(BF16), v1)
            else:
                m_new = jnp.maximum(m[i], m_blk)
                pv = _dot(jnp.exp2(s - m_new).astype(BF16), v1)
                acc[i] = jnp.exp2(m[i] - m_new) * acc[i] + pv
                m[i] = m_new
    return [a[:, :LANES] / a[:, LANES:] for a in acc]


def _attn_kernel(q_ref, k_ref, v_ref, lam_ref, sg_ref, *rest, diff, lam_init):
    o_ref = rest[-1]
    q = q_ref[0]
    if not diff:
        o_ref[0] = _softmax_pv([q], k_ref, v_ref)[0].astype(o_ref.dtype)
        return
    first = lax.broadcasted_iota(jnp.int32, q.shape, 1) < DA_HEAD_DIM
    zero = jnp.zeros_like(q)
    o1, o2 = _softmax_pv([jnp.where(first, q, zero), jnp.where(first, zero, q)], k_ref, v_ref)
    lp = lam_ref[...]
    lam = (jnp.exp(jnp.sum(lp[0:1] * lp[1:2], axis=-1, keepdims=True))
           - jnp.exp(jnp.sum(lp[2:3] * lp[3:4], axis=-1, keepdims=True)) + lam_init)
    o_ref[0] = (_rms(o1 - lam * o2) * sg_ref[...] * (1.0 - lam_init)).astype(o_ref.dtype)


def _attention(q, k, v1, lam_p, subln_g, q_rows, tq, n_keys, out_rows, out_start, diff, lam_init, earlier=None):
    b, _, width = q.shape
    heads = width // LANES
    q_start, q_stop = q_rows

    def rows_from(start):
        return pl.BlockSpec((pl.Element(1), pl.Element(tq), pl.Element(LANES)),
                            lambda bi, hi, ti: (bi, pl.multiple_of(start + ti * tq, math.gcd(start, tq)),
                                                pl.multiple_of(hi * LANES, LANES)))

    khead = pl.BlockSpec((1, n_keys, LANES), lambda bi, hi, ti: (bi, 0, hi))
    vhead = pl.BlockSpec((1, n_keys, 2 * LANES), lambda bi, hi, ti: (bi, 0, hi))

    def const(shape):
        zeros = (0,) * len(shape)
        return pl.BlockSpec(shape, lambda bi, hi, ti: zeros)

    in_specs = [rows_from(q_start), khead, vhead, const(lam_p.shape), const((1, LANES))]
    args = [q, k, v1, lam_p, subln_g.reshape(1, LANES)]
    aliases = {}
    if earlier is not None:
        in_specs.append(pl.BlockSpec(memory_space=pl.ANY))
        args.append(earlier)
        aliases = {len(args) - 1: 0}
    return pl.pallas_call(
        functools.partial(_attn_kernel, diff=diff, lam_init=lam_init),
        out_shape=jax.ShapeDtypeStruct((b, out_rows, width), BF16),
        grid=(b, heads, (q_stop - q_start) // tq),
        in_specs=in_specs,
        out_specs=rows_from(out_start),
        input_output_aliases=aliases,
        compiler_params=_params("arbitrary", "arbitrary", "arbitrary"),
        name="diff_attention" if diff else "mla_attention",
    )(*args)


def _attend_segments(q, k, v1, lam_p, subln_g, ctx_len, diff, lam_init, need_ctx):
    t = k.shape[1]
    n_lat = t - ctx_len
    tq_lat = math.gcd(n_lat, ATTN_QUERY_TILE)
    if not need_ctx:
        return _attention(q, k, v1, lam_p, subln_g, (ctx_len, t), tq_lat, t, n_lat, 0, diff, lam_init)
    o = _attention(q, k, v1, lam_p, subln_g, (ctx_len, t), tq_lat, t, t, ctx_len, diff, lam_init)
    return _attention(q, k, v1, lam_p, subln_g, (0, ctx_len), ctx_len, ctx_len, t, 0, diff, lam_init, earlier=o)


def kernel(x, c, ctx, c_ctx, ada_w, ada_b, norm_mix_g, norm_ffn_g, ffn_w_gu, ffn_w_down, ev_w_in, ev_w_out, hg_lower_bound, hg_out_norm_g, hy_short_w, hy_short_b, hy_filt_w1, hy_filt_b1, hy_filt_freq1, hy_filt_w2, hy_filt_b2, hy_filt_freq2, hy_filt_w3, hy_skip, od_w_in, od_w_out, da_lambda, da_subln_g, mla_q_norm_g, mla_w_uq, mla_kv_norm_g, mla_w_ukv, final_norm_g):
    n_batch, n_lat, d = x.shape
    ctx_len = ctx.shape[1]
    depth = ada_w.shape[0]
    assert n_batch % 2 == 0 and n_batch < MOD_ROWS
    assert n_lat % GRID_W == 0 and ctx_len % HG_CHUNK == 0 and n_lat % FFT_N2 == 0 and ctx_len % FFT_N2 == 0
    tm = math.gcd(math.gcd(ctx_len, n_lat), 256)
    n_ctx_tiles = ctx_len // tm
    hg_width = d // 2
    hy_width = d - hg_width
    da_w = d // 2
    q_rank, kv_rank = mla_q_norm_g.shape[1], mla_kv_norm_g.shape[1]

    cc = jnp.concatenate([c, c_ctx[None], jnp.zeros((MOD_ROWS - n_batch - 1, d), F32)], axis=0)
    mods = _ada(cc, ada_w, ada_b).reshape(depth, MOD_ROWS, 6, d)
    h = jnp.concatenate([ctx, x], axis=1)
    rope_tabs = _rope_tables(n_lat, ctx_len)

    for i in range(depth):
        last = i == depth - 1
        if i % 2 == 0:
            e = i // 2
            z = _in_even(h, mods[i], norm_mix_g[i], ev_w_in[e].astype(BF16), tm, n_ctx_tiles)
            a = _hgrn(z, hg_lower_bound, hg_out_norm_g[e], e, hg_width // HG_HEAD_DIM, ctx_len)
            filt_params = (hy_filt_w1[e], hy_filt_b1[e], hy_filt_freq1[e], hy_filt_w2[e], hy_filt_b2[e],
                           hy_filt_freq2[e], hy_filt_w3[e])
            bb = _hyena(z, 5 * hg_width, hy_width, ctx_len, hy_short_w[e], hy_short_b[e], filt_params, hy_skip[e])
            w_out = ev_w_out[e]
        else:
            o = i // 2
            lam_init = 0.8 - 0.6 * math.exp(-0.3 * i)
            w_big, wuq, wukv = _odd_weights(od_w_in[o], mla_w_uq[o], mla_w_ukv[o], da_w, q_rank, kv_rank)
            qd, kd, vd, qm, km, vm = _in_odd(h, mods[i], norm_mix_g[i], w_big, wuq, wukv, mla_q_norm_g[o],
                                             mla_kv_norm_g[o], rope_tabs, tm, n_ctx_tiles, da_w)
            a = _attend_segments(qd, kd, vd, da_lambda[o], da_subln_g[o], ctx_len, True, lam_init, not last)
            bb = _attend_segments(qm, km, vm, da_lambda[o], da_subln_g[o], ctx_len, False, lam_init, not last)
            w_out = od_w_out[o]
        weights = (w_out.astype(BF16), ffn_w_gu[i].astype(BF16), ffn_w_down[i].astype(BF16))
        if not last:
            h = _post(h, a, bb, mods[i], norm_ffn_g[i], *weights, tm, n_ctx_tiles)
    if a.shape[1] != n_lat:
        a, bb = a[:, ctx_len:], bb[:, ctx_len:]
    return _post_final(h, a, bb, mods[depth - 1], norm_ffn_g[depth - 1], *weights, final_norm_g, tm, n_ctx_tiles)
```

```python
import functools
import math

import numpy as np
import jax
import jax.numpy as jnp
from jax import lax
from jax.experimental import pallas as pl
from jax.experimental.pallas import tpu as pltpu

F32 = jnp.float32
BF16 = jnp.bfloat16
HIGHEST = lax.Precision.HIGHEST

GRID_W = 64
EPS = 1e-6
ROPE_BASE = 10000.0
HG_HEAD_DIM = 128
HG_CHUNK = 64
HY_BANDS = 16
HY_EMB = 2 * HY_BANDS + 1
HY_TARGET = 1e-2
HY_STEEP_PCT = 0.3
HY_GENTLE_PCT = 1.5
DA_HEAD_DIM = 64
MLA_HEADS = 4
MLA_NOPE = 64
MLA_ROPE = 32
MLA_V = 128
LOG2_E = 1.4426950408889634

LANES = 128
SUBLANES = 8
BF16_SUBLANES = 16
V7X_VMEM_LIMIT_BYTES = 56 * 1024 * 1024
MOD_ROWS = 16
FFT_N2 = 128
FFT_PITCH = FFT_N2 + SUBLANES
FFT_UNROLL = 16
HGRN_LOCKSTEP = 1
HGRN_UNROLL = 2
ATTN_KEY_BLOCK = 256
ATTN_QUERY_TILE = 2048


def _params(*sem):
    return pltpu.CompilerParams(dimension_semantics=sem, vmem_limit_bytes=V7X_VMEM_LIMIT_BYTES)


def _dot(a, b, precision=None):
    return jnp.dot(a, b, preferred_element_type=F32, precision=precision)


def _dot3(a, b):
    ah, bh = a.astype(BF16), b.astype(BF16)
    al, bl = (a - ah.astype(F32)).astype(BF16), (b - bh.astype(F32)).astype(BF16)
    return _dot(ah, bh) + _dot(al, bh) + _dot(ah, bl)


def _dot_nt(a, b):
    return lax.dot_general(a, b, (((1,), (1,)), ((), ())), preferred_element_type=F32)


def _dot_tn(a, b):
    return lax.dot_general(a, b, (((0,), (0,)), ((), ())), preferred_element_type=F32)


def _rms(x):
    return x * lax.rsqrt(jnp.mean(x * x, axis=-1, keepdims=True) + EPS)


def _silu(x):
    return x * jax.nn.sigmoid(x)


def _tile_mod(mod_ref, modc_ref, n_ctx_tiles):
    return jnp.where(pl.program_id(1) < n_ctx_tiles, modc_ref[...], mod_ref[...])


def _ada_kernel(c_ref, w_ref, b_ref, o_ref):
    o_ref[0] = _dot(_silu(c_ref[...]), w_ref[0], HIGHEST) + b_ref[0]


def _ada(cc, ada_w, ada_b):
    depth, d, n = ada_w.shape
    rows = cc.shape[0]
    tn = n // 4
    return pl.pallas_call(
        _ada_kernel,
        out_shape=jax.ShapeDtypeStruct((depth, rows, n), F32),
        grid=(depth, n // tn),
        in_specs=[
            pl.BlockSpec((rows, d), lambda i, j: (0, 0)),
            pl.BlockSpec((1, d, tn), lambda i, j: (i, 0, j)),
            pl.BlockSpec((1, 1, tn), lambda i, j: (i, 0, j)),
        ],
        out_specs=pl.BlockSpec((1, rows, tn), lambda i, j: (i, 0, j)),
        compiler_params=_params("arbitrary", "arbitrary"),
        name="ada_mod",
    )(cc, ada_w, ada_b.reshape(depth, 1, n))


def _tok_spec(tm, width):
    return pl.BlockSpec((1, tm, width), lambda b, t: (b, t, 0))


def _const_spec(shape):
    zeros = (0,) * len(shape)
    return pl.BlockSpec(shape, lambda b, t: zeros)


def _mod_specs(d, n_batch):
    lat = pl.BlockSpec((None, 6, d), lambda b, t: (b, 0, 0))
    ctx = pl.BlockSpec((None, 6, d), lambda b, t: (n_batch, 0, 0))
    return lat, ctx


def _stream(h, tm, n_ctx_tiles):
    if not isinstance(h, tuple):
        b, t, d = h.shape
        return b, t, d, [_tok_spec(tm, d)], [h]
    ctx, x = h
    b, n_lat, d = x.shape
    specs = [pl.BlockSpec((1, tm, d), lambda bi, ti: (bi, jnp.minimum(ti, n_ctx_tiles - 1), 0)),
             pl.BlockSpec((1, tm, d), lambda bi, ti: (bi, jnp.maximum(ti - n_ctx_tiles, 0), 0))]
    return b, ctx.shape[1] + n_lat, d, specs, [ctx, x]


def _stream_tile(refs, n_ctx_tiles):
    if len(refs) == 1:
        return refs[0][0]
    return jnp.where(pl.program_id(1) < n_ctx_tiles, refs[0][0], refs[1][0])


def _in_even_kernel(*refs, n_ctx_tiles, n_stream):
    mod_ref, modc_ref, g_ref, w_ref, o_ref = refs[n_stream:]
    m = _tile_mod(mod_ref, modc_ref, n_ctx_tiles)
    xn = (_rms(_stream_tile(refs[:n_stream], n_ctx_tiles)) * g_ref[...]) * (1.0 + m[1:2]) + m[0:1]
    o_ref[0] = _dot(xn.astype(BF16), w_ref[...])


def _in_even(h, mods, g, w, tm, n_ctx_tiles):
    b, t, d, h_specs, h_args = _stream(h, tm, n_ctx_tiles)
    n = w.shape[1]
    lat, ctx = _mod_specs(d, b)
    return pl.pallas_call(
        functools.partial(_in_even_kernel, n_ctx_tiles=n_ctx_tiles, n_stream=len(h_args)),
        out_shape=jax.ShapeDtypeStruct((b, t, n), F32),
        grid=(b, t // tm),
        in_specs=h_specs + [lat, ctx, _const_spec((1, d)), _const_spec((d, n))],
        out_specs=_tok_spec(tm, n),
        compiler_params=_params("arbitrary", "arbitrary"),
        name="even_in_proj",
    )(*h_args, mods, mods, g.reshape(1, d), w)


def _post_body(m, h, a, bb, g_ref, wo_ref, wgu_ref, wd_ref, hidden):
    half = a.shape[-1]
    y = _dot(a, wo_ref[:half, :]) + _dot(bb, wo_ref[half:, :])
    h1 = h + m[2:3] * y
    xn = (_rms(h1) * g_ref[...]) * (1.0 + m[4:5]) + m[3:4]
    gu = _dot(xn.astype(BF16), wgu_ref[...])
    act = _silu(gu[:, :hidden]) * gu[:, hidden:]
    return h1 + m[5:6] * _dot(act.astype(BF16), wd_ref[...])


def _post_kernel(*refs, n_ctx_tiles, hidden, n_stream):
    a_ref, b_ref, mod_ref, modc_ref, g_ref, wo_ref, wgu_ref, wd_ref, o_ref = refs[n_stream:]
    m = _tile_mod(mod_ref, modc_ref, n_ctx_tiles)
    h = _stream_tile(refs[:n_stream], n_ctx_tiles)
    o_ref[0] = _post_body(m, h, a_ref[0], b_ref[0], g_ref, wo_ref, wgu_ref, wd_ref, hidden)


def _post_final_kernel(h_ref, a_ref, b_ref, mod_ref, g_ref, wo_ref, wgu_ref, wd_ref, fg_ref, o_ref,
                       *, hidden):
    h2 = _post_body(mod_ref[...], h_ref[0], a_ref[0], b_ref[0], g_ref, wo_ref, wgu_ref, wd_ref, hidden)
    o_ref[0] = _rms(h2) * fg_ref[...]


def _post(h, a, bb, mods, g, wo, wgu, wd, tm, n_ctx_tiles):
    b, t, d, h_specs, h_args = _stream(h, tm, n_ctx_tiles)
    half = a.shape[-1]
    hidden = wd.shape[0]
    lat, ctx = _mod_specs(d, b)
    return pl.pallas_call(
        functools.partial(_post_kernel, n_ctx_tiles=n_ctx_tiles, hidden=hidden, n_stream=len(h_args)),
        out_shape=jax.ShapeDtypeStruct((b, t, d), F32),
        grid=(b, t // tm),
        in_specs=h_specs + [_tok_spec(tm, half), _tok_spec(tm, half), lat, ctx,
                            _const_spec((1, d)), _const_spec((2 * half, d)), _const_spec((d, 2 * hidden)),
                            _const_spec((hidden, d))],
        out_specs=_tok_spec(tm, d),
        compiler_params=_params("arbitrary", "arbitrary"),
        name="out_proj_ffn",
    )(*h_args, a, bb, mods, mods, g.reshape(1, d), wo, wgu, wd)


def _post_final(h, a_lat, b_lat, mods, g, wo, wgu, wd, final_g, tm, n_ctx_tiles):
    b, t, d = h.shape
    n_lat = a_lat.shape[1]
    half = a_lat.shape[-1]
    hidden = wd.shape[0]
    lat, _ = _mod_specs(d, b)
    return pl.pallas_call(
        functools.partial(_post_final_kernel, hidden=hidden),
        out_shape=jax.ShapeDtypeStruct((b, n_lat, d), F32),
        grid=(b, n_lat // tm),
        in_specs=[pl.BlockSpec((1, tm, d), lambda bi, ti: (bi, ti + n_ctx_tiles, 0)),
                  _tok_spec(tm, half), _tok_spec(tm, half), lat,
                  _const_spec((1, d)), _const_spec((2 * half, d)), _const_spec((d, 2 * hidden)),
                  _const_spec((hidden, d)), _const_spec((1, d))],
        out_specs=_tok_spec(tm, d),
        compiler_params=_params("arbitrary", "arbitrary"),
        name="out_proj_ffn_final",
    )(h, a_lat, b_lat, mods, g.reshape(1, d), wo, wgu, wd, final_g.reshape(1, d))


def _hgrn_tables(tile):
    c = HG_CHUNK
    r = np.arange(c)
    masks, bms = [], []
    m = c
    while m >= 2:
        upper = ((r % m) >= m // 2)[:, None]
        masks.append(np.broadcast_to(upper, (c, LANES)))
        bms.append(((r[:, None] // m) == (r[None, :] // m)) & upper & (~upper.T))
        m //= 2
    fwd = (r[None, :] <= r[:, None]).astype(np.float32)
    mall = np.stack([fwd, fwd[::-1, ::-1]])
    mall = np.concatenate([mall, mall], axis=2)
    mu = np.stack(masks).astype(np.float32)
    masku = np.tile(np.stack([mu, mu[:, ::-1]]), (1, 1, tile // c, 1))
    bmp = np.stack([np.concatenate([bms[i], bms[i + 1]], axis=1) for i in range(0, len(bms), 2)])
    bmp_bwd = np.stack([np.concatenate([bms[i][::-1, ::-1], bms[i + 1][::-1, ::-1]], axis=1)
                        for i in range(0, len(bms), 2)])
    return (jnp.asarray(mall, dtype=BF16), jnp.asarray(masku, dtype=F32),
            jnp.asarray(np.stack([bmp, bmp_bwd]), dtype=F32))


def _hgrn_kernel(q_ref, ff_ref, fb_ref, i_ref, g_ref, lbp_ref, ng_ref, mall_ref, masku_ref, bmp_ref,
                 o_ref, of_ref, ob_ref, *, layer, n_ctx_tiles, n_tiles):
    c = HG_CHUNK
    dk = HG_HEAD_DIM
    n_levels = masku_ref.shape[1]
    tile = masku_ref.shape[2]
    n_sub = tile // c
    lbp = lbp_ref[...]
    ex = jnp.exp(lbp - jnp.max(lbp, axis=0, keepdims=True))
    p = ex / jnp.sum(ex, axis=0, keepdims=True)
    lb = jnp.sum(p[:layer + 1], axis=0) - p[0]

    def sub(x, g):
        return x[g * c:(g + 1) * c]

    dirs = (0, 1)

    def stack_chunks(x):
        return jnp.concatenate([sub(x, g) for g in range(n_sub)], axis=1)

    def bwd_tile(t):
        if isinstance(t, int):
            return n_ctx_tiles - 1 - t if t < n_ctx_tiles else n_tiles - 1 - (t - n_ctx_tiles)
        return jnp.where(t < n_ctx_tiles, n_ctx_tiles - 1 - t, n_tiles - 1 - (t - n_ctx_tiles))

    def tile_rows(ti):
        start = ti * tile
        return pl.ds(start if isinstance(ti, int) else pl.multiple_of(start, tile), tile)

    def run_trips(trips, sts):
        lanes = [(s, d) for s in range(len(trips)) for d in dirs]
        n = range(len(lanes))
        tis = [bwd_tile(t) if d else t for t in trips for d in dirs]
        rows = [tile_rows(ti) for ti in tis]
        q = [q_ref[0, r, :] for r in rows]
        v = [i_ref[0, r, :] for r in rows]
        lbd = [lb[d:d + 1] for _, d in lanes]
        zf = [(ff_ref, fb_ref)[d][0, rows[l], :] for l, (_, d) in enumerate(lanes)]
        f = [lbd[l] + (1.0 - lbd[l]) * jax.nn.sigmoid(zf[l]) for l in n]
        lf = [jnp.log(x) for x in f]
        k = [1.0 - x for x in f]
        hi = [x.astype(BF16) for x in lf]
        lo = [(lf[l] - hi[l].astype(F32)).astype(BF16) for l in n]
        e2 = [_dot(mall_ref[lanes[l][1]], jnp.concatenate([stack_chunks(hi[l]), stack_chunks(lo[l])], axis=0))
              for l in n]
        cum = [[e2[l][:, g * dk:(g + 1) * dk] for g in range(n_sub)] for l in n]

        def tail_decay(l, g):
            last = c - 1 if lanes[l][1] == 0 else 0
            return jnp.exp(cum[l][g][last:last + 1] - cum[l][g])

        def level_decay(l, lvl):
            m = c >> lvl
            ref = m // 2 - 1 if lanes[l][1] == 0 else m // 2
            parts = []
            for g in range(n_sub):
                cg = cum[l][g]
                if m >= SUBLANES:
                    x = cg.reshape(c // m, m, dk)
                    mid = jnp.broadcast_to(x[:, ref:ref + 1, :], x.shape)
                else:
                    x = cg.reshape(c // SUBLANES, SUBLANES, dk)
                    srow = lax.broadcasted_iota(jnp.int32, x.shape, 1)
                    mid = jnp.where(srow < m, jnp.broadcast_to(x[:, ref:ref + 1, :], x.shape),
                                    jnp.broadcast_to(x[:, m + ref:m + ref + 1, :], x.shape))
                parts.append(jnp.exp(-jnp.abs(cg - mid.reshape(c, dk))))
            return jnp.concatenate(parts, axis=0)

        g_in = [jnp.concatenate([jnp.exp(cum[l][g]) for g in range(n_sub)], axis=0) for l in n]
        qin = [(q[l] * g_in[l]).astype(BF16) for l in n]
        kout = [(k[l] * jnp.concatenate([tail_decay(l, g) for g in range(n_sub)], axis=0)).astype(BF16)
                for l in n]
        vb = [x.astype(BF16) for x in v]
        zero = jnp.zeros((c, dk), BF16)
        ds = [[_dot_tn(sub(vb[l], g), sub(kout[l], g)) for g in range(n_sub)] for l in n]
        qf = [q[l] * f[l] for l in n]
        w = [[None] * n_levels for _ in n]
        for lvl in range(n_levels):
            for l, (_, d) in enumerate(lanes):
                later = masku_ref[d, lvl] != 0.0
                if lvl == n_levels - 1:
                    w[l][lvl] = jnp.where(later, qf[l], k[l]).astype(BF16)
                else:
                    w[l][lvl] = (jnp.where(later, q[l], k[l]) * level_decay(l, lvl)).astype(BF16)
        scores = [[[None] * (n_levels // 2) for _ in range(n_sub)] for _ in n]
        for pair in range(n_levels // 2):
            lv_a, lv_b = 2 * pair, 2 * pair + 1
            for g in range(n_sub):
                for l, (_, d) in enumerate(lanes):
                    wa, wb = sub(w[l][lv_a], g), sub(w[l][lv_b], g)
                    rhs = jnp.concatenate([jnp.concatenate([wa, zero], axis=1),
                                           jnp.concatenate([zero, wb], axis=1)], axis=0)
                    raw = _dot_nt(jnp.concatenate([wa, wb], axis=1), rhs)
                    scores[l][g][pair] = raw * bmp_ref[d, pair]
        o = [jnp.concatenate([_dot(sum(scores[l][g]).astype(BF16),
                                   jnp.concatenate([sub(vb[l], g)] * 2, axis=0))
                              for g in range(n_sub)], axis=0)
             + jnp.sum(q[l] * k[l], axis=-1, keepdims=True) * v[l] for l in n]
        order = [list(range(n_sub)), list(reversed(range(n_sub)))]
        entering = [[None] * n_sub for _ in n]
        sts = list(sts)
        for l, (_, d) in enumerate(lanes):
            for g in order[d]:
                entering[l][g] = sts[d].astype(BF16)
                last = (g + 1) * c - 1 if d == 0 else g * c
                sts[d] = sts[d] * g_in[l][last:last + 1] + ds[l][g]
        for l, (_, d) in enumerate(lanes):
            out = jnp.concatenate([sub(o[l], g) + _dot_nt(sub(qin[l], g), entering[l][g])
                                   for g in range(n_sub)], axis=0)
            (of_ref, ob_ref)[d][rows[l], :] = out
        return tuple(sts)

    def body(i, carry):
        return run_trips([i * HGRN_LOCKSTEP + s for s in range(HGRN_LOCKSTEP)], carry)

    s0 = jnp.zeros((dk, dk), F32)
    sts = lax.fori_loop(0, n_tiles // HGRN_LOCKSTEP, body, (s0, s0), unroll=HGRN_UNROLL)
    rest = list(range(n_tiles - n_tiles % HGRN_LOCKSTEP, n_tiles))
    if rest:
        run_trips(rest, sts)
    o = _rms(of_ref[...] + ob_ref[...]) * ng_ref[...]
    o_ref[0] = (o * _silu(g_ref[0])).astype(o_ref.dtype)


def _hgrn(z, hg_lower_bound, ng, layer, n_heads, ctx_len):
    b, t, _ = z.shape
    dk = HG_HEAD_DIM
    n_even = hg_lower_bound.shape[0]
    tile = math.gcd(math.gcd(ctx_len, t - ctx_len), 256)
    mall, masku, bm = _hgrn_tables(tile)

    def col(kind):
        return pl.BlockSpec((1, t, dk), lambda bi, hi: (bi, 0, kind * n_heads + hi))

    def const(arr):
        zeros = (0,) * arr.ndim
        return pl.BlockSpec(arr.shape, lambda bi, hi: zeros)

    return pl.pallas_call(
        functools.partial(_hgrn_kernel, layer=layer, n_ctx_tiles=ctx_len // tile, n_tiles=t // tile),
        out_shape=jax.ShapeDtypeStruct((b, t, n_heads * dk), BF16),
        grid=(b, n_heads),
        in_specs=[col(0), col(1), col(2), col(3), col(4),
                  pl.BlockSpec((n_even, 2, dk), lambda bi, hi: (0, 0, hi)),
                  const(ng.reshape(1, dk)), const(mall), const(masku), const(bm)],
        out_specs=pl.BlockSpec((1, t, dk), lambda bi, hi: (bi, 0, hi)),
        scratch_shapes=[pltpu.VMEM((t, dk), F32), pltpu.VMEM((t, dk), F32)],
        compiler_params=_params("arbitrary", "arbitrary"),
        name="hgrn2_scan",
    )(z, z, z, z, z, hg_lower_bound, ng.reshape(1, dk), mall, masku, bm)


def _hy_pre_kernel(x0_ref, x1_ref, v_ref, w0_ref, w1_ref, wv_ref, b0_ref, b1_ref, bv_ref,
                   x0c_ctx_ref, x0c_lat_ref, u_ctx_ref, u_lat_ref, *, ctx_len):
    t = x0_ref.shape[1]
    row = lax.broadcasted_iota(jnp.int32, (t, LANES), 0)
    first = (row == 0) | (row == ctx_len)
    final = (row == ctx_len - 1) | (row == t - 1)

    def short_conv(z_ref, w_ref, b_ref):
        z = z_ref[0]
        prev = jnp.where(first, 0.0, pltpu.roll(z, 1, 0))
        nxt = jnp.where(final, 0.0, pltpu.roll(z, t - 1, 0))
        w = w_ref[...]
        return prev * w[0:1] + z * w[1:2] + nxt * w[2:3] + b_ref[...]

    x0c = short_conv(x0_ref, w0_ref, b0_ref).astype(x0c_ctx_ref.dtype)
    u = (short_conv(x1_ref, w1_ref, b1_ref) * short_conv(v_ref, wv_ref, bv_ref)).astype(u_ctx_ref.dtype)
    x0c_ctx_ref[0] = x0c[:ctx_len]
    x0c_lat_ref[0] = x0c[ctx_len:]
    u_ctx_ref[0] = u[:ctx_len]
    u_lat_ref[0] = u[ctx_len:]


def _hy_pre(z, short_w, short_b, col0, width, ctx_len):
    b, t, _ = z.shape
    nb = width // LANES
    c0 = col0 // LANES

    def zcol(kind):
        return pl.BlockSpec((1, t, LANES), lambda bi, j: (bi, 0, c0 + kind * nb + j))

    def wcol(kind, rows):
        return pl.BlockSpec((rows, LANES), lambda bi, j: (0, kind * nb + j))

    def seg(rows):
        return (jax.ShapeDtypeStruct((b, rows, width), BF16),
                pl.BlockSpec((1, rows, LANES), lambda bi, j: (bi, 0, j)))

    (ctx_shape, ctx_spec), (lat_shape, lat_spec) = seg(ctx_len), seg(t - ctx_len)
    sb = short_b.reshape(1, -1)
    return pl.pallas_call(
        functools.partial(_hy_pre_kernel, ctx_len=ctx_len),
        out_shape=(ctx_shape, lat_shape, ctx_shape, lat_shape),
        grid=(b, nb),
        in_specs=[zcol(0), zcol(1), zcol(2), wcol(0, 3), wcol(1, 3), wcol(2, 3),
                  wcol(0, 1), wcol(1, 1), wcol(2, 1)],
        out_specs=(ctx_spec, lat_spec, ctx_spec, lat_spec),
        compiler_params=_params("arbitrary", "arbitrary"),
        name="hyena_short_conv",
    )(z, z, z, short_w, short_w, short_w, sb, sb, sb)


def _filter_features(length, n):
    p = np.arange(n)
    is_f = p < length
    is_b = p > n - length
    lag = np.where(is_f, p, np.where(is_b, n - 1 - p, 0))
    tt = np.linspace(0.0, 1.0, length, dtype=np.float32)[lag][:, None]
    w = (2.0 * math.pi * lag.astype(np.float32) / length)[:, None].astype(np.float32)
    bands = np.linspace(1e-4, HY_BANDS - 1, HY_BANDS, dtype=np.float32)[None, :]
    feat = np.concatenate([tt, np.cos(bands * w), -np.sin(bands * w), is_f[:, None], is_b[:, None]],
                          axis=-1).astype(np.float32)
    pad = (-feat.shape[1]) % BF16_SUBLANES
    return jnp.asarray(np.pad(feat, ((0, 0), (0, pad))))


def _filt_kernel(feat_ref, w1_ref, b1_ref, fr1_ref, w2_ref, b2_ref, fr2_ref,
                 w3f_ref, w3b_ref, dl_ref, o_ref, hid_ref):
    z = feat_ref[...]
    mf = z[:, HY_EMB:HY_EMB + 1]
    mb = z[:, HY_EMB + 1:HY_EMB + 2]

    @pl.when(pl.program_id(0) == 0)
    def _():
        h1 = jnp.sin(fr1_ref[...] * (_dot3(z, w1_ref[...]) + b1_ref[...]))
        hid_ref[...] = jnp.sin(fr2_ref[...] * (_dot3(h1, w2_ref[...]) + b2_ref[...]))

    hid = hid_ref[...]
    hf = _dot3(hid, w3f_ref[...])
    hb = _dot3(hid, w3b_ref[...])
    win = jnp.exp(-z[:, 0:1] * dl_ref[...])
    f = (mf * hf + mb * hb) * win
    o_ref[...] = f / jnp.sum(jnp.abs(f), axis=0, keepdims=True)


def _hyena_filter(length, n, w1, b1, fr1, w2, b2, fr2, w3, width):
    feat = _filter_features(length, n)
    nf = feat.shape[1]
    hid = w1.shape[1]
    w1p = jnp.pad(w1, ((0, nf - w1.shape[0]), (0, 0)))
    d_lo = -math.log(HY_TARGET) / HY_GENTLE_PCT
    d_hi = -math.log(HY_TARGET) / HY_STEEP_PCT
    deltas = jnp.asarray(np.linspace(d_lo, d_hi, width, dtype=np.float32)[None, :])
    nb = width // LANES

    def full(shape):
        zeros = (0,) * len(shape)
        return pl.BlockSpec(shape, lambda j: zeros)

    return pl.pallas_call(
        _filt_kernel,
        out_shape=jax.ShapeDtypeStruct((n, width), F32),
        grid=(nb,),
        in_specs=[full((n, nf)), full((nf, hid)), full((1, hid)),
                  full((1, hid)), full((hid, hid)), full((1, hid)), full((1, hid)),
                  pl.BlockSpec((hid, LANES), lambda j: (0, j)),
                  pl.BlockSpec((hid, LANES), lambda j: (0, nb + j)),
                  pl.BlockSpec((1, LANES), lambda j: (0, j))],
        out_specs=pl.BlockSpec((n, LANES), lambda j: (0, j)),
        scratch_shapes=[pltpu.VMEM((n, hid), F32)],
        compiler_params=_params("arbitrary"),
        name="hyena_filter_mlp",
    )(feat, w1p, b1.reshape(1, -1), fr1.reshape(1, -1), w2, b2.reshape(1, -1),
      fr2.reshape(1, -1), w3, w3, deltas)


def _dft_tables(n1, n1_in):
    n2 = FFT_N2
    n = n1 * n2
    a = np.arange(n1)
    j = np.arange(n2)
    ang = -2.0 * np.pi * (a[None, None, :] * a[None, :, None] / n1 + j[:, None, None] * a[None, :, None] / n)
    tr, ti = np.cos(ang), np.sin(ang)
    fwd_a = np.concatenate([np.concatenate([tr, -ti], 2), np.concatenate([ti, tr], 2)], 1)
    trt, tit = np.swapaxes(tr, 1, 2), -np.swapaxes(ti, 1, 2)
    inv_a = np.concatenate([np.concatenate([trt, -tit], 2), np.concatenate([tit, trt], 2)], 1)
    keep = np.concatenate([np.arange(n1_in), n1 + np.arange(n1_in)])
    ang2 = -2.0 * np.pi * (j[:, None] * j[None, :]) / n2
    cr, ci = np.cos(ang2), np.sin(ang2)
    fwd_c = np.block([[cr, -ci], [ci, cr]])
    inv_c = np.block([[cr, ci], [-ci, cr]])
    real_a = np.concatenate([tr, ti], 1)
    return dict(fwd_a=fwd_a[:, :, keep], inv_a=inv_a[:, keep, :], fwd_c=fwd_c, inv_c=inv_c,
                real_a=real_a)


def _slab(idx):
    return pl.ds(pl.multiple_of(idx * FFT_PITCH, SUBLANES), FFT_N2)


def _fft_filter_kernel(f_ref, wa_ref, wc_ref, o_ref, scr_ref, *, n1):
    n2 = FFT_N2
    rows = 2 * n1
    scale = 1.0 / (n1 * n2)

    def split3(x):
        hi = x.astype(BF16)
        lo = (x - hi.astype(F32)).astype(BF16)
        return jnp.concatenate([hi, lo, hi], axis=0)

    def stage_a(j, carry):
        scr_ref[pl.ds(j, rows, stride=FFT_PITCH), :] = _dot(wa_ref[j], split3(f_ref[j]))
        return carry

    lax.fori_loop(0, n2, stage_a, 0, unroll=FFT_UNROLL)

    def stage_c(k, carry):
        x = jnp.concatenate([scr_ref[_slab(k), :], scr_ref[_slab(n1 + k), :]], axis=0)
        o_ref[k] = _dot(wc_ref[...], split3(x)) * scale
        return carry

    lax.fori_loop(0, n1, stage_c, 0, unroll=FFT_UNROLL)


def _split3_cols(w):
    hi = w.astype(np.float32).astype(jnp.bfloat16)
    lo = (w.astype(np.float32) - np.asarray(hi, np.float32)).astype(jnp.bfloat16)
    return jnp.asarray(np.concatenate([hi, hi, lo], axis=-1))


def _fft_filter(filt, n1, tables):
    n2 = FFT_N2
    width = filt.shape[1]
    ft = filt.reshape(n1, n2, width).transpose(1, 0, 2)
    wa = _split3_cols(tables["real_a"])
    wc = _split3_cols(tables["fwd_c"])
    return pl.pallas_call(
        functools.partial(_fft_filter_kernel, n1=n1),
        out_shape=jax.ShapeDtypeStruct((n1, 2 * n2, width), F32),
        grid=(width // LANES,),
        in_specs=[pl.BlockSpec((n2, n1, LANES), lambda c: (0, 0, c)),
                  pl.BlockSpec(wa.shape, lambda c: (0, 0, 0)),
                  pl.BlockSpec(wc.shape, lambda c: (0, 0))],
        out_specs=pl.BlockSpec((n1, 2 * n2, LANES), lambda c: (0, 0, c)),
        scratch_shapes=[pltpu.VMEM((2 * n1 * FFT_PITCH, LANES), F32)],
        compiler_params=_params("arbitrary"),
        name="hyena_filter_dft",
    )(ft, wa, wc)


def _fftconv_kernel(u_ref, x0_ref, fh_ref, skip_ref, wfa_ref, wfc_ref, wic_ref, wia_ref, o_ref,
                    scr_ref, *, n1):
    n2 = FFT_N2
    rows = 2 * n1

    def stage_a(j, carry):
        scr_ref[pl.ds(j, rows, stride=FFT_PITCH), :] = _dot(wfa_ref[j], u_ref[0, j])
        return carry

    lax.fori_loop(0, n2, stage_a, 0, unroll=FFT_UNROLL)

    def stage_c(k, carry):
        x = jnp.concatenate([scr_ref[_slab(k), :], scr_ref[_slab(n1 + k), :]], axis=0)
        xf = _dot(wfc_ref[...], x.astype(BF16))
        fh = fh_ref[k]
        xr, xi, fr, fi = xf[:n2], xf[n2:], fh[:n2], fh[n2:]
        y = jnp.concatenate([xr * fr - xi * fi, xr * fi + xi * fr], axis=0)
        zt = _dot(wic_ref[...], y.astype(BF16))
        scr_ref[_slab(k), :] = zt[:n2]
        scr_ref[_slab(n1 + k), :] = zt[n2:]
        return carry

    lax.fori_loop(0, n1, stage_c, 0, unroll=FFT_UNROLL)

    def stage_ai(j, carry):
        x = scr_ref[pl.ds(j, rows, stride=FFT_PITCH), :]
        y = _dot(wia_ref[j], x.astype(BF16))
        u = u_ref[0, j].astype(F32)
        o_ref[0, j] = (x0_ref[0, j].astype(F32) * (y + skip_ref[...] * u)).astype(o_ref.dtype)
        return carry

    lax.fori_loop(0, n2, stage_ai, 0, unroll=FFT_UNROLL)


def _to_fft_layout(x, n1_in):
    b, length, c = x.shape
    n1_used = length // FFT_N2
    x = x.reshape(b // 2, 2, n1_used, FFT_N2, c)
    x = jnp.pad(x, ((0, 0), (0, 0), (0, n1_in - n1_used), (0, 0), (0, 0)))
    return x.transpose(0, 3, 1, 2, 4).reshape(b // 2, FFT_N2, 2 * n1_in, c)


def _from_fft_layout(y, length):
    p, n2, rows, c = y.shape
    n1_in = rows // 2
    y = y.reshape(p, n2, 2, n1_in, c).transpose(0, 2, 3, 1, 4)
    return y.reshape(2 * p, n1_in * n2, c)[:, :length]


def _fftconv(u, x0c, fh, skip, n1, n1_in, tables):
    _, length, width = u.shape
    ut = _to_fft_layout(u, n1_in)
    xt = _to_fft_layout(x0c, n1_in)
    pairs, n2, rin, _ = ut.shape
    wfa = jnp.asarray(tables["fwd_a"], dtype=BF16)
    wia = jnp.asarray(tables["inv_a"], dtype=BF16)
    wfc = jnp.asarray(tables["fwd_c"], dtype=BF16)
    wic = jnp.asarray(tables["inv_c"], dtype=BF16)
    data = pl.BlockSpec((1, n2, rin, LANES), lambda c, p: (p, 0, 0, c))

    def const(arr):
        zeros = (0,) * arr.ndim
        return pl.BlockSpec(arr.shape, lambda c, p: zeros, pipeline_mode=pl.Buffered(1))

    out = pl.pallas_call(
        functools.partial(_fftconv_kernel, n1=n1),
        out_shape=jax.ShapeDtypeStruct(ut.shape, BF16),
        grid=(width // LANES, pairs),
        in_specs=[data, data,
                  pl.BlockSpec((n1, 2 * n2, LANES), lambda c, p: (0, 0, c)),
                  pl.BlockSpec((1, LANES), lambda c, p: (0, c)),
                  const(wfa), const(wfc), const(wic), const(wia)],
        out_specs=data,
        scratch_shapes=[pltpu.VMEM((2 * n1 * FFT_PITCH, LANES), F32)],
        compiler_params=_params("arbitrary", "arbitrary"),
        name="hyena_dft_conv",
    )(ut, xt, fh, skip.reshape(1, width), wfa, wfc, wic, wia)
    return _from_fft_layout(out, length)


def _hyena(z, col0, width, ctx_len, short_w, short_b, filt_params, skip):
    x0c_ctx, x0c_lat, u_ctx, u_lat = _hy_pre(z, short_w, short_b, col0, width, ctx_len)
    outs = []
    for x0c, u in ((x0c_ctx, u_ctx), (x0c_lat, u_lat)):
        length = u.shape[1]
        n1_in = max(length // FFT_N2, SUBLANES)
        n1 = 2 * n1_in
        tables = _dft_tables(n1, n1_in)
        filt = _hyena_filter(length, n1 * FFT_N2, *filt_params, width)
        fh = _fft_filter(filt, n1, tables)
        outs.append(_fftconv(u, x0c, fh, skip, n1, n1_in, tables))
    return jnp.concatenate(outs, axis=1)


def _rope_tables(n_lat, ctx_len):
    tok = np.arange(n_lat)
    row, colp = tok // GRID_W, tok % GRID_W

    def axial(half):
        inv = ROPE_BASE ** (-np.arange(half, dtype=np.float32) / half)
        parts_c, parts_s = [], []
        for pos in (row, colp):
            ang = pos.astype(np.float32)[:, None] * inv
            parts_c += [np.cos(ang), np.cos(ang)]
            parts_s += [-np.sin(ang), np.sin(ang)]
        return np.concatenate(parts_c, 1), np.concatenate(parts_s, 1)

    dc, ds = axial(DA_HEAD_DIM // 4)
    mc, ms = axial(MLA_ROPE // 4)
    ones, zeros = np.ones((n_lat, MLA_NOPE), np.float32), np.zeros((n_lat, MLA_NOPE), np.float32)
    padc = np.ones((n_lat, LANES - MLA_NOPE - MLA_ROPE), np.float32)
    tabs = [np.concatenate([dc, dc], 1), np.concatenate([ds, ds], 1),
            np.concatenate([ones, mc, padc], 1), np.concatenate([zeros, ms, 0 * padc], 1)]
    out = []
    for i, tb in enumerate(tabs):
        ctx_rows = np.ones((ctx_len, LANES), np.float32) if i % 2 == 0 else np.zeros((ctx_len, LANES), np.float32)
        out.append(jnp.asarray(np.concatenate([ctx_rows, tb.astype(np.float32)], 0)))
    return out


def _with_ones(v):
    ones = jnp.ones((v.shape[0], LANES), v.dtype)
    parts = []
    for hd in range(v.shape[1] // LANES):
        parts += [v[:, hd * LANES:(hd + 1) * LANES], ones]
    return jnp.concatenate(parts, axis=1)


def _in_odd_kernel(x_ref, mod_ref, modc_ref, g_ref, w_ref, dc_ref, ds_ref, mc_ref, ms_ref,
                   qg_ref, kvg_ref, wuq_ref, wukv_ref,
                   qd_ref, kd_ref, vd_ref, qm_ref, km_ref, vm_ref, *, n_ctx_tiles, da_w, q_rank, kv_rank):
    m = _tile_mod(mod_ref, modc_ref, n_ctx_tiles)
    xn = (_rms(x_ref[0]) * g_ref[...]) * (1.0 + m[1:2]) + m[0:1]
    z = _dot(xn.astype(BF16), w_ref[...])
    nrep = da_w // LANES
    dc = jnp.concatenate([dc_ref[...]] * nrep, axis=1)
    ds = jnp.concatenate([ds_ref[...]] * nrep, axis=1)
    mc = jnp.concatenate([mc_ref[...]] * MLA_HEADS, axis=1)
    ms = jnp.concatenate([ms_ref[...]] * MLA_HEADS, axis=1)
    sa = DA_HEAD_DIM ** -0.5 * LOG2_E
    sm = (MLA_NOPE + MLA_ROPE) ** -0.5 * LOG2_E
    lane = lax.broadcasted_iota(jnp.int32, (x_ref.shape[1], LANES), 1)

    def partner(x, half):
        first = (lane % (2 * half)) < half
        cols = []
        for j in range(x.shape[1] // LANES):
            xj = x[:, j * LANES:(j + 1) * LANES]
            cols.append(jnp.where(first, pltpu.roll(xj, LANES - half, 1), pltpu.roll(xj, half, 1)))
        return cols[0] if len(cols) == 1 else jnp.concatenate(cols, axis=1)

    o = 0
    zq = z[:, o:o + da_w]
    qd_ref[0] = ((zq * dc + partner(zq, DA_HEAD_DIM // 4) * ds) * sa).astype(BF16)
    o += da_w
    zk = z[:, o:o + da_w]
    kd_ref[0] = (zk * dc + partner(zk, DA_HEAD_DIM // 4) * ds).astype(BF16)
    o += da_w
    vd_ref[0] = _with_ones(z[:, o:o + da_w]).astype(BF16)
    o += da_w
    cq = _rms(z[:, o:o + q_rank]) * qg_ref[...]
    o += q_rank
    ckv = _rms(z[:, o:o + kv_rank]) * kvg_ref[...]
    o += kv_rank
    zr = z[:, o:o + LANES]
    kr = zr * mc_ref[...] + partner(zr, MLA_ROPE // 4) * ms_ref[...]
    mw = MLA_HEADS * LANES
    qu = _dot(cq.astype(BF16), wuq_ref[...])
    qm_ref[0] = ((qu * mc + partner(qu, MLA_ROPE // 4) * ms) * sm).astype(BF16)
    kvu = _dot(ckv.astype(BF16), wukv_ref[...])
    km_ref[0] = (kvu[:, :mw] + jnp.concatenate([kr] * MLA_HEADS, axis=1)).astype(BF16)
    vm_ref[0] = _with_ones(kvu[:, mw:]).astype(BF16)


def _odd_weights(w_in, w_uq, w_ukv, da_w, q_rank, kv_rank):
    o = 3 * da_w + q_rank + kv_rank
    krw = jnp.pad(w_in[:, o:], ((0, 0), (MLA_NOPE, LANES - MLA_NOPE - MLA_ROPE)))
    w_big = jnp.concatenate([w_in[:, :o], krw], axis=1)
    dq = MLA_NOPE + MLA_ROPE
    uq = jnp.pad(w_uq.reshape(q_rank, MLA_HEADS, dq), ((0, 0), (0, 0), (0, LANES - dq)))
    ukv = w_ukv.reshape(kv_rank, MLA_HEADS, MLA_NOPE + MLA_V)
    uk = jnp.pad(ukv[:, :, :MLA_NOPE], ((0, 0), (0, 0), (0, LANES - MLA_NOPE))).reshape(kv_rank, MLA_HEADS * LANES)
    uv = ukv[:, :, MLA_NOPE:].reshape(kv_rank, MLA_HEADS * MLA_V)
    return (w_big.astype(BF16), uq.reshape(q_rank, MLA_HEADS * LANES).astype(BF16),
            jnp.concatenate([uk, uv], axis=1).astype(BF16))


def _in_odd(h, mods, g, w_big, wuq, wukv, qg, kvg, tabs, tm, n_ctx_tiles, da_w):
    b, t, d = h.shape
    q_rank, kv_rank = qg.shape[0], kvg.shape[0]
    lat, ctx = _mod_specs(d, b)
    tab = pl.BlockSpec((tm, LANES), lambda bi, ti: (ti, 0))
    mw = MLA_HEADS * LANES

    def tok_major(width):
        return jax.ShapeDtypeStruct((b, t, width), BF16), _tok_spec(tm, width)

    outs, ospecs = zip(tok_major(da_w), tok_major(da_w), tok_major(2 * da_w),
                       tok_major(mw), tok_major(mw), tok_major(2 * mw))
    return pl.pallas_call(
        functools.partial(_in_odd_kernel, n_ctx_tiles=n_ctx_tiles, da_w=da_w, q_rank=q_rank, kv_rank=kv_rank),
        out_shape=tuple(outs),
        grid=(b, t // tm),
        in_specs=[_tok_spec(tm, d), lat, ctx, _const_spec((1, d)), _const_spec(w_big.shape),
                  tab, tab, tab, tab, _const_spec((1, q_rank)), _const_spec((1, kv_rank)),
                  _const_spec(wuq.shape), _const_spec(wukv.shape)],
        out_specs=tuple(ospecs),
        compiler_params=_params("arbitrary", "arbitrary"),
        name="odd_in_proj",
    )(h, mods, mods, g.reshape(1, d), w_big, *tabs, qg.reshape(1, -1), kvg.reshape(1, -1), wuq, wukv)


def _softmax_pv(qs, k_ref, v_ref):
    n_keys = k_ref.shape[1]
    starts = list(range(0, n_keys, ATTN_KEY_BLOCK))
    m = [None] * len(qs)
    acc = [None] * len(qs)
    for start in starts:
        size = min(ATTN_KEY_BLOCK, n_keys - start)
        k = k_ref[0, start:start + size, :]
        v1 = v_ref[0, start:start + size, :]
        for i, q in enumerate(qs):
            s = _dot_nt(q, k)
            m_blk = jnp.max(s, axis=-1, keepdims=True)
            if start == 0:
                m[i] = m_blk
                acc[i] = _dot(jnp.exp2(s - m_blk).astype(BF16), v1)
            else:
                m_new = jnp.maximum(m[i], m_blk)
                pv = _dot(jnp.exp2(s - m_new).astype(BF16), v1)
                acc[i] = jnp.exp2(m[i] - m_new) * acc[i] + pv
                m[i] = m_new
    return [a[:, :LANES] / a[:, LANES:] for a in acc]


def _attn_kernel(q_ref, k_ref, v_ref, lam_ref, sg_ref, *rest, diff, lam_init):
    o_ref = rest[-1]
    q = q_ref[0]
    if not diff:
        o_ref[0] = _softmax_pv([q], k_ref, v_ref)[0].astype(o_ref.dtype)
        return
    first = lax.broadcasted_iota(jnp.int32, q.shape, 1) < DA_HEAD_DIM
    zero = jnp.zeros_like(q)
    o1, o2 = _softmax_pv([jnp.where(first, q, zero), jnp.where(first, zero, q)], k_ref, v_ref)
    lp = lam_ref[...]
    lam = (jnp.exp(jnp.sum(lp[0:1] * lp[1:2], axis=-1, keepdims=True))
           - jnp.exp(jnp.sum(lp[2:3] * lp[3:4], axis=-1, keepdims=True)) + lam_init)
    o_ref[0] = (_rms(o1 - lam * o2) * sg_ref[...] * (1.0 - lam_init)).astype(o_ref.dtype)


def _attention(q, k, v1, lam_p, subln_g, q_rows, tq, n_keys, out_rows, out_start, diff, lam_init, earlier=None):
    b, _, width = q.shape
    heads = width // LANES
    q_start, q_stop = q_rows

    def rows_from(start):
        return pl.BlockSpec((pl.Element(1), pl.Element(tq), pl.Element(LANES)),
                            lambda bi, hi, ti: (bi, pl.multiple_of(start + ti * tq, math.gcd(start, tq)),
                                                pl.multiple_of(hi * LANES, LANES)))

    khead = pl.BlockSpec((1, n_keys, LANES), lambda bi, hi, ti: (bi, 0, hi))
    vhead = pl.BlockSpec((1, n_keys, 2 * LANES), lambda bi, hi, ti: (bi, 0, hi))

    def const(shape):
        zeros = (0,) * len(shape)
        return pl.BlockSpec(shape, lambda bi, hi, ti: zeros)

    in_specs = [rows_from(q_start), khead, vhead, const(lam_p.shape), const((1, LANES))]
    args = [q, k, v1, lam_p, subln_g.reshape(1, LANES)]
    aliases = {}
    if earlier is not None:
        in_specs.append(pl.BlockSpec(memory_space=pl.ANY))
        args.append(earlier)
        aliases = {len(args) - 1: 0}
    return pl.pallas_call(
        functools.partial(_attn_kernel, diff=diff, lam_init=lam_init),
        out_shape=jax.ShapeDtypeStruct((b, out_rows, width), BF16),
        grid=(b, heads, (q_stop - q_start) // tq),
        in_specs=in_specs,
        out_specs=rows_from(out_start),
        input_output_aliases=aliases,
        compiler_params=_params("arbitrary", "arbitrary", "arbitrary"),
        name="diff_attention" if diff else "mla_attention",
    )(*args)


def _attend_segments(q, k, v1, lam_p, subln_g, ctx_len, diff, lam_init, need_ctx):
    t = k.shape[1]
    n_lat = t - ctx_len
    tq_lat = math.gcd(n_lat, ATTN_QUERY_TILE)
    if not need_ctx:
        return _attention(q, k, v1, lam_p, subln_g, (ctx_len, t), tq_lat, t, n_lat, 0, diff, lam_init)
    o = _attention(q, k, v1, lam_p, subln_g, (ctx_len, t), tq_lat, t, t, ctx_len, diff, lam_init)
    return _attention(q, k, v1, lam_p, subln_g, (0, ctx_len), ctx_len, ctx_len, t, 0, diff, lam_init, earlier=o)


def kernel(x, c, ctx, c_ctx, ada_w, ada_b, norm_mix_g, norm_ffn_g, ffn_w_gu, ffn_w_down, ev_w_in, ev_w_out, hg_lower_bound, hg_out_norm_g, hy_short_w, hy_short_b, hy_filt_w1, hy_filt_b1, hy_filt_freq1, hy_filt_w2, hy_filt_b2, hy_filt_freq2, hy_filt_w3, hy_skip, od_w_in, od_w_out, da_lambda, da_subln_g, mla_q_norm_g, mla_w_uq, mla_kv_norm_g, mla_w_ukv, final_norm_g):
    n_batch, n_lat, d = x.shape
    ctx_len = ctx.shape[1]
    depth = ada_w.shape[0]
    assert n_batch % 2 == 0 and n_batch < MOD_ROWS
    assert n_lat % GRID_W == 0 and ctx_len % HG_CHUNK == 0 and n_lat % FFT_N2 == 0 and ctx_len % FFT_N2 == 0
    tm = math.gcd(math.gcd(ctx_len, n_lat), 256)
    n_ctx_tiles = ctx_len // tm
    hg_width = d // 2
    hy_width = d - hg_width
    da_w = d // 2
    q_rank, kv_rank = mla_q_norm_g.shape[1], mla_kv_norm_g.shape[1]

    cc = jnp.concatenate([c, c_ctx[None], jnp.zeros((MOD_ROWS - n_batch - 1, d), F32)], axis=0)
    mods = _ada(cc, ada_w, ada_b).reshape(depth, MOD_ROWS, 6, d)
    h = (ctx, x) if depth > 1 else jnp.concatenate([ctx, x], axis=1)
    rope_tabs = _rope_tables(n_lat, ctx_len)

    for i in range(depth):
        last = i == depth - 1
        if i % 2 == 0:
            e = i // 2
            z = _in_even(h, mods[i], norm_mix_g[i], ev_w_in[e].astype(BF16), tm, n_ctx_tiles)
            a = _hgrn(z, hg_lower_bound, hg_out_norm_g[e], e, hg_width // HG_HEAD_DIM, ctx_len)
            filt_params = (hy_filt_w1[e], hy_filt_b1[e], hy_filt_freq1[e], hy_filt_w2[e], hy_filt_b2[e],
                           hy_filt_freq2[e], hy_filt_w3[e])
            bb = _hyena(z, 5 * hg_width, hy_width, ctx_len, hy_short_w[e], hy_short_b[e], filt_params, hy_skip[e])
            w_out = ev_w_out[e]
        else:
            o = i // 2
            lam_init = 0.8 - 0.6 * math.exp(-0.3 * i)
            w_big, wuq, wukv = _odd_weights(od_w_in[o], mla_w_uq[o], mla_w_ukv[o], da_w, q_rank, kv_rank)
            qd, kd, vd, qm, km, vm = _in_odd(h, mods[i], norm_mix_g[i], w_big, wuq, wukv, mla_q_norm_g[o],
                                             mla_kv_norm_g[o], rope_tabs, tm, n_ctx_tiles, da_w)
            a = _attend_segments(qd, kd, vd, da_lambda[o], da_subln_g[o], ctx_len, True, lam_init, not last)
            bb = _attend_segments(qm, km, vm, da_lambda[o], da_subln_g[o], ctx_len, False, lam_init, not last)
            w_out = od_w_out[o]
        weights = (w_out.astype(BF16), ffn_w_gu[i].astype(BF16), ffn_w_down[i].astype(BF16))
        if not last:
            h = _post(h, a, bb, mods[i], norm_ffn_g[i], *weights, tm, n_ctx_tiles)
    if a.shape[1] != n_lat:
        a, bb = a[:, ctx_len:], bb[:, ctx_len:]
    return _post_final(h, a, bb, mods[depth - 1], norm_ffn_g[depth - 1], *weights, final_norm_g, tm, n_ctx_tiles)
```

```python
import functools
import math

import numpy as np
import jax
import jax.numpy as jnp
from jax import lax
from jax.experimental import pallas as pl
from jax.experimental.pallas import tpu as pltpu

F32 = jnp.float32
BF16 = jnp.bfloat16
HIGHEST = lax.Precision.HIGHEST

GRID_W = 64
EPS = 1e-6
ROPE_BASE = 10000.0
HG_HEAD_DIM = 128
HG_CHUNK = 64
HY_BANDS = 16
HY_EMB = 2 * HY_BANDS + 1
HY_TARGET = 1e-2
HY_STEEP_PCT = 0.3
HY_GENTLE_PCT = 1.5
DA_HEAD_DIM = 64
MLA_HEADS = 4
MLA_NOPE = 64
MLA_ROPE = 32
MLA_V = 128
LOG2_E = 1.4426950408889634

LANES = 128
SUBLANES = 8
BF16_SUBLANES = 16
V7X_VMEM_LIMIT_BYTES = 56 * 1024 * 1024
MOD_ROWS = 16
FFT_N2 = 128
FFT_PITCH = FFT_N2 + SUBLANES
FFT_UNROLL = 16
HGRN_LOCKSTEP = 1
HGRN_UNROLL = 2
ATTN_KEY_BLOCK = 256
ATTN_QUERY_TILE = 2048


def _params(*sem):
    return pltpu.CompilerParams(dimension_semantics=sem, vmem_limit_bytes=V7X_VMEM_LIMIT_BYTES)


def _dot(a, b, precision=None):
    return jnp.dot(a, b, preferred_element_type=F32, precision=precision)


def _dot3(a, b):
    ah, bh = a.astype(BF16), b.astype(BF16)
    al, bl = (a - ah.astype(F32)).astype(BF16), (b - bh.astype(F32)).astype(BF16)
    return _dot(ah, bh) + _dot(al, bh) + _dot(ah, bl)


def _dot_nt(a, b):
    return lax.dot_general(a, b, (((1,), (1,)), ((), ())), preferred_element_type=F32)


def _dot_tn(a, b):
    return lax.dot_general(a, b, (((0,), (0,)), ((), ())), preferred_element_type=F32)


def _rms(x):
    return x * lax.rsqrt(jnp.mean(x * x, axis=-1, keepdims=True) + EPS)


def _neg_abs(x):
    bits = lax.bitcast_convert_type(x, jnp.int32) | jnp.int32(-2 ** 31)
    return lax.bitcast_convert_type(bits, F32)


def _silu(x):
    return x * jax.nn.sigmoid(x)


def _tile_mod(mod_ref, modc_ref, n_ctx_tiles):
    return jnp.where(pl.program_id(1) < n_ctx_tiles, modc_ref[...], mod_ref[...])


def _ada_kernel(c_ref, w_ref, b_ref, o_ref):
    o_ref[0] = _dot(_silu(c_ref[...]), w_ref[0], HIGHEST) + b_ref[0]


def _ada(cc, ada_w, ada_b):
    depth, d, n = ada_w.shape
    rows = cc.shape[0]
    tn = n // 4
    return pl.pallas_call(
        _ada_kernel,
        out_shape=jax.ShapeDtypeStruct((depth, rows, n), F32),
        grid=(depth, n // tn),
        in_specs=[
            pl.BlockSpec((rows, d), lambda i, j: (0, 0)),
            pl.BlockSpec((1, d, tn), lambda i, j: (i, 0, j)),
            pl.BlockSpec((1, 1, tn), lambda i, j: (i, 0, j)),
        ],
        out_specs=pl.BlockSpec((1, rows, tn), lambda i, j: (i, 0, j)),
        compiler_params=_params("arbitrary", "arbitrary"),
        name="ada_mod",
    )(cc, ada_w, ada_b.reshape(depth, 1, n))


def _tok_spec(tm, width):
    return pl.BlockSpec((1, tm, width), lambda b, t: (b, t, 0))


def _const_spec(shape):
    zeros = (0,) * len(shape)
    return pl.BlockSpec(shape, lambda b, t: zeros)


def _mod_specs(d, n_batch):
    lat = pl.BlockSpec((None, 6, d), lambda b, t: (b, 0, 0))
    ctx = pl.BlockSpec((None, 6, d), lambda b, t: (n_batch, 0, 0))
    return lat, ctx


def _stream(h, tm, n_ctx_tiles):
    if not isinstance(h, tuple):
        b, t, d = h.shape
        return b, t, d, [_tok_spec(tm, d)], [h]
    ctx, x = h
    b, n_lat, d = x.shape
    specs = [pl.BlockSpec((1, tm, d), lambda bi, ti: (bi, jnp.minimum(ti, n_ctx_tiles - 1), 0)),
             pl.BlockSpec((1, tm, d), lambda bi, ti: (bi, jnp.maximum(ti - n_ctx_tiles, 0), 0))]
    return b, ctx.shape[1] + n_lat, d, specs, [ctx, x]


def _stream_tile(refs, n_ctx_tiles):
    if len(refs) == 1:
        return refs[0][0]
    return jnp.where(pl.program_id(1) < n_ctx_tiles, refs[0][0], refs[1][0])


def _in_even_kernel(*refs, n_ctx_tiles, n_stream):
    mod_ref, modc_ref, g_ref, w_ref, o_ref = refs[n_stream:]
    m = _tile_mod(mod_ref, modc_ref, n_ctx_tiles)
    xn = (_rms(_stream_tile(refs[:n_stream], n_ctx_tiles)) * g_ref[...]) * (1.0 + m[1:2]) + m[0:1]
    o_ref[0] = _dot(xn.astype(BF16), w_ref[...])


def _in_even(h, mods, g, w, tm, n_ctx_tiles):
    b, t, d, h_specs, h_args = _stream(h, tm, n_ctx_tiles)
    n = w.shape[1]
    lat, ctx = _mod_specs(d, b)
    return pl.pallas_call(
        functools.partial(_in_even_kernel, n_ctx_tiles=n_ctx_tiles, n_stream=len(h_args)),
        out_shape=jax.ShapeDtypeStruct((b, t, n), F32),
        grid=(b, t // tm),
        in_specs=h_specs + [lat, ctx, _const_spec((1, d)), _const_spec((d, n))],
        out_specs=_tok_spec(tm, n),
        compiler_params=_params("arbitrary", "arbitrary"),
        name="even_in_proj",
    )(*h_args, mods, mods, g.reshape(1, d), w)


def _post_body(m, h, a, bb, g_ref, wo_ref, wgu_ref, wd_ref, hidden):
    half = a.shape[-1]
    y = _dot(a, wo_ref[:half, :]) + _dot(bb, wo_ref[half:, :])
    h1 = h + m[2:3] * y
    xn = (_rms(h1) * g_ref[...]) * (1.0 + m[4:5]) + m[3:4]
    gu = _dot(xn.astype(BF16), wgu_ref[...])
    act = _silu(gu[:, :hidden]) * gu[:, hidden:]
    return h1 + m[5:6] * _dot(act.astype(BF16), wd_ref[...])


def _post_kernel(*refs, n_ctx_tiles, hidden, n_stream):
    a_ref, b_ref, mod_ref, modc_ref, g_ref, wo_ref, wgu_ref, wd_ref, o_ref = refs[n_stream:]
    m = _tile_mod(mod_ref, modc_ref, n_ctx_tiles)
    h = _stream_tile(refs[:n_stream], n_ctx_tiles)
    o_ref[0] = _post_body(m, h, a_ref[0], b_ref[0], g_ref, wo_ref, wgu_ref, wd_ref, hidden)


def _post_final_kernel(h_ref, a_ref, b_ref, mod_ref, g_ref, wo_ref, wgu_ref, wd_ref, fg_ref, o_ref,
                       *, hidden):
    h2 = _post_body(mod_ref[...], h_ref[0], a_ref[0], b_ref[0], g_ref, wo_ref, wgu_ref, wd_ref, hidden)
    o_ref[0] = _rms(h2) * fg_ref[...]


def _post(h, a, bb, mods, g, wo, wgu, wd, tm, n_ctx_tiles):
    b, t, d, h_specs, h_args = _stream(h, tm, n_ctx_tiles)
    half = a.shape[-1]
    hidden = wd.shape[0]
    lat, ctx = _mod_specs(d, b)
    return pl.pallas_call(
        functools.partial(_post_kernel, n_ctx_tiles=n_ctx_tiles, hidden=hidden, n_stream=len(h_args)),
        out_shape=jax.ShapeDtypeStruct((b, t, d), F32),
        grid=(b, t // tm),
        in_specs=h_specs + [_tok_spec(tm, half), _tok_spec(tm, half), lat, ctx,
                            _const_spec((1, d)), _const_spec((2 * half, d)), _const_spec((d, 2 * hidden)),
                            _const_spec((hidden, d))],
        out_specs=_tok_spec(tm, d),
        compiler_params=_params("arbitrary", "arbitrary"),
        name="out_proj_ffn",
    )(*h_args, a, bb, mods, mods, g.reshape(1, d), wo, wgu, wd)


def _post_final(h, a_lat, b_lat, mods, g, wo, wgu, wd, final_g, tm, n_ctx_tiles):
    b, t, d = h.shape
    n_lat = a_lat.shape[1]
    half = a_lat.shape[-1]
    hidden = wd.shape[0]
    lat, _ = _mod_specs(d, b)
    return pl.pallas_call(
        functools.partial(_post_final_kernel, hidden=hidden),
        out_shape=jax.ShapeDtypeStruct((b, n_lat, d), F32),
        grid=(b, n_lat // tm),
        in_specs=[pl.BlockSpec((1, tm, d), lambda bi, ti: (bi, ti + n_ctx_tiles, 0)),
                  _tok_spec(tm, half), _tok_spec(tm, half), lat,
                  _const_spec((1, d)), _const_spec((2 * half, d)), _const_spec((d, 2 * hidden)),
                  _const_spec((hidden, d)), _const_spec((1, d))],
        out_specs=_tok_spec(tm, d),
        compiler_params=_params("arbitrary", "arbitrary"),
        name="out_proj_ffn_final",
    )(h, a_lat, b_lat, mods, g.reshape(1, d), wo, wgu, wd, final_g.reshape(1, d))


def _hgrn_tables(tile):
    c = HG_CHUNK
    r = np.arange(c)
    masks, bms = [], []
    m = c
    while m >= 2:
        upper = ((r % m) >= m // 2)[:, None]
        masks.append(np.broadcast_to(upper, (c, LANES)))
        bms.append(((r[:, None] // m) == (r[None, :] // m)) & upper & (~upper.T))
        m //= 2
    fwd = (r[None, :] <= r[:, None]).astype(np.float32)
    mall = np.stack([fwd, fwd[::-1, ::-1]])
    mall = np.concatenate([mall, mall], axis=2)
    mu = np.stack(masks).astype(np.float32)
    masku = np.tile(np.stack([mu, mu[:, ::-1]]), (1, 1, tile // c, 1))
    bmp = np.stack([np.concatenate([bms[i], bms[i + 1]], axis=1) for i in range(0, len(bms), 2)])
    bmp_bwd = np.stack([np.concatenate([bms[i][::-1, ::-1], bms[i + 1][::-1, ::-1]], axis=1)
                        for i in range(0, len(bms), 2)])
    return (jnp.asarray(mall, dtype=BF16), jnp.asarray(masku, dtype=F32),
            jnp.asarray(np.stack([bmp, bmp_bwd]), dtype=F32))


def _hgrn_kernel(q_ref, ff_ref, fb_ref, i_ref, g_ref, lbp_ref, ng_ref, mall_ref, masku_ref, bmp_ref,
                 o_ref, of_ref, ob_ref, *, layer, n_ctx_tiles, n_tiles):
    c = HG_CHUNK
    dk = HG_HEAD_DIM
    n_levels = masku_ref.shape[1]
    tile = masku_ref.shape[2]
    n_sub = tile // c
    lbp = lbp_ref[...]
    ex = jnp.exp(lbp - jnp.max(lbp, axis=0, keepdims=True))
    p = ex / jnp.sum(ex, axis=0, keepdims=True)
    lb = jnp.sum(p[:layer + 1], axis=0) - p[0]

    def sub(x, g):
        return x[g * c:(g + 1) * c]

    dirs = (0, 1)

    def stack_chunks(x):
        return jnp.concatenate([sub(x, g) for g in range(n_sub)], axis=1)

    def bwd_tile(t):
        if isinstance(t, int):
            return n_ctx_tiles - 1 - t if t < n_ctx_tiles else n_tiles - 1 - (t - n_ctx_tiles)
        return jnp.where(t < n_ctx_tiles, n_ctx_tiles - 1 - t, n_tiles - 1 - (t - n_ctx_tiles))

    def tile_rows(ti):
        start = ti * tile
        return pl.ds(start if isinstance(ti, int) else pl.multiple_of(start, tile), tile)

    def run_trips(trips, sts):
        lanes = [(s, d) for s in range(len(trips)) for d in dirs]
        n = range(len(lanes))
        tis = [bwd_tile(t) if d else t for t in trips for d in dirs]
        rows = [tile_rows(ti) for ti in tis]
        q = [q_ref[0, r, :] for r in rows]
        v = [i_ref[0, r, :] for r in rows]
        lbd = [lb[d:d + 1] for _, d in lanes]
        zf = [(ff_ref, fb_ref)[d][0, rows[l], :] for l, (_, d) in enumerate(lanes)]
        f = [lbd[l] + (1.0 - lbd[l]) * jax.nn.sigmoid(zf[l]) for l in n]
        lf = [jnp.log(x) * LOG2_E for x in f]
        k = [1.0 - x for x in f]
        hi = [x.astype(BF16) for x in lf]
        lo = [(lf[l] - hi[l].astype(F32)).astype(BF16) for l in n]
        e2 = [_dot(mall_ref[lanes[l][1]], jnp.concatenate([stack_chunks(hi[l]), stack_chunks(lo[l])], axis=0))
              for l in n]
        cum = [[e2[l][:, g * dk:(g + 1) * dk] for g in range(n_sub)] for l in n]

        def tail_decay(l, g):
            last = c - 1 if lanes[l][1] == 0 else 0
            return jnp.exp2(cum[l][g][last:last + 1] - cum[l][g])

        def level_decay(l, lvl):
            m = c >> lvl
            ref = m // 2 - 1 if lanes[l][1] == 0 else m // 2
            parts = []
            for g in range(n_sub):
                cg = cum[l][g]
                if m >= SUBLANES:
                    x = cg.reshape(c // m, m, dk)
                    mid = jnp.broadcast_to(x[:, ref:ref + 1, :], x.shape)
                else:
                    x = cg.reshape(c // SUBLANES, SUBLANES, dk)
                    srow = lax.broadcasted_iota(jnp.int32, x.shape, 1)
                    mid = jnp.where(srow < m, jnp.broadcast_to(x[:, ref:ref + 1, :], x.shape),
                                    jnp.broadcast_to(x[:, m + ref:m + ref + 1, :], x.shape))
                parts.append(jnp.exp2(_neg_abs(cg - mid.reshape(c, dk))))
            return jnp.concatenate(parts, axis=0)

        g_in = [jnp.concatenate([jnp.exp2(cum[l][g]) for g in range(n_sub)], axis=0) for l in n]
        qin = [(q[l] * g_in[l]).astype(BF16) for l in n]
        kout = [(k[l] * jnp.concatenate([tail_decay(l, g) for g in range(n_sub)], axis=0)).astype(BF16)
                for l in n]
        vb = [x.astype(BF16) for x in v]
        zero = jnp.zeros((c, dk), BF16)
        ds = [[_dot_tn(sub(vb[l], g), sub(kout[l], g)) for g in range(n_sub)] for l in n]
        qf = [q[l] * f[l] for l in n]
        w = [[None] * n_levels for _ in n]
        for lvl in range(n_levels):
            for l, (_, d) in enumerate(lanes):
                later = masku_ref[d, lvl] != 0.0
                if lvl == n_levels - 1:
                    w[l][lvl] = jnp.where(later, qf[l], k[l]).astype(BF16)
                else:
                    w[l][lvl] = (jnp.where(later, q[l], k[l]) * level_decay(l, lvl)).astype(BF16)
        scores = [[[None] * (n_levels // 2) for _ in range(n_sub)] for _ in n]
        for pair in range(n_levels // 2):
            lv_a, lv_b = 2 * pair, 2 * pair + 1
            for g in range(n_sub):
                for l, (_, d) in enumerate(lanes):
                    wa, wb = sub(w[l][lv_a], g), sub(w[l][lv_b], g)
                    rhs = jnp.concatenate([jnp.concatenate([wa, zero], axis=1),
                                           jnp.concatenate([zero, wb], axis=1)], axis=0)
                    raw = _dot_nt(jnp.concatenate([wa, wb], axis=1), rhs)
                    scores[l][g][pair] = raw * bmp_ref[d, pair]
        o = [jnp.concatenate([_dot(sum(scores[l][g]).astype(BF16),
                                   jnp.concatenate([sub(vb[l], g)] * 2, axis=0))
                              for g in range(n_sub)], axis=0)
             + jnp.sum(q[l] * k[l], axis=-1, keepdims=True) * v[l] for l in n]
        order = [list(range(n_sub)), list(reversed(range(n_sub)))]
        entering = [[None] * n_sub for _ in n]
        sts = list(sts)
        for l, (_, d) in enumerate(lanes):
            for g in order[d]:
                entering[l][g] = sts[d].astype(BF16)
                last = (g + 1) * c - 1 if d == 0 else g * c
                sts[d] = sts[d] * g_in[l][last:last + 1] + ds[l][g]
        for l, (_, d) in enumerate(lanes):
            out = jnp.concatenate([sub(o[l], g) + _dot_nt(sub(qin[l], g), entering[l][g])
                                   for g in range(n_sub)], axis=0)
            (of_ref, ob_ref)[d][rows[l], :] = out
        return tuple(sts)

    def body(i, carry):
        return run_trips([i * HGRN_LOCKSTEP + s for s in range(HGRN_LOCKSTEP)], carry)

    s0 = jnp.zeros((dk, dk), F32)
    sts = lax.fori_loop(0, n_tiles // HGRN_LOCKSTEP, body, (s0, s0), unroll=HGRN_UNROLL)
    rest = list(range(n_tiles - n_tiles % HGRN_LOCKSTEP, n_tiles))
    if rest:
        run_trips(rest, sts)
    o = _rms(of_ref[...] + ob_ref[...]) * ng_ref[...]
    o_ref[0] = (o * _silu(g_ref[0])).astype(o_ref.dtype)


def _hgrn(z, hg_lower_bound, ng, layer, n_heads, ctx_len):
    b, t, _ = z.shape
    dk = HG_HEAD_DIM
    n_even = hg_lower_bound.shape[0]
    tile = math.gcd(math.gcd(ctx_len, t - ctx_len), 256)
    mall, masku, bm = _hgrn_tables(tile)

    def col(kind):
        return pl.BlockSpec((1, t, dk), lambda bi, hi: (bi, 0, kind * n_heads + hi))

    def const(arr):
        zeros = (0,) * arr.ndim
        return pl.BlockSpec(arr.shape, lambda bi, hi: zeros)

    return pl.pallas_call(
        functools.partial(_hgrn_kernel, layer=layer, n_ctx_tiles=ctx_len // tile, n_tiles=t // tile),
        out_shape=jax.ShapeDtypeStruct((b, t, n_heads * dk), BF16),
        grid=(b, n_heads),
        in_specs=[col(0), col(1), col(2), col(3), col(4),
                  pl.BlockSpec((n_even, 2, dk), lambda bi, hi: (0, 0, hi)),
                  const(ng.reshape(1, dk)), const(mall), const(masku), const(bm)],
        out_specs=pl.BlockSpec((1, t, dk), lambda bi, hi: (bi, 0, hi)),
        scratch_shapes=[pltpu.VMEM((t, dk), F32), pltpu.VMEM((t, dk), F32)],
        compiler_params=_params("arbitrary", "arbitrary"),
        name="hgrn2_scan",
    )(z, z, z, z, z, hg_lower_bound, ng.reshape(1, dk), mall, masku, bm)


def _hy_pre_kernel(x0_ref, x1_ref, v_ref, w0_ref, w1_ref, wv_ref, b0_ref, b1_ref, bv_ref,
                   x0c_ctx_ref, x0c_lat_ref, u_ctx_ref, u_lat_ref, *, ctx_len):
    t = x0_ref.shape[1]
    row = lax.broadcasted_iota(jnp.int32, (t, LANES), 0)
    first = (row == 0) | (row == ctx_len)
    final = (row == ctx_len - 1) | (row == t - 1)

    def short_conv(z_ref, w_ref, b_ref):
        z = z_ref[0]
        prev = jnp.where(first, 0.0, pltpu.roll(z, 1, 0))
        nxt = jnp.where(final, 0.0, pltpu.roll(z, t - 1, 0))
        w = w_ref[...]
        return prev * w[0:1] + z * w[1:2] + nxt * w[2:3] + b_ref[...]

    x0c = short_conv(x0_ref, w0_ref, b0_ref).astype(x0c_ctx_ref.dtype)
    u = (short_conv(x1_ref, w1_ref, b1_ref) * short_conv(v_ref, wv_ref, bv_ref)).astype(u_ctx_ref.dtype)
    x0c_ctx_ref[0] = x0c[:ctx_len]
    x0c_lat_ref[0] = x0c[ctx_len:]
    u_ctx_ref[0] = u[:ctx_len]
    u_lat_ref[0] = u[ctx_len:]


def _hy_pre(z, short_w, short_b, col0, width, ctx_len):
    b, t, _ = z.shape
    nb = width // LANES
    c0 = col0 // LANES

    def zcol(kind):
        return pl.BlockSpec((1, t, LANES), lambda bi, j: (bi, 0, c0 + kind * nb + j))

    def wcol(kind, rows):
        return pl.BlockSpec((rows, LANES), lambda bi, j: (0, kind * nb + j))

    def seg(rows):
        return (jax.ShapeDtypeStruct((b, rows, width), BF16),
                pl.BlockSpec((1, rows, LANES), lambda bi, j: (bi, 0, j)))

    (ctx_shape, ctx_spec), (lat_shape, lat_spec) = seg(ctx_len), seg(t - ctx_len)
    sb = short_b.reshape(1, -1)
    return pl.pallas_call(
        functools.partial(_hy_pre_kernel, ctx_len=ctx_len),
        out_shape=(ctx_shape, lat_shape, ctx_shape, lat_shape),
        grid=(b, nb),
        in_specs=[zcol(0), zcol(1), zcol(2), wcol(0, 3), wcol(1, 3), wcol(2, 3),
                  wcol(0, 1), wcol(1, 1), wcol(2, 1)],
        out_specs=(ctx_spec, lat_spec, ctx_spec, lat_spec),
        compiler_params=_params("arbitrary", "arbitrary"),
        name="hyena_short_conv",
    )(z, z, z, short_w, short_w, short_w, sb, sb, sb)


def _filter_features(length, n):
    p = np.arange(n)
    is_f = p < length
    is_b = p > n - length
    lag = np.where(is_f, p, np.where(is_b, n - 1 - p, 0))
    tt = np.linspace(0.0, 1.0, length, dtype=np.float32)[lag][:, None]
    w = (2.0 * math.pi * lag.astype(np.float32) / length)[:, None].astype(np.float32)
    bands = np.linspace(1e-4, HY_BANDS - 1, HY_BANDS, dtype=np.float32)[None, :]
    feat = np.concatenate([tt, np.cos(bands * w), -np.sin(bands * w), is_f[:, None], is_b[:, None]],
                          axis=-1).astype(np.float32)
    pad = (-feat.shape[1]) % BF16_SUBLANES
    return jnp.asarray(np.pad(feat, ((0, 0), (0, pad))))


def _filt_kernel(feat_ref, w1_ref, b1_ref, fr1_ref, w2_ref, b2_ref, fr2_ref,
                 w3f_ref, w3b_ref, dl_ref, o_ref, hid_ref):
    z = feat_ref[...]
    mf = z[:, HY_EMB:HY_EMB + 1]
    mb = z[:, HY_EMB + 1:HY_EMB + 2]

    @pl.when(pl.program_id(0) == 0)
    def _():
        h1 = jnp.sin(fr1_ref[...] * (_dot3(z, w1_ref[...]) + b1_ref[...]))
        hid_ref[...] = jnp.sin(fr2_ref[...] * (_dot3(h1, w2_ref[...]) + b2_ref[...]))

    hid = hid_ref[...]
    hf = _dot3(hid, w3f_ref[...])
    hb = _dot3(hid, w3b_ref[...])
    win = jnp.exp(-z[:, 0:1] * dl_ref[...])
    f = (mf * hf + mb * hb) * win
    o_ref[...] = f / jnp.sum(jnp.abs(f), axis=0, keepdims=True)


def _hyena_filter(length, n, w1, b1, fr1, w2, b2, fr2, w3, width):
    feat = _filter_features(length, n)
    nf = feat.shape[1]
    hid = w1.shape[1]
    w1p = jnp.pad(w1, ((0, nf - w1.shape[0]), (0, 0)))
    d_lo = -math.log(HY_TARGET) / HY_GENTLE_PCT
    d_hi = -math.log(HY_TARGET) / HY_STEEP_PCT
    deltas = jnp.asarray(np.linspace(d_lo, d_hi, width, dtype=np.float32)[None, :])
    nb = width // LANES

    def full(shape):
        zeros = (0,) * len(shape)
        return pl.BlockSpec(shape, lambda j: zeros)

    return pl.pallas_call(
        _filt_kernel,
        out_shape=jax.ShapeDtypeStruct((n, width), F32),
        grid=(nb,),
        in_specs=[full((n, nf)), full((nf, hid)), full((1, hid)),
                  full((1, hid)), full((hid, hid)), full((1, hid)), full((1, hid)),
                  pl.BlockSpec((hid, LANES), lambda j: (0, j)),
                  pl.BlockSpec((hid, LANES), lambda j: (0, nb + j)),
                  pl.BlockSpec((1, LANES), lambda j: (0, j))],
        out_specs=pl.BlockSpec((n, LANES), lambda j: (0, j)),
        scratch_shapes=[pltpu.VMEM((n, hid), F32)],
        compiler_params=_params("arbitrary"),
        name="hyena_filter_mlp",
    )(feat, w1p, b1.reshape(1, -1), fr1.reshape(1, -1), w2, b2.reshape(1, -1),
      fr2.reshape(1, -1), w3, w3, deltas)


def _dft_tables(n1, n1_in):
    n2 = FFT_N2
    n = n1 * n2
    a = np.arange(n1)
    j = np.arange(n2)
    ang = -2.0 * np.pi * (a[None, None, :] * a[None, :, None] / n1 + j[:, None, None] * a[None, :, None] / n)
    tr, ti = np.cos(ang), np.sin(ang)
    fwd_a = np.concatenate([np.concatenate([tr, -ti], 2), np.concatenate([ti, tr], 2)], 1)
    trt, tit = np.swapaxes(tr, 1, 2), -np.swapaxes(ti, 1, 2)
    inv_a = np.concatenate([np.concatenate([trt, -tit], 2), np.concatenate([tit, trt], 2)], 1)
    keep = np.concatenate([np.arange(n1_in), n1 + np.arange(n1_in)])
    ang2 = -2.0 * np.pi * (j[:, None] * j[None, :]) / n2
    cr, ci = np.cos(ang2), np.sin(ang2)
    fwd_c = np.block([[cr, -ci], [ci, cr]])
    inv_c = np.block([[cr, ci], [-ci, cr]])
    real_a = np.concatenate([tr, ti], 1)
    return dict(fwd_a=fwd_a[:, :, keep], inv_a=inv_a[:, keep, :], fwd_c=fwd_c, inv_c=inv_c,
                real_a=real_a)


def _slab(idx):
    return pl.ds(pl.multiple_of(idx * FFT_PITCH, SUBLANES), FFT_N2)


def _fft_filter_kernel(f_ref, wa_ref, wc_ref, o_ref, scr_ref, *, n1):
    n2 = FFT_N2
    rows = 2 * n1
    scale = 1.0 / (n1 * n2)

    def split3(x):
        hi = x.astype(BF16)
        lo = (x - hi.astype(F32)).astype(BF16)
        return jnp.concatenate([hi, lo, hi], axis=0)

    def stage_a(j, carry):
        scr_ref[pl.ds(j, rows, stride=FFT_PITCH), :] = _dot(wa_ref[j], split3(f_ref[j]))
        return carry

    lax.fori_loop(0, n2, stage_a, 0, unroll=FFT_UNROLL)

    def stage_c(k, carry):
        x = jnp.concatenate([scr_ref[_slab(k), :], scr_ref[_slab(n1 + k), :]], axis=0)
        o_ref[k] = _dot(wc_ref[...], split3(x)) * scale
        return carry

    lax.fori_loop(0, n1, stage_c, 0, unroll=FFT_UNROLL)


def _split3_cols(w):
    hi = w.astype(np.float32).astype(jnp.bfloat16)
    lo = (w.astype(np.float32) - np.asarray(hi, np.float32)).astype(jnp.bfloat16)
    return jnp.asarray(np.concatenate([hi, hi, lo], axis=-1))


def _fft_filter(filt, n1, tables):
    n2 = FFT_N2
    width = filt.shape[1]
    ft = filt.reshape(n1, n2, width).transpose(1, 0, 2)
    wa = _split3_cols(tables["real_a"])
    wc = _split3_cols(tables["fwd_c"])
    return pl.pallas_call(
        functools.partial(_fft_filter_kernel, n1=n1),
        out_shape=jax.ShapeDtypeStruct((n1, 2 * n2, width), F32),
        grid=(width // LANES,),
        in_specs=[pl.BlockSpec((n2, n1, LANES), lambda c: (0, 0, c)),
                  pl.BlockSpec(wa.shape, lambda c: (0, 0, 0)),
                  pl.BlockSpec(wc.shape, lambda c: (0, 0))],
        out_specs=pl.BlockSpec((n1, 2 * n2, LANES), lambda c: (0, 0, c)),
        scratch_shapes=[pltpu.VMEM((2 * n1 * FFT_PITCH, LANES), F32)],
        compiler_params=_params("arbitrary"),
        name="hyena_filter_dft",
    )(ft, wa, wc)


def _fftconv_kernel(u_ref, x0_ref, fh_ref, skip_ref, wfa_ref, wfc_ref, wic_ref, wia_ref, o_ref,
                    scr_ref, *, n1):
    n2 = FFT_N2
    rows = 2 * n1

    def stage_a(j, carry):
        scr_ref[pl.ds(j, rows, stride=FFT_PITCH), :] = _dot(wfa_ref[j], u_ref[0, j])
        return carry

    lax.fori_loop(0, n2, stage_a, 0, unroll=FFT_UNROLL)

    def stage_c(k, carry):
        x = jnp.concatenate([scr_ref[_slab(k), :], scr_ref[_slab(n1 + k), :]], axis=0)
        xf = _dot(wfc_ref[...], x.astype(BF16))
        fh = fh_ref[k]
        xr, xi, fr, fi = xf[:n2], xf[n2:], fh[:n2], fh[n2:]
        y = jnp.concatenate([xr * fr - xi * fi, xr * fi + xi * fr], axis=0)
        zt = _dot(wic_ref[...], y.astype(BF16))
        scr_ref[_slab(k), :] = zt[:n2]
        scr_ref[_slab(n1 + k), :] = zt[n2:]
        return carry

    lax.fori_loop(0, n1, stage_c, 0, unroll=FFT_UNROLL)

    def stage_ai(j, carry):
        x = scr_ref[pl.ds(j, rows, stride=FFT_PITCH), :]
        y = _dot(wia_ref[j], x.astype(BF16))
        u = u_ref[0, j].astype(F32)
        o_ref[0, j] = (x0_ref[0, j].astype(F32) * (y + skip_ref[...] * u)).astype(o_ref.dtype)
        return carry

    lax.fori_loop(0, n2, stage_ai, 0, unroll=FFT_UNROLL)


def _to_fft_layout(x, n1_in):
    b, length, c = x.shape
    n1_used = length // FFT_N2
    x = x.reshape(b // 2, 2, n1_used, FFT_N2, c)
    x = jnp.pad(x, ((0, 0), (0, 0), (0, n1_in - n1_used), (0, 0), (0, 0)))
    return x.transpose(0, 3, 1, 2, 4).reshape(b // 2, FFT_N2, 2 * n1_in, c)


def _from_fft_layout(y, length):
    p, n2, rows, c = y.shape
    n1_in = rows // 2
    y = y.reshape(p, n2, 2, n1_in, c).transpose(0, 2, 3, 1, 4)
    return y.reshape(2 * p, n1_in * n2, c)[:, :length]


def _fftconv(u, x0c, fh, skip, n1, n1_in, tables):
    _, length, width = u.shape
    ut = _to_fft_layout(u, n1_in)
    xt = _to_fft_layout(x0c, n1_in)
    pairs, n2, rin, _ = ut.shape
    wfa = jnp.asarray(tables["fwd_a"], dtype=BF16)
    wia = jnp.asarray(tables["inv_a"], dtype=BF16)
    wfc = jnp.asarray(tables["fwd_c"], dtype=BF16)
    wic = jnp.asarray(tables["inv_c"], dtype=BF16)
    data = pl.BlockSpec((1, n2, rin, LANES), lambda c, p: (p, 0, 0, c))

    def const(arr):
        zeros = (0,) * arr.ndim
        return pl.BlockSpec(arr.shape, lambda c, p: zeros, pipeline_mode=pl.Buffered(1))

    out = pl.pallas_call(
        functools.partial(_fftconv_kernel, n1=n1),
        out_shape=jax.ShapeDtypeStruct(ut.shape, BF16),
        grid=(width // LANES, pairs),
        in_specs=[data, data,
                  pl.BlockSpec((n1, 2 * n2, LANES), lambda c, p: (0, 0, c)),
                  pl.BlockSpec((1, LANES), lambda c, p: (0, c)),
                  const(wfa), const(wfc), const(wic), const(wia)],
        out_specs=data,
        scratch_shapes=[pltpu.VMEM((2 * n1 * FFT_PITCH, LANES), F32)],
        compiler_params=_params("arbitrary", "arbitrary"),
        name="hyena_dft_conv",
    )(ut, xt, fh, skip.reshape(1, width), wfa, wfc, wic, wia)
    return _from_fft_layout(out, length)


def _hyena(z, col0, width, ctx_len, short_w, short_b, filt_params, skip):
    x0c_ctx, x0c_lat, u_ctx, u_lat = _hy_pre(z, short_w, short_b, col0, width, ctx_len)
    outs = []
    for x0c, u in ((x0c_ctx, u_ctx), (x0c_lat, u_lat)):
        length = u.shape[1]
        n1_in = max(length // FFT_N2, SUBLANES)
        n1 = 2 * n1_in
        tables = _dft_tables(n1, n1_in)
        filt = _hyena_filter(length, n1 * FFT_N2, *filt_params, width)
        fh = _fft_filter(filt, n1, tables)
        outs.append(_fftconv(u, x0c, fh, skip, n1, n1_in, tables))
    return jnp.concatenate(outs, axis=1)


def _rope_tables(n_lat, ctx_len):
    tok = np.arange(n_lat)
    row, colp = tok // GRID_W, tok % GRID_W

    def axial(half):
        inv = ROPE_BASE ** (-np.arange(half, dtype=np.float32) / half)
        parts_c, parts_s = [], []
        for pos in (row, colp):
            ang = pos.astype(np.float32)[:, None] * inv
            parts_c += [np.cos(ang), np.cos(ang)]
            parts_s += [-np.sin(ang), np.sin(ang)]
        return np.concatenate(parts_c, 1), np.concatenate(parts_s, 1)

    dc, ds = axial(DA_HEAD_DIM // 4)
    mc, ms = axial(MLA_ROPE // 4)
    ones, zeros = np.ones((n_lat, MLA_NOPE), np.float32), np.zeros((n_lat, MLA_NOPE), np.float32)
    padc = np.ones((n_lat, LANES - MLA_NOPE - MLA_ROPE), np.float32)
    tabs = [np.concatenate([dc, dc], 1), np.concatenate([ds, ds], 1),
            np.concatenate([ones, mc, padc], 1), np.concatenate([zeros, ms, 0 * padc], 1)]
    out = []
    for i, tb in enumerate(tabs):
        ctx_rows = np.ones((ctx_len, LANES), np.float32) if i % 2 == 0 else np.zeros((ctx_len, LANES), np.float32)
        out.append(jnp.asarray(np.concatenate([ctx_rows, tb.astype(np.float32)], 0)))
    return out


def _with_ones(v):
    ones = jnp.ones((v.shape[0], LANES), v.dtype)
    parts = []
    for hd in range(v.shape[1] // LANES):
        parts += [v[:, hd * LANES:(hd + 1) * LANES], ones]
    return jnp.concatenate(parts, axis=1)


def _in_odd_kernel(x_ref, mod_ref, modc_ref, g_ref, w_ref, dc_ref, ds_ref, mc_ref, ms_ref,
                   qg_ref, kvg_ref, wuq_ref, wukv_ref,
                   qd_ref, kd_ref, vd_ref, qm_ref, km_ref, vm_ref, *, n_ctx_tiles, da_w, q_rank, kv_rank):
    m = _tile_mod(mod_ref, modc_ref, n_ctx_tiles)
    xn = (_rms(x_ref[0]) * g_ref[...]) * (1.0 + m[1:2]) + m[0:1]
    z = _dot(xn.astype(BF16), w_ref[...])
    nrep = da_w // LANES
    dc = jnp.concatenate([dc_ref[...]] * nrep, axis=1)
    ds = jnp.concatenate([ds_ref[...]] * nrep, axis=1)
    mc = jnp.concatenate([mc_ref[...]] * MLA_HEADS, axis=1)
    ms = jnp.concatenate([ms_ref[...]] * MLA_HEADS, axis=1)
    sa = DA_HEAD_DIM ** -0.5 * LOG2_E
    sm = (MLA_NOPE + MLA_ROPE) ** -0.5 * LOG2_E
    lane = lax.broadcasted_iota(jnp.int32, (x_ref.shape[1], LANES), 1)

    def partner(x, half):
        first = (lane % (2 * half)) < half
        cols = []
        for j in range(x.shape[1] // LANES):
            xj = x[:, j * LANES:(j + 1) * LANES]
            cols.append(jnp.where(first, pltpu.roll(xj, LANES - half, 1), pltpu.roll(xj, half, 1)))
        return cols[0] if len(cols) == 1 else jnp.concatenate(cols, axis=1)

    o = 0
    zq = z[:, o:o + da_w]
    qd_ref[0] = ((zq * dc + partner(zq, DA_HEAD_DIM // 4) * ds) * sa).astype(BF16)
    o += da_w
    zk = z[:, o:o + da_w]
    kd_ref[0] = (zk * dc + partner(zk, DA_HEAD_DIM // 4) * ds).astype(BF16)
    o += da_w
    vd_ref[0] = _with_ones(z[:, o:o + da_w]).astype(BF16)
    o += da_w
    cq = _rms(z[:, o:o + q_rank]) * qg_ref[...]
    o += q_rank
    ckv = _rms(z[:, o:o + kv_rank]) * kvg_ref[...]
    o += kv_rank
    zr = z[:, o:o + LANES]
    kr = zr * mc_ref[...] + partner(zr, MLA_ROPE // 4) * ms_ref[...]
    mw = MLA_HEADS * LANES
    qu = _dot(cq.astype(BF16), wuq_ref[...])
    qm_ref[0] = ((qu * mc + partner(qu, MLA_ROPE // 4) * ms) * sm).astype(BF16)
    kvu = _dot(ckv.astype(BF16), wukv_ref[...])
    km_ref[0] = (kvu[:, :mw] + jnp.concatenate([kr] * MLA_HEADS, axis=1)).astype(BF16)
    vm_ref[0] = _with_ones(kvu[:, mw:]).astype(BF16)


def _odd_weights(w_in, w_uq, w_ukv, da_w, q_rank, kv_rank):
    o = 3 * da_w + q_rank + kv_rank
    krw = jnp.pad(w_in[:, o:], ((0, 0), (MLA_NOPE, LANES - MLA_NOPE - MLA_ROPE)))
    w_big = jnp.concatenate([w_in[:, :o], krw], axis=1)
    dq = MLA_NOPE + MLA_ROPE
    uq = jnp.pad(w_uq.reshape(q_rank, MLA_HEADS, dq), ((0, 0), (0, 0), (0, LANES - dq)))
    ukv = w_ukv.reshape(kv_rank, MLA_HEADS, MLA_NOPE + MLA_V)
    uk = jnp.pad(ukv[:, :, :MLA_NOPE], ((0, 0), (0, 0), (0, LANES - MLA_NOPE))).reshape(kv_rank, MLA_HEADS * LANES)
    uv = ukv[:, :, MLA_NOPE:].reshape(kv_rank, MLA_HEADS * MLA_V)
    return (w_big.astype(BF16), uq.reshape(q_rank, MLA_HEADS * LANES).astype(BF16),
            jnp.concatenate([uk, uv], axis=1).astype(BF16))


def _in_odd(h, mods, g, w_big, wuq, wukv, qg, kvg, tabs, tm, n_ctx_tiles, da_w):
    b, t, d = h.shape
    q_rank, kv_rank = qg.shape[0], kvg.shape[0]
    lat, ctx = _mod_specs(d, b)
    tab = pl.BlockSpec((tm, LANES), lambda bi, ti: (ti, 0))
    mw = MLA_HEADS * LANES

    def tok_major(width):
        return jax.ShapeDtypeStruct((b, t, width), BF16), _tok_spec(tm, width)

    outs, ospecs = zip(tok_major(da_w), tok_major(da_w), tok_major(2 * da_w),
                       tok_major(mw), tok_major(mw), tok_major(2 * mw))
    return pl.pallas_call(
        functools.partial(_in_odd_kernel, n_ctx_tiles=n_ctx_tiles, da_w=da_w, q_rank=q_rank, kv_rank=kv_rank),
        out_shape=tuple(outs),
        grid=(b, t // tm),
        in_specs=[_tok_spec(tm, d), lat, ctx, _const_spec((1, d)), _const_spec(w_big.shape),
                  tab, tab, tab, tab, _const_spec((1, q_rank)), _const_spec((1, kv_rank)),
                  _const_spec(wuq.shape), _const_spec(wukv.shape)],
        out_specs=tuple(ospecs),
        compiler_params=_params("arbitrary", "arbitrary"),
        name="odd_in_proj",
    )(h, mods, mods, g.reshape(1, d), w_big, *tabs, qg.reshape(1, -1), kvg.reshape(1, -1), wuq, wukv)


def _softmax_pv(qs, k_ref, v_ref):
    n_keys = k_ref.shape[1]
    starts = list(range(0, n_keys, ATTN_KEY_BLOCK))
    m = [None] * len(qs)
    acc = [None] * len(qs)
    for start in starts:
        size = min(ATTN_KEY_BLOCK, n_keys - start)
        k = k_ref[0, start:start + size, :]
        v1 = v_ref[0, start:start + size, :]
        for i, q in enumerate(qs):
            s = _dot_nt(q, k)
            m_blk = jnp.max(s, axis=-1, keepdims=True)
            if start == 0:
                m[i] = m_blk
                acc[i] = _dot(jnp.exp2(s - m_blk).astype(BF16), v1)
            else:
                m_new = jnp.maximum(m[i], m_blk)
                pv = _dot(jnp.exp2(s - m_new).astype(BF16), v1)
                acc[i] = jnp.exp2(m[i] - m_new) * acc[i] + pv
                m[i] = m_new
    return [a[:, :LANES] / a[:, LANES:] for a in acc]


def _attn_kernel(q_ref, k_ref, v_ref, lam_ref, sg_ref, *rest, diff, lam_init):
    o_ref = rest[-1]
    q = q_ref[0]
    if not diff:
        o_ref[0] = _softmax_pv([q], k_ref, v_ref)[0].astype(o_ref.dtype)
        return
    first = lax.broadcasted_iota(jnp.int32, q.shape, 1) < DA_HEAD_DIM
    zero = jnp.zeros_like(q)
    o1, o2 = _softmax_pv([jnp.where(first, q, zero), jnp.where(first, zero, q)], k_ref, v_ref)
    lp = lam_ref[...]
    lam = (jnp.exp(jnp.sum(lp[0:1] * lp[1:2], axis=-1, keepdims=True))
           - jnp.exp(jnp.sum(lp[2:3] * lp[3:4], axis=-1, keepdims=True)) + lam_init)
    o_ref[0] = (_rms(o1 - lam * o2) * sg_ref[...] * (1.0 - lam_init)).astype(o_ref.dtype)


def _attention(q, k, v1, lam_p, subln_g, q_rows, tq, n_keys, out_rows, out_start, diff, lam_init, earlier=None):
    b, _, width = q.shape
    heads = width // LANES
    q_start, q_stop = q_rows

    def rows_from(start):
        return pl.BlockSpec((pl.Element(1), pl.Element(tq), pl.Element(LANES)),
                            lambda bi, hi, ti: (bi, pl.multiple_of(start + ti * tq, math.gcd(start, tq)),
                                                pl.multiple_of(hi * LANES, LANES)))

    khead = pl.BlockSpec((1, n_keys, LANES), lambda bi, hi, ti: (bi, 0, hi))
    vhead = pl.BlockSpec((1, n_keys, 2 * LANES), lambda bi, hi, ti: (bi, 0, hi))

    def const(shape):
        zeros = (0,) * len(shape)
        return pl.BlockSpec(shape, lambda bi, hi, ti: zeros)

    in_specs = [rows_from(q_start), khead, vhead, const(lam_p.shape), const((1, LANES))]
    args = [q, k, v1, lam_p, subln_g.reshape(1, LANES)]
    aliases = {}
    if earlier is not None:
        in_specs.append(pl.BlockSpec(memory_space=pl.ANY))
        args.append(earlier)
        aliases = {len(args) - 1: 0}
    return pl.pallas_call(
        functools.partial(_attn_kernel, diff=diff, lam_init=lam_init),
        out_shape=jax.ShapeDtypeStruct((b, out_rows, width), BF16),
        grid=(b, heads, (q_stop - q_start) // tq),
        in_specs=in_specs,
        out_specs=rows_from(out_start),
        input_output_aliases=aliases,
        compiler_params=_params("arbitrary", "arbitrary", "arbitrary"),
        name="diff_attention" if diff else "mla_attention",
    )(*args)


def _attend_segments(q, k, v1, lam_p, subln_g, ctx_len, diff, lam_init, need_ctx):
    t = k.shape[1]
    n_lat = t - ctx_len
    tq_lat = math.gcd(n_lat, ATTN_QUERY_TILE)
    if not need_ctx:
        return _attention(q, k, v1, lam_p, subln_g, (ctx_len, t), tq_lat, t, n_lat, 0, diff, lam_init)
    o = _attention(q, k, v1, lam_p, subln_g, (ctx_len, t), tq_lat, t, t, ctx_len, diff, lam_init)
    return _attention(q, k, v1, lam_p, subln_g, (0, ctx_len), ctx_len, ctx_len, t, 0, diff, lam_init, earlier=o)


def kernel(x, c, ctx, c_ctx, ada_w, ada_b, norm_mix_g, norm_ffn_g, ffn_w_gu, ffn_w_down, ev_w_in, ev_w_out, hg_lower_bound, hg_out_norm_g, hy_short_w, hy_short_b, hy_filt_w1, hy_filt_b1, hy_filt_freq1, hy_filt_w2, hy_filt_b2, hy_filt_freq2, hy_filt_w3, hy_skip, od_w_in, od_w_out, da_lambda, da_subln_g, mla_q_norm_g, mla_w_uq, mla_kv_norm_g, mla_w_ukv, final_norm_g):
    n_batch, n_lat, d = x.shape
    ctx_len = ctx.shape[1]
    depth = ada_w.shape[0]
    assert n_batch % 2 == 0 and n_batch < MOD_ROWS
    assert n_lat % GRID_W == 0 and ctx_len % HG_CHUNK == 0 and n_lat % FFT_N2 == 0 and ctx_len % FFT_N2 == 0
    tm = math.gcd(math.gcd(ctx_len, n_lat), 256)
    n_ctx_tiles = ctx_len // tm
    hg_width = d // 2
    hy_width = d - hg_width
    da_w = d // 2
    q_rank, kv_rank = mla_q_norm_g.shape[1], mla_kv_norm_g.shape[1]

    cc = jnp.concatenate([c, c_ctx[None], jnp.zeros((MOD_ROWS - n_batch - 1, d), F32)], axis=0)
    mods = _ada(cc, ada_w, ada_b).reshape(depth, MOD_ROWS, 6, d)
    h = (ctx, x) if depth > 1 else jnp.concatenate([ctx, x], axis=1)
    rope_tabs = _rope_tables(n_lat, ctx_len)

    for i in range(depth):
        last = i == depth - 1
        if i % 2 == 0:
            e = i // 2
            z = _in_even(h, mods[i], norm_mix_g[i], ev_w_in[e].astype(BF16), tm, n_ctx_tiles)
            a = _hgrn(z, hg_lower_bound, hg_out_norm_g[e], e, hg_width // HG_HEAD_DIM, ctx_len)
            filt_params = (hy_filt_w1[e], hy_filt_b1[e], hy_filt_freq1[e], hy_filt_w2[e], hy_filt_b2[e],
                           hy_filt_freq2[e], hy_filt_w3[e])
            bb = _hyena(z, 5 * hg_width, hy_width, ctx_len, hy_short_w[e], hy_short_b[e], filt_params, hy_skip[e])
            w_out = ev_w_out[e]
        else:
            o = i // 2
            lam_init = 0.8 - 0.6 * math.exp(-0.3 * i)
            w_big, wuq, wukv = _odd_weights(od_w_in[o], mla_w_uq[o], mla_w_ukv[o], da_w, q_rank, kv_rank)
            qd, kd, vd, qm, km, vm = _in_odd(h, mods[i], norm_mix_g[i], w_big, wuq, wukv, mla_q_norm_g[o],
                                             mla_kv_norm_g[o], rope_tabs, tm, n_ctx_tiles, da_w)
            a = _attend_segments(qd, kd, vd, da_lambda[o], da_subln_g[o], ctx_len, True, lam_init, not last)
            bb = _attend_segments(qm, km, vm, da_lambda[o], da_subln_g[o], ctx_len, False, lam_init, not last)
            w_out = od_w_out[o]
        weights = (w_out.astype(BF16), ffn_w_gu[i].astype(BF16), ffn_w_down[i].astype(BF16))
        if not last:
            h = _post(h, a, bb, mods[i], norm_ffn_g[i], *weights, tm, n_ctx_tiles)
    if a.shape[1] != n_lat:
        a, bb = a[:, ctx_len:], bb[:, ctx_len:]
    return _post_final(h, a, bb, mods[depth - 1], norm_ffn_g[depth - 1], *weights, final_norm_g, tm, n_ctx_tiles)
```

```python
import functools
import math

import numpy as np
import jax
import jax.numpy as jnp
from jax import lax
from jax.experimental import pallas as pl
from jax.experimental.pallas import tpu as pltpu

F32 = jnp.float32
BF16 = jnp.bfloat16
HIGHEST = lax.Precision.HIGHEST

GRID_W = 64
EPS = 1e-6
ROPE_BASE = 10000.0
HG_HEAD_DIM = 128
HG_CHUNK = 64
HY_BANDS = 16
HY_EMB = 2 * HY_BANDS + 1
HY_TARGET = 1e-2
HY_STEEP_PCT = 0.3
HY_GENTLE_PCT = 1.5
DA_HEAD_DIM = 64
MLA_HEADS = 4
MLA_NOPE = 64
MLA_ROPE = 32
MLA_V = 128
LOG2_E = 1.4426950408889634

LANES = 128
SUBLANES = 8
BF16_SUBLANES = 16
V7X_VMEM_LIMIT_BYTES = 56 * 1024 * 1024
MOD_ROWS = 16
FFT_N2 = 128
FFT_PITCH = FFT_N2 + SUBLANES
FFT_UNROLL = 16
HGRN_LOCKSTEP = 1
HGRN_UNROLL = 4
HGRN_TILE = 128
ATTN_KEY_BLOCK = 256
ATTN_QUERY_TILE = 2048


def _params(*sem):
    return pltpu.CompilerParams(dimension_semantics=sem, vmem_limit_bytes=V7X_VMEM_LIMIT_BYTES)


def _dot(a, b, precision=None):
    return jnp.dot(a, b, preferred_element_type=F32, precision=precision)


def _dot3(a, b):
    ah, bh = a.astype(BF16), b.astype(BF16)
    al, bl = (a - ah.astype(F32)).astype(BF16), (b - bh.astype(F32)).astype(BF16)
    return _dot(ah, bh) + _dot(al, bh) + _dot(ah, bl)


def _dot_nt(a, b):
    return lax.dot_general(a, b, (((1,), (1,)), ((), ())), preferred_element_type=F32)


def _dot_tn(a, b):
    return lax.dot_general(a, b, (((0,), (0,)), ((), ())), preferred_element_type=F32)


def _rms(x):
    return x * lax.rsqrt(jnp.mean(x * x, axis=-1, keepdims=True) + EPS)


def _neg_abs(x):
    bits = lax.bitcast_convert_type(x, jnp.int32) | jnp.int32(-2 ** 31)
    return lax.bitcast_convert_type(bits, F32)


def _silu(x):
    return x * jax.nn.sigmoid(x)


def _tile_mod(mod_ref, modc_ref, n_ctx_tiles):
    return jnp.where(pl.program_id(1) < n_ctx_tiles, modc_ref[...], mod_ref[...])


def _ada_kernel(c_ref, w_ref, b_ref, o_ref):
    o_ref[0] = _dot(_silu(c_ref[...]), w_ref[0], HIGHEST) + b_ref[0]


def _ada(cc, ada_w, ada_b):
    depth, d, n = ada_w.shape
    rows = cc.shape[0]
    tn = n // 4
    return pl.pallas_call(
        _ada_kernel,
        out_shape=jax.ShapeDtypeStruct((depth, rows, n), F32),
        grid=(depth, n // tn),
        in_specs=[
            pl.BlockSpec((rows, d), lambda i, j: (0, 0)),
            pl.BlockSpec((1, d, tn), lambda i, j: (i, 0, j)),
            pl.BlockSpec((1, 1, tn), lambda i, j: (i, 0, j)),
        ],
        out_specs=pl.BlockSpec((1, rows, tn), lambda i, j: (i, 0, j)),
        compiler_params=_params("arbitrary", "arbitrary"),
        name="ada_mod",
    )(cc, ada_w, ada_b.reshape(depth, 1, n))


def _tok_spec(tm, width):
    return pl.BlockSpec((1, tm, width), lambda b, t: (b, t, 0))


def _const_spec(shape):
    zeros = (0,) * len(shape)
    return pl.BlockSpec(shape, lambda b, t: zeros)


def _mod_specs(d, n_batch):
    lat = pl.BlockSpec((None, 6, d), lambda b, t: (b, 0, 0))
    ctx = pl.BlockSpec((None, 6, d), lambda b, t: (n_batch, 0, 0))
    return lat, ctx


def _stream(h, tm, n_ctx_tiles):
    if not isinstance(h, tuple):
        b, t, d = h.shape
        return b, t, d, [_tok_spec(tm, d)], [h]
    ctx, x = h
    b, n_lat, d = x.shape
    specs = [pl.BlockSpec((1, tm, d), lambda bi, ti: (bi, jnp.minimum(ti, n_ctx_tiles - 1), 0)),
             pl.BlockSpec((1, tm, d), lambda bi, ti: (bi, jnp.maximum(ti - n_ctx_tiles, 0), 0))]
    return b, ctx.shape[1] + n_lat, d, specs, [ctx, x]


def _stream_tile(refs, n_ctx_tiles):
    if len(refs) == 1:
        return refs[0][0]
    return jnp.where(pl.program_id(1) < n_ctx_tiles, refs[0][0], refs[1][0])


def _in_even_kernel(*refs, n_ctx_tiles, n_stream):
    mod_ref, modc_ref, g_ref, w_ref, o_ref = refs[n_stream:]
    m = _tile_mod(mod_ref, modc_ref, n_ctx_tiles)
    xn = (_rms(_stream_tile(refs[:n_stream], n_ctx_tiles)) * g_ref[...]) * (1.0 + m[1:2]) + m[0:1]
    o_ref[0] = _dot(xn.astype(BF16), w_ref[...])


def _in_even(h, mods, g, w, tm, n_ctx_tiles):
    b, t, d, h_specs, h_args = _stream(h, tm, n_ctx_tiles)
    n = w.shape[1]
    lat, ctx = _mod_specs(d, b)
    return pl.pallas_call(
        functools.partial(_in_even_kernel, n_ctx_tiles=n_ctx_tiles, n_stream=len(h_args)),
        out_shape=jax.ShapeDtypeStruct((b, t, n), F32),
        grid=(b, t // tm),
        in_specs=h_specs + [lat, ctx, _const_spec((1, d)), _const_spec((d, n))],
        out_specs=_tok_spec(tm, n),
        compiler_params=_params("arbitrary", "arbitrary"),
        name="even_in_proj",
    )(*h_args, mods, mods, g.reshape(1, d), w)


def _post_body(m, h, a, bb, g_ref, wo_ref, wgu_ref, wd_ref, hidden):
    half = a.shape[-1]
    y = _dot(a, wo_ref[:half, :]) + _dot(bb, wo_ref[half:, :])
    h1 = h + m[2:3] * y
    xn = (_rms(h1) * g_ref[...]) * (1.0 + m[4:5]) + m[3:4]
    gu = _dot(xn.astype(BF16), wgu_ref[...])
    act = _silu(gu[:, :hidden]) * gu[:, hidden:]
    return h1 + m[5:6] * _dot(act.astype(BF16), wd_ref[...])


def _post_kernel(*refs, n_ctx_tiles, hidden, n_stream):
    a_ref, b_ref, mod_ref, modc_ref, g_ref, wo_ref, wgu_ref, wd_ref, o_ref = refs[n_stream:]
    m = _tile_mod(mod_ref, modc_ref, n_ctx_tiles)
    h = _stream_tile(refs[:n_stream], n_ctx_tiles)
    o_ref[0] = _post_body(m, h, a_ref[0], b_ref[0], g_ref, wo_ref, wgu_ref, wd_ref, hidden)


def _post_final_kernel(h_ref, a_ref, b_ref, mod_ref, g_ref, wo_ref, wgu_ref, wd_ref, fg_ref, o_ref,
                       *, hidden):
    h2 = _post_body(mod_ref[...], h_ref[0], a_ref[0], b_ref[0], g_ref, wo_ref, wgu_ref, wd_ref, hidden)
    o_ref[0] = _rms(h2) * fg_ref[...]


def _post(h, a, bb, mods, g, wo, wgu, wd, tm, n_ctx_tiles):
    b, t, d, h_specs, h_args = _stream(h, tm, n_ctx_tiles)
    half = a.shape[-1]
    hidden = wd.shape[0]
    lat, ctx = _mod_specs(d, b)
    return pl.pallas_call(
        functools.partial(_post_kernel, n_ctx_tiles=n_ctx_tiles, hidden=hidden, n_stream=len(h_args)),
        out_shape=jax.ShapeDtypeStruct((b, t, d), F32),
        grid=(b, t // tm),
        in_specs=h_specs + [_tok_spec(tm, half), _tok_spec(tm, half), lat, ctx,
                            _const_spec((1, d)), _const_spec((2 * half, d)), _const_spec((d, 2 * hidden)),
                            _const_spec((hidden, d))],
        out_specs=_tok_spec(tm, d),
        compiler_params=_params("arbitrary", "arbitrary"),
        name="out_proj_ffn",
    )(*h_args, a, bb, mods, mods, g.reshape(1, d), wo, wgu, wd)


def _post_final(h, a_lat, b_lat, mods, g, wo, wgu, wd, final_g, tm, n_ctx_tiles):
    b, t, d = h.shape
    n_lat = a_lat.shape[1]
    half = a_lat.shape[-1]
    hidden = wd.shape[0]
    lat, _ = _mod_specs(d, b)
    return pl.pallas_call(
        functools.partial(_post_final_kernel, hidden=hidden),
        out_shape=jax.ShapeDtypeStruct((b, n_lat, d), F32),
        grid=(b, n_lat // tm),
        in_specs=[pl.BlockSpec((1, tm, d), lambda bi, ti: (bi, ti + n_ctx_tiles, 0)),
                  _tok_spec(tm, half), _tok_spec(tm, half), lat,
                  _const_spec((1, d)), _const_spec((2 * half, d)), _const_spec((d, 2 * hidden)),
                  _const_spec((hidden, d)), _const_spec((1, d))],
        out_specs=_tok_spec(tm, d),
        compiler_params=_params("arbitrary", "arbitrary"),
        name="out_proj_ffn_final",
    )(h, a_lat, b_lat, mods, g.reshape(1, d), wo, wgu, wd, final_g.reshape(1, d))


def _hgrn_tables(tile):
    c = HG_CHUNK
    r = np.arange(c)
    masks, bms = [], []
    m = c
    while m >= 2:
        upper = ((r % m) >= m // 2)[:, None]
        masks.append(np.broadcast_to(upper, (c, LANES)))
        bms.append(((r[:, None] // m) == (r[None, :] // m)) & upper & (~upper.T))
        m //= 2
    fwd = (r[None, :] <= r[:, None]).astype(np.float32)
    mall = np.stack([fwd, fwd[::-1, ::-1]])
    mall = np.concatenate([mall, mall], axis=2)
    mu = np.stack(masks).astype(np.float32)
    masku = np.tile(np.stack([mu, mu[:, ::-1]]), (1, 1, tile // c, 1))
    bmp = np.stack([np.concatenate([bms[i], bms[i + 1]], axis=1) for i in range(0, len(bms), 2)])
    bmp_bwd = np.stack([np.concatenate([bms[i][::-1, ::-1], bms[i + 1][::-1, ::-1]], axis=1)
                        for i in range(0, len(bms), 2)])
    return (jnp.asarray(mall, dtype=BF16), jnp.asarray(masku, dtype=F32),
            jnp.asarray(np.stack([bmp, bmp_bwd]), dtype=F32))


def _hgrn_kernel(q_ref, ff_ref, fb_ref, i_ref, g_ref, lbp_ref, ng_ref, mall_ref, masku_ref, bmp_ref,
                 o_ref, of_ref, ob_ref, *, layer, n_ctx_tiles, n_tiles):
    c = HG_CHUNK
    dk = HG_HEAD_DIM
    n_levels = masku_ref.shape[1]
    tile = masku_ref.shape[2]
    n_sub = tile // c
    lbp = lbp_ref[...]
    ex = jnp.exp(lbp - jnp.max(lbp, axis=0, keepdims=True))
    p = ex / jnp.sum(ex, axis=0, keepdims=True)
    lb = jnp.sum(p[:layer + 1], axis=0) - p[0]

    def sub(x, g):
        return x[g * c:(g + 1) * c]

    dirs = (0, 1)

    def stack_chunks(x):
        return jnp.concatenate([sub(x, g) for g in range(n_sub)], axis=1)

    def bwd_tile(t):
        if isinstance(t, int):
            return n_ctx_tiles - 1 - t if t < n_ctx_tiles else n_tiles - 1 - (t - n_ctx_tiles)
        return jnp.where(t < n_ctx_tiles, n_ctx_tiles - 1 - t, n_tiles - 1 - (t - n_ctx_tiles))

    def tile_rows(ti):
        start = ti * tile
        return pl.ds(start if isinstance(ti, int) else pl.multiple_of(start, tile), tile)

    def run_trips(trips, sts):
        lanes = [(s, d) for s in range(len(trips)) for d in dirs]
        n = range(len(lanes))
        tis = [bwd_tile(t) if d else t for t in trips for d in dirs]
        rows = [tile_rows(ti) for ti in tis]
        q = [q_ref[0, r, :] for r in rows]
        v = [i_ref[0, r, :] for r in rows]
        lbd = [lb[d:d + 1] for _, d in lanes]
        zf = [(ff_ref, fb_ref)[d][0, rows[l], :] for l, (_, d) in enumerate(lanes)]
        f = [lbd[l] + (1.0 - lbd[l]) * jax.nn.sigmoid(zf[l]) for l in n]
        lf = [jnp.log(x) * LOG2_E for x in f]
        k = [1.0 - x for x in f]
        hi = [x.astype(BF16) for x in lf]
        lo = [(lf[l] - hi[l].astype(F32)).astype(BF16) for l in n]
        e2 = [_dot(mall_ref[lanes[l][1]], jnp.concatenate([stack_chunks(hi[l]), stack_chunks(lo[l])], axis=0))
              for l in n]
        cum = [[e2[l][:, g * dk:(g + 1) * dk] for g in range(n_sub)] for l in n]

        def tail_decay(l, g):
            last = c - 1 if lanes[l][1] == 0 else 0
            return jnp.exp2(cum[l][g][last:last + 1] - cum[l][g])

        def level_decay(l, lvl):
            m = c >> lvl
            ref = m // 2 - 1 if lanes[l][1] == 0 else m // 2
            parts = []
            for g in range(n_sub):
                cg = cum[l][g]
                if m >= SUBLANES:
                    x = cg.reshape(c // m, m, dk)
                    mid = jnp.broadcast_to(x[:, ref:ref + 1, :], x.shape)
                else:
                    x = cg.reshape(c // SUBLANES, SUBLANES, dk)
                    srow = lax.broadcasted_iota(jnp.int32, x.shape, 1)
                    mid = jnp.where(srow < m, jnp.broadcast_to(x[:, ref:ref + 1, :], x.shape),
                                    jnp.broadcast_to(x[:, m + ref:m + ref + 1, :], x.shape))
                parts.append(jnp.exp2(_neg_abs(cg - mid.reshape(c, dk))))
            return jnp.concatenate(parts, axis=0)

        g_in = [jnp.concatenate([jnp.exp2(cum[l][g]) for g in range(n_sub)], axis=0) for l in n]
        qin = [(q[l] * g_in[l]).astype(BF16) for l in n]
        kout = [(k[l] * jnp.concatenate([tail_decay(l, g) for g in range(n_sub)], axis=0)).astype(BF16)
                for l in n]
        vb = [x.astype(BF16) for x in v]
        zero = jnp.zeros((c, dk), BF16)
        ds = [[_dot_tn(sub(vb[l], g), sub(kout[l], g)) for g in range(n_sub)] for l in n]
        qf = [q[l] * f[l] for l in n]
        w = [[None] * n_levels for _ in n]
        for lvl in range(n_levels):
            for l, (_, d) in enumerate(lanes):
                later = masku_ref[d, lvl] != 0.0
                if lvl == n_levels - 1:
                    w[l][lvl] = jnp.where(later, qf[l], k[l]).astype(BF16)
                else:
                    w[l][lvl] = (jnp.where(later, q[l], k[l]) * level_decay(l, lvl)).astype(BF16)
        scores = [[[None] * (n_levels // 2) for _ in range(n_sub)] for _ in n]
        for pair in range(n_levels // 2):
            lv_a, lv_b = 2 * pair, 2 * pair + 1
            for g in range(n_sub):
                for l, (_, d) in enumerate(lanes):
                    wa, wb = sub(w[l][lv_a], g), sub(w[l][lv_b], g)
                    rhs = jnp.concatenate([jnp.concatenate([wa, zero], axis=1),
                                           jnp.concatenate([zero, wb], axis=1)], axis=0)
                    raw = _dot_nt(jnp.concatenate([wa, wb], axis=1), rhs)
                    scores[l][g][pair] = raw * bmp_ref[d, pair]
        o = [jnp.concatenate([_dot(sum(scores[l][g]).astype(BF16),
                                   jnp.concatenate([sub(vb[l], g)] * 2, axis=0))
                              for g in range(n_sub)], axis=0)
             + jnp.sum(q[l] * k[l], axis=-1, keepdims=True) * v[l] for l in n]
        order = [list(range(n_sub)), list(reversed(range(n_sub)))]
        entering = [[None] * n_sub for _ in n]
        sts = list(sts)
        for l, (_, d) in enumerate(lanes):
            for g in order[d]:
                entering[l][g] = sts[d].astype(BF16)
                last = (g + 1) * c - 1 if d == 0 else g * c
                sts[d] = sts[d] * g_in[l][last:last + 1] + ds[l][g]
        for l, (_, d) in enumerate(lanes):
            out = jnp.concatenate([sub(o[l], g) + _dot_nt(sub(qin[l], g), entering[l][g])
                                   for g in range(n_sub)], axis=0)
            (of_ref, ob_ref)[d][rows[l], :] = out
        return tuple(sts)

    def body(i, carry):
        return run_trips([i * HGRN_LOCKSTEP + s for s in range(HGRN_LOCKSTEP)], carry)

    s0 = jnp.zeros((dk, dk), F32)
    sts = lax.fori_loop(0, n_tiles // HGRN_LOCKSTEP, body, (s0, s0), unroll=HGRN_UNROLL)
    rest = list(range(n_tiles - n_tiles % HGRN_LOCKSTEP, n_tiles))
    if rest:
        run_trips(rest, sts)
    o = _rms(of_ref[...] + ob_ref[...]) * ng_ref[...]
    o_ref[0] = (o * _silu(g_ref[0])).astype(o_ref.dtype)


def _hgrn(z, hg_lower_bound, ng, layer, n_heads, ctx_len):
    b, t, _ = z.shape
    dk = HG_HEAD_DIM
    n_even = hg_lower_bound.shape[0]
    tile = math.gcd(math.gcd(ctx_len, t - ctx_len), HGRN_TILE)
    mall, masku, bm = _hgrn_tables(tile)

    def col(kind):
        return pl.BlockSpec((1, t, dk), lambda bi, hi: (bi, 0, kind * n_heads + hi))

    def const(arr):
        zeros = (0,) * arr.ndim
        return pl.BlockSpec(arr.shape, lambda bi, hi: zeros)

    return pl.pallas_call(
        functools.partial(_hgrn_kernel, layer=layer, n_ctx_tiles=ctx_len // tile, n_tiles=t // tile),
        out_shape=jax.ShapeDtypeStruct((b, t, n_heads * dk), BF16),
        grid=(b, n_heads),
        in_specs=[col(0), col(1), col(2), col(3), col(4),
                  pl.BlockSpec((n_even, 2, dk), lambda bi, hi: (0, 0, hi)),
                  const(ng.reshape(1, dk)), const(mall), const(masku), const(bm)],
        out_specs=pl.BlockSpec((1, t, dk), lambda bi, hi: (bi, 0, hi)),
        scratch_shapes=[pltpu.VMEM((t, dk), F32), pltpu.VMEM((t, dk), F32)],
        compiler_params=_params("arbitrary", "arbitrary"),
        name="hgrn2_scan",
    )(z, z, z, z, z, hg_lower_bound, ng.reshape(1, dk), mall, masku, bm)


def _hy_pre_kernel(x0_ref, x1_ref, v_ref, w0_ref, w1_ref, wv_ref, b0_ref, b1_ref, bv_ref,
                   x0c_ctx_ref, x0c_lat_ref, u_ctx_ref, u_lat_ref, *, ctx_len):
    t = x0_ref.shape[1]
    row = lax.broadcasted_iota(jnp.int32, (t, LANES), 0)
    first = (row == 0) | (row == ctx_len)
    final = (row == ctx_len - 1) | (row == t - 1)

    def short_conv(z_ref, w_ref, b_ref):
        z = z_ref[0]
        prev = jnp.where(first, 0.0, pltpu.roll(z, 1, 0))
        nxt = jnp.where(final, 0.0, pltpu.roll(z, t - 1, 0))
        w = w_ref[...]
        return prev * w[0:1] + z * w[1:2] + nxt * w[2:3] + b_ref[...]

    x0c = short_conv(x0_ref, w0_ref, b0_ref).astype(x0c_ctx_ref.dtype)
    u = (short_conv(x1_ref, w1_ref, b1_ref) * short_conv(v_ref, wv_ref, bv_ref)).astype(u_ctx_ref.dtype)
    x0c_ctx_ref[0] = x0c[:ctx_len]
    x0c_lat_ref[0] = x0c[ctx_len:]
    u_ctx_ref[0] = u[:ctx_len]
    u_lat_ref[0] = u[ctx_len:]


def _hy_pre(z, short_w, short_b, col0, width, ctx_len):
    b, t, _ = z.shape
    nb = width // LANES
    c0 = col0 // LANES

    def zcol(kind):
        return pl.BlockSpec((1, t, LANES), lambda bi, j: (bi, 0, c0 + kind * nb + j))

    def wcol(kind, rows):
        return pl.BlockSpec((rows, LANES), lambda bi, j: (0, kind * nb + j))

    def seg(rows):
        return (jax.ShapeDtypeStruct((b, rows, width), BF16),
                pl.BlockSpec((1, rows, LANES), lambda bi, j: (bi, 0, j)))

    (ctx_shape, ctx_spec), (lat_shape, lat_spec) = seg(ctx_len), seg(t - ctx_len)
    sb = short_b.reshape(1, -1)
    return pl.pallas_call(
        functools.partial(_hy_pre_kernel, ctx_len=ctx_len),
        out_shape=(ctx_shape, lat_shape, ctx_shape, lat_shape),
        grid=(b, nb),
        in_specs=[zcol(0), zcol(1), zcol(2), wcol(0, 3), wcol(1, 3), wcol(2, 3),
                  wcol(0, 1), wcol(1, 1), wcol(2, 1)],
        out_specs=(ctx_spec, lat_spec, ctx_spec, lat_spec),
        compiler_params=_params("arbitrary", "arbitrary"),
        name="hyena_short_conv",
    )(z, z, z, short_w, short_w, short_w, sb, sb, sb)


def _filter_features(length, n):
    p = np.arange(n)
    is_f = p < length
    is_b = p > n - length
    lag = np.where(is_f, p, np.where(is_b, n - 1 - p, 0))
    tt = np.linspace(0.0, 1.0, length, dtype=np.float32)[lag][:, None]
    w = (2.0 * math.pi * lag.astype(np.float32) / length)[:, None].astype(np.float32)
    bands = np.linspace(1e-4, HY_BANDS - 1, HY_BANDS, dtype=np.float32)[None, :]
    feat = np.concatenate([tt, np.cos(bands * w), -np.sin(bands * w), is_f[:, None], is_b[:, None]],
                          axis=-1).astype(np.float32)
    pad = (-feat.shape[1]) % BF16_SUBLANES
    return jnp.asarray(np.pad(feat, ((0, 0), (0, pad))))


def _filt_kernel(feat_ref, w1_ref, b1_ref, fr1_ref, w2_ref, b2_ref, fr2_ref,
                 w3f_ref, w3b_ref, dl_ref, o_ref, hid_ref):
    z = feat_ref[...]
    mf = z[:, HY_EMB:HY_EMB + 1]
    mb = z[:, HY_EMB + 1:HY_EMB + 2]

    @pl.when(pl.program_id(0) == 0)
    def _():
        h1 = jnp.sin(fr1_ref[...] * (_dot3(z, w1_ref[...]) + b1_ref[...]))
        hid_ref[...] = jnp.sin(fr2_ref[...] * (_dot3(h1, w2_ref[...]) + b2_ref[...]))

    hid = hid_ref[...]
    hf = _dot3(hid, w3f_ref[...])
    hb = _dot3(hid, w3b_ref[...])
    win = jnp.exp(-z[:, 0:1] * dl_ref[...])
    f = (mf * hf + mb * hb) * win
    o_ref[...] = f / jnp.sum(jnp.abs(f), axis=0, keepdims=True)


def _hyena_filter(length, n, w1, b1, fr1, w2, b2, fr2, w3, width):
    feat = _filter_features(length, n)
    nf = feat.shape[1]
    hid = w1.shape[1]
    w1p = jnp.pad(w1, ((0, nf - w1.shape[0]), (0, 0)))
    d_lo = -math.log(HY_TARGET) / HY_GENTLE_PCT
    d_hi = -math.log(HY_TARGET) / HY_STEEP_PCT
    deltas = jnp.asarray(np.linspace(d_lo, d_hi, width, dtype=np.float32)[None, :])
    nb = width // LANES

    def full(shape):
        zeros = (0,) * len(shape)
        return pl.BlockSpec(shape, lambda j: zeros)

    return pl.pallas_call(
        _filt_kernel,
        out_shape=jax.ShapeDtypeStruct((n, width), F32),
        grid=(nb,),
        in_specs=[full((n, nf)), full((nf, hid)), full((1, hid)),
                  full((1, hid)), full((hid, hid)), full((1, hid)), full((1, hid)),
                  pl.BlockSpec((hid, LANES), lambda j: (0, j)),
                  pl.BlockSpec((hid, LANES), lambda j: (0, nb + j)),
                  pl.BlockSpec((1, LANES), lambda j: (0, j))],
        out_specs=pl.BlockSpec((n, LANES), lambda j: (0, j)),
        scratch_shapes=[pltpu.VMEM((n, hid), F32)],
        compiler_params=_params("arbitrary"),
        name="hyena_filter_mlp",
    )(feat, w1p, b1.reshape(1, -1), fr1.reshape(1, -1), w2, b2.reshape(1, -1),
      fr2.reshape(1, -1), w3, w3, deltas)


def _dft_tables(n1, n1_in):
    n2 = FFT_N2
    n = n1 * n2
    a = np.arange(n1)
    j = np.arange(n2)
    ang = -2.0 * np.pi * (a[None, None, :] * a[None, :, None] / n1 + j[:, None, None] * a[None, :, None] / n)
    tr, ti = np.cos(ang), np.sin(ang)
    fwd_a = np.concatenate([np.concatenate([tr, -ti], 2), np.concatenate([ti, tr], 2)], 1)
    trt, tit = np.swapaxes(tr, 1, 2), -np.swapaxes(ti, 1, 2)
    inv_a = np.concatenate([np.concatenate([trt, -tit], 2), np.concatenate([tit, trt], 2)], 1)
    keep = np.concatenate([np.arange(n1_in), n1 + np.arange(n1_in)])
    ang2 = -2.0 * np.pi * (j[:, None] * j[None, :]) / n2
    cr, ci = np.cos(ang2), np.sin(ang2)
    fwd_c = np.block([[cr, -ci], [ci, cr]])
    inv_c = np.block([[cr, ci], [-ci, cr]])
    real_a = np.concatenate([tr, ti], 1)
    return dict(fwd_a=fwd_a[:, :, keep], inv_a=inv_a[:, keep, :], fwd_c=fwd_c, inv_c=inv_c,
                real_a=real_a)


def _slab(idx):
    return pl.ds(pl.multiple_of(idx * FFT_PITCH, SUBLANES), FFT_N2)


def _fft_filter_kernel(f_ref, wa_ref, wc_ref, o_ref, scr_ref, *, n1):
    n2 = FFT_N2
    rows = 2 * n1
    scale = 1.0 / (n1 * n2)

    def split3(x):
        hi = x.astype(BF16)
        lo = (x - hi.astype(F32)).astype(BF16)
        return jnp.concatenate([hi, lo, hi], axis=0)

    def stage_a(j, carry):
        scr_ref[pl.ds(j, rows, stride=FFT_PITCH), :] = _dot(wa_ref[j], split3(f_ref[j]))
        return carry

    lax.fori_loop(0, n2, stage_a, 0, unroll=FFT_UNROLL)

    def stage_c(k, carry):
        x = jnp.concatenate([scr_ref[_slab(k), :], scr_ref[_slab(n1 + k), :]], axis=0)
        o_ref[k] = _dot(wc_ref[...], split3(x)) * scale
        return carry

    lax.fori_loop(0, n1, stage_c, 0, unroll=FFT_UNROLL)


def _split3_cols(w):
    hi = w.astype(np.float32).astype(jnp.bfloat16)
    lo = (w.astype(np.float32) - np.asarray(hi, np.float32)).astype(jnp.bfloat16)
    return jnp.asarray(np.concatenate([hi, hi, lo], axis=-1))


def _fft_filter(filt, n1, tables):
    n2 = FFT_N2
    width = filt.shape[1]
    ft = filt.reshape(n1, n2, width).transpose(1, 0, 2)
    wa = _split3_cols(tables["real_a"])
    wc = _split3_cols(tables["fwd_c"])
    return pl.pallas_call(
        functools.partial(_fft_filter_kernel, n1=n1),
        out_shape=jax.ShapeDtypeStruct((n1, 2 * n2, width), F32),
        grid=(width // LANES,),
        in_specs=[pl.BlockSpec((n2, n1, LANES), lambda c: (0, 0, c)),
                  pl.BlockSpec(wa.shape, lambda c: (0, 0, 0)),
                  pl.BlockSpec(wc.shape, lambda c: (0, 0))],
        out_specs=pl.BlockSpec((n1, 2 * n2, LANES), lambda c: (0, 0, c)),
        scratch_shapes=[pltpu.VMEM((2 * n1 * FFT_PITCH, LANES), F32)],
        compiler_params=_params("arbitrary"),
        name="hyena_filter_dft",
    )(ft, wa, wc)


def _fftconv_kernel(u_ref, x0_ref, fh_ref, skip_ref, wfa_ref, wfc_ref, wic_ref, wia_ref, o_ref,
                    scr_ref, *, n1):
    n2 = FFT_N2
    rows = 2 * n1

    def stage_a(j, carry):
        scr_ref[pl.ds(j, rows, stride=FFT_PITCH), :] = _dot(wfa_ref[j], u_ref[0, j])
        return carry

    lax.fori_loop(0, n2, stage_a, 0, unroll=FFT_UNROLL)

    def stage_c(k, carry):
        x = jnp.concatenate([scr_ref[_slab(k), :], scr_ref[_slab(n1 + k), :]], axis=0)
        xf = _dot(wfc_ref[...], x.astype(BF16))
        fh = fh_ref[k]
        xr, xi, fr, fi = xf[:n2], xf[n2:], fh[:n2], fh[n2:]
        y = jnp.concatenate([xr * fr - xi * fi, xr * fi + xi * fr], axis=0)
        zt = _dot(wic_ref[...], y.astype(BF16))
        scr_ref[_slab(k), :] = zt[:n2]
        scr_ref[_slab(n1 + k), :] = zt[n2:]
        return carry

    lax.fori_loop(0, n1, stage_c, 0, unroll=FFT_UNROLL)

    def stage_ai(j, carry):
        x = scr_ref[pl.ds(j, rows, stride=FFT_PITCH), :]
        y = _dot(wia_ref[j], x.astype(BF16))
        u = u_ref[0, j].astype(F32)
        o_ref[0, j] = (x0_ref[0, j].astype(F32) * (y + skip_ref[...] * u)).astype(o_ref.dtype)
        return carry

    lax.fori_loop(0, n2, stage_ai, 0, unroll=FFT_UNROLL)


def _to_fft_layout(x, n1_in):
    b, length, c = x.shape
    n1_used = length // FFT_N2
    x = x.reshape(b // 2, 2, n1_used, FFT_N2, c)
    x = jnp.pad(x, ((0, 0), (0, 0), (0, n1_in - n1_used), (0, 0), (0, 0)))
    return x.transpose(0, 3, 1, 2, 4).reshape(b // 2, FFT_N2, 2 * n1_in, c)


def _from_fft_layout(y, length):
    p, n2, rows, c = y.shape
    n1_in = rows // 2
    y = y.reshape(p, n2, 2, n1_in, c).transpose(0, 2, 3, 1, 4)
    return y.reshape(2 * p, n1_in * n2, c)[:, :length]


def _fftconv(u, x0c, fh, skip, n1, n1_in, tables):
    _, length, width = u.shape
    ut = _to_fft_layout(u, n1_in)
    xt = _to_fft_layout(x0c, n1_in)
    pairs, n2, rin, _ = ut.shape
    wfa = jnp.asarray(tables["fwd_a"], dtype=BF16)
    wia = jnp.asarray(tables["inv_a"], dtype=BF16)
    wfc = jnp.asarray(tables["fwd_c"], dtype=BF16)
    wic = jnp.asarray(tables["inv_c"], dtype=BF16)
    data = pl.BlockSpec((1, n2, rin, LANES), lambda c, p: (p, 0, 0, c))

    def const(arr):
        zeros = (0,) * arr.ndim
        return pl.BlockSpec(arr.shape, lambda c, p: zeros, pipeline_mode=pl.Buffered(1))

    out = pl.pallas_call(
        functools.partial(_fftconv_kernel, n1=n1),
        out_shape=jax.ShapeDtypeStruct(ut.shape, BF16),
        grid=(width // LANES, pairs),
        in_specs=[data, data,
                  pl.BlockSpec((n1, 2 * n2, LANES), lambda c, p: (0, 0, c)),
                  pl.BlockSpec((1, LANES), lambda c, p: (0, c)),
                  const(wfa), const(wfc), const(wic), const(wia)],
        out_specs=data,
        scratch_shapes=[pltpu.VMEM((2 * n1 * FFT_PITCH, LANES), F32)],
        compiler_params=_params("arbitrary", "arbitrary"),
        name="hyena_dft_conv",
    )(ut, xt, fh, skip.reshape(1, width), wfa, wfc, wic, wia)
    return _from_fft_layout(out, length)


def _hyena(z, col0, width, ctx_len, short_w, short_b, filt_params, skip):
    x0c_ctx, x0c_lat, u_ctx, u_lat = _hy_pre(z, short_w, short_b, col0, width, ctx_len)
    outs = []
    for x0c, u in ((x0c_ctx, u_ctx), (x0c_lat, u_lat)):
        length = u.shape[1]
        n1_in = max(length // FFT_N2, SUBLANES)
        n1 = 2 * n1_in
        tables = _dft_tables(n1, n1_in)
        filt = _hyena_filter(length, n1 * FFT_N2, *filt_params, width)
        fh = _fft_filter(filt, n1, tables)
        outs.append(_fftconv(u, x0c, fh, skip, n1, n1_in, tables))
    return jnp.concatenate(outs, axis=1)


def _rope_tables(n_lat, ctx_len):
    tok = np.arange(n_lat)
    row, colp = tok // GRID_W, tok % GRID_W

    def axial(half):
        inv = ROPE_BASE ** (-np.arange(half, dtype=np.float32) / half)
        parts_c, parts_s = [], []
        for pos in (row, colp):
            ang = pos.astype(np.float32)[:, None] * inv
            parts_c += [np.cos(ang), np.cos(ang)]
            parts_s += [-np.sin(ang), np.sin(ang)]
        return np.concatenate(parts_c, 1), np.concatenate(parts_s, 1)

    dc, ds = axial(DA_HEAD_DIM // 4)
    mc, ms = axial(MLA_ROPE // 4)
    ones, zeros = np.ones((n_lat, MLA_NOPE), np.float32), np.zeros((n_lat, MLA_NOPE), np.float32)
    padc = np.ones((n_lat, LANES - MLA_NOPE - MLA_ROPE), np.float32)
    tabs = [np.concatenate([dc, dc], 1), np.concatenate([ds, ds], 1),
            np.concatenate([ones, mc, padc], 1), np.concatenate([zeros, ms, 0 * padc], 1)]
    out = []
    for i, tb in enumerate(tabs):
        ctx_rows = np.ones((ctx_len, LANES), np.float32) if i % 2 == 0 else np.zeros((ctx_len, LANES), np.float32)
        out.append(jnp.asarray(np.concatenate([ctx_rows, tb.astype(np.float32)], 0)))
    return out


def _with_ones(v):
    ones = jnp.ones((v.shape[0], LANES), v.dtype)
    parts = []
    for hd in range(v.shape[1] // LANES):
        parts += [v[:, hd * LANES:(hd + 1) * LANES], ones]
    return jnp.concatenate(parts, axis=1)


def _in_odd_kernel(x_ref, mod_ref, modc_ref, g_ref, w_ref, dc_ref, ds_ref, mc_ref, ms_ref,
                   qg_ref, kvg_ref, wuq_ref, wukv_ref,
                   qd_ref, kd_ref, vd_ref, qm_ref, km_ref, vm_ref, *, n_ctx_tiles, da_w, q_rank, kv_rank):
    m = _tile_mod(mod_ref, modc_ref, n_ctx_tiles)
    xn = (_rms(x_ref[0]) * g_ref[...]) * (1.0 + m[1:2]) + m[0:1]
    z = _dot(xn.astype(BF16), w_ref[...])
    nrep = da_w // LANES
    dc = jnp.concatenate([dc_ref[...]] * nrep, axis=1)
    ds = jnp.concatenate([ds_ref[...]] * nrep, axis=1)
    mc = jnp.concatenate([mc_ref[...]] * MLA_HEADS, axis=1)
    ms = jnp.concatenate([ms_ref[...]] * MLA_HEADS, axis=1)
    sa = DA_HEAD_DIM ** -0.5 * LOG2_E
    sm = (MLA_NOPE + MLA_ROPE) ** -0.5 * LOG2_E
    lane = lax.broadcasted_iota(jnp.int32, (x_ref.shape[1], LANES), 1)

    def partner(x, half):
        first = (lane % (2 * half)) < half
        cols = []
        for j in range(x.shape[1] // LANES):
            xj = x[:, j * LANES:(j + 1) * LANES]
            cols.append(jnp.where(first, pltpu.roll(xj, LANES - half, 1), pltpu.roll(xj, half, 1)))
        return cols[0] if len(cols) == 1 else jnp.concatenate(cols, axis=1)

    o = 0
    zq = z[:, o:o + da_w]
    qd_ref[0] = ((zq * dc + partner(zq, DA_HEAD_DIM // 4) * ds) * sa).astype(BF16)
    o += da_w
    zk = z[:, o:o + da_w]
    kd_ref[0] = (zk * dc + partner(zk, DA_HEAD_DIM // 4) * ds).astype(BF16)
    o += da_w
    vd_ref[0] = _with_ones(z[:, o:o + da_w]).astype(BF16)
    o += da_w
    cq = _rms(z[:, o:o + q_rank]) * qg_ref[...]
    o += q_rank
    ckv = _rms(z[:, o:o + kv_rank]) * kvg_ref[...]
    o += kv_rank
    zr = z[:, o:o + LANES]
    kr = zr * mc_ref[...] + partner(zr, MLA_ROPE // 4) * ms_ref[...]
    mw = MLA_HEADS * LANES
    qu = _dot(cq.astype(BF16), wuq_ref[...])
    qm_ref[0] = ((qu * mc + partner(qu, MLA_ROPE // 4) * ms) * sm).astype(BF16)
    kvu = _dot(ckv.astype(BF16), wukv_ref[...])
    km_ref[0] = (kvu[:, :mw] + jnp.concatenate([kr] * MLA_HEADS, axis=1)).astype(BF16)
    vm_ref[0] = _with_ones(kvu[:, mw:]).astype(BF16)


def _odd_weights(w_in, w_uq, w_ukv, da_w, q_rank, kv_rank):
    o = 3 * da_w + q_rank + kv_rank
    krw = jnp.pad(w_in[:, o:], ((0, 0), (MLA_NOPE, LANES - MLA_NOPE - MLA_ROPE)))
    w_big = jnp.concatenate([w_in[:, :o], krw], axis=1)
    dq = MLA_NOPE + MLA_ROPE
    uq = jnp.pad(w_uq.reshape(q_rank, MLA_HEADS, dq), ((0, 0), (0, 0), (0, LANES - dq)))
    ukv = w_ukv.reshape(kv_rank, MLA_HEADS, MLA_NOPE + MLA_V)
    uk = jnp.pad(ukv[:, :, :MLA_NOPE], ((0, 0), (0, 0), (0, LANES - MLA_NOPE))).reshape(kv_rank, MLA_HEADS * LANES)
    uv = ukv[:, :, MLA_NOPE:].reshape(kv_rank, MLA_HEADS * MLA_V)
    return (w_big.astype(BF16), uq.reshape(q_rank, MLA_HEADS * LANES).astype(BF16),
            jnp.concatenate([uk, uv], axis=1).astype(BF16))


def _in_odd(h, mods, g, w_big, wuq, wukv, qg, kvg, tabs, tm, n_ctx_tiles, da_w):
    b, t, d = h.shape
    q_rank, kv_rank = qg.shape[0], kvg.shape[0]
    lat, ctx = _mod_specs(d, b)
    tab = pl.BlockSpec((tm, LANES), lambda bi, ti: (ti, 0))
    mw = MLA_HEADS * LANES

    def tok_major(width):
        return jax.ShapeDtypeStruct((b, t, width), BF16), _tok_spec(tm, width)

    outs, ospecs = zip(tok_major(da_w), tok_major(da_w), tok_major(2 * da_w),
                       tok_major(mw), tok_major(mw), tok_major(2 * mw))
    return pl.pallas_call(
        functools.partial(_in_odd_kernel, n_ctx_tiles=n_ctx_tiles, da_w=da_w, q_rank=q_rank, kv_rank=kv_rank),
        out_shape=tuple(outs),
        grid=(b, t // tm),
        in_specs=[_tok_spec(tm, d), lat, ctx, _const_spec((1, d)), _const_spec(w_big.shape),
                  tab, tab, tab, tab, _const_spec((1, q_rank)), _const_spec((1, kv_rank)),
                  _const_spec(wuq.shape), _const_spec(wukv.shape)],
        out_specs=tuple(ospecs),
        compiler_params=_params("arbitrary", "arbitrary"),
        name="odd_in_proj",
    )(h, mods, mods, g.reshape(1, d), w_big, *tabs, qg.reshape(1, -1), kvg.reshape(1, -1), wuq, wukv)


def _softmax_pv(qs, k_ref, v_ref):
    n_keys = k_ref.shape[1]
    starts = list(range(0, n_keys, ATTN_KEY_BLOCK))
    m = [None] * len(qs)
    acc = [None] * len(qs)
    for start in starts:
        size = min(ATTN_KEY_BLOCK, n_keys - start)
        k = k_ref[0, start:start + size, :]
        v1 = v_ref[0, start:start + size, :]
        for i, q in enumerate(qs):
            s = _dot_nt(q, k)
            m_blk = jnp.max(s, axis=-1, keepdims=True)
            if start == 0:
                m[i] = m_blk
                acc[i] = _dot(jnp.exp2(s - m_blk).astype(BF16), v1)
            else:
                m_new = jnp.maximum(m[i], m_blk)
                pv = _dot(jnp.exp2(s - m_new).astype(BF16), v1)
                acc[i] = jnp.exp2(m[i] - m_new) * acc[i] + pv
                m[i] = m_new
    return [a[:, :LANES] / a[:, LANES:] for a in acc]


def _attn_kernel(q_ref, k_ref, v_ref, lam_ref, sg_ref, *rest, diff, lam_init):
    o_ref = rest[-1]
    q = q_ref[0]
    if not diff:
        o_ref[0] = _softmax_pv([q], k_ref, v_ref)[0].astype(o_ref.dtype)
        return
    first = lax.broadcasted_iota(jnp.int32, q.shape, 1) < DA_HEAD_DIM
    zero = jnp.zeros_like(q)
    o1, o2 = _softmax_pv([jnp.where(first, q, zero), jnp.where(first, zero, q)], k_ref, v_ref)
    lp = lam_ref[...]
    lam = (jnp.exp(jnp.sum(lp[0:1] * lp[1:2], axis=-1, keepdims=True))
           - jnp.exp(jnp.sum(lp[2:3] * lp[3:4], axis=-1, keepdims=True)) + lam_init)
    o_ref[0] = (_rms(o1 - lam * o2) * sg_ref[...] * (1.0 - lam_init)).astype(o_ref.dtype)


def _attention(q, k, v1, lam_p, subln_g, q_rows, tq, n_keys, out_rows, out_start, diff, lam_init, earlier=None):
    b, _, width = q.shape
    heads = width // LANES
    q_start, q_stop = q_rows

    def rows_from(start):
        return pl.BlockSpec((pl.Element(1), pl.Element(tq), pl.Element(LANES)),
                            lambda bi, hi, ti: (bi, pl.multiple_of(start + ti * tq, math.gcd(start, tq)),
                                                pl.multiple_of(hi * LANES, LANES)))

    khead = pl.BlockSpec((1, n_keys, LANES), lambda bi, hi, ti: (bi, 0, hi))
    vhead = pl.BlockSpec((1, n_keys, 2 * LANES), lambda bi, hi, ti: (bi, 0, hi))

    def const(shape):
        zeros = (0,) * len(shape)
        return pl.BlockSpec(shape, lambda bi, hi, ti: zeros)

    in_specs = [rows_from(q_start), khead, vhead, const(lam_p.shape), const((1, LANES))]
    args = [q, k, v1, lam_p, subln_g.reshape(1, LANES)]
    aliases = {}
    if earlier is not None:
        in_specs.append(pl.BlockSpec(memory_space=pl.ANY))
        args.append(earlier)
        aliases = {len(args) - 1: 0}
    return pl.pallas_call(
        functools.partial(_attn_kernel, diff=diff, lam_init=lam_init),
        out_shape=jax.ShapeDtypeStruct((b, out_rows, width), BF16),
        grid=(b, heads, (q_stop - q_start) // tq),
        in_specs=in_specs,
        out_specs=rows_from(out_start),
        input_output_aliases=aliases,
        compiler_params=_params("arbitrary", "arbitrary", "arbitrary"),
        name="diff_attention" if diff else "mla_attention",
    )(*args)


def _attend_segments(q, k, v1, lam_p, subln_g, ctx_len, diff, lam_init, need_ctx):
    t = k.shape[1]
    n_lat = t - ctx_len
    tq_lat = math.gcd(n_lat, ATTN_QUERY_TILE)
    if not need_ctx:
        return _attention(q, k, v1, lam_p, subln_g, (ctx_len, t), tq_lat, t, n_lat, 0, diff, lam_init)
    o = _attention(q, k, v1, lam_p, subln_g, (ctx_len, t), tq_lat, t, t, ctx_len, diff, lam_init)
    return _attention(q, k, v1, lam_p, subln_g, (0, ctx_len), ctx_len, ctx_len, t, 0, diff, lam_init, earlier=o)


def kernel(x, c, ctx, c_ctx, ada_w, ada_b, norm_mix_g, norm_ffn_g, ffn_w_gu, ffn_w_down, ev_w_in, ev_w_out, hg_lower_bound, hg_out_norm_g, hy_short_w, hy_short_b, hy_filt_w1, hy_filt_b1, hy_filt_freq1, hy_filt_w2, hy_filt_b2, hy_filt_freq2, hy_filt_w3, hy_skip, od_w_in, od_w_out, da_lambda, da_subln_g, mla_q_norm_g, mla_w_uq, mla_kv_norm_g, mla_w_ukv, final_norm_g):
    n_batch, n_lat, d = x.shape
    ctx_len = ctx.shape[1]
    depth = ada_w.shape[0]
    assert n_batch % 2 == 0 and n_batch < MOD_ROWS
    assert n_lat % GRID_W == 0 and ctx_len % HG_CHUNK == 0 and n_lat % FFT_N2 == 0 and ctx_len % FFT_N2 == 0
    tm = math.gcd(math.gcd(ctx_len, n_lat), 256)
    n_ctx_tiles = ctx_len // tm
    hg_width = d // 2
    hy_width = d - hg_width
    da_w = d // 2
    q_rank, kv_rank = mla_q_norm_g.shape[1], mla_kv_norm_g.shape[1]

    cc = jnp.concatenate([c, c_ctx[None], jnp.zeros((MOD_ROWS - n_batch - 1, d), F32)], axis=0)
    mods = _ada(cc, ada_w, ada_b).reshape(depth, MOD_ROWS, 6, d)
    h = (ctx, x) if depth > 1 else jnp.concatenate([ctx, x], axis=1)
    rope_tabs = _rope_tables(n_lat, ctx_len)

    for i in range(depth):
        last = i == depth - 1
        if i % 2 == 0:
            e = i // 2
            z = _in_even(h, mods[i], norm_mix_g[i], ev_w_in[e].astype(BF16), tm, n_ctx_tiles)
            a = _hgrn(z, hg_lower_bound, hg_out_norm_g[e], e, hg_width // HG_HEAD_DIM, ctx_len)
            filt_params = (hy_filt_w1[e], hy_filt_b1[e], hy_filt_freq1[e], hy_filt_w2[e], hy_filt_b2[e],
                           hy_filt_freq2[e], hy_filt_w3[e])
            bb = _hyena(z, 5 * hg_width, hy_width, ctx_len, hy_short_w[e], hy_short_b[e], filt_params, hy_skip[e])
            w_out = ev_w_out[e]
        else:
            o = i // 2
            lam_init = 0.8 - 0.6 * math.exp(-0.3 * i)
            w_big, wuq, wukv = _odd_weights(od_w_in[o], mla_w_uq[o], mla_w_ukv[o], da_w, q_rank, kv_rank)
            qd, kd, vd, qm, km, vm = _in_odd(h, mods[i], norm_mix_g[i], w_big, wuq, wukv, mla_q_norm_g[o],
                                             mla_kv_norm_g[o], rope_tabs, tm, n_ctx_tiles, da_w)
            a = _attend_segments(qd, kd, vd, da_lambda[o], da_subln_g[o], ctx_len, True, lam_init, not last)
            bb = _attend_segments(qm, km, vm, da_lambda[o], da_subln_g[o], ctx_len, False, lam_init, not last)
            w_out = od_w_out[o]
        weights = (w_out.astype(BF16), ffn_w_gu[i].astype(BF16), ffn_w_down[i].astype(BF16))
        if not last:
            h = _post(h, a, bb, mods[i], norm_ffn_g[i], *weights, tm, n_ctx_tiles)
    if a.shape[1] != n_lat:
        a, bb = a[:, ctx_len:], bb[:, ctx_len:]
    return _post_final(h, a, bb, mods[depth - 1], norm_ffn_g[depth - 1], *weights, final_norm_g, tm, n_ctx_tiles)
```

```python
import functools
import math

import numpy as np
import jax
import jax.numpy as jnp
from jax import lax
from jax.experimental import pallas as pl
from jax.experimental.pallas import tpu as pltpu

F32 = jnp.float32
BF16 = jnp.bfloat16
HIGHEST = lax.Precision.HIGHEST

GRID_W = 64
EPS = 1e-6
ROPE_BASE = 10000.0
HG_HEAD_DIM = 128
HG_CHUNK = 64
HY_BANDS = 16
HY_EMB = 2 * HY_BANDS + 1
HY_TARGET = 1e-2
HY_STEEP_PCT = 0.3
HY_GENTLE_PCT = 1.5
DA_HEAD_DIM = 64
MLA_HEADS = 4
MLA_NOPE = 64
MLA_ROPE = 32
MLA_V = 128
LOG2_E = 1.4426950408889634

LANES = 128
SUBLANES = 8
BF16_SUBLANES = 16
V7X_VMEM_LIMIT_BYTES = 56 * 1024 * 1024
MOD_ROWS = 16
FFT_N2 = 128
FFT_PITCH = FFT_N2 + SUBLANES
FFT_UNROLL = 32
HGRN_LOCKSTEP = 1
HGRN_UNROLL = 3
ATTN_KEY_BLOCK = 256
ATTN_QUERY_TILE = 2048


def _params(*sem):
    return pltpu.CompilerParams(dimension_semantics=sem, vmem_limit_bytes=V7X_VMEM_LIMIT_BYTES)


def _dot(a, b, precision=None):
    return jnp.dot(a, b, preferred_element_type=F32, precision=precision)


def _dot3(a, b):
    ah, bh = a.astype(BF16), b.astype(BF16)
    al, bl = (a - ah.astype(F32)).astype(BF16), (b - bh.astype(F32)).astype(BF16)
    return _dot(ah, bh) + _dot(al, bh) + _dot(ah, bl)


def _dot_nt(a, b):
    return lax.dot_general(a, b, (((1,), (1,)), ((), ())), preferred_element_type=F32)


def _dot_tn(a, b):
    return lax.dot_general(a, b, (((0,), (0,)), ((), ())), preferred_element_type=F32)


def _rms(x):
    return x * lax.rsqrt(jnp.mean(x * x, axis=-1, keepdims=True) + EPS)


def _neg_abs(x):
    bits = lax.bitcast_convert_type(x, jnp.int32) | jnp.int32(-2 ** 31)
    return lax.bitcast_convert_type(bits, F32)


def _silu(x):
    return x * jax.nn.sigmoid(x)


def _tile_mod(mod_ref, modc_ref, n_ctx_tiles):
    return jnp.where(pl.program_id(1) < n_ctx_tiles, modc_ref[...], mod_ref[...])


def _ada_kernel(c_ref, w_ref, b_ref, o_ref):
    o_ref[0] = _dot(_silu(c_ref[...]), w_ref[0], HIGHEST) + b_ref[0]


def _ada(cc, ada_w, ada_b):
    depth, d, n = ada_w.shape
    rows = cc.shape[0]
    tn = n // 4
    return pl.pallas_call(
        _ada_kernel,
        out_shape=jax.ShapeDtypeStruct((depth, rows, n), F32),
        grid=(depth, n // tn),
        in_specs=[
            pl.BlockSpec((rows, d), lambda i, j: (0, 0)),
            pl.BlockSpec((1, d, tn), lambda i, j: (i, 0, j)),
            pl.BlockSpec((1, 1, tn), lambda i, j: (i, 0, j)),
        ],
        out_specs=pl.BlockSpec((1, rows, tn), lambda i, j: (i, 0, j)),
        compiler_params=_params("arbitrary", "arbitrary"),
        name="ada_mod",
    )(cc, ada_w, ada_b.reshape(depth, 1, n))


def _tok_spec(tm, width):
    return pl.BlockSpec((1, tm, width), lambda b, t: (b, t, 0))


def _const_spec(shape):
    zeros = (0,) * len(shape)
    return pl.BlockSpec(shape, lambda b, t: zeros)


def _mod_specs(d, n_batch):
    lat = pl.BlockSpec((None, 6, d), lambda b, t: (b, 0, 0))
    ctx = pl.BlockSpec((None, 6, d), lambda b, t: (n_batch, 0, 0))
    return lat, ctx


def _stream(h, tm, n_ctx_tiles):
    if not isinstance(h, tuple):
        b, t, d = h.shape
        return b, t, d, [_tok_spec(tm, d)], [h]
    ctx, x = h
    b, n_lat, d = x.shape
    specs = [pl.BlockSpec((1, tm, d), lambda bi, ti: (bi, jnp.minimum(ti, n_ctx_tiles - 1), 0)),
             pl.BlockSpec((1, tm, d), lambda bi, ti: (bi, jnp.maximum(ti - n_ctx_tiles, 0), 0))]
    return b, ctx.shape[1] + n_lat, d, specs, [ctx, x]


def _stream_tile(refs, n_ctx_tiles):
    if len(refs) == 1:
        return refs[0][0]
    return jnp.where(pl.program_id(1) < n_ctx_tiles, refs[0][0], refs[1][0])


def _in_even_kernel(*refs, n_ctx_tiles, n_stream):
    mod_ref, modc_ref, g_ref, w_ref, o_ref = refs[n_stream:]
    m = _tile_mod(mod_ref, modc_ref, n_ctx_tiles)
    xn = (_rms(_stream_tile(refs[:n_stream], n_ctx_tiles)) * g_ref[...]) * (1.0 + m[1:2]) + m[0:1]
    o_ref[0] = _dot(xn.astype(BF16), w_ref[...])


def _in_even(h, mods, g, w, tm, n_ctx_tiles):
    b, t, d, h_specs, h_args = _stream(h, tm, n_ctx_tiles)
    n = w.shape[1]
    lat, ctx = _mod_specs(d, b)
    return pl.pallas_call(
        functools.partial(_in_even_kernel, n_ctx_tiles=n_ctx_tiles, n_stream=len(h_args)),
        out_shape=jax.ShapeDtypeStruct((b, t, n), F32),
        grid=(b, t // tm),
        in_specs=h_specs + [lat, ctx, _const_spec((1, d)), _const_spec((d, n))],
        out_specs=_tok_spec(tm, n),
        compiler_params=_params("arbitrary", "arbitrary"),
        name="even_in_proj",
    )(*h_args, mods, mods, g.reshape(1, d), w)


def _post_body(m, h, a, bb, g_ref, wo_ref, wgu_ref, wd_ref, hidden):
    half = a.shape[-1]
    y = _dot(a, wo_ref[:half, :]) + _dot(bb, wo_ref[half:, :])
    h1 = h + m[2:3] * y
    xn = (_rms(h1) * g_ref[...]) * (1.0 + m[4:5]) + m[3:4]
    gu = _dot(xn.astype(BF16), wgu_ref[...])
    act = _silu(gu[:, :hidden]) * gu[:, hidden:]
    return h1 + m[5:6] * _dot(act.astype(BF16), wd_ref[...])


def _post_kernel(*refs, n_ctx_tiles, hidden, n_stream):
    a_ref, b_ref, mod_ref, modc_ref, g_ref, wo_ref, wgu_ref, wd_ref, o_ref = refs[n_stream:]
    m = _tile_mod(mod_ref, modc_ref, n_ctx_tiles)
    h = _stream_tile(refs[:n_stream], n_ctx_tiles)
    o_ref[0] = _post_body(m, h, a_ref[0], b_ref[0], g_ref, wo_ref, wgu_ref, wd_ref, hidden)


def _post_final_kernel(h_ref, a_ref, b_ref, mod_ref, g_ref, wo_ref, wgu_ref, wd_ref, fg_ref, o_ref,
                       *, hidden):
    h2 = _post_body(mod_ref[...], h_ref[0], a_ref[0], b_ref[0], g_ref, wo_ref, wgu_ref, wd_ref, hidden)
    o_ref[0] = _rms(h2) * fg_ref[...]


def _post(h, a, bb, mods, g, wo, wgu, wd, tm, n_ctx_tiles):
    b, t, d, h_specs, h_args = _stream(h, tm, n_ctx_tiles)
    half = a.shape[-1]
    hidden = wd.shape[0]
    lat, ctx = _mod_specs(d, b)
    return pl.pallas_call(
        functools.partial(_post_kernel, n_ctx_tiles=n_ctx_tiles, hidden=hidden, n_stream=len(h_args)),
        out_shape=jax.ShapeDtypeStruct((b, t, d), F32),
        grid=(b, t // tm),
        in_specs=h_specs + [_tok_spec(tm, half), _tok_spec(tm, half), lat, ctx,
                            _const_spec((1, d)), _const_spec((2 * half, d)), _const_spec((d, 2 * hidden)),
                            _const_spec((hidden, d))],
        out_specs=_tok_spec(tm, d),
        compiler_params=_params("arbitrary", "arbitrary"),
        name="out_proj_ffn",
    )(*h_args, a, bb, mods, mods, g.reshape(1, d), wo, wgu, wd)


def _post_final(h, a_lat, b_lat, mods, g, wo, wgu, wd, final_g, tm, n_ctx_tiles):
    b, t, d = h.shape
    n_lat = a_lat.shape[1]
    half = a_lat.shape[-1]
    hidden = wd.shape[0]
    lat, _ = _mod_specs(d, b)
    return pl.pallas_call(
        functools.partial(_post_final_kernel, hidden=hidden),
        out_shape=jax.ShapeDtypeStruct((b, n_lat, d), F32),
        grid=(b, n_lat // tm),
        in_specs=[pl.BlockSpec((1, tm, d), lambda bi, ti: (bi, ti + n_ctx_tiles, 0)),
                  _tok_spec(tm, half), _tok_spec(tm, half), lat,
                  _const_spec((1, d)), _const_spec((2 * half, d)), _const_spec((d, 2 * hidden)),
                  _const_spec((hidden, d)), _const_spec((1, d))],
        out_specs=_tok_spec(tm, d),
        compiler_params=_params("arbitrary", "arbitrary"),
        name="out_proj_ffn_final",
    )(h, a_lat, b_lat, mods, g.reshape(1, d), wo, wgu, wd, final_g.reshape(1, d))


def _hgrn_tables(tile):
    c = HG_CHUNK
    r = np.arange(c)
    masks, bms = [], []
    m = c
    while m >= 2:
        upper = ((r % m) >= m // 2)[:, None]
        masks.append(np.broadcast_to(upper, (c, LANES)))
        bms.append(((r[:, None] // m) == (r[None, :] // m)) & upper & (~upper.T))
        m //= 2
    fwd = (r[None, :] <= r[:, None]).astype(np.float32)
    mall = np.stack([fwd, fwd[::-1, ::-1]])
    mall = np.concatenate([mall, mall], axis=2)
    mu = np.stack(masks).astype(np.float32)
    masku = np.tile(np.stack([mu, mu[:, ::-1]]), (1, 1, tile // c, 1))
    bmp = np.stack([np.concatenate([bms[i], bms[i + 1]], axis=1) for i in range(0, len(bms), 2)])
    bmp_bwd = np.stack([np.concatenate([bms[i][::-1, ::-1], bms[i + 1][::-1, ::-1]], axis=1)
                        for i in range(0, len(bms), 2)])
    return (jnp.asarray(mall, dtype=BF16), jnp.asarray(masku, dtype=F32),
            jnp.asarray(np.stack([bmp, bmp_bwd]), dtype=F32))


def _hgrn_kernel(q_ref, ff_ref, fb_ref, i_ref, g_ref, lbp_ref, ng_ref, mall_ref, masku_ref, bmp_ref,
                 o_ref, of_ref, ob_ref, *, layer, n_ctx_tiles, n_tiles):
    c = HG_CHUNK
    dk = HG_HEAD_DIM
    n_levels = masku_ref.shape[1]
    tile = masku_ref.shape[2]
    n_sub = tile // c
    lbp = lbp_ref[...]
    ex = jnp.exp(lbp - jnp.max(lbp, axis=0, keepdims=True))
    p = ex / jnp.sum(ex, axis=0, keepdims=True)
    lb = jnp.sum(p[:layer + 1], axis=0) - p[0]

    def sub(x, g):
        return x[g * c:(g + 1) * c]

    dirs = (0, 1)

    def stack_chunks(x):
        return jnp.concatenate([sub(x, g) for g in range(n_sub)], axis=1)

    def bwd_tile(t):
        if isinstance(t, int):
            return n_ctx_tiles - 1 - t if t < n_ctx_tiles else n_tiles - 1 - (t - n_ctx_tiles)
        return jnp.where(t < n_ctx_tiles, n_ctx_tiles - 1 - t, n_tiles - 1 - (t - n_ctx_tiles))

    def tile_rows(ti):
        start = ti * tile
        return pl.ds(start if isinstance(ti, int) else pl.multiple_of(start, tile), tile)

    def run_trips(trips, sts):
        lanes = [(s, d) for s in range(len(trips)) for d in dirs]
        n = range(len(lanes))
        tis = [bwd_tile(t) if d else t for t in trips for d in dirs]
        rows = [tile_rows(ti) for ti in tis]
        q = [q_ref[0, r, :] for r in rows]
        v = [i_ref[0, r, :] for r in rows]
        lbd = [lb[d:d + 1] for _, d in lanes]
        zf = [(ff_ref, fb_ref)[d][0, rows[l], :] for l, (_, d) in enumerate(lanes)]
        f = [lbd[l] + (1.0 - lbd[l]) * jax.nn.sigmoid(zf[l]) for l in n]
        lf = [jnp.log(x) * LOG2_E for x in f]
        k = [1.0 - x for x in f]
        hi = [x.astype(BF16) for x in lf]
        lo = [(lf[l] - hi[l].astype(F32)).astype(BF16) for l in n]
        e2 = [_dot(mall_ref[lanes[l][1]], jnp.concatenate([stack_chunks(hi[l]), stack_chunks(lo[l])], axis=0))
              for l in n]
        cum = [[e2[l][:, g * dk:(g + 1) * dk] for g in range(n_sub)] for l in n]

        def tail_decay(l, g):
            last = c - 1 if lanes[l][1] == 0 else 0
            return jnp.exp2(cum[l][g][last:last + 1] - cum[l][g])

        def level_decay(l, lvl):
            m = c >> lvl
            ref = m // 2 - 1 if lanes[l][1] == 0 else m // 2
            parts = []
            for g in range(n_sub):
                cg = cum[l][g]
                if m >= SUBLANES:
                    x = cg.reshape(c // m, m, dk)
                    mid = jnp.broadcast_to(x[:, ref:ref + 1, :], x.shape)
                else:
                    x = cg.reshape(c // SUBLANES, SUBLANES, dk)
                    srow = lax.broadcasted_iota(jnp.int32, x.shape, 1)
                    mid = jnp.where(srow < m, jnp.broadcast_to(x[:, ref:ref + 1, :], x.shape),
                                    jnp.broadcast_to(x[:, m + ref:m + ref + 1, :], x.shape))
                parts.append(jnp.exp2(_neg_abs(cg - mid.reshape(c, dk))))
            return jnp.concatenate(parts, axis=0)

        g_in = [jnp.concatenate([jnp.exp2(cum[l][g]) for g in range(n_sub)], axis=0) for l in n]
        qin = [(q[l] * g_in[l]).astype(BF16) for l in n]
        kout = [(k[l] * jnp.concatenate([tail_decay(l, g) for g in range(n_sub)], axis=0)).astype(BF16)
                for l in n]
        vb = [x.astype(BF16) for x in v]
        zero = jnp.zeros((c, dk), BF16)
        ds = [[_dot_tn(sub(vb[l], g), sub(kout[l], g)) for g in range(n_sub)] for l in n]
        qf = [q[l] * f[l] for l in n]
        w = [[None] * n_levels for _ in n]
        for lvl in range(n_levels):
            for l, (_, d) in enumerate(lanes):
                later = masku_ref[d, lvl] != 0.0
                if lvl == n_levels - 1:
                    w[l][lvl] = jnp.where(later, qf[l], k[l]).astype(BF16)
                else:
                    w[l][lvl] = (jnp.where(later, q[l], k[l]) * level_decay(l, lvl)).astype(BF16)
        scores = [[[None] * (n_levels // 2) for _ in range(n_sub)] for _ in n]
        for pair in range(n_levels // 2):
            lv_a, lv_b = 2 * pair, 2 * pair + 1
            for g in range(n_sub):
                for l, (_, d) in enumerate(lanes):
                    wa, wb = sub(w[l][lv_a], g), sub(w[l][lv_b], g)
                    rhs = jnp.concatenate([jnp.concatenate([wa, zero], axis=1),
                                           jnp.concatenate([zero, wb], axis=1)], axis=0)
                    raw = _dot_nt(jnp.concatenate([wa, wb], axis=1), rhs)
                    scores[l][g][pair] = raw * bmp_ref[d, pair]
        o = [jnp.concatenate([_dot(sum(scores[l][g]).astype(BF16),
                                   jnp.concatenate([sub(vb[l], g)] * 2, axis=0))
                              for g in range(n_sub)], axis=0)
             + jnp.sum(q[l] * k[l], axis=-1, keepdims=True) * v[l] for l in n]
        order = [list(range(n_sub)), list(reversed(range(n_sub)))]
        entering = [[None] * n_sub for _ in n]
        sts = list(sts)
        for l, (_, d) in enumerate(lanes):
            for g in order[d]:
                entering[l][g] = sts[d].astype(BF16)
                last = (g + 1) * c - 1 if d == 0 else g * c
                sts[d] = sts[d] * g_in[l][last:last + 1] + ds[l][g]
        for l, (_, d) in enumerate(lanes):
            out = jnp.concatenate([sub(o[l], g) + _dot_nt(sub(qin[l], g), entering[l][g])
                                   for g in range(n_sub)], axis=0)
            (of_ref, ob_ref)[d][rows[l], :] = out
        return tuple(sts)

    def body(i, carry):
        return run_trips([i * HGRN_LOCKSTEP + s for s in range(HGRN_LOCKSTEP)], carry)

    s0 = jnp.zeros((dk, dk), F32)
    sts = lax.fori_loop(0, n_tiles // HGRN_LOCKSTEP, body, (s0, s0), unroll=HGRN_UNROLL)
    rest = list(range(n_tiles - n_tiles % HGRN_LOCKSTEP, n_tiles))
    if rest:
        run_trips(rest, sts)
    o = _rms(of_ref[...] + ob_ref[...]) * ng_ref[...]
    o_ref[0] = (o * _silu(g_ref[0])).astype(o_ref.dtype)


def _hgrn(z, hg_lower_bound, ng, layer, n_heads, ctx_len):
    b, t, _ = z.shape
    dk = HG_HEAD_DIM
    n_even = hg_lower_bound.shape[0]
    tile = math.gcd(math.gcd(ctx_len, t - ctx_len), 256)
    mall, masku, bm = _hgrn_tables(tile)

    def col(kind):
        return pl.BlockSpec((1, t, dk), lambda bi, hi: (bi, 0, kind * n_heads + hi))

    def const(arr):
        zeros = (0,) * arr.ndim
        return pl.BlockSpec(arr.shape, lambda bi, hi: zeros)

    return pl.pallas_call(
        functools.partial(_hgrn_kernel, layer=layer, n_ctx_tiles=ctx_len // tile, n_tiles=t // tile),
        out_shape=jax.ShapeDtypeStruct((b, t, n_heads * dk), BF16),
        grid=(b, n_heads),
        in_specs=[col(0), col(1), col(2), col(3), col(4),
                  pl.BlockSpec((n_even, 2, dk), lambda bi, hi: (0, 0, hi)),
                  const(ng.reshape(1, dk)), const(mall), const(masku), const(bm)],
        out_specs=pl.BlockSpec((1, t, dk), lambda bi, hi: (bi, 0, hi)),
        scratch_shapes=[pltpu.VMEM((t, dk), F32), pltpu.VMEM((t, dk), F32)],
        compiler_params=_params("arbitrary", "arbitrary"),
        name="hgrn2_scan",
    )(z, z, z, z, z, hg_lower_bound, ng.reshape(1, dk), mall, masku, bm)


def _hy_pre_kernel(x0_ref, x1_ref, v_ref, w0_ref, w1_ref, wv_ref, b0_ref, b1_ref, bv_ref,
                   x0c_ctx_ref, x0c_lat_ref, u_ctx_ref, u_lat_ref, *, ctx_len):
    t = x0_ref.shape[1]
    row = lax.broadcasted_iota(jnp.int32, (t, LANES), 0)
    first = (row == 0) | (row == ctx_len)
    final = (row == ctx_len - 1) | (row == t - 1)

    def short_conv(z_ref, w_ref, b_ref):
        z = z_ref[0]
        prev = jnp.where(first, 0.0, pltpu.roll(z, 1, 0))
        nxt = jnp.where(final, 0.0, pltpu.roll(z, t - 1, 0))
        w = w_ref[...]
        return prev * w[0:1] + z * w[1:2] + nxt * w[2:3] + b_ref[...]

    x0c = short_conv(x0_ref, w0_ref, b0_ref).astype(x0c_ctx_ref.dtype)
    u = (short_conv(x1_ref, w1_ref, b1_ref) * short_conv(v_ref, wv_ref, bv_ref)).astype(u_ctx_ref.dtype)
    x0c_ctx_ref[0] = x0c[:ctx_len]
    x0c_lat_ref[0] = x0c[ctx_len:]
    u_ctx_ref[0] = u[:ctx_len]
    u_lat_ref[0] = u[ctx_len:]


def _hy_pre(z, short_w, short_b, col0, width, ctx_len):
    b, t, _ = z.shape
    nb = width // LANES
    c0 = col0 // LANES

    def zcol(kind):
        return pl.BlockSpec((1, t, LANES), lambda bi, j: (bi, 0, c0 + kind * nb + j))

    def wcol(kind, rows):
        return pl.BlockSpec((rows, LANES), lambda bi, j: (0, kind * nb + j))

    def seg(rows):
        return (jax.ShapeDtypeStruct((b, rows, width), BF16),
                pl.BlockSpec((1, rows, LANES), lambda bi, j: (bi, 0, j)))

    (ctx_shape, ctx_spec), (lat_shape, lat_spec) = seg(ctx_len), seg(t - ctx_len)
    sb = short_b.reshape(1, -1)
    return pl.pallas_call(
        functools.partial(_hy_pre_kernel, ctx_len=ctx_len),
        out_shape=(ctx_shape, lat_shape, ctx_shape, lat_shape),
        grid=(b, nb),
        in_specs=[zcol(0), zcol(1), zcol(2), wcol(0, 3), wcol(1, 3), wcol(2, 3),
                  wcol(0, 1), wcol(1, 1), wcol(2, 1)],
        out_specs=(ctx_spec, lat_spec, ctx_spec, lat_spec),
        compiler_params=_params("arbitrary", "arbitrary"),
        name="hyena_short_conv",
    )(z, z, z, short_w, short_w, short_w, sb, sb, sb)


def _filter_features(length, n):
    p = np.arange(n)
    is_f = p < length
    is_b = p > n - length
    lag = np.where(is_f, p, np.where(is_b, n - 1 - p, 0))
    tt = np.linspace(0.0, 1.0, length, dtype=np.float32)[lag][:, None]
    w = (2.0 * math.pi * lag.astype(np.float32) / length)[:, None].astype(np.float32)
    bands = np.linspace(1e-4, HY_BANDS - 1, HY_BANDS, dtype=np.float32)[None, :]
    feat = np.concatenate([tt, np.cos(bands * w), -np.sin(bands * w), is_f[:, None], is_b[:, None]],
                          axis=-1).astype(np.float32)
    pad = (-feat.shape[1]) % BF16_SUBLANES
    return jnp.asarray(np.pad(feat, ((0, 0), (0, pad))))


def _filt_kernel(feat_ref, w1_ref, b1_ref, fr1_ref, w2_ref, b2_ref, fr2_ref,
                 w3f_ref, w3b_ref, dl_ref, o_ref, hid_ref):
    z = feat_ref[...]
    mf = z[:, HY_EMB:HY_EMB + 1]
    mb = z[:, HY_EMB + 1:HY_EMB + 2]

    @pl.when(pl.program_id(0) == 0)
    def _():
        h1 = jnp.sin(fr1_ref[...] * (_dot3(z, w1_ref[...]) + b1_ref[...]))
        hid_ref[...] = jnp.sin(fr2_ref[...] * (_dot3(h1, w2_ref[...]) + b2_ref[...]))

    hid = hid_ref[...]
    hf = _dot3(hid, w3f_ref[...])
    hb = _dot3(hid, w3b_ref[...])
    win = jnp.exp(-z[:, 0:1] * dl_ref[...])
    f = (mf * hf + mb * hb) * win
    o_ref[...] = f / jnp.sum(jnp.abs(f), axis=0, keepdims=True)


def _hyena_filter(length, n, w1, b1, fr1, w2, b2, fr2, w3, width):
    feat = _filter_features(length, n)
    nf = feat.shape[1]
    hid = w1.shape[1]
    w1p = jnp.pad(w1, ((0, nf - w1.shape[0]), (0, 0)))
    d_lo = -math.log(HY_TARGET) / HY_GENTLE_PCT
    d_hi = -math.log(HY_TARGET) / HY_STEEP_PCT
    deltas = jnp.asarray(np.linspace(d_lo, d_hi, width, dtype=np.float32)[None, :])
    nb = width // LANES

    def full(shape):
        zeros = (0,) * len(shape)
        return pl.BlockSpec(shape, lambda j: zeros)

    return pl.pallas_call(
        _filt_kernel,
        out_shape=jax.ShapeDtypeStruct((n, width), F32),
        grid=(nb,),
        in_specs=[full((n, nf)), full((nf, hid)), full((1, hid)),
                  full((1, hid)), full((hid, hid)), full((1, hid)), full((1, hid)),
                  pl.BlockSpec((hid, LANES), lambda j: (0, j)),
                  pl.BlockSpec((hid, LANES), lambda j: (0, nb + j)),
                  pl.BlockSpec((1, LANES), lambda j: (0, j))],
        out_specs=pl.BlockSpec((n, LANES), lambda j: (0, j)),
        scratch_shapes=[pltpu.VMEM((n, hid), F32)],
        compiler_params=_params("arbitrary"),
        name="hyena_filter_mlp",
    )(feat, w1p, b1.reshape(1, -1), fr1.reshape(1, -1), w2, b2.reshape(1, -1),
      fr2.reshape(1, -1), w3, w3, deltas)


def _dft_tables(n1, n1_in):
    n2 = FFT_N2
    n = n1 * n2
    a = np.arange(n1)
    j = np.arange(n2)
    ang = -2.0 * np.pi * (a[None, None, :] * a[None, :, None] / n1 + j[:, None, None] * a[None, :, None] / n)
    tr, ti = np.cos(ang), np.sin(ang)
    fwd_a = np.concatenate([np.concatenate([tr, -ti], 2), np.concatenate([ti, tr], 2)], 1)
    trt, tit = np.swapaxes(tr, 1, 2), -np.swapaxes(ti, 1, 2)
    inv_a = np.concatenate([np.concatenate([trt, -tit], 2), np.concatenate([tit, trt], 2)], 1)
    keep = np.concatenate([np.arange(n1_in), n1 + np.arange(n1_in)])
    ang2 = -2.0 * np.pi * (j[:, None] * j[None, :]) / n2
    cr, ci = np.cos(ang2), np.sin(ang2)
    fwd_c = np.block([[cr, -ci], [ci, cr]])
    inv_c = np.block([[cr, ci], [-ci, cr]])
    real_a = np.concatenate([tr, ti], 1)
    return dict(fwd_a=fwd_a[:, :, keep], inv_a=inv_a[:, keep, :], fwd_c=fwd_c, inv_c=inv_c,
                real_a=real_a)


def _slab(idx):
    return pl.ds(pl.multiple_of(idx * FFT_PITCH, SUBLANES), FFT_N2)


def _fft_filter_kernel(f_ref, wa_ref, wc_ref, o_ref, scr_ref, *, n1):
    n2 = FFT_N2
    rows = 2 * n1
    scale = 1.0 / (n1 * n2)

    def split3(x):
        hi = x.astype(BF16)
        lo = (x - hi.astype(F32)).astype(BF16)
        return jnp.concatenate([hi, lo, hi], axis=0)

    def stage_a(j, carry):
        scr_ref[pl.ds(j, rows, stride=FFT_PITCH), :] = _dot(wa_ref[j], split3(f_ref[j]))
        return carry

    lax.fori_loop(0, n2, stage_a, 0, unroll=FFT_UNROLL)

    def stage_c(k, carry):
        x = jnp.concatenate([scr_ref[_slab(k), :], scr_ref[_slab(n1 + k), :]], axis=0)
        o_ref[k] = _dot(wc_ref[...], split3(x)) * scale
        return carry

    lax.fori_loop(0, n1, stage_c, 0, unroll=FFT_UNROLL)


def _split3_cols(w):
    hi = w.astype(np.float32).astype(jnp.bfloat16)
    lo = (w.astype(np.float32) - np.asarray(hi, np.float32)).astype(jnp.bfloat16)
    return jnp.asarray(np.concatenate([hi, hi, lo], axis=-1))


def _fft_filter(filt, n1, tables):
    n2 = FFT_N2
    width = filt.shape[1]
    ft = filt.reshape(n1, n2, width).transpose(1, 0, 2)
    wa = _split3_cols(tables["real_a"])
    wc = _split3_cols(tables["fwd_c"])
    return pl.pallas_call(
        functools.partial(_fft_filter_kernel, n1=n1),
        out_shape=jax.ShapeDtypeStruct((n1, 2 * n2, width), F32),
        grid=(width // LANES,),
        in_specs=[pl.BlockSpec((n2, n1, LANES), lambda c: (0, 0, c)),
                  pl.BlockSpec(wa.shape, lambda c: (0, 0, 0)),
                  pl.BlockSpec(wc.shape, lambda c: (0, 0))],
        out_specs=pl.BlockSpec((n1, 2 * n2, LANES), lambda c: (0, 0, c)),
        scratch_shapes=[pltpu.VMEM((2 * n1 * FFT_PITCH, LANES), F32)],
        compiler_params=_params("arbitrary"),
        name="hyena_filter_dft",
    )(ft, wa, wc)


def _fftconv_kernel(u_ref, x0_ref, fh_ref, skip_ref, wfa_ref, wfc_ref, wic_ref, wia_ref, o_ref,
                    scr_ref, *, n1):
    n2 = FFT_N2
    rows = 2 * n1

    def stage_a(j, carry):
        scr_ref[pl.ds(j, rows, stride=FFT_PITCH), :] = _dot(wfa_ref[j], u_ref[0, j])
        return carry

    lax.fori_loop(0, n2, stage_a, 0, unroll=FFT_UNROLL)

    def stage_c(k, carry):
        x = jnp.concatenate([scr_ref[_slab(k), :], scr_ref[_slab(n1 + k), :]], axis=0)
        xf = _dot(wfc_ref[...], x.astype(BF16))
        fh = fh_ref[k]
        xr, xi, fr, fi = xf[:n2], xf[n2:], fh[:n2], fh[n2:]
        y = jnp.concatenate([xr * fr - xi * fi, xr * fi + xi * fr], axis=0)
        zt = _dot(wic_ref[...], y.astype(BF16))
        scr_ref[_slab(k), :] = zt[:n2]
        scr_ref[_slab(n1 + k), :] = zt[n2:]
        return carry

    lax.fori_loop(0, n1, stage_c, 0, unroll=FFT_UNROLL)

    def stage_ai(j, carry):
        x = scr_ref[pl.ds(j, rows, stride=FFT_PITCH), :]
        y = _dot(wia_ref[j], x.astype(BF16))
        u = u_ref[0, j].astype(F32)
        o_ref[0, j] = (x0_ref[0, j].astype(F32) * (y + skip_ref[...] * u)).astype(o_ref.dtype)
        return carry

    lax.fori_loop(0, n2, stage_ai, 0, unroll=FFT_UNROLL)


def _to_fft_layout(x, n1_in):
    b, length, c = x.shape
    n1_used = length // FFT_N2
    x = x.reshape(b // 2, 2, n1_used, FFT_N2, c)
    x = jnp.pad(x, ((0, 0), (0, 0), (0, n1_in - n1_used), (0, 0), (0, 0)))
    return x.transpose(0, 3, 1, 2, 4).reshape(b // 2, FFT_N2, 2 * n1_in, c)


def _from_fft_layout(y, length):
    p, n2, rows, c = y.shape
    n1_in = rows // 2
    y = y.reshape(p, n2, 2, n1_in, c).transpose(0, 2, 3, 1, 4)
    return y.reshape(2 * p, n1_in * n2, c)[:, :length]


def _fftconv(u, x0c, fh, skip, n1, n1_in, tables):
    _, length, width = u.shape
    ut = _to_fft_layout(u, n1_in)
    xt = _to_fft_layout(x0c, n1_in)
    pairs, n2, rin, _ = ut.shape
    wfa = jnp.asarray(tables["fwd_a"], dtype=BF16)
    wia = jnp.asarray(tables["inv_a"], dtype=BF16)
    wfc = jnp.asarray(tables["fwd_c"], dtype=BF16)
    wic = jnp.asarray(tables["inv_c"], dtype=BF16)
    data = pl.BlockSpec((1, n2, rin, LANES), lambda c, p: (p, 0, 0, c))

    def const(arr):
        zeros = (0,) * arr.ndim
        return pl.BlockSpec(arr.shape, lambda c, p: zeros, pipeline_mode=pl.Buffered(1))

    out = pl.pallas_call(
        functools.partial(_fftconv_kernel, n1=n1),
        out_shape=jax.ShapeDtypeStruct(ut.shape, BF16),
        grid=(width // LANES, pairs),
        in_specs=[data, data,
                  pl.BlockSpec((n1, 2 * n2, LANES), lambda c, p: (0, 0, c)),
                  pl.BlockSpec((1, LANES), lambda c, p: (0, c)),
                  const(wfa), const(wfc), const(wic), const(wia)],
        out_specs=data,
        scratch_shapes=[pltpu.VMEM((2 * n1 * FFT_PITCH, LANES), F32)],
        compiler_params=_params("arbitrary", "arbitrary"),
        name="hyena_dft_conv",
    )(ut, xt, fh, skip.reshape(1, width), wfa, wfc, wic, wia)
    return _from_fft_layout(out, length)


def _hyena(z, col0, width, ctx_len, short_w, short_b, filt_params, skip):
    x0c_ctx, x0c_lat, u_ctx, u_lat = _hy_pre(z, short_w, short_b, col0, width, ctx_len)
    outs = []
    for x0c, u in ((x0c_ctx, u_ctx), (x0c_lat, u_lat)):
        length = u.shape[1]
        n1_in = max(length // FFT_N2, SUBLANES)
        n1 = 2 * n1_in
        tables = _dft_tables(n1, n1_in)
        filt = _hyena_filter(length, n1 * FFT_N2, *filt_params, width)
        fh = _fft_filter(filt, n1, tables)
        outs.append(_fftconv(u, x0c, fh, skip, n1, n1_in, tables))
    return jnp.concatenate(outs, axis=1)


def _rope_tables(n_lat, ctx_len):
    tok = np.arange(n_lat)
    row, colp = tok // GRID_W, tok % GRID_W

    def axial(half):
        inv = ROPE_BASE ** (-np.arange(half, dtype=np.float32) / half)
        parts_c, parts_s = [], []
        for pos in (row, colp):
            ang = pos.astype(np.float32)[:, None] * inv
            parts_c += [np.cos(ang), np.cos(ang)]
            parts_s += [-np.sin(ang), np.sin(ang)]
        return np.concatenate(parts_c, 1), np.concatenate(parts_s, 1)

    dc, ds = axial(DA_HEAD_DIM // 4)
    mc, ms = axial(MLA_ROPE // 4)
    ones, zeros = np.ones((n_lat, MLA_NOPE), np.float32), np.zeros((n_lat, MLA_NOPE), np.float32)
    padc = np.ones((n_lat, LANES - MLA_NOPE - MLA_ROPE), np.float32)
    tabs = [np.concatenate([dc, dc], 1), np.concatenate([ds, ds], 1),
            np.concatenate([ones, mc, padc], 1), np.concatenate([zeros, ms, 0 * padc], 1)]
    out = []
    for i, tb in enumerate(tabs):
        ctx_rows = np.ones((ctx_len, LANES), np.float32) if i % 2 == 0 else np.zeros((ctx_len, LANES), np.float32)
        out.append(jnp.asarray(np.concatenate([ctx_rows, tb.astype(np.float32)], 0)))
    return out


def _with_ones(v):
    ones = jnp.ones((v.shape[0], LANES), v.dtype)
    parts = []
    for hd in range(v.shape[1] // LANES):
        parts += [v[:, hd * LANES:(hd + 1) * LANES], ones]
    return jnp.concatenate(parts, axis=1)


def _in_odd_kernel(x_ref, mod_ref, modc_ref, g_ref, w_ref, dc_ref, ds_ref, mc_ref, ms_ref,
                   qg_ref, kvg_ref, wuq_ref, wukv_ref,
                   qd_ref, kd_ref, vd_ref, qm_ref, km_ref, vm_ref, *, n_ctx_tiles, da_w, q_rank, kv_rank):
    m = _tile_mod(mod_ref, modc_ref, n_ctx_tiles)
    xn = (_rms(x_ref[0]) * g_ref[...]) * (1.0 + m[1:2]) + m[0:1]
    z = _dot(xn.astype(BF16), w_ref[...])
    nrep = da_w // LANES
    dc = jnp.concatenate([dc_ref[...]] * nrep, axis=1)
    ds = jnp.concatenate([ds_ref[...]] * nrep, axis=1)
    mc = jnp.concatenate([mc_ref[...]] * MLA_HEADS, axis=1)
    ms = jnp.concatenate([ms_ref[...]] * MLA_HEADS, axis=1)
    sa = DA_HEAD_DIM ** -0.5 * LOG2_E
    sm = (MLA_NOPE + MLA_ROPE) ** -0.5 * LOG2_E
    lane = lax.broadcasted_iota(jnp.int32, (x_ref.shape[1], LANES), 1)

    def partner(x, half):
        first = (lane % (2 * half)) < half
        cols = []
        for j in range(x.shape[1] // LANES):
            xj = x[:, j * LANES:(j + 1) * LANES]
            cols.append(jnp.where(first, pltpu.roll(xj, LANES - half, 1), pltpu.roll(xj, half, 1)))
        return cols[0] if len(cols) == 1 else jnp.concatenate(cols, axis=1)

    o = 0
    zq = z[:, o:o + da_w]
    qd_ref[0] = ((zq * dc + partner(zq, DA_HEAD_DIM // 4) * ds) * sa).astype(BF16)
    o += da_w
    zk = z[:, o:o + da_w]
    kd_ref[0] = (zk * dc + partner(zk, DA_HEAD_DIM // 4) * ds).astype(BF16)
    o += da_w
    vd_ref[0] = _with_ones(z[:, o:o + da_w]).astype(BF16)
    o += da_w
    cq = _rms(z[:, o:o + q_rank]) * qg_ref[...]
    o += q_rank
    ckv = _rms(z[:, o:o + kv_rank]) * kvg_ref[...]
    o += kv_rank
    zr = z[:, o:o + LANES]
    kr = zr * mc_ref[...] + partner(zr, MLA_ROPE // 4) * ms_ref[...]
    mw = MLA_HEADS * LANES
    qu = _dot(cq.astype(BF16), wuq_ref[...])
    qm_ref[0] = ((qu * mc + partner(qu, MLA_ROPE // 4) * ms) * sm).astype(BF16)
    kvu = _dot(ckv.astype(BF16), wukv_ref[...])
    km_ref[0] = (kvu[:, :mw] + jnp.concatenate([kr] * MLA_HEADS, axis=1)).astype(BF16)
    vm_ref[0] = _with_ones(kvu[:, mw:]).astype(BF16)


def _odd_weights(w_in, w_uq, w_ukv, da_w, q_rank, kv_rank):
    o = 3 * da_w + q_rank + kv_rank
    krw = jnp.pad(w_in[:, o:], ((0, 0), (MLA_NOPE, LANES - MLA_NOPE - MLA_ROPE)))
    w_big = jnp.concatenate([w_in[:, :o], krw], axis=1)
    dq = MLA_NOPE + MLA_ROPE
    uq = jnp.pad(w_uq.reshape(q_rank, MLA_HEADS, dq), ((0, 0), (0, 0), (0, LANES - dq)))
    ukv = w_ukv.reshape(kv_rank, MLA_HEADS, MLA_NOPE + MLA_V)
    uk = jnp.pad(ukv[:, :, :MLA_NOPE], ((0, 0), (0, 0), (0, LANES - MLA_NOPE))).reshape(kv_rank, MLA_HEADS * LANES)
    uv = ukv[:, :, MLA_NOPE:].reshape(kv_rank, MLA_HEADS * MLA_V)
    return (w_big.astype(BF16), uq.reshape(q_rank, MLA_HEADS * LANES).astype(BF16),
            jnp.concatenate([uk, uv], axis=1).astype(BF16))


def _in_odd(h, mods, g, w_big, wuq, wukv, qg, kvg, tabs, tm, n_ctx_tiles, da_w):
    b, t, d = h.shape
    q_rank, kv_rank = qg.shape[0], kvg.shape[0]
    lat, ctx = _mod_specs(d, b)
    tab = pl.BlockSpec((tm, LANES), lambda bi, ti: (ti, 0))
    mw = MLA_HEADS * LANES

    def tok_major(width):
        return jax.ShapeDtypeStruct((b, t, width), BF16), _tok_spec(tm, width)

    outs, ospecs = zip(tok_major(da_w), tok_major(da_w), tok_major(2 * da_w),
                       tok_major(mw), tok_major(mw), tok_major(2 * mw))
    return pl.pallas_call(
        functools.partial(_in_odd_kernel, n_ctx_tiles=n_ctx_tiles, da_w=da_w, q_rank=q_rank, kv_rank=kv_rank),
        out_shape=tuple(outs),
        grid=(b, t // tm),
        in_specs=[_tok_spec(tm, d), lat, ctx, _const_spec((1, d)), _const_spec(w_big.shape),
                  tab, tab, tab, tab, _const_spec((1, q_rank)), _const_spec((1, kv_rank)),
                  _const_spec(wuq.shape), _const_spec(wukv.shape)],
        out_specs=tuple(ospecs),
        compiler_params=_params("arbitrary", "arbitrary"),
        name="odd_in_proj",
    )(h, mods, mods, g.reshape(1, d), w_big, *tabs, qg.reshape(1, -1), kvg.reshape(1, -1), wuq, wukv)


def _softmax_pv(qs, k_ref, v_ref):
    n_keys = k_ref.shape[1]
    starts = list(range(0, n_keys, ATTN_KEY_BLOCK))
    m = [None] * len(qs)
    acc = [None] * len(qs)
    for start in starts:
        size = min(ATTN_KEY_BLOCK, n_keys - start)
        k = k_ref[0, start:start + size, :]
        v1 = v_ref[0, start:start + size, :]
        for i, q in enumerate(qs):
            s = _dot_nt(q, k)
            m_blk = jnp.max(s, axis=-1, keepdims=True)
            if start == 0:
                m[i] = m_blk
                acc[i] = _dot(jnp.exp2(s - m_blk).astype(BF16), v1)
            else:
                m_new = jnp.maximum(m[i], m_blk)
                pv = _dot(jnp.exp2(s - m_new).astype(BF16), v1)
                acc[i] = jnp.exp2(m[i] - m_new) * acc[i] + pv
                m[i] = m_new
    return [a[:, :LANES] / a[:, LANES:] for a in acc]


def _attn_kernel(q_ref, k_ref, v_ref, lam_ref, sg_ref, *rest, diff, lam_init):
    o_ref = rest[-1]
    q = q_ref[0]
    if not diff:
        o_ref[0] = _softmax_pv([q], k_ref, v_ref)[0].astype(o_ref.dtype)
        return
    first = lax.broadcasted_iota(jnp.int32, q.shape, 1) < DA_HEAD_DIM
    zero = jnp.zeros_like(q)
    o1, o2 = _softmax_pv([jnp.where(first, q, zero), jnp.where(first, zero, q)], k_ref, v_ref)
    lp = lam_ref[...]
    lam = (jnp.exp(jnp.sum(lp[0:1] * lp[1:2], axis=-1, keepdims=True))
           - jnp.exp(jnp.sum(lp[2:3] * lp[3:4], axis=-1, keepdims=True)) + lam_init)
    o_ref[0] = (_rms(o1 - lam * o2) * sg_ref[...] * (1.0 - lam_init)).astype(o_ref.dtype)


def _attention(q, k, v1, lam_p, subln_g, q_rows, tq, n_keys, out_rows, out_start, diff, lam_init, earlier=None):
    b, _, width = q.shape
    heads = width // LANES
    q_start, q_stop = q_rows

    def rows_from(start):
        return pl.BlockSpec((pl.Element(1), pl.Element(tq), pl.Element(LANES)),
                            lambda bi, hi, ti: (bi, pl.multiple_of(start + ti * tq, math.gcd(start, tq)),
                                                pl.multiple_of(hi * LANES, LANES)))

    khead = pl.BlockSpec((1, n_keys, LANES), lambda bi, hi, ti: (bi, 0, hi))
    vhead = pl.BlockSpec((1, n_keys, 2 * LANES), lambda bi, hi, ti: (bi, 0, hi))

    def const(shape):
        zeros = (0,) * len(shape)
        return pl.BlockSpec(shape, lambda bi, hi, ti: zeros)

    in_specs = [rows_from(q_start), khead, vhead, const(lam_p.shape), const((1, LANES))]
    args = [q, k, v1, lam_p, subln_g.reshape(1, LANES)]
    aliases = {}
    if earlier is not None:
        in_specs.append(pl.BlockSpec(memory_space=pl.ANY))
        args.append(earlier)
        aliases = {len(args) - 1: 0}
    return pl.pallas_call(
        functools.partial(_attn_kernel, diff=diff, lam_init=lam_init),
        out_shape=jax.ShapeDtypeStruct((b, out_rows, width), BF16),
        grid=(b, heads, (q_stop - q_start) // tq),
        in_specs=in_specs,
        out_specs=rows_from(out_start),
        input_output_aliases=aliases,
        compiler_params=_params("arbitrary", "arbitrary", "arbitrary"),
        name="diff_attention" if diff else "mla_attention",
    )(*args)


def _attend_segments(q, k, v1, lam_p, subln_g, ctx_len, diff, lam_init, need_ctx):
    t = k.shape[1]
    n_lat = t - ctx_len
    tq_lat = math.gcd(n_lat, ATTN_QUERY_TILE)
    if not need_ctx:
        return _attention(q, k, v1, lam_p, subln_g, (ctx_len, t), tq_lat, t, n_lat, 0, diff, lam_init)
    o = _attention(q, k, v1, lam_p, subln_g, (ctx_len, t), tq_lat, t, t, ctx_len, diff, lam_init)
    return _attention(q, k, v1, lam_p, subln_g, (0, ctx_len), ctx_len, ctx_len, t, 0, diff, lam_init, earlier=o)


def kernel(x, c, ctx, c_ctx, ada_w, ada_b, norm_mix_g, norm_ffn_g, ffn_w_gu, ffn_w_down, ev_w_in, ev_w_out, hg_lower_bound, hg_out_norm_g, hy_short_w, hy_short_b, hy_filt_w1, hy_filt_b1, hy_filt_freq1, hy_filt_w2, hy_filt_b2, hy_filt_freq2, hy_filt_w3, hy_skip, od_w_in, od_w_out, da_lambda, da_subln_g, mla_q_norm_g, mla_w_uq, mla_kv_norm_g, mla_w_ukv, final_norm_g):
    n_batch, n_lat, d = x.shape
    ctx_len = ctx.shape[1]
    depth = ada_w.shape[0]
    assert n_batch % 2 == 0 and n_batch < MOD_ROWS
    assert n_lat % GRID_W == 0 and ctx_len % HG_CHUNK == 0 and n_lat % FFT_N2 == 0 and ctx_len % FFT_N2 == 0
    tm = math.gcd(math.gcd(ctx_len, n_lat), 256)
    n_ctx_tiles = ctx_len // tm
    hg_width = d // 2
    hy_width = d - hg_width
    da_w = d // 2
    q_rank, kv_rank = mla_q_norm_g.shape[1], mla_kv_norm_g.shape[1]

    cc = jnp.concatenate([c, c_ctx[None], jnp.zeros((MOD_ROWS - n_batch - 1, d), F32)], axis=0)
    mods = _ada(cc, ada_w, ada_b).reshape(depth, MOD_ROWS, 6, d)
    h = (ctx, x) if depth > 1 else jnp.concatenate([ctx, x], axis=1)
    rope_tabs = _rope_tables(n_lat, ctx_len)

    for i in range(depth):
        last = i == depth - 1
        if i % 2 == 0:
            e = i // 2
            z = _in_even(h, mods[i], norm_mix_g[i], ev_w_in[e].astype(BF16), tm, n_ctx_tiles)
            a = _hgrn(z, hg_lower_bound, hg_out_norm_g[e], e, hg_width // HG_HEAD_DIM, ctx_len)
            filt_params = (hy_filt_w1[e], hy_filt_b1[e], hy_filt_freq1[e], hy_filt_w2[e], hy_filt_b2[e],
                           hy_filt_freq2[e], hy_filt_w3[e])
            bb = _hyena(z, 5 * hg_width, hy_width, ctx_len, hy_short_w[e], hy_short_b[e], filt_params, hy_skip[e])
            w_out = ev_w_out[e]
        else:
            o = i // 2
            lam_init = 0.8 - 0.6 * math.exp(-0.3 * i)
            w_big, wuq, wukv = _odd_weights(od_w_in[o], mla_w_uq[o], mla_w_ukv[o], da_w, q_rank, kv_rank)
            qd, kd, vd, qm, km, vm = _in_odd(h, mods[i], norm_mix_g[i], w_big, wuq, wukv, mla_q_norm_g[o],
                                             mla_kv_norm_g[o], rope_tabs, tm, n_ctx_tiles, da_w)
            a = _attend_segments(qd, kd, vd, da_lambda[o], da_subln_g[o], ctx_len, True, lam_init, not last)
            bb = _attend_segments(qm, km, vm, da_lambda[o], da_subln_g[o], ctx_len, False, lam_init, not last)
            w_out = od_w_out[o]
        weights = (w_out.astype(BF16), ffn_w_gu[i].astype(BF16), ffn_w_down[i].astype(BF16))
        if not last:
            h = _post(h, a, bb, mods[i], norm_ffn_g[i], *weights, tm, n_ctx_tiles)
    if a.shape[1] != n_lat:
        a, bb = a[:, ctx_len:], bb[:, ctx_len:]
    return _post_final(h, a, bb, mods[depth - 1], norm_ffn_g[depth - 1], *weights, final_norm_g, tm, n_ctx_tiles)
```

```python
import functools
import math

import numpy as np
import jax
import jax.numpy as jnp
from jax import lax
from jax.experimental import pallas as pl
from jax.experimental.pallas import tpu as pltpu

F32 = jnp.float32
BF16 = jnp.bfloat16
HIGHEST = lax.Precision.HIGHEST

GRID_W = 64
EPS = 1e-6
ROPE_BASE = 10000.0
HG_HEAD_DIM = 128
HG_CHUNK = 64
HY_BANDS = 16
HY_EMB = 2 * HY_BANDS + 1
HY_TARGET = 1e-2
HY_STEEP_PCT = 0.3
HY_GENTLE_PCT = 1.5
DA_HEAD_DIM = 64
MLA_HEADS = 4
MLA_NOPE = 64
MLA_ROPE = 32
MLA_V = 128
LOG2_E = 1.4426950408889634

LANES = 128
SUBLANES = 8
BF16_SUBLANES = 16
V7X_VMEM_LIMIT_BYTES = 56 * 1024 * 1024
MOD_ROWS = 16
FFT_N2 = 128
FFT_PITCH = FFT_N2 + SUBLANES
FFT_UNROLL = 64
HGRN_LOCKSTEP = 1
HGRN_UNROLL = 4
ATTN_KEY_BLOCK = 256
ATTN_QUERY_TILE = 2048


def _params(*sem):
    return pltpu.CompilerParams(dimension_semantics=sem, vmem_limit_bytes=V7X_VMEM_LIMIT_BYTES)


def _dot(a, b, precision=None):
    return jnp.dot(a, b, preferred_element_type=F32, precision=precision)


def _dot3(a, b):
    ah, bh = a.astype(BF16), b.astype(BF16)
    al, bl = (a - ah.astype(F32)).astype(BF16), (b - bh.astype(F32)).astype(BF16)
    return _dot(ah, bh) + _dot(al, bh) + _dot(ah, bl)


def _dot_nt(a, b):
    return lax.dot_general(a, b, (((1,), (1,)), ((), ())), preferred_element_type=F32)


def _dot_tn(a, b):
    return lax.dot_general(a, b, (((0,), (0,)), ((), ())), preferred_element_type=F32)


def _rms(x):
    return x * lax.rsqrt(jnp.mean(x * x, axis=-1, keepdims=True) + EPS)


def _neg_abs(x):
    bits = lax.bitcast_convert_type(x, jnp.int32) | jnp.int32(-2 ** 31)
    return lax.bitcast_convert_type(bits, F32)


def _silu(x):
    return x * jax.nn.sigmoid(x)


def _tile_mod(mod_ref, modc_ref, n_ctx_tiles):
    return jnp.where(pl.program_id(1) < n_ctx_tiles, modc_ref[...], mod_ref[...])


def _ada_kernel(c_ref, w_ref, b_ref, o_ref):
    o_ref[0] = _dot(_silu(c_ref[...]), w_ref[0], HIGHEST) + b_ref[0]


def _ada(cc, ada_w, ada_b):
    depth, d, n = ada_w.shape
    rows = cc.shape[0]
    tn = n // 4
    return pl.pallas_call(
        _ada_kernel,
        out_shape=jax.ShapeDtypeStruct((depth, rows, n), F32),
        grid=(depth, n // tn),
        in_specs=[
            pl.BlockSpec((rows, d), lambda i, j: (0, 0)),
            pl.BlockSpec((1, d, tn), lambda i, j: (i, 0, j)),
            pl.BlockSpec((1, 1, tn), lambda i, j: (i, 0, j)),
        ],
        out_specs=pl.BlockSpec((1, rows, tn), lambda i, j: (i, 0, j)),
        compiler_params=_params("arbitrary", "arbitrary"),
        name="ada_mod",
    )(cc, ada_w, ada_b.reshape(depth, 1, n))


def _tok_spec(tm, width):
    return pl.BlockSpec((1, tm, width), lambda b, t: (b, t, 0))


def _const_spec(shape):
    zeros = (0,) * len(shape)
    return pl.BlockSpec(shape, lambda b, t: zeros)


def _mod_specs(d, n_batch):
    lat = pl.BlockSpec((None, 6, d), lambda b, t: (b, 0, 0))
    ctx = pl.BlockSpec((None, 6, d), lambda b, t: (n_batch, 0, 0))
    return lat, ctx


def _stream(h, tm, n_ctx_tiles):
    if not isinstance(h, tuple):
        b, t, d = h.shape
        return b, t, d, [_tok_spec(tm, d)], [h]
    ctx, x = h
    b, n_lat, d = x.shape
    specs = [pl.BlockSpec((1, tm, d), lambda bi, ti: (bi, jnp.minimum(ti, n_ctx_tiles - 1), 0)),
             pl.BlockSpec((1, tm, d), lambda bi, ti: (bi, jnp.maximum(ti - n_ctx_tiles, 0), 0))]
    return b, ctx.shape[1] + n_lat, d, specs, [ctx, x]


def _stream_tile(refs, n_ctx_tiles):
    if len(refs) == 1:
        return refs[0][0]
    return jnp.where(pl.program_id(1) < n_ctx_tiles, refs[0][0], refs[1][0])


def _in_even_kernel(*refs, n_ctx_tiles, n_stream):
    mod_ref, modc_ref, g_ref, w_ref, o_ref = refs[n_stream:]
    m = _tile_mod(mod_ref, modc_ref, n_ctx_tiles)
    xn = (_rms(_stream_tile(refs[:n_stream], n_ctx_tiles)) * g_ref[...]) * (1.0 + m[1:2]) + m[0:1]
    o_ref[0] = _dot(xn.astype(BF16), w_ref[...])


def _in_even(h, mods, g, w, tm, n_ctx_tiles):
    b, t, d, h_specs, h_args = _stream(h, tm, n_ctx_tiles)
    n = w.shape[1]
    lat, ctx = _mod_specs(d, b)
    return pl.pallas_call(
        functools.partial(_in_even_kernel, n_ctx_tiles=n_ctx_tiles, n_stream=len(h_args)),
        out_shape=jax.ShapeDtypeStruct((b, t, n), F32),
        grid=(b, t // tm),
        in_specs=h_specs + [lat, ctx, _const_spec((1, d)), _const_spec((d, n))],
        out_specs=_tok_spec(tm, n),
        compiler_params=_params("arbitrary", "arbitrary"),
        name="even_in_proj",
    )(*h_args, mods, mods, g.reshape(1, d), w)


def _post_body(m, h, a, bb, g_ref, wo_ref, wgu_ref, wd_ref, hidden):
    half = a.shape[-1]
    y = _dot(a, wo_ref[:half, :]) + _dot(bb, wo_ref[half:, :])
    h1 = h + m[2:3] * y
    xn = (_rms(h1) * g_ref[...]) * (1.0 + m[4:5]) + m[3:4]
    gu = _dot(xn.astype(BF16), wgu_ref[...])
    act = _silu(gu[:, :hidden]) * gu[:, hidden:]
    return h1 + m[5:6] * _dot(act.astype(BF16), wd_ref[...])


def _post_kernel(*refs, n_ctx_tiles, hidden, n_stream):
    a_ref, b_ref, mod_ref, modc_ref, g_ref, wo_ref, wgu_ref, wd_ref, o_ref = refs[n_stream:]
    m = _tile_mod(mod_ref, modc_ref, n_ctx_tiles)
    h = _stream_tile(refs[:n_stream], n_ctx_tiles)
    o_ref[0] = _post_body(m, h, a_ref[0], b_ref[0], g_ref, wo_ref, wgu_ref, wd_ref, hidden)


def _post_final_kernel(h_ref, a_ref, b_ref, mod_ref, g_ref, wo_ref, wgu_ref, wd_ref, fg_ref, o_ref,
                       *, hidden):
    h2 = _post_body(mod_ref[...], h_ref[0], a_ref[0], b_ref[0], g_ref, wo_ref, wgu_ref, wd_ref, hidden)
    o_ref[0] = _rms(h2) * fg_ref[...]


def _post(h, a, bb, mods, g, wo, wgu, wd, tm, n_ctx_tiles):
    b, t, d, h_specs, h_args = _stream(h, tm, n_ctx_tiles)
    half = a.shape[-1]
    hidden = wd.shape[0]
    lat, ctx = _mod_specs(d, b)
    return pl.pallas_call(
        functools.partial(_post_kernel, n_ctx_tiles=n_ctx_tiles, hidden=hidden, n_stream=len(h_args)),
        out_shape=jax.ShapeDtypeStruct((b, t, d), F32),
        grid=(b, t // tm),
        in_specs=h_specs + [_tok_spec(tm, half), _tok_spec(tm, half), lat, ctx,
                            _const_spec((1, d)), _const_spec((2 * half, d)), _const_spec((d, 2 * hidden)),
                            _const_spec((hidden, d))],
        out_specs=_tok_spec(tm, d),
        compiler_params=_params("arbitrary", "arbitrary"),
        name="out_proj_ffn",
    )(*h_args, a, bb, mods, mods, g.reshape(1, d), wo, wgu, wd)


def _post_final(h, a_lat, b_lat, mods, g, wo, wgu, wd, final_g, tm, n_ctx_tiles):
    b, t, d = h.shape
    n_lat = a_lat.shape[1]
    half = a_lat.shape[-1]
    hidden = wd.shape[0]
    lat, _ = _mod_specs(d, b)
    return pl.pallas_call(
        functools.partial(_post_final_kernel, hidden=hidden),
        out_shape=jax.ShapeDtypeStruct((b, n_lat, d), F32),
        grid=(b, n_lat // tm),
        in_specs=[pl.BlockSpec((1, tm, d), lambda bi, ti: (bi, ti + n_ctx_tiles, 0)),
                  _tok_spec(tm, half), _tok_spec(tm, half), lat,
                  _const_spec((1, d)), _const_spec((2 * half, d)), _const_spec((d, 2 * hidden)),
                  _const_spec((hidden, d)), _const_spec((1, d))],
        out_specs=_tok_spec(tm, d),
        compiler_params=_params("arbitrary", "arbitrary"),
        name="out_proj_ffn_final",
    )(h, a_lat, b_lat, mods, g.reshape(1, d), wo, wgu, wd, final_g.reshape(1, d))


def _hgrn_tables(tile):
    c = HG_CHUNK
    r = np.arange(c)
    masks, bms = [], []
    m = c
    while m >= 2:
        upper = ((r % m) >= m // 2)[:, None]
        masks.append(np.broadcast_to(upper, (c, LANES)))
        bms.append(((r[:, None] // m) == (r[None, :] // m)) & upper & (~upper.T))
        m //= 2
    fwd = (r[None, :] <= r[:, None]).astype(np.float32)
    mall = np.stack([fwd, fwd[::-1, ::-1]])
    mall = np.concatenate([mall, mall], axis=2)
    mu = np.stack(masks).astype(np.float32)
    masku = np.tile(np.stack([mu, mu[:, ::-1]]), (1, 1, tile // c, 1))
    bmp = np.stack([np.concatenate([bms[i], bms[i + 1]], axis=1) for i in range(0, len(bms), 2)])
    bmp_bwd = np.stack([np.concatenate([bms[i][::-1, ::-1], bms[i + 1][::-1, ::-1]], axis=1)
                        for i in range(0, len(bms), 2)])
    return (jnp.asarray(mall, dtype=BF16), jnp.asarray(masku, dtype=F32),
            jnp.asarray(np.stack([bmp, bmp_bwd]), dtype=F32))


def _hgrn_kernel(q_ref, ff_ref, fb_ref, i_ref, g_ref, lbp_ref, ng_ref, mall_ref, masku_ref, bmp_ref,
                 o_ref, of_ref, ob_ref, *, layer, n_ctx_tiles, n_tiles):
    c = HG_CHUNK
    dk = HG_HEAD_DIM
    n_levels = masku_ref.shape[1]
    tile = masku_ref.shape[2]
    n_sub = tile // c
    lbp = lbp_ref[...]
    ex = jnp.exp(lbp - jnp.max(lbp, axis=0, keepdims=True))
    p = ex / jnp.sum(ex, axis=0, keepdims=True)
    lb = jnp.sum(p[:layer + 1], axis=0) - p[0]

    def sub(x, g):
        return x[g * c:(g + 1) * c]

    dirs = (0, 1)

    def stack_chunks(x):
        return jnp.concatenate([sub(x, g) for g in range(n_sub)], axis=1)

    def bwd_tile(t):
        if isinstance(t, int):
            return n_ctx_tiles - 1 - t if t < n_ctx_tiles else n_tiles - 1 - (t - n_ctx_tiles)
        return jnp.where(t < n_ctx_tiles, n_ctx_tiles - 1 - t, n_tiles - 1 - (t - n_ctx_tiles))

    def tile_rows(ti):
        start = ti * tile
        return pl.ds(start if isinstance(ti, int) else pl.multiple_of(start, tile), tile)

    def run_trips(trips, sts):
        lanes = [(s, d) for s in range(len(trips)) for d in dirs]
        n = range(len(lanes))
        tis = [bwd_tile(t) if d else t for t in trips for d in dirs]
        rows = [tile_rows(ti) for ti in tis]
        q = [q_ref[0, r, :] for r in rows]
        v = [i_ref[0, r, :] for r in rows]
        lbd = [lb[d:d + 1] for _, d in lanes]
        zf = [(ff_ref, fb_ref)[d][0, rows[l], :] for l, (_, d) in enumerate(lanes)]
        f = [lbd[l] + (1.0 - lbd[l]) * jax.nn.sigmoid(zf[l]) for l in n]
        lf = [jnp.log(x) * LOG2_E for x in f]
        k = [1.0 - x for x in f]
        hi = [x.astype(BF16) for x in lf]
        lo = [(lf[l] - hi[l].astype(F32)).astype(BF16) for l in n]
        e2 = [_dot(mall_ref[lanes[l][1]], jnp.concatenate([stack_chunks(hi[l]), stack_chunks(lo[l])], axis=0))
              for l in n]
        cum = [[e2[l][:, g * dk:(g + 1) * dk] for g in range(n_sub)] for l in n]

        def tail_decay(l, g):
            last = c - 1 if lanes[l][1] == 0 else 0
            return jnp.exp2(cum[l][g][last:last + 1] - cum[l][g])

        def level_decay(l, lvl):
            m = c >> lvl
            ref = m // 2 - 1 if lanes[l][1] == 0 else m // 2
            parts = []
            for g in range(n_sub):
                cg = cum[l][g]
                if m >= SUBLANES:
                    x = cg.reshape(c // m, m, dk)
                    mid = jnp.broadcast_to(x[:, ref:ref + 1, :], x.shape)
                else:
                    x = cg.reshape(c // SUBLANES, SUBLANES, dk)
                    srow = lax.broadcasted_iota(jnp.int32, x.shape, 1)
                    mid = jnp.where(srow < m, jnp.broadcast_to(x[:, ref:ref + 1, :], x.shape),
                                    jnp.broadcast_to(x[:, m + ref:m + ref + 1, :], x.shape))
                parts.append(jnp.exp2(_neg_abs(cg - mid.reshape(c, dk))))
            return jnp.concatenate(parts, axis=0)

        g_in = [jnp.concatenate([jnp.exp2(cum[l][g]) for g in range(n_sub)], axis=0) for l in n]
        qin = [(q[l] * g_in[l]).astype(BF16) for l in n]
        kout = [(k[l] * jnp.concatenate([tail_decay(l, g) for g in range(n_sub)], axis=0)).astype(BF16)
                for l in n]
        vb = [x.astype(BF16) for x in v]
        zero = jnp.zeros((c, dk), BF16)
        ds = [[_dot_tn(sub(vb[l], g), sub(kout[l], g)) for g in range(n_sub)] for l in n]
        qf = [q[l] * f[l] for l in n]
        w = [[None] * n_levels for _ in n]
        for lvl in range(n_levels):
            for l, (_, d) in enumerate(lanes):
                later = masku_ref[d, lvl] != 0.0
                if lvl == n_levels - 1:
                    w[l][lvl] = jnp.where(later, qf[l], k[l]).astype(BF16)
                else:
                    w[l][lvl] = (jnp.where(later, q[l], k[l]) * level_decay(l, lvl)).astype(BF16)
        scores = [[[None] * (n_levels // 2) for _ in range(n_sub)] for _ in n]
        for pair in range(n_levels // 2):
            lv_a, lv_b = 2 * pair, 2 * pair + 1
            for g in range(n_sub):
                for l, (_, d) in enumerate(lanes):
                    wa, wb = sub(w[l][lv_a], g), sub(w[l][lv_b], g)
                    rhs = jnp.concatenate([jnp.concatenate([wa, zero], axis=1),
                                           jnp.concatenate([zero, wb], axis=1)], axis=0)
                    raw = _dot_nt(jnp.concatenate([wa, wb], axis=1), rhs)
                    scores[l][g][pair] = raw * bmp_ref[d, pair]
        o = [jnp.concatenate([_dot(sum(scores[l][g]).astype(BF16),
                                   jnp.concatenate([sub(vb[l], g)] * 2, axis=0))
                              for g in range(n_sub)], axis=0)
             + jnp.sum(q[l] * k[l], axis=-1, keepdims=True) * v[l] for l in n]
        order = [list(range(n_sub)), list(reversed(range(n_sub)))]
        entering = [[None] * n_sub for _ in n]
        sts = list(sts)
        for l, (_, d) in enumerate(lanes):
            for g in order[d]:
                entering[l][g] = sts[d].astype(BF16)
                last = (g + 1) * c - 1 if d == 0 else g * c
                sts[d] = sts[d] * g_in[l][last:last + 1] + ds[l][g]
        for l, (_, d) in enumerate(lanes):
            out = jnp.concatenate([sub(o[l], g) + _dot_nt(sub(qin[l], g), entering[l][g])
                                   for g in range(n_sub)], axis=0)
            (of_ref, ob_ref)[d][rows[l], :] = out
        return tuple(sts)

    def body(i, carry):
        return run_trips([i * HGRN_LOCKSTEP + s for s in range(HGRN_LOCKSTEP)], carry)

    s0 = jnp.zeros((dk, dk), F32)
    sts = lax.fori_loop(0, n_tiles // HGRN_LOCKSTEP, body, (s0, s0), unroll=HGRN_UNROLL)
    rest = list(range(n_tiles - n_tiles % HGRN_LOCKSTEP, n_tiles))
    if rest:
        run_trips(rest, sts)
    o = _rms(of_ref[...] + ob_ref[...]) * ng_ref[...]
    o_ref[0] = (o * _silu(g_ref[0])).astype(o_ref.dtype)


def _hgrn(z, hg_lower_bound, ng, layer, n_heads, ctx_len):
    b, t, _ = z.shape
    dk = HG_HEAD_DIM
    n_even = hg_lower_bound.shape[0]
    tile = math.gcd(math.gcd(ctx_len, t - ctx_len), 256)
    mall, masku, bm = _hgrn_tables(tile)

    def col(kind):
        return pl.BlockSpec((1, t, dk), lambda bi, hi: (bi, 0, kind * n_heads + hi))

    def const(arr):
        zeros = (0,) * arr.ndim
        return pl.BlockSpec(arr.shape, lambda bi, hi: zeros)

    return pl.pallas_call(
        functools.partial(_hgrn_kernel, layer=layer, n_ctx_tiles=ctx_len // tile, n_tiles=t // tile),
        out_shape=jax.ShapeDtypeStruct((b, t, n_heads * dk), BF16),
        grid=(b, n_heads),
        in_specs=[col(0), col(1), col(2), col(3), col(4),
                  pl.BlockSpec((n_even, 2, dk), lambda bi, hi: (0, 0, hi)),
                  const(ng.reshape(1, dk)), const(mall), const(masku), const(bm)],
        out_specs=pl.BlockSpec((1, t, dk), lambda bi, hi: (bi, 0, hi)),
        scratch_shapes=[pltpu.VMEM((t, dk), F32), pltpu.VMEM((t, dk), F32)],
        compiler_params=_params("arbitrary", "arbitrary"),
        name="hgrn2_scan",
    )(z, z, z, z, z, hg_lower_bound, ng.reshape(1, dk), mall, masku, bm)


def _hy_pre_kernel(x0_ref, x1_ref, v_ref, w0_ref, w1_ref, wv_ref, b0_ref, b1_ref, bv_ref,
                   x0c_ctx_ref, x0c_lat_ref, u_ctx_ref, u_lat_ref, *, ctx_len):
    t = x0_ref.shape[1]
    row = lax.broadcasted_iota(jnp.int32, (t, LANES), 0)
    first = (row == 0) | (row == ctx_len)
    final = (row == ctx_len - 1) | (row == t - 1)

    def short_conv(z_ref, w_ref, b_ref):
        z = z_ref[0]
        prev = jnp.where(first, 0.0, pltpu.roll(z, 1, 0))
        nxt = jnp.where(final, 0.0, pltpu.roll(z, t - 1, 0))
        w = w_ref[...]
        return prev * w[0:1] + z * w[1:2] + nxt * w[2:3] + b_ref[...]

    x0c = short_conv(x0_ref, w0_ref, b0_ref).astype(x0c_ctx_ref.dtype)
    u = (short_conv(x1_ref, w1_ref, b1_ref) * short_conv(v_ref, wv_ref, bv_ref)).astype(u_ctx_ref.dtype)
    x0c_ctx_ref[0] = x0c[:ctx_len]
    x0c_lat_ref[0] = x0c[ctx_len:]
    u_ctx_ref[0] = u[:ctx_len]
    u_lat_ref[0] = u[ctx_len:]


def _hy_pre(z, short_w, short_b, col0, width, ctx_len):
    b, t, _ = z.shape
    nb = width // LANES
    c0 = col0 // LANES

    def zcol(kind):
        return pl.BlockSpec((1, t, LANES), lambda bi, j: (bi, 0, c0 + kind * nb + j))

    def wcol(kind, rows):
        return pl.BlockSpec((rows, LANES), lambda bi, j: (0, kind * nb + j))

    def seg(rows):
        return (jax.ShapeDtypeStruct((b, rows, width), BF16),
                pl.BlockSpec((1, rows, LANES), lambda bi, j: (bi, 0, j)))

    (ctx_shape, ctx_spec), (lat_shape, lat_spec) = seg(ctx_len), seg(t - ctx_len)
    sb = short_b.reshape(1, -1)
    return pl.pallas_call(
        functools.partial(_hy_pre_kernel, ctx_len=ctx_len),
        out_shape=(ctx_shape, lat_shape, ctx_shape, lat_shape),
        grid=(b, nb),
        in_specs=[zcol(0), zcol(1), zcol(2), wcol(0, 3), wcol(1, 3), wcol(2, 3),
                  wcol(0, 1), wcol(1, 1), wcol(2, 1)],
        out_specs=(ctx_spec, lat_spec, ctx_spec, lat_spec),
        compiler_params=_params("arbitrary", "arbitrary"),
        name="hyena_short_conv",
    )(z, z, z, short_w, short_w, short_w, sb, sb, sb)


def _filter_features(length, n):
    p = np.arange(n)
    is_f = p < length
    is_b = p > n - length
    lag = np.where(is_f, p, np.where(is_b, n - 1 - p, 0))
    tt = np.linspace(0.0, 1.0, length, dtype=np.float32)[lag][:, None]
    w = (2.0 * math.pi * lag.astype(np.float32) / length)[:, None].astype(np.float32)
    bands = np.linspace(1e-4, HY_BANDS - 1, HY_BANDS, dtype=np.float32)[None, :]
    feat = np.concatenate([tt, np.cos(bands * w), -np.sin(bands * w), is_f[:, None], is_b[:, None]],
                          axis=-1).astype(np.float32)
    pad = (-feat.shape[1]) % BF16_SUBLANES
    return jnp.asarray(np.pad(feat, ((0, 0), (0, pad))))


def _filt_kernel(feat_ref, w1_ref, b1_ref, fr1_ref, w2_ref, b2_ref, fr2_ref,
                 w3f_ref, w3b_ref, dl_ref, o_ref, hid_ref):
    z = feat_ref[...]
    mf = z[:, HY_EMB:HY_EMB + 1]
    mb = z[:, HY_EMB + 1:HY_EMB + 2]

    @pl.when(pl.program_id(0) == 0)
    def _():
        h1 = jnp.sin(fr1_ref[...] * (_dot3(z, w1_ref[...]) + b1_ref[...]))
        hid_ref[...] = jnp.sin(fr2_ref[...] * (_dot3(h1, w2_ref[...]) + b2_ref[...]))

    hid = hid_ref[...]
    hf = _dot3(hid, w3f_ref[...])
    hb = _dot3(hid, w3b_ref[...])
    win = jnp.exp(-z[:, 0:1] * dl_ref[...])
    f = (mf * hf + mb * hb) * win
    o_ref[...] = f / jnp.sum(jnp.abs(f), axis=0, keepdims=True)


def _hyena_filter(length, n, w1, b1, fr1, w2, b2, fr2, w3, width):
    feat = _filter_features(length, n)
    nf = feat.shape[1]
    hid = w1.shape[1]
    w1p = jnp.pad(w1, ((0, nf - w1.shape[0]), (0, 0)))
    d_lo = -math.log(HY_TARGET) / HY_GENTLE_PCT
    d_hi = -math.log(HY_TARGET) / HY_STEEP_PCT
    deltas = jnp.asarray(np.linspace(d_lo, d_hi, width, dtype=np.float32)[None, :])
    nb = width // LANES

    def full(shape):
        zeros = (0,) * len(shape)
        return pl.BlockSpec(shape, lambda j: zeros)

    return pl.pallas_call(
        _filt_kernel,
        out_shape=jax.ShapeDtypeStruct((n, width), F32),
        grid=(nb,),
        in_specs=[full((n, nf)), full((nf, hid)), full((1, hid)),
                  full((1, hid)), full((hid, hid)), full((1, hid)), full((1, hid)),
                  pl.BlockSpec((hid, LANES), lambda j: (0, j)),
                  pl.BlockSpec((hid, LANES), lambda j: (0, nb + j)),
                  pl.BlockSpec((1, LANES), lambda j: (0, j))],
        out_specs=pl.BlockSpec((n, LANES), lambda j: (0, j)),
        scratch_shapes=[pltpu.VMEM((n, hid), F32)],
        compiler_params=_params("arbitrary"),
        name="hyena_filter_mlp",
    )(feat, w1p, b1.reshape(1, -1), fr1.reshape(1, -1), w2, b2.reshape(1, -1),
      fr2.reshape(1, -1), w3, w3, deltas)


def _dft_tables(n1, n1_in):
    n2 = FFT_N2
    n = n1 * n2
    a = np.arange(n1)
    j = np.arange(n2)
    ang = -2.0 * np.pi * (a[None, None, :] * a[None, :, None] / n1 + j[:, None, None] * a[None, :, None] / n)
    tr, ti = np.cos(ang), np.sin(ang)
    fwd_a = np.concatenate([np.concatenate([tr, -ti], 2), np.concatenate([ti, tr], 2)], 1)
    trt, tit = np.swapaxes(tr, 1, 2), -np.swapaxes(ti, 1, 2)
    inv_a = np.concatenate([np.concatenate([trt, -tit], 2), np.concatenate([tit, trt], 2)], 1)
    keep = np.concatenate([np.arange(n1_in), n1 + np.arange(n1_in)])
    ang2 = -2.0 * np.pi * (j[:, None] * j[None, :]) / n2
    cr, ci = np.cos(ang2), np.sin(ang2)
    fwd_c = np.block([[cr, -ci], [ci, cr]])
    inv_c = np.block([[cr, ci], [-ci, cr]])
    real_a = np.concatenate([tr, ti], 1)
    return dict(fwd_a=fwd_a[:, :, keep], inv_a=inv_a[:, keep, :], fwd_c=fwd_c, inv_c=inv_c,
                real_a=real_a)


def _slab(idx):
    return pl.ds(pl.multiple_of(idx * FFT_PITCH, SUBLANES), FFT_N2)


def _fft_filter_kernel(f_ref, wa_ref, wc_ref, o_ref, scr_ref, *, n1):
    n2 = FFT_N2
    rows = 2 * n1
    scale = 1.0 / (n1 * n2)

    def split3(x):
        hi = x.astype(BF16)
        lo = (x - hi.astype(F32)).astype(BF16)
        return jnp.concatenate([hi, lo, hi], axis=0)

    def stage_a(j, carry):
        scr_ref[pl.ds(j, rows, stride=FFT_PITCH), :] = _dot(wa_ref[j], split3(f_ref[j]))
        return carry

    lax.fori_loop(0, n2, stage_a, 0, unroll=FFT_UNROLL)

    def stage_c(k, carry):
        x = jnp.concatenate([scr_ref[_slab(k), :], scr_ref[_slab(n1 + k), :]], axis=0)
        o_ref[k] = _dot(wc_ref[...], split3(x)) * scale
        return carry

    lax.fori_loop(0, n1, stage_c, 0, unroll=FFT_UNROLL)


def _split3_cols(w):
    hi = w.astype(np.float32).astype(jnp.bfloat16)
    lo = (w.astype(np.float32) - np.asarray(hi, np.float32)).astype(jnp.bfloat16)
    return jnp.asarray(np.concatenate([hi, hi, lo], axis=-1))


def _fft_filter(filt, n1, tables):
    n2 = FFT_N2
    width = filt.shape[1]
    ft = filt.reshape(n1, n2, width).transpose(1, 0, 2)
    wa = _split3_cols(tables["real_a"])
    wc = _split3_cols(tables["fwd_c"])
    return pl.pallas_call(
        functools.partial(_fft_filter_kernel, n1=n1),
        out_shape=jax.ShapeDtypeStruct((n1, 2 * n2, width), F32),
        grid=(width // LANES,),
        in_specs=[pl.BlockSpec((n2, n1, LANES), lambda c: (0, 0, c)),
                  pl.BlockSpec(wa.shape, lambda c: (0, 0, 0)),
                  pl.BlockSpec(wc.shape, lambda c: (0, 0))],
        out_specs=pl.BlockSpec((n1, 2 * n2, LANES), lambda c: (0, 0, c)),
        scratch_shapes=[pltpu.VMEM((2 * n1 * FFT_PITCH, LANES), F32)],
        compiler_params=_params("arbitrary"),
        name="hyena_filter_dft",
    )(ft, wa, wc)


def _fftconv_kernel(u_ref, x0_ref, fh_ref, skip_ref, wfa_ref, wfc_ref, wic_ref, wia_ref, o_ref,
                    scr_ref, *, n1):
    n2 = FFT_N2
    rows = 2 * n1

    def stage_a(j, carry):
        scr_ref[pl.ds(j, rows, stride=FFT_PITCH), :] = _dot(wfa_ref[j], u_ref[0, j])
        return carry

    lax.fori_loop(0, n2, stage_a, 0, unroll=FFT_UNROLL)

    def stage_c(k, carry):
        x = jnp.concatenate([scr_ref[_slab(k), :], scr_ref[_slab(n1 + k), :]], axis=0)
        xf = _dot(wfc_ref[...], x.astype(BF16))
        fh = fh_ref[k]
        xr, xi, fr, fi = xf[:n2], xf[n2:], fh[:n2], fh[n2:]
        y = jnp.concatenate([xr * fr - xi * fi, xr * fi + xi * fr], axis=0)
        zt = _dot(wic_ref[...], y.astype(BF16))
        scr_ref[_slab(k), :] = zt[:n2]
        scr_ref[_slab(n1 + k), :] = zt[n2:]
        return carry

    lax.fori_loop(0, n1, stage_c, 0, unroll=FFT_UNROLL)

    def stage_ai(j, carry):
        x = scr_ref[pl.ds(j, rows, stride=FFT_PITCH), :]
        y = _dot(wia_ref[j], x.astype(BF16))
        u = u_ref[0, j].astype(F32)
        o_ref[0, j] = (x0_ref[0, j].astype(F32) * (y + skip_ref[...] * u)).astype(o_ref.dtype)
        return carry

    lax.fori_loop(0, n2, stage_ai, 0, unroll=FFT_UNROLL)


def _to_fft_layout(x, n1_in):
    b, length, c = x.shape
    n1_used = length // FFT_N2
    x = x.reshape(b // 2, 2, n1_used, FFT_N2, c)
    x = jnp.pad(x, ((0, 0), (0, 0), (0, n1_in - n1_used), (0, 0), (0, 0)))
    return x.transpose(0, 3, 1, 2, 4).reshape(b // 2, FFT_N2, 2 * n1_in, c)


def _from_fft_layout(y, length):
    p, n2, rows, c = y.shape
    n1_in = rows // 2
    y = y.reshape(p, n2, 2, n1_in, c).transpose(0, 2, 3, 1, 4)
    return y.reshape(2 * p, n1_in * n2, c)[:, :length]


def _fftconv(u, x0c, fh, skip, n1, n1_in, tables):
    _, length, width = u.shape
    ut = _to_fft_layout(u, n1_in)
    xt = _to_fft_layout(x0c, n1_in)
    pairs, n2, rin, _ = ut.shape
    wfa = jnp.asarray(tables["fwd_a"], dtype=BF16)
    wia = jnp.asarray(tables["inv_a"], dtype=BF16)
    wfc = jnp.asarray(tables["fwd_c"], dtype=BF16)
    wic = jnp.asarray(tables["inv_c"], dtype=BF16)
    data = pl.BlockSpec((1, n2, rin, LANES), lambda c, p: (p, 0, 0, c))

    def const(arr):
        zeros = (0,) * arr.ndim
        return pl.BlockSpec(arr.shape, lambda c, p: zeros, pipeline_mode=pl.Buffered(1))

    out = pl.pallas_call(
        functools.partial(_fftconv_kernel, n1=n1),
        out_shape=jax.ShapeDtypeStruct(ut.shape, BF16),
        grid=(width // LANES, pairs),
        in_specs=[data, data,
                  pl.BlockSpec((n1, 2 * n2, LANES), lambda c, p: (0, 0, c)),
                  pl.BlockSpec((1, LANES), lambda c, p: (0, c)),
                  const(wfa), const(wfc), const(wic), const(wia)],
        out_specs=data,
        scratch_shapes=[pltpu.VMEM((2 * n1 * FFT_PITCH, LANES), F32)],
        compiler_params=_params("arbitrary", "arbitrary"),
        name="hyena_dft_conv",
    )(ut, xt, fh, skip.reshape(1, width), wfa, wfc, wic, wia)
    return _from_fft_layout(out, length)


def _hyena(z, col0, width, ctx_len, short_w, short_b, filt_params, skip):
    x0c_ctx, x0c_lat, u_ctx, u_lat = _hy_pre(z, short_w, short_b, col0, width, ctx_len)
    outs = []
    for x0c, u in ((x0c_ctx, u_ctx), (x0c_lat, u_lat)):
        length = u.shape[1]
        n1_in = max(length // FFT_N2, SUBLANES)
        n1 = 2 * n1_in
        tables = _dft_tables(n1, n1_in)
        filt = _hyena_filter(length, n1 * FFT_N2, *filt_params, width)
        fh = _fft_filter(filt, n1, tables)
        outs.append(_fftconv(u, x0c, fh, skip, n1, n1_in, tables))
    return jnp.concatenate(outs, axis=1)


def _rope_tables(n_lat, ctx_len):
    tok = np.arange(n_lat)
    row, colp = tok // GRID_W, tok % GRID_W

    def axial(half):
        inv = ROPE_BASE ** (-np.arange(half, dtype=np.float32) / half)
        parts_c, parts_s = [], []
        for pos in (row, colp):
            ang = pos.astype(np.float32)[:, None] * inv
            parts_c += [np.cos(ang), np.cos(ang)]
            parts_s += [-np.sin(ang), np.sin(ang)]
        return np.concatenate(parts_c, 1), np.concatenate(parts_s, 1)

    dc, ds = axial(DA_HEAD_DIM // 4)
    mc, ms = axial(MLA_ROPE // 4)
    ones, zeros = np.ones((n_lat, MLA_NOPE), np.float32), np.zeros((n_lat, MLA_NOPE), np.float32)
    padc = np.ones((n_lat, LANES - MLA_NOPE - MLA_ROPE), np.float32)
    tabs = [np.concatenate([dc, dc], 1), np.concatenate([ds, ds], 1),
            np.concatenate([ones, mc, padc], 1), np.concatenate([zeros, ms, 0 * padc], 1)]
    out = []
    for i, tb in enumerate(tabs):
        ctx_rows = np.ones((ctx_len, LANES), np.float32) if i % 2 == 0 else np.zeros((ctx_len, LANES), np.float32)
        out.append(jnp.asarray(np.concatenate([ctx_rows, tb.astype(np.float32)], 0)))
    return out


def _with_ones(v):
    ones = jnp.ones((v.shape[0], LANES), v.dtype)
    parts = []
    for hd in range(v.shape[1] // LANES):
        parts += [v[:, hd * LANES:(hd + 1) * LANES], ones]
    return jnp.concatenate(parts, axis=1)


def _in_odd_kernel(x_ref, mod_ref, modc_ref, g_ref, w_ref, dc_ref, ds_ref, mc_ref, ms_ref,
                   qg_ref, kvg_ref, wuq_ref, wukv_ref,
                   qd_ref, kd_ref, vd_ref, qm_ref, km_ref, vm_ref, *, n_ctx_tiles, da_w, q_rank, kv_rank):
    m = _tile_mod(mod_ref, modc_ref, n_ctx_tiles)
    xn = (_rms(x_ref[0]) * g_ref[...]) * (1.0 + m[1:2]) + m[0:1]
    z = _dot(xn.astype(BF16), w_ref[...])
    nrep = da_w // LANES
    dc = jnp.concatenate([dc_ref[...]] * nrep, axis=1)
    ds = jnp.concatenate([ds_ref[...]] * nrep, axis=1)
    mc = jnp.concatenate([mc_ref[...]] * MLA_HEADS, axis=1)
    ms = jnp.concatenate([ms_ref[...]] * MLA_HEADS, axis=1)
    sa = DA_HEAD_DIM ** -0.5 * LOG2_E
    sm = (MLA_NOPE + MLA_ROPE) ** -0.5 * LOG2_E
    lane = lax.broadcasted_iota(jnp.int32, (x_ref.shape[1], LANES), 1)

    def partner(x, half):
        first = (lane % (2 * half)) < half
        cols = []
        for j in range(x.shape[1] // LANES):
            xj = x[:, j * LANES:(j + 1) * LANES]
            cols.append(jnp.where(first, pltpu.roll(xj, LANES - half, 1), pltpu.roll(xj, half, 1)))
        return cols[0] if len(cols) == 1 else jnp.concatenate(cols, axis=1)

    o = 0
    zq = z[:, o:o + da_w]
    qd_ref[0] = ((zq * dc + partner(zq, DA_HEAD_DIM // 4) * ds) * sa).astype(BF16)
    o += da_w
    zk = z[:, o:o + da_w]
    kd_ref[0] = (zk * dc + partner(zk, DA_HEAD_DIM // 4) * ds).astype(BF16)
    o += da_w
    vd_ref[0] = _with_ones(z[:, o:o + da_w]).astype(BF16)
    o += da_w
    cq = _rms(z[:, o:o + q_rank]) * qg_ref[...]
    o += q_rank
    ckv = _rms(z[:, o:o + kv_rank]) * kvg_ref[...]
    o += kv_rank
    zr = z[:, o:o + LANES]
    kr = zr * mc_ref[...] + partner(zr, MLA_ROPE // 4) * ms_ref[...]
    mw = MLA_HEADS * LANES
    qu = _dot(cq.astype(BF16), wuq_ref[...])
    qm_ref[0] = ((qu * mc + partner(qu, MLA_ROPE // 4) * ms) * sm).astype(BF16)
    kvu = _dot(ckv.astype(BF16), wukv_ref[...])
    km_ref[0] = (kvu[:, :mw] + jnp.concatenate([kr] * MLA_HEADS, axis=1)).astype(BF16)
    vm_ref[0] = _with_ones(kvu[:, mw:]).astype(BF16)


def _odd_weights(w_in, w_uq, w_ukv, da_w, q_rank, kv_rank):
    o = 3 * da_w + q_rank + kv_rank
    krw = jnp.pad(w_in[:, o:], ((0, 0), (MLA_NOPE, LANES - MLA_NOPE - MLA_ROPE)))
    w_big = jnp.concatenate([w_in[:, :o], krw], axis=1)
    dq = MLA_NOPE + MLA_ROPE
    uq = jnp.pad(w_uq.reshape(q_rank, MLA_HEADS, dq), ((0, 0), (0, 0), (0, LANES - dq)))
    ukv = w_ukv.reshape(kv_rank, MLA_HEADS, MLA_NOPE + MLA_V)
    uk = jnp.pad(ukv[:, :, :MLA_NOPE], ((0, 0), (0, 0), (0, LANES - MLA_NOPE))).reshape(kv_rank, MLA_HEADS * LANES)
    uv = ukv[:, :, MLA_NOPE:].reshape(kv_rank, MLA_HEADS * MLA_V)
    return (w_big.astype(BF16), uq.reshape(q_rank, MLA_HEADS * LANES).astype(BF16),
            jnp.concatenate([uk, uv], axis=1).astype(BF16))


def _in_odd(h, mods, g, w_big, wuq, wukv, qg, kvg, tabs, tm, n_ctx_tiles, da_w):
    b, t, d = h.shape
    q_rank, kv_rank = qg.shape[0], kvg.shape[0]
    lat, ctx = _mod_specs(d, b)
    tab = pl.BlockSpec((tm, LANES), lambda bi, ti: (ti, 0))
    mw = MLA_HEADS * LANES

    def tok_major(width):
        return jax.ShapeDtypeStruct((b, t, width), BF16), _tok_spec(tm, width)

    outs, ospecs = zip(tok_major(da_w), tok_major(da_w), tok_major(2 * da_w),
                       tok_major(mw), tok_major(mw), tok_major(2 * mw))
    return pl.pallas_call(
        functools.partial(_in_odd_kernel, n_ctx_tiles=n_ctx_tiles, da_w=da_w, q_rank=q_rank, kv_rank=kv_rank),
        out_shape=tuple(outs),
        grid=(b, t // tm),
        in_specs=[_tok_spec(tm, d), lat, ctx, _const_spec((1, d)), _const_spec(w_big.shape),
                  tab, tab, tab, tab, _const_spec((1, q_rank)), _const_spec((1, kv_rank)),
                  _const_spec(wuq.shape), _const_spec(wukv.shape)],
        out_specs=tuple(ospecs),
        compiler_params=_params("arbitrary", "arbitrary"),
        name="odd_in_proj",
    )(h, mods, mods, g.reshape(1, d), w_big, *tabs, qg.reshape(1, -1), kvg.reshape(1, -1), wuq, wukv)


def _softmax_pv(qs, k_ref, v_ref):
    n_keys = k_ref.shape[1]
    starts = list(range(0, n_keys, ATTN_KEY_BLOCK))
    m = [None] * len(qs)
    acc = [None] * len(qs)
    for start in starts:
        size = min(ATTN_KEY_BLOCK, n_keys - start)
        k = k_ref[0, start:start + size, :]
        v1 = v_ref[0, start:start + size, :]
        for i, q in enumerate(qs):
            s = _dot_nt(q, k)
            m_blk = jnp.max(s, axis=-1, keepdims=True)
            if start == 0:
                m[i] = m_blk
                acc[i] = _dot(jnp.exp2(s - m_blk).astype(BF16), v1)
            else:
                m_new = jnp.maximum(m[i], m_blk)
                pv = _dot(jnp.exp2(s - m_new).astype(BF16), v1)
                acc[i] = jnp.exp2(m[i] - m_new) * acc[i] + pv
                m[i] = m_new
    return [a[:, :LANES] / a[:, LANES:] for a in acc]


def _attn_kernel(q_ref, k_ref, v_ref, lam_ref, sg_ref, *rest, diff, lam_init):
    o_ref = rest[-1]
    q = q_ref[0]
    if not diff:
        o_ref[0] = _softmax_pv([q], k_ref, v_ref)[0].astype(o_ref.dtype)
        return
    first = lax.broadcasted_iota(jnp.int32, q.shape, 1) < DA_HEAD_DIM
    zero = jnp.zeros_like(q)
    o1, o2 = _softmax_pv([jnp.where(first, q, zero), jnp.where(first, zero, q)], k_ref, v_ref)
    lp = lam_ref[...]
    lam = (jnp.exp(jnp.sum(lp[0:1] * lp[1:2], axis=-1, keepdims=True))
           - jnp.exp(jnp.sum(lp[2:3] * lp[3:4], axis=-1, keepdims=True)) + lam_init)
    o_ref[0] = (_rms(o1 - lam * o2) * sg_ref[...] * (1.0 - lam_init)).astype(o_ref.dtype)


def _attention(q, k, v1, lam_p, subln_g, q_rows, tq, n_keys, out_rows, out_start, diff, lam_init, earlier=None):
    b, _, width = q.shape
    heads = width // LANES
    q_start, q_stop = q_rows

    def rows_from(start):
        return pl.BlockSpec((pl.Element(1), pl.Element(tq), pl.Element(LANES)),
                            lambda bi, hi, ti: (bi, pl.multiple_of(start + ti * tq, math.gcd(start, tq)),
                                                pl.multiple_of(hi * LANES, LANES)))

    khead = pl.BlockSpec((1, n_keys, LANES), lambda bi, hi, ti: (bi, 0, hi))
    vhead = pl.BlockSpec((1, n_keys, 2 * LANES), lambda bi, hi, ti: (bi, 0, hi))

    def const(shape):
        zeros = (0,) * len(shape)
        return pl.BlockSpec(shape, lambda bi, hi, ti: zeros)

    in_specs = [rows_from(q_start), khead, vhead, const(lam_p.shape), const((1, LANES))]
    args = [q, k, v1, lam_p, subln_g.reshape(1, LANES)]
    aliases = {}
    if earlier is not None:
        in_specs.append(pl.BlockSpec(memory_space=pl.ANY))
        args.append(earlier)
        aliases = {len(args) - 1: 0}
    return pl.pallas_call(
        functools.partial(_attn_kernel, diff=diff, lam_init=lam_init),
        out_shape=jax.ShapeDtypeStruct((b, out_rows, width), BF16),
        grid=(b, heads, (q_stop - q_start) // tq),
        in_specs=in_specs,
        out_specs=rows_from(out_start),
        input_output_aliases=aliases,
        compiler_params=_params("arbitrary", "arbitrary", "arbitrary"),
        name="diff_attention" if diff else "mla_attention",
    )(*args)


def _attend_segments(q, k, v1, lam_p, subln_g, ctx_len, diff, lam_init, need_ctx):
    t = k.shape[1]
    n_lat = t - ctx_len
    tq_lat = math.gcd(n_lat, ATTN_QUERY_TILE)
    if not need_ctx:
        return _attention(q, k, v1, lam_p, subln_g, (ctx_len, t), tq_lat, t, n_lat, 0, diff, lam_init)
    o = _attention(q, k, v1, lam_p, subln_g, (ctx_len, t), tq_lat, t, t, ctx_len, diff, lam_init)
    return _attention(q, k, v1, lam_p, subln_g, (0, ctx_len), ctx_len, ctx_len, t, 0, diff, lam_init, earlier=o)


def kernel(x, c, ctx, c_ctx, ada_w, ada_b, norm_mix_g, norm_ffn_g, ffn_w_gu, ffn_w_down, ev_w_in, ev_w_out, hg_lower_bound, hg_out_norm_g, hy_short_w, hy_short_b, hy_filt_w1, hy_filt_b1, hy_filt_freq1, hy_filt_w2, hy_filt_b2, hy_filt_freq2, hy_filt_w3, hy_skip, od_w_in, od_w_out, da_lambda, da_subln_g, mla_q_norm_g, mla_w_uq, mla_kv_norm_g, mla_w_ukv, final_norm_g):
    n_batch, n_lat, d = x.shape
    ctx_len = ctx.shape[1]
    depth = ada_w.shape[0]
    assert n_batch % 2 == 0 and n_batch < MOD_ROWS
    assert n_lat % GRID_W == 0 and ctx_len % HG_CHUNK == 0 and n_lat % FFT_N2 == 0 and ctx_len % FFT_N2 == 0
    tm = math.gcd(math.gcd(ctx_len, n_lat), 256)
    n_ctx_tiles = ctx_len // tm
    hg_width = d // 2
    hy_width = d - hg_width
    da_w = d // 2
    q_rank, kv_rank = mla_q_norm_g.shape[1], mla_kv_norm_g.shape[1]

    cc = jnp.concatenate([c, c_ctx[None], jnp.zeros((MOD_ROWS - n_batch - 1, d), F32)], axis=0)
    mods = _ada(cc, ada_w, ada_b).reshape(depth, MOD_ROWS, 6, d)
    h = (ctx, x) if depth > 1 else jnp.concatenate([ctx, x], axis=1)
    rope_tabs = _rope_tables(n_lat, ctx_len)

    for i in range(depth):
        last = i == depth - 1
        if i % 2 == 0:
            e = i // 2
            z = _in_even(h, mods[i], norm_mix_g[i], ev_w_in[e].astype(BF16), tm, n_ctx_tiles)
            a = _hgrn(z, hg_lower_bound, hg_out_norm_g[e], e, hg_width // HG_HEAD_DIM, ctx_len)
            filt_params = (hy_filt_w1[e], hy_filt_b1[e], hy_filt_freq1[e], hy_filt_w2[e], hy_filt_b2[e],
                           hy_filt_freq2[e], hy_filt_w3[e])
            bb = _hyena(z, 5 * hg_width, hy_width, ctx_len, hy_short_w[e], hy_short_b[e], filt_params, hy_skip[e])
            w_out = ev_w_out[e]
        else:
            o = i // 2
            lam_init = 0.8 - 0.6 * math.exp(-0.3 * i)
            w_big, wuq, wukv = _odd_weights(od_w_in[o], mla_w_uq[o], mla_w_ukv[o], da_w, q_rank, kv_rank)
            qd, kd, vd, qm, km, vm = _in_odd(h, mods[i], norm_mix_g[i], w_big, wuq, wukv, mla_q_norm_g[o],
                                             mla_kv_norm_g[o], rope_tabs, tm, n_ctx_tiles, da_w)
            a = _attend_segments(qd, kd, vd, da_lambda[o], da_subln_g[o], ctx_len, True, lam_init, not last)
            bb = _attend_segments(qm, km, vm, da_lambda[o], da_subln_g[o], ctx_len, False, lam_init, not last)
            w_out = od_w_out[o]
        weights = (w_out.astype(BF16), ffn_w_gu[i].astype(BF16), ffn_w_down[i].astype(BF16))
        if not last:
            h = _post(h, a, bb, mods[i], norm_ffn_g[i], *weights, tm, n_ctx_tiles)
    if a.shape[1] != n_lat:
        a, bb = a[:, ctx_len:], bb[:, ctx_len:]
    return _post_final(h, a, bb, mods[depth - 1], norm_ffn_g[depth - 1], *weights, final_norm_g, tm, n_ctx_tiles)
```
